```python
import jax, jax.numpy as jnp
from jax import lax
import numpy as np

D_MODEL = 1024
BATCH = 32
SEQ = 256
DEPTH = 1
DEC_BATCH = 8
DEC_SEQ = 2048
PAST_LEN = 512

GRID_W = 64
H_RET = 4
DK_RET = 64
DV_RET = 128
H_GLA = 4
DK_GLA = 64
DV_GLA = 128
GLA_RANK = 16
GLA_TAU = 16.0
CHUNK = 64
ROPE_BASE = 10000.0
EPS = 1e-6
QK_RET = H_RET * DK_RET
W_RET = H_RET * DV_RET
QK_GLA = H_GLA * DK_GLA
W_GLA = H_GLA * DV_GLA
D_MIX = W_RET + W_GLA
D_IN_PROJ = 2 * QK_RET + 2 * W_RET + 2 * QK_GLA + 2 * W_GLA + 2 * GLA_RANK

kernel_name = "hybrid_retention_gla_diffusion_step"


def rms_norm(x):
    xf = x.astype(jnp.float32)
    return (xf * lax.rsqrt(jnp.mean(xf * xf, axis=-1, keepdims=True) + EPS)).astype(x.dtype)


def modulation(cond, w_mod, b_mod):
    m = jax.nn.silu(cond) @ w_mod + b_mod
    shift, scale, gate = jnp.split(m, 3, axis=-1)
    return shift, scale, gate


def axial_rope(n_tokens):
    rows = n_tokens // GRID_W
    rr, cc = jnp.meshgrid(jnp.arange(rows), jnp.arange(GRID_W), indexing="ij")
    rr = rr.reshape(-1).astype(jnp.float32)
    cc = cc.reshape(-1).astype(jnp.float32)
    n_freq = DK_RET // 4
    inv = ROPE_BASE ** (-jnp.arange(n_freq, dtype=jnp.float32) / n_freq)
    ang = jnp.concatenate([rr[:, None] * inv, cc[:, None] * inv], axis=-1)
    return jnp.cos(ang), jnp.sin(ang)


def apply_rope(x, cos, sin):
    x1, x2 = jnp.split(x, 2, axis=-1)
    c = cos.astype(x.dtype)
    s = sin.astype(x.dtype)
    return jnp.concatenate([x1 * c - x2 * s, x2 * c + x1 * s], axis=-1)


def _split_heads(a, n_heads):
    B, T, W = a.shape
    return a.reshape(B, T, n_heads, W // n_heads).transpose(0, 2, 1, 3)


def _merge_heads(a):
    B, H, T, d = a.shape
    return a.transpose(0, 2, 1, 3).reshape(B, T, H * d)


def chunked_gated_recurrence(q, k, v, log_g, s0):
    f32 = jnp.float32
    B, H, T, dk = q.shape
    dv = v.shape[-1]
    n = T // CHUNK
    qc = q.astype(f32).reshape(B, H, n, CHUNK, dk)
    kc = k.astype(f32).reshape(B, H, n, CHUNK, dk)
    vc = v.astype(f32).reshape(B, H, n, CHUNK, dv)
    b = jnp.cumsum(log_g.astype(f32).reshape(B, H, n, CHUNK, dk), axis=3)
    b_last = b[:, :, :, -1:, :]
    q_dec = qc * jnp.exp(b)
    k_inv = kc * jnp.exp(-b)
    k_dec = kc * jnp.exp(b_last - b)
    lower = jnp.tril(jnp.ones((CHUNK, CHUNK), dtype=bool))
    scores = jnp.where(lower, jnp.einsum("bhnid,bhnjd->bhnij", q_dec, k_inv), 0.0)
    o_intra = jnp.einsum("bhnij,bhnje->bhnie", scores, vc)
    kv = jnp.einsum("bhncd,bhnce->bhnde", k_dec, vc)
    decay = jnp.exp(b_last[:, :, :, 0, :])

    def step(s, inp):
        dec_n, kv_n = inp
        return dec_n[..., None] * s + kv_n, s

    s_final, s_starts = lax.scan(step, s0.astype(f32),
                                 (jnp.moveaxis(decay, 2, 0), jnp.moveaxis(kv, 2, 0)))
    s_starts = jnp.moveaxis(s_starts, 0, 2)
    o_inter = jnp.einsum("bhncd,bhnde->bhnce", q_dec, s_starts)
    return (o_intra + o_inter).reshape(B, H, T, dv), s_final


def bidirectional_recurrence(q, k, v, log_g_fwd, log_g_bwd, s0_fwd, s0_bwd):
    o_f, s_f = chunked_gated_recurrence(q, k, v, log_g_fwd, s0_fwd)
    flip = lambda a: jnp.flip(a, axis=2)
    o_b, s_b = chunked_gated_recurrence(flip(q), flip(k), flip(v), flip(log_g_bwd), s0_bwd)
    return o_f + flip(o_b), s_f, s_b


def mixer_layer(x, shift, scale, gate, s_ret0, s_gla0, rope,
                w_in, ret_log_decay, gla_w_alpha, gla_b_alpha, gla_norm_w, w_out):
    f32 = jnp.float32
    dt = x.dtype
    h = rms_norm(x) * (1.0 + scale) + shift
    proj = h @ w_in
    widths = (QK_RET, QK_RET, W_RET, W_RET, QK_GLA, QK_GLA, W_GLA, W_GLA)
    points = []
    acc = 0
    for w in widths:
        acc += w
        points.append(acc)
    q_r, k_r, v_r, z_r, q_g, k_g, v_g, z_g, lr = jnp.split(proj, points, axis=-1)

    q_r = _split_heads(q_r, H_RET)
    k_r = _split_heads(k_r, H_RET) * (DK_RET ** -0.5)
    if rope is not None:
        cos, sin = rope
        q_r = apply_rope(q_r, cos, sin)
        k_r = apply_rope(k_r, cos, sin)
    v_r = _split_heads(v_r, H_RET)
    B, _, T, _ = q_r.shape
    ld = ret_log_decay.astype(f32)
    g_rf = jnp.broadcast_to(ld[0][None, :, None, None], (B, H_RET, T, DK_RET))
    g_rb = jnp.broadcast_to(ld[1][None, :, None, None], (B, H_RET, T, DK_RET))
    o_r, sr_f, sr_b = bidirectional_recurrence(q_r, k_r, v_r, g_rf, g_rb, s_ret0[:, 0], s_ret0[:, 1])
    mu = jnp.mean(o_r, axis=-1, keepdims=True)
    var = jnp.mean(jnp.square(o_r - mu), axis=-1, keepdims=True)
    o_r = (o_r - mu) * lax.rsqrt(var + EPS)
    o_r = _merge_heads(o_r).astype(dt) * jax.nn.silu(z_r)

    q_g = _split_heads(q_g, H_GLA) * (DK_GLA ** -0.5)
    k_g = _split_heads(k_g, H_GLA)
    v_g = _split_heads(v_g, H_GLA)
    lr_f, lr_b = jnp.split(lr, 2, axis=-1)

    def gla_log_gate(lr_d, w_a, b_a):
        logit = (lr_d @ w_a + b_a).astype(f32)
        return _split_heads(jax.nn.log_sigmoid(logit) / GLA_TAU, H_GLA)

    o_g, sg_f, sg_b = bidirectional_recurrence(
        q_g, k_g, v_g,
        gla_log_gate(lr_f, gla_w_alpha[0], gla_b_alpha[0]),
        gla_log_gate(lr_b, gla_w_alpha[1], gla_b_alpha[1]),
        s_gla0[:, 0], s_gla0[:, 1])
    o_g = o_g * lax.rsqrt(jnp.mean(o_g * o_g, axis=-1, keepdims=True) + EPS) * gla_norm_w.astype(f32)
    o_g = _merge_heads(o_g).astype(dt) * jax.nn.silu(z_g)

    out = jnp.concatenate([o_r, o_g], axis=-1) @ w_out
    y = x + gate * out
    return y, jnp.stack([sr_f, sr_b], axis=1), jnp.stack([sg_f, sg_b], axis=1)


def setup_inputs(seed: int = 0) -> dict:
    key = jax.random.key(seed)
    ks = jax.random.split(key, 16)
    f32 = jnp.float32
    base_decay = np.log(1.0 - 2.0 ** (-5.0 - np.arange(H_RET))).astype(np.float32)
    ret_log_decay = jnp.asarray(base_decay)[None, None, :] * jnp.exp(
        0.1 * jax.random.normal(ks[9], (DEPTH, 2, H_RET), f32))
    return {
        "x_prompt": jax.random.normal(ks[0], (BATCH, SEQ, D_MODEL), f32),
        "x_sample": jax.random.normal(ks[1], (DEC_BATCH, DEC_SEQ, D_MODEL), f32),
        "c": jax.random.normal(ks[2], (DEC_BATCH, D_MODEL), f32),
        "state_ret": 0.5 * jax.random.normal(ks[3], (DEC_BATCH, DEPTH, 2, H_RET, DK_RET, DV_RET), f32),
        "state_gla": 0.5 * jax.random.normal(ks[4], (DEC_BATCH, DEPTH, 2, H_GLA, DK_GLA, DV_GLA), f32),
        "c_ctx": jax.random.normal(ks[5], (D_MODEL,), f32),
        "w_mod": jax.random.normal(ks[6], (DEPTH, D_MODEL, 3 * D_MODEL), f32) * (D_MODEL ** -0.5),
        "b_mod": 0.01 * jax.random.normal(ks[7], (DEPTH, 3 * D_MODEL), f32),
        "w_in": jax.random.normal(ks[8], (DEPTH, D_MODEL, D_IN_PROJ), f32) * (D_MODEL ** -0.5),
        "ret_log_decay": ret_log_decay,
        "gla_w_alpha": jax.random.normal(ks[10], (DEPTH, 2, GLA_RANK, QK_GLA), f32) * (GLA_RANK ** -0.5),
        "gla_b_alpha": 0.01 * jax.random.normal(ks[11], (DEPTH, 2, QK_GLA), f32),
        "gla_norm_w": 1.0 + 0.05 * jax.random.normal(ks[12], (DEPTH, DV_GLA), f32),
        "w_out": jax.random.normal(ks[13], (DEPTH, D_MIX, D_MODEL), f32) * (D_MIX ** -0.5),
        "final_norm_w": 1.0 + 0.05 * jax.random.normal(ks[14], (D_MODEL,), f32),
    }


def reference(x_prompt, x_sample, c, state_ret, state_gla, c_ctx, w_mod, b_mod, w_in,
              ret_log_decay, gla_w_alpha, gla_b_alpha, gla_norm_w, w_out, final_norm_w):
    x = x_prompt
    B_ctx = x_prompt.shape[0]
    new_ret, new_gla = [], []
    for l in range(DEPTH):
        shift, scale, gate = modulation(c_ctx[None, :], w_mod[l], b_mod[l])
        z_ret = jnp.zeros((B_ctx, 2, H_RET, DK_RET, DV_RET), jnp.float32)
        z_gla = jnp.zeros((B_ctx, 2, H_GLA, DK_GLA, DV_GLA), jnp.float32)
        x, s_r, s_g = mixer_layer(x, shift, scale, gate, z_ret, z_gla, None,
                                  w_in[l], ret_log_decay[l], gla_w_alpha[l], gla_b_alpha[l],
                                  gla_norm_w[l], w_out[l])
        new_ret.append(s_r.astype(x_prompt.dtype))
        new_gla.append(s_g.astype(x_prompt.dtype))
    y_prompt = rms_norm(x) * final_norm_w
    new_state_ret = jnp.stack(new_ret, axis=1)
    new_state_gla = jnp.stack(new_gla, axis=1)

    rope = axial_rope(x_sample.shape[1])
    xs = x_sample
    for l in range(DEPTH):
        shift, scale, gate = modulation(c[:, None, :], w_mod[l], b_mod[l])
        xs, _, _ = mixer_layer(xs, shift, scale, gate, state_ret[:, l], state_gla[:, l], rope,
                               w_in[l], ret_log_decay[l], gla_w_alpha[l], gla_b_alpha[l],
                               gla_norm_w[l], w_out[l])
    y_sample = rms_norm(xs) * final_norm_w
    return (y_prompt, y_sample, new_state_ret, new_state_gla)
```

```python
import functools

import jax
import jax.numpy as jnp
from jax import lax
from jax.experimental import pallas as pl
from jax.experimental.pallas import tpu as pltpu

F32 = jnp.float32
BF16 = jnp.bfloat16

D_MODEL = 1024
N_HEADS = 4
DK = 64
DV = 128
QK = N_HEADS * DK
WV = N_HEADS * DV
GLA_RANK = 16
GLA_TAU = 16.0
EPS = 1e-6
GRID_W = 64
ROPE_BASE = 10000.0

LANES = 128
SCAN_CHUNK = 128
TOKEN_BLOCK = 512
MOD_ROWS = 16
MOD_COL_BLOCK = 512
LR_PAD = LANES
VMEM_LIMIT = 48 * 1024 * 1024

_NT = (((1,), (1,)), ((), ()))


def _dot(a, b):
    return jnp.dot(a, b, preferred_element_type=F32)


def _dot_nt(a, b):
    return lax.dot_general(a, b, _NT, preferred_element_type=F32)


def _silu(x):
    return x * jax.nn.sigmoid(x)


def _mod_kernel(c_ref, w_ref, b_ref, o_ref):
    s = _silu(c_ref[...]).astype(BF16)
    o_ref[...] = _dot(s, w_ref[...].astype(BF16)) + b_ref[...]


def _modulation(cond, w_mod, b_mod):
    n_col = w_mod.shape[1]
    return pl.pallas_call(
        _mod_kernel,
        grid=(n_col // MOD_COL_BLOCK,),
        in_specs=[
            pl.BlockSpec((MOD_ROWS, D_MODEL), lambda j: (0, 0)),
            pl.BlockSpec((D_MODEL, MOD_COL_BLOCK), lambda j: (0, j)),
            pl.BlockSpec((1, MOD_COL_BLOCK), lambda j: (0, j)),
        ],
        out_specs=pl.BlockSpec((MOD_ROWS, MOD_COL_BLOCK), lambda j: (0, j)),
        out_shape=jax.ShapeDtypeStruct((MOD_ROWS, n_col), F32),
        compiler_params=pltpu.CompilerParams(vmem_limit_bytes=VMEM_LIMIT),
        name="modulation",
    )(cond, w_mod, b_mod.reshape(1, n_col))


def _rope_tile(t, cos, sin_signed):
    lane = lax.broadcasted_iota(jnp.int32, t.shape, 1)
    first_half = (lane & 32) == 0
    swapped = jnp.where(first_half, pltpu.roll(t, LANES - 32, 1), pltpu.roll(t, 32, 1))
    return t * cos + swapped * sin_signed


def _inproj_kernel(use_rope, x_ref, mod_ref, wqk_ref, wz_ref, wvt_ref, wlr_ref, wa_ref, ba_ref,
                   *rest):
    if use_rope:
        cos_ref, sin_ref, qk_ref, z_ref, vt_ref, lg_ref = rest
    else:
        qk_ref, z_ref, vt_ref, lg_ref = rest
    x = x_ref[0]
    shift = mod_ref[0, :, 0:D_MODEL]
    scale = mod_ref[0, :, D_MODEL:2 * D_MODEL]
    inv = lax.rsqrt(jnp.mean(x * x, axis=-1, keepdims=True) + EPS)
    hb = ((x * inv) * (1.0 + scale) + shift).astype(BF16)

    for half in range(2):
        res = _dot(hb, wqk_ref[:, half * 2 * QK:(half + 1) * 2 * QK])
        for t in range(4):
            tile = res[:, t * LANES:(t + 1) * LANES]
            is_q = t < 2
            if (half == 0 and not is_q) or (half == 1 and is_q):
                tile = tile * (DK ** -0.5)
            if use_rope and half == 0:
                tile = _rope_tile(tile, cos_ref[...], sin_ref[...])
            col = half * 2 * QK + t * LANES
            qk_ref[0, :, col:col + LANES] = tile.astype(BF16)

    for half in range(2):
        z_ref[0, :, half * WV:(half + 1) * WV] = _dot(
            hb, wz_ref[:, half * WV:(half + 1) * WV]).astype(BF16)

    for half in range(2):
        vt = _dot_nt(wvt_ref[half * WV:(half + 1) * WV, :], hb)
        for j in range(TOKEN_BLOCK // SCAN_CHUNK):
            vt_ref[0, j, half * WV:(half + 1) * WV, :] = vt[:, j * SCAN_CHUNK:(j + 1) * SCAN_CHUNK].astype(BF16)

    lr = _dot(hb, wlr_ref[...]).astype(BF16)
    logit = _dot(lr, wa_ref[...]) + ba_ref[...]
    log_sig = jnp.minimum(logit, 0.0) - jnp.log1p(jnp.exp(-jnp.abs(logit)))
    lg_ref[0] = log_sig * (1.0 / GLA_TAU)


def _inproj(x, mod, mod_row_fn, weights, rope):
    B, T, _ = x.shape
    wqk, wz, wvt, wlr, wa, ba = weights
    nt = T // TOKEN_BLOCK
    cpb = TOKEN_BLOCK // SCAN_CHUNK
    const = lambda b, t: (0, 0)
    in_specs = [
        pl.BlockSpec((1, TOKEN_BLOCK, D_MODEL), lambda b, t: (b, t, 0)),
        pl.BlockSpec((1, 1, 3 * D_MODEL), lambda b, t: (mod_row_fn(b), 0, 0)),
        pl.BlockSpec(wqk.shape, const),
        pl.BlockSpec(wz.shape, const),
        pl.BlockSpec(wvt.shape, const),
        pl.BlockSpec(wlr.shape, const),
        pl.BlockSpec(wa.shape, const),
        pl.BlockSpec(ba.shape, const),
    ]
    args = [x, mod, wqk, wz, wvt, wlr, wa, ba]
    if rope is not None:
        in_specs += [pl.BlockSpec((TOKEN_BLOCK, LANES), lambda b, t: (t, 0))] * 2
        args += list(rope)
    out_shape = (
        jax.ShapeDtypeStruct((B, T, 4 * QK), BF16),
        jax.ShapeDtypeStruct((B, T, 2 * WV), BF16),
        jax.ShapeDtypeStruct((B, T // SCAN_CHUNK, 2 * WV, SCAN_CHUNK), BF16),
        jax.ShapeDtypeStruct((B, T, 2 * QK), F32),
    )
    out_specs = (
        pl.BlockSpec((1, TOKEN_BLOCK, 4 * QK), lambda b, t: (b, t, 0)),
        pl.BlockSpec((1, TOKEN_BLOCK, 2 * WV), lambda b, t: (b, t, 0)),
        pl.BlockSpec((1, cpb, 2 * WV, SCAN_CHUNK), lambda b, t: (b, t, 0, 0)),
        pl.BlockSpec((1, TOKEN_BLOCK, 2 * QK), lambda b, t: (b, t, 0)),
    )
    return pl.pallas_call(
        functools.partial(_inproj_kernel, rope is not None),
        grid=(B, nt),
        in_specs=in_specs,
        out_specs=out_specs,
        out_shape=out_shape,
        compiler_params=pltpu.CompilerParams(
            dimension_semantics=("parallel", "parallel"), vmem_limit_bytes=VMEM_LIMIT),
        name="inproj_rope" if rope is not None else "inproj",
    )(*args)


def _scan_kernel(is_gla, has_init, want_state, n_chunks, *refs):
    refs = list(refs)
    q_ref, k_ref, vt_ref = refs[:3]
    pos = 3
    if is_gla:
        lgf_ref, lgb_ref = refs[pos:pos + 2]
        pos += 2
    else:
        ld_ref = refs[pos]
        pos += 1
    if has_init:
        s0_ref = refs[pos]
        pos += 1
    o_ref = refs[pos]
    pos += 1
    if want_state:
        sn_ref = refs[pos]
        pos += 1
    st_ref = refs[pos]

    C = SCAN_CHUNK
    mid = C // 2
    pair = pl.program_id(1)
    lane = lax.broadcasted_iota(jnp.int32, (1, LANES), 1)
    head_mask = (lane < DK, lane >= DK)
    ri = lax.broadcasted_iota(jnp.int32, (C, C), 0)
    ci = lax.broadcasted_iota(jnp.int32, (C, C), 1)
    tpos = lax.broadcasted_iota(jnp.int32, (C, 1), 0).astype(F32)

    for d in range(2):
        fwd = d == 0
        tri = (ri >= ci) if fwd else (ci >= ri)

        if has_init:
            s0 = s0_ref[0, 0, d]
            both = jnp.concatenate([s0[0], s0[1]], axis=0).T
            st_ref[0] = both
            st_ref[1] = both
        else:
            st_ref[...] = jnp.zeros(st_ref.shape, F32)

        if not is_gla:
            ld0 = ld_ref[d, 2 * pair]
            ld1 = ld_ref[d, 2 * pair + 1]
            ldr = jnp.where(lane < DK, ld0, ld1)
            if fwd:
                q_mul = jnp.exp(ldr * (tpos + 1.0))
                k_mul = jnp.exp(ldr * (C - 1.0 - tpos))
            else:
                q_mul = jnp.exp(ldr * (C - tpos))
                k_mul = jnp.exp(ldr * tpos)
            ret_decay = jnp.exp(ldr * float(C))
            dist = ((ri - ci) if fwd else (ci - ri)).astype(F32)
            score_mul = [jnp.where(tri, jnp.exp(ldh * dist), 0.0) for ldh in (ld0, ld1)]
        else:
            tri_ones = tri.astype(BF16)

        def chunk_step(n, carry):
            c = n if fwd else n_chunks - 1 - n
            rows = pl.ds(pl.multiple_of(c * C, C), C)
            q = q_ref[0, rows, :]
            k = k_ref[0, rows, :]
            if is_gla:
                lg = (lgf_ref if fwd else lgb_ref)[0, rows, :]
                hi = lg.astype(BF16)
                lo = (lg - hi.astype(F32)).astype(BF16)
                b = _dot(tri_ones, hi) + _dot(tri_ones, lo)
                r = b[mid - 1:mid] if fwd else b[mid:mid + 1]
                bl = b[C - 1:C] if fwd else b[0:1]
                qs = q.astype(F32) * jnp.exp(b - r)
                ks = k.astype(F32) * jnp.exp(r - b)
                q_sc = qs.astype(BF16)
                k_sc = ks.astype(BF16)
                q_dec = (qs * jnp.exp(r)).astype(BF16)
                k_dec = (ks * jnp.exp(bl - r)).astype(BF16)
                decay = jnp.exp(bl)
            else:
                q_sc = q
                k_sc = k
                q_dec = (q.astype(F32) * q_mul).astype(BF16)
                k_dec = (k.astype(F32) * k_mul).astype(BF16)
                decay = ret_decay
            zero = jnp.zeros((), BF16)
            for hh in range(2):
                sc = _dot_nt(jnp.where(head_mask[hh], q_sc, zero), k_sc)
                if is_gla:
                    sc = jnp.where(tri, sc, 0.0)
                else:
                    sc = sc * score_mul[hh]
                vth = vt_ref[0, c, hh * DV:(hh + 1) * DV, :]
                st = st_ref[hh]
                oh = _dot_nt(sc.astype(BF16), vth) + _dot_nt(
                    jnp.where(head_mask[hh], q_dec, zero), st.astype(BF16))
                cols = slice(hh * DV, (hh + 1) * DV)
                if fwd:
                    o_ref[0, rows, cols] = oh
                else:
                    o_ref[0, rows, cols] = o_ref[0, rows, cols] + oh
                st_ref[hh] = st * decay + _dot(vth, k_dec)
            return carry

        lax.fori_loop(0, n_chunks, chunk_step, 0)

        if want_state:
            for hh in range(2):
                sn_ref[0, 0, d, hh] = st_ref[hh].T[hh * DK:(hh + 1) * DK, :]


def _scan(is_gla, qk, vt, lg, ld, s0, want_state):
    B, T, _ = qk.shape
    n_chunks = T // SCAN_CHUNK
    q_blk = 4 if is_gla else 0
    k_blk = q_blk + 2
    v_blk = 2 if is_gla else 0
    in_specs = [
        pl.BlockSpec((1, T, LANES), lambda b, p: (b, 0, q_blk + p)),
        pl.BlockSpec((1, T, LANES), lambda b, p: (b, 0, k_blk + p)),
        pl.BlockSpec((1, n_chunks, 2 * DV, SCAN_CHUNK), lambda b, p: (b, 0, v_blk + p, 0)),
    ]
    args = [qk, qk, vt]
    if is_gla:
        in_specs += [
            pl.BlockSpec((1, T, LANES), lambda b, p: (b, 0, p)),
            pl.BlockSpec((1, T, LANES), lambda b, p: (b, 0, 2 + p)),
        ]
        args += [lg, lg]
    else:
        in_specs += [pl.BlockSpec(memory_space=pltpu.SMEM)]
        args += [ld]
    state_spec = pl.BlockSpec((1, 1, 2, 2, DK, DV), lambda b, p: (b, 0, 0, p, 0, 0))
    if s0 is not None:
        in_specs += [state_spec]
        args += [s0]
    out_shape = [jax.ShapeDtypeStruct((B, T, WV), F32)]
    out_specs = [pl.BlockSpec((1, T, 2 * DV), lambda b, p: (b, 0, p))]
    if want_state:
        out_shape += [jax.ShapeDtypeStruct((B, 1, 2, N_HEADS, DK, DV), F32)]
        out_specs += [state_spec]
    name = ("gla" if is_gla else "ret") + ("_init" if s0 is not None else "") + "_scan"
    return pl.pallas_call(
        functools.partial(_scan_kernel, is_gla, s0 is not None, want_state, n_chunks),
        grid=(B, 2),
        in_specs=in_specs,
        out_specs=out_specs,
        out_shape=out_shape,
        scratch_shapes=[pltpu.VMEM((2, DV, LANES), F32)],
        compiler_params=pltpu.CompilerParams(
            dimension_semantics=("arbitrary", "arbitrary"), vmem_limit_bytes=VMEM_LIMIT),
        name=name,
    )(*args)


def _outproj_kernel(x_ref, mod_ref, or_ref, og_ref, zr_ref, zg_ref, gw_ref, fw_ref, wout_ref, y_ref):
    gate = mod_ref[0, :, 2 * D_MODEL:3 * D_MODEL]
    gw = gw_ref[...]
    parts = []
    for h in range(N_HEADS):
        cols = slice(h * DV, (h + 1) * DV)
        t = or_ref[0, :, cols]
        mu = jnp.mean(t, axis=-1, keepdims=True)
        dlt = t - mu
        var = jnp.mean(dlt * dlt, axis=-1, keepdims=True)
        n = dlt * lax.rsqrt(var + EPS)
        parts.append((n * _silu(zr_ref[0, :, cols].astype(F32))).astype(BF16))
    for h in range(N_HEADS):
        cols = slice(h * DV, (h + 1) * DV)
        t = og_ref[0, :, cols]
        n = t * lax.rsqrt(jnp.mean(t * t, axis=-1, keepdims=True) + EPS) * gw
        parts.append((n * _silu(zg_ref[0, :, cols].astype(F32))).astype(BF16))
    mixed = jnp.concatenate(parts, axis=-1)
    out = _dot(mixed, wout_ref[...])
    y = x_ref[0] + gate * out
    yn = y * lax.rsqrt(jnp.mean(y * y, axis=-1, keepdims=True) + EPS)
    y_ref[0] = yn * fw_ref[...]


def _outproj(x, mod, mod_row_fn, o_r, o_g, z, gw, fw, wout):
    B, T, _ = x.shape
    nt = T // TOKEN_BLOCK
    const = lambda b, t: (0, 0)
    return pl.pallas_call(
        _outproj_kernel,
        grid=(B, nt),
        in_specs=[
            pl.BlockSpec((1, TOKEN_BLOCK, D_MODEL), lambda b, t: (b, t, 0)),
            pl.BlockSpec((1, 1, 3 * D_MODEL), lambda b, t: (mod_row_fn(b), 0, 0)),
            pl.BlockSpec((1, TOKEN_BLOCK, WV), lambda b, t: (b, t, 0)),
            pl.BlockSpec((1, TOKEN_BLOCK, WV), lambda b, t: (b, t, 0)),
            pl.BlockSpec((1, TOKEN_BLOCK, WV), lambda b, t: (b, t, 0)),
            pl.BlockSpec((1, TOKEN_BLOCK, WV), lambda b, t: (b, t, 1)),
            pl.BlockSpec((1, DV), const),
            pl.BlockSpec((1, D_MODEL), const),
            pl.BlockSpec(wout.shape, const),
        ],
        out_specs=pl.BlockSpec((1, TOKEN_BLOCK, D_MODEL), lambda b, t: (b, t, 0)),
        out_shape=jax.ShapeDtypeStruct((B, T, D_MODEL), F32),
        compiler_params=pltpu.CompilerParams(
            dimension_semantics=("parallel", "parallel"), vmem_limit_bytes=VMEM_LIMIT),
        name="outproj",
    )(x, mod, o_r, o_g, z, z, gw, fw, wout)


def _rope_tables(n_tokens):
    rows = n_tokens // GRID_W
    rr, cc = jnp.meshgrid(jnp.arange(rows), jnp.arange(GRID_W), indexing="ij")
    rr = rr.reshape(-1).astype(F32)
    cc = cc.reshape(-1).astype(F32)
    n_freq = DK // 4
    inv = ROPE_BASE ** (-jnp.arange(n_freq, dtype=F32) / n_freq)
    ang = jnp.concatenate([rr[:, None] * inv, cc[:, None] * inv], axis=-1)
    cos, sin = jnp.cos(ang), jnp.sin(ang)
    cos_t = jnp.tile(jnp.concatenate([cos, cos], axis=-1), (1, LANES // DK))
    sin_t = jnp.tile(jnp.concatenate([-sin, sin], axis=-1), (1, LANES // DK))
    return cos_t, sin_t


def _prep_weights(w_in, gla_w_alpha, gla_b_alpha):
    q_r, k_r = w_in[:, 0:QK], w_in[:, QK:2 * QK]
    o = 2 * QK
    v_r, z_r = w_in[:, o:o + WV], w_in[:, o + WV:o + 2 * WV]
    o += 2 * WV
    q_g, k_g = w_in[:, o:o + QK], w_in[:, o + QK:o + 2 * QK]
    o += 2 * QK
    v_g, z_g = w_in[:, o:o + WV], w_in[:, o + WV:o + 2 * WV]
    o += 2 * WV
    lr = w_in[:, o:o + 2 * GLA_RANK]
    wqk = jnp.concatenate([q_r, k_r, q_g, k_g], axis=1).astype(BF16)
    wz = jnp.concatenate([z_r, z_g], axis=1).astype(BF16)
    wvt = jnp.concatenate([v_r, v_g], axis=1).T.astype(BF16)
    wlr = jnp.pad(lr, ((0, 0), (0, LR_PAD - 2 * GLA_RANK))).astype(BF16)
    wa = jnp.zeros((LR_PAD, 2 * QK), F32)
    wa = wa.at[0:GLA_RANK, 0:QK].set(gla_w_alpha[0])
    wa = wa.at[GLA_RANK:2 * GLA_RANK, QK:2 * QK].set(gla_w_alpha[1])
    ba = jnp.concatenate([gla_b_alpha[0], gla_b_alpha[1]]).reshape(1, 2 * QK)
    return wqk, wz, wvt, wlr, wa.astype(BF16), ba


def kernel(x_prompt, x_sample, c, state_ret, state_gla, c_ctx, w_mod, b_mod, w_in, ret_log_decay,
           gla_w_alpha, gla_b_alpha, gla_norm_w, w_out, final_norm_w):
    assert w_mod.shape[0] == 1, "single-layer model"
    b_ctx, t_ctx, _ = x_prompt.shape
    b_dec, t_dec, _ = x_sample.shape
    assert t_ctx % SCAN_CHUNK == 0 and t_dec % TOKEN_BLOCK == 0 and TOKEN_BLOCK % t_ctx == 0
    assert 1 + b_dec <= MOD_ROWS

    cond = jnp.concatenate(
        [c_ctx[None, :], c, jnp.zeros((MOD_ROWS - 1 - b_dec, D_MODEL), F32)], axis=0)
    mod = _modulation(cond, w_mod[0], b_mod[0]).reshape(MOD_ROWS, 1, 3 * D_MODEL)

    weights = _prep_weights(w_in[0], gla_w_alpha[0], gla_b_alpha[0])
    wout = w_out[0].astype(BF16)
    gw = gla_norm_w[0].reshape(1, DV)
    fw = final_norm_w.reshape(1, D_MODEL)
    ld = ret_log_decay[0]

    per_blk = TOKEN_BLOCK // t_ctx
    xp = x_prompt.reshape(b_ctx // per_blk, TOKEN_BLOCK, D_MODEL)
    ctx_row = lambda b: 0
    qk, z, vt, lg = _inproj(xp, mod, ctx_row, weights, None)
    qk = qk.reshape(b_ctx, t_ctx, 4 * QK)
    vt = vt.reshape(b_ctx, t_ctx // SCAN_CHUNK, 2 * WV, SCAN_CHUNK)
    lg = lg.reshape(b_ctx, t_ctx, 2 * QK)
    o_r, new_ret = _scan(False, qk, vt, None, ld, None, True)
    o_g, new_gla = _scan(True, qk, vt, lg, None, None, True)
    y_prompt = _outproj(xp, mod, ctx_row, o_r.reshape(xp.shape[0], TOKEN_BLOCK, WV),
                        o_g.reshape(xp.shape[0], TOKEN_BLOCK, WV), z, gw, fw, wout)
    y_prompt = y_prompt.reshape(b_ctx, t_ctx, D_MODEL)

    dec_row = lambda b: b + 1
    rope = _rope_tables(t_dec)
    qk, z, vt, lg = _inproj(x_sample, mod, dec_row, weights, rope)
    (o_r,) = _scan(False, qk, vt, None, ld, state_ret, False)
    (o_g,) = _scan(True, qk, vt, lg, None, state_gla, False)
    y_sample = _outproj(x_sample, mod, dec_row, o_r, o_g, z, gw, fw, wout)

    return (y_prompt, y_sample, new_ret, new_gla)
```

```python
import functools

import jax
import jax.numpy as jnp
from jax import lax
from jax.experimental import pallas as pl
from jax.experimental.pallas import tpu as pltpu

F32 = jnp.float32
BF16 = jnp.bfloat16

D_MODEL = 1024
N_HEADS = 4
DK = 64
DV = 128
QK = N_HEADS * DK
WV = N_HEADS * DV
GLA_RANK = 16
GLA_TAU = 16.0
EPS = 1e-6
GRID_W = 64
ROPE_BASE = 10000.0

LANES = 128
SCAN_CHUNK = 128
CTX_SEQ_BLOCK = 4
DEC_UNROLL = 4
TOKEN_BLOCK = 512
MOD_ROWS = 16
MOD_COL_BLOCK = 512
LR_PAD = LANES
VMEM_LIMIT = 48 * 1024 * 1024

_NT = (((1,), (1,)), ((), ()))


def _dot(a, b):
    return jnp.dot(a, b, preferred_element_type=F32)


def _dot_nt(a, b):
    return lax.dot_general(a, b, _NT, preferred_element_type=F32)


def _silu(x):
    return x * jax.nn.sigmoid(x)


def _mod_kernel(c_ref, w_ref, b_ref, o_ref):
    s = _silu(c_ref[...]).astype(BF16)
    o_ref[...] = _dot(s, w_ref[...].astype(BF16)) + b_ref[...]


def _modulation(cond, w_mod, b_mod):
    n_col = w_mod.shape[1]
    return pl.pallas_call(
        _mod_kernel,
        grid=(n_col // MOD_COL_BLOCK,),
        in_specs=[
            pl.BlockSpec((MOD_ROWS, D_MODEL), lambda j: (0, 0)),
            pl.BlockSpec((D_MODEL, MOD_COL_BLOCK), lambda j: (0, j)),
            pl.BlockSpec((1, MOD_COL_BLOCK), lambda j: (0, j)),
        ],
        out_specs=pl.BlockSpec((MOD_ROWS, MOD_COL_BLOCK), lambda j: (0, j)),
        out_shape=jax.ShapeDtypeStruct((MOD_ROWS, n_col), F32),
        compiler_params=pltpu.CompilerParams(vmem_limit_bytes=VMEM_LIMIT),
        name="modulation",
    )(cond, w_mod, b_mod.reshape(1, n_col))


def _rope_tile(t, cos, sin_signed):
    lane = lax.broadcasted_iota(jnp.int32, t.shape, 1)
    first_half = (lane & 32) == 0
    swapped = jnp.where(first_half, pltpu.roll(t, LANES - 32, 1), pltpu.roll(t, 32, 1))
    return t * cos + swapped * sin_signed


def _inproj_kernel(use_rope, x_ref, mod_ref, wqk_ref, wz_ref, wvt_ref, wlr_ref, wa_ref, ba_ref,
                   *rest):
    if use_rope:
        cos_ref, sin_ref, qk_ref, z_ref, vt_ref, lg_ref = rest
    else:
        qk_ref, z_ref, vt_ref, lg_ref = rest
    x = x_ref[0]
    shift = mod_ref[0, :, 0:D_MODEL]
    scale = mod_ref[0, :, D_MODEL:2 * D_MODEL]
    inv = lax.rsqrt(jnp.mean(x * x, axis=-1, keepdims=True) + EPS)
    hb = ((x * inv) * (1.0 + scale) + shift).astype(BF16)

    for half in range(2):
        res = _dot(hb, wqk_ref[:, half * 2 * QK:(half + 1) * 2 * QK])
        for t in range(4):
            tile = res[:, t * LANES:(t + 1) * LANES]
            is_q = t < 2
            if (half == 0 and not is_q) or (half == 1 and is_q):
                tile = tile * (DK ** -0.5)
            if use_rope and half == 0:
                tile = _rope_tile(tile, cos_ref[...], sin_ref[...])
            col = half * 2 * QK + t * LANES
            qk_ref[0, :, col:col + LANES] = tile.astype(BF16)

    for half in range(2):
        z_ref[0, :, half * WV:(half + 1) * WV] = _dot(
            hb, wz_ref[:, half * WV:(half + 1) * WV]).astype(BF16)

    for half in range(2):
        vt = _dot_nt(wvt_ref[half * WV:(half + 1) * WV, :], hb)
        for j in range(TOKEN_BLOCK // SCAN_CHUNK):
            vt_ref[0, j, half * WV:(half + 1) * WV, :] = vt[:, j * SCAN_CHUNK:(j + 1) * SCAN_CHUNK].astype(BF16)

    lr = _dot(hb, wlr_ref[...]).astype(BF16)
    logit = _dot(lr, wa_ref[...]) + ba_ref[...]
    log_sig = jnp.minimum(logit, 0.0) - jnp.log1p(jnp.exp(-jnp.abs(logit)))
    lg_ref[0] = log_sig * (1.0 / GLA_TAU)


def _inproj(x, mod, mod_row_fn, weights, rope):
    B, T, _ = x.shape
    wqk, wz, wvt, wlr, wa, ba = weights
    nt = T // TOKEN_BLOCK
    cpb = TOKEN_BLOCK // SCAN_CHUNK
    const = lambda b, t: (0, 0)
    in_specs = [
        pl.BlockSpec((1, TOKEN_BLOCK, D_MODEL), lambda b, t: (b, t, 0)),
        pl.BlockSpec((1, 1, 3 * D_MODEL), lambda b, t: (mod_row_fn(b), 0, 0)),
        pl.BlockSpec(wqk.shape, const),
        pl.BlockSpec(wz.shape, const),
        pl.BlockSpec(wvt.shape, const),
        pl.BlockSpec(wlr.shape, const),
        pl.BlockSpec(wa.shape, const),
        pl.BlockSpec(ba.shape, const),
    ]
    args = [x, mod, wqk, wz, wvt, wlr, wa, ba]
    if rope is not None:
        in_specs += [pl.BlockSpec((TOKEN_BLOCK, LANES), lambda b, t: (t, 0))] * 2
        args += list(rope)
    out_shape = (
        jax.ShapeDtypeStruct((B, T, 4 * QK), BF16),
        jax.ShapeDtypeStruct((B, T, 2 * WV), BF16),
        jax.ShapeDtypeStruct((B, T // SCAN_CHUNK, 2 * WV, SCAN_CHUNK), BF16),
        jax.ShapeDtypeStruct((B, T, 2 * QK), F32),
    )
    out_specs = (
        pl.BlockSpec((1, TOKEN_BLOCK, 4 * QK), lambda b, t: (b, t, 0)),
        pl.BlockSpec((1, TOKEN_BLOCK, 2 * WV), lambda b, t: (b, t, 0)),
        pl.BlockSpec((1, cpb, 2 * WV, SCAN_CHUNK), lambda b, t: (b, t, 0, 0)),
        pl.BlockSpec((1, TOKEN_BLOCK, 2 * QK), lambda b, t: (b, t, 0)),
    )
    return pl.pallas_call(
        functools.partial(_inproj_kernel, rope is not None),
        grid=(B, nt),
        in_specs=in_specs,
        out_specs=out_specs,
        out_shape=out_shape,
        compiler_params=pltpu.CompilerParams(
            dimension_semantics=("parallel", "parallel"), vmem_limit_bytes=VMEM_LIMIT),
        name="inproj_rope" if rope is not None else "inproj",
    )(*args)


def _scan_kernel(is_gla, has_init, want_state, n_chunks, seq_blk, unroll, *refs):
    refs = list(refs)
    q_ref, k_ref, vt_ref = refs[:3]
    pos = 3
    if is_gla:
        lgf_ref, lgb_ref = refs[pos:pos + 2]
        pos += 2
    else:
        ld_ref = refs[pos]
        pos += 1
    if has_init:
        s0_ref = refs[pos]
        pos += 1
    o_ref = refs[pos]
    pos += 1
    if want_state:
        sn_ref = refs[pos]
        pos += 1
    st_ref = refs[pos]

    C = SCAN_CHUNK
    mid = C // 2
    pair = pl.program_id(1)
    lane = lax.broadcasted_iota(jnp.int32, (1, LANES), 1)
    head_mask = (lane < DK, lane >= DK)
    ri = lax.broadcasted_iota(jnp.int32, (C, C), 0)
    ci = lax.broadcasted_iota(jnp.int32, (C, C), 1)
    tpos = lax.broadcasted_iota(jnp.int32, (C, 1), 0).astype(F32)
    tri = ((ri >= ci), (ci >= ri))

    if has_init:
        for s in range(seq_blk):
            for d in range(2):
                s0 = s0_ref[s, 0, d]
                both = jnp.concatenate([s0[0], s0[1]], axis=0).T
                st_ref[s, d, 0:DV] = both
                st_ref[s, d, DV:2 * DV] = both
    else:
        st_ref[...] = jnp.zeros(st_ref.shape, F32)

    if is_gla:
        tri_ones = [t.astype(BF16) for t in tri]
    else:
        q_mul, k_mul, ret_decay, score_mul = [], [], [], []
        for d in range(2):
            ld0 = ld_ref[d, 2 * pair]
            ld1 = ld_ref[d, 2 * pair + 1]
            ldr = jnp.where(lane < DK, ld0, ld1)
            if d == 0:
                q_mul.append(jnp.exp(ldr * (tpos + 1.0)))
                k_mul.append(jnp.exp(ldr * (C - 1.0 - tpos)))
                dist = (ri - ci).astype(F32)
            else:
                q_mul.append(jnp.exp(ldr * (C - tpos)))
                k_mul.append(jnp.exp(ldr * tpos))
                dist = (ci - ri).astype(F32)
            ret_decay.append(jnp.exp(ldr * float(C)))
            score_mul.append([jnp.where(tri[d], jnp.exp(ldh * dist), 0.0) for ldh in (ld0, ld1)])

    zero = jnp.zeros((), BF16)
    groups = [(s, d) for s in range(seq_blk) for d in range(2)]

    def run_block(n0, accumulate):
        chains = {}
        for s, d in groups:
            for u in range(unroll):
                n = n0 + u
                c = n if d == 0 else n_chunks - 1 - n
                t = dict(c=c, rows=pl.ds(pl.multiple_of(c * C, C), C))
                t["q"] = q_ref[s, t["rows"], :]
                t["k"] = k_ref[s, t["rows"], :]
                if is_gla:
                    lg = (lgf_ref if d == 0 else lgb_ref)[s, t["rows"], :]
                    hi = lg.astype(BF16)
                    lo = (lg - hi.astype(F32)).astype(BF16)
                    bb = _dot(tri_ones[d], jnp.concatenate([hi, lo], axis=1))
                    t["b"] = bb[:, :LANES] + bb[:, LANES:]
                chains[s, d, u] = t

        for (s, d, u), t in chains.items():
            q, k = t["q"], t["k"]
            if is_gla:
                b = t["b"]
                r = b[mid - 1:mid] if d == 0 else b[mid:mid + 1]
                bl = b[C - 1:C] if d == 0 else b[0:1]
                qs = q.astype(F32) * jnp.exp(b - r)
                ks = k.astype(F32) * jnp.exp(r - b)
                q_sc = qs.astype(BF16)
                k_sc = ks.astype(BF16)
                q_dec = (qs * jnp.exp(r)).astype(BF16)
                t["k_dec"] = (ks * jnp.exp(bl - r)).astype(BF16)
                t["decay"] = jnp.exp(bl)
            else:
                q_sc = q
                k_sc = k
                q_dec = (q.astype(F32) * q_mul[d]).astype(BF16)
                t["k_dec"] = (k.astype(F32) * k_mul[d]).astype(BF16)
                t["decay"] = ret_decay[d]
            t["q_dec"] = [jnp.where(head_mask[hh], q_dec, zero) for hh in range(2)]
            q_heads = jnp.concatenate([jnp.where(head_mask[hh], q_sc, zero) for hh in range(2)], axis=0)
            t["sc"] = _dot_nt(q_heads, k_sc)

        for (s, d, u), t in chains.items():
            t["vt"] = vt_ref[s, t["c"]]
            t["kv"] = _dot(t["vt"], t["k_dec"])

        for (s, d, u), t in chains.items():
            t["scb"] = []
            for hh in range(2):
                sc = t["sc"][hh * C:(hh + 1) * C]
                if is_gla:
                    sc = jnp.where(tri[d], sc, 0.0)
                else:
                    sc = sc * score_mul[d][hh]
                t["scb"].append(sc.astype(BF16))

        state = {(s, d): st_ref[s, d] for s, d in groups}
        for u in range(unroll):
            for s, d in groups:
                t = chains[s, d, u]
                st = state[s, d]
                stb = st.astype(BF16)
                t["o"] = []
                for hh in range(2):
                    hrows = slice(hh * DV, (hh + 1) * DV)
                    lhs = jnp.concatenate([t["scb"][hh], t["q_dec"][hh]], axis=1)
                    rhs = jnp.concatenate([t["vt"][hrows], stb[hrows]], axis=1)
                    t["o"].append(_dot_nt(lhs, rhs))
                state[s, d] = st * t["decay"] + t["kv"]
        for key, st in state.items():
            st_ref[key] = st

        for (s, d, u), t in chains.items():
            for hh in range(2):
                cols = slice(hh * DV, (hh + 1) * DV)
                if accumulate:
                    o_ref[s, t["rows"], cols] = o_ref[s, t["rows"], cols] + t["o"][hh]
                else:
                    o_ref[s, t["rows"], cols] = t["o"][hh]

    def loop_body(accumulate, base):
        def body(m, carry):
            run_block(base + m * unroll, accumulate)
            return carry
        return body

    half = n_chunks // 2
    assert half % unroll == 0
    lax.fori_loop(0, half // unroll, loop_body(False, 0), 0)
    lax.fori_loop(0, half // unroll, loop_body(True, half), 0)

    if want_state:
        for s in range(seq_blk):
            for d in range(2):
                for hh in range(2):
                    sn_ref[s, 0, d, hh] = st_ref[s, d, hh * DV:(hh + 1) * DV].T[hh * DK:(hh + 1) * DK, :]


def _scan(is_gla, qk, vt, lg, ld, s0, want_state, seq_blk, unroll):
    B, T, _ = qk.shape
    n_chunks = T // SCAN_CHUNK
    assert n_chunks % 2 == 0 and B % seq_blk == 0
    sb = seq_blk
    q_blk = 4 if is_gla else 0
    k_blk = q_blk + 2
    v_blk = 2 if is_gla else 0
    in_specs = [
        pl.BlockSpec((sb, T, LANES), lambda b, p: (b, 0, q_blk + p)),
        pl.BlockSpec((sb, T, LANES), lambda b, p: (b, 0, k_blk + p)),
        pl.BlockSpec((sb, n_chunks, 2 * DV, SCAN_CHUNK), lambda b, p: (b, 0, v_blk + p, 0)),
    ]
    args = [qk, qk, vt]
    if is_gla:
        in_specs += [
            pl.BlockSpec((sb, T, LANES), lambda b, p: (b, 0, p)),
            pl.BlockSpec((sb, T, LANES), lambda b, p: (b, 0, 2 + p)),
        ]
        args += [lg, lg]
    else:
        in_specs += [pl.BlockSpec(memory_space=pltpu.SMEM)]
        args += [ld]
    state_spec = pl.BlockSpec((sb, 1, 2, 2, DK, DV), lambda b, p: (b, 0, 0, p, 0, 0))
    if s0 is not None:
        in_specs += [state_spec]
        args += [s0]
    out_shape = [jax.ShapeDtypeStruct((B, T, WV), F32)]
    out_specs = [pl.BlockSpec((sb, T, 2 * DV), lambda b, p: (b, 0, p))]
    if want_state:
        out_shape += [jax.ShapeDtypeStruct((B, 1, 2, N_HEADS, DK, DV), F32)]
        out_specs += [state_spec]
    name = ("gla" if is_gla else "ret") + ("_init" if s0 is not None else "") + "_scan"
    return pl.pallas_call(
        functools.partial(_scan_kernel, is_gla, s0 is not None, want_state, n_chunks, sb, unroll),
        grid=(B // sb, 2),
        in_specs=in_specs,
        out_specs=out_specs,
        out_shape=out_shape,
        scratch_shapes=[pltpu.VMEM((sb, 2, 2 * DV, LANES), F32)],
        compiler_params=pltpu.CompilerParams(
            dimension_semantics=("arbitrary", "arbitrary"), vmem_limit_bytes=VMEM_LIMIT),
        name=name,
    )(*args)


def _outproj_kernel(x_ref, mod_ref, or_ref, og_ref, zr_ref, zg_ref, gw_ref, fw_ref, wout_ref, y_ref):
    gate = mod_ref[0, :, 2 * D_MODEL:3 * D_MODEL]
    gw = gw_ref[...]
    parts = []
    for h in range(N_HEADS):
        cols = slice(h * DV, (h + 1) * DV)
        t = or_ref[0, :, cols]
        mu = jnp.mean(t, axis=-1, keepdims=True)
        dlt = t - mu
        var = jnp.mean(dlt * dlt, axis=-1, keepdims=True)
        n = dlt * lax.rsqrt(var + EPS)
        parts.append((n * _silu(zr_ref[0, :, cols].astype(F32))).astype(BF16))
    for h in range(N_HEADS):
        cols = slice(h * DV, (h + 1) * DV)
        t = og_ref[0, :, cols]
        n = t * lax.rsqrt(jnp.mean(t * t, axis=-1, keepdims=True) + EPS) * gw
        parts.append((n * _silu(zg_ref[0, :, cols].astype(F32))).astype(BF16))
    mixed = jnp.concatenate(parts, axis=-1)
    out = _dot(mixed, wout_ref[...])
    y = x_ref[0] + gate * out
    yn = y * lax.rsqrt(jnp.mean(y * y, axis=-1, keepdims=True) + EPS)
    y_ref[0] = yn * fw_ref[...]


def _outproj(x, mod, mod_row_fn, o_r, o_g, z, gw, fw, wout):
    B, T, _ = x.shape
    nt = T // TOKEN_BLOCK
    const = lambda b, t: (0, 0)
    return pl.pallas_call(
        _outproj_kernel,
        grid=(B, nt),
        in_specs=[
            pl.BlockSpec((1, TOKEN_BLOCK, D_MODEL), lambda b, t: (b, t, 0)),
            pl.BlockSpec((1, 1, 3 * D_MODEL), lambda b, t: (mod_row_fn(b), 0, 0)),
            pl.BlockSpec((1, TOKEN_BLOCK, WV), lambda b, t: (b, t, 0)),
            pl.BlockSpec((1, TOKEN_BLOCK, WV), lambda b, t: (b, t, 0)),
            pl.BlockSpec((1, TOKEN_BLOCK, WV), lambda b, t: (b, t, 0)),
            pl.BlockSpec((1, TOKEN_BLOCK, WV), lambda b, t: (b, t, 1)),
            pl.BlockSpec((1, DV), const),
            pl.BlockSpec((1, D_MODEL), const),
            pl.BlockSpec(wout.shape, const),
        ],
        out_specs=pl.BlockSpec((1, TOKEN_BLOCK, D_MODEL), lambda b, t: (b, t, 0)),
        out_shape=jax.ShapeDtypeStruct((B, T, D_MODEL), F32),
        compiler_params=pltpu.CompilerParams(
            dimension_semantics=("parallel", "parallel"), vmem_limit_bytes=VMEM_LIMIT),
        name="outproj",
    )(x, mod, o_r, o_g, z, z, gw, fw, wout)


def _rope_tables(n_tokens):
    rows = n_tokens // GRID_W
    rr, cc = jnp.meshgrid(jnp.arange(rows), jnp.arange(GRID_W), indexing="ij")
    rr = rr.reshape(-1).astype(F32)
    cc = cc.reshape(-1).astype(F32)
    n_freq = DK // 4
    inv = ROPE_BASE ** (-jnp.arange(n_freq, dtype=F32) / n_freq)
    ang = jnp.concatenate([rr[:, None] * inv, cc[:, None] * inv], axis=-1)
    cos, sin = jnp.cos(ang), jnp.sin(ang)
    cos_t = jnp.tile(jnp.concatenate([cos, cos], axis=-1), (1, LANES // DK))
    sin_t = jnp.tile(jnp.concatenate([-sin, sin], axis=-1), (1, LANES // DK))
    return cos_t, sin_t


def _prep_weights(w_in, gla_w_alpha, gla_b_alpha):
    q_r, k_r = w_in[:, 0:QK], w_in[:, QK:2 * QK]
    o = 2 * QK
    v_r, z_r = w_in[:, o:o + WV], w_in[:, o + WV:o + 2 * WV]
    o += 2 * WV
    q_g, k_g = w_in[:, o:o + QK], w_in[:, o + QK:o + 2 * QK]
    o += 2 * QK
    v_g, z_g = w_in[:, o:o + WV], w_in[:, o + WV:o + 2 * WV]
    o += 2 * WV
    lr = w_in[:, o:o + 2 * GLA_RANK]
    wqk = jnp.concatenate([q_r, k_r, q_g, k_g], axis=1).astype(BF16)
    wz = jnp.concatenate([z_r, z_g], axis=1).astype(BF16)
    wvt = jnp.concatenate([v_r, v_g], axis=1).T.astype(BF16)
    wlr = jnp.pad(lr, ((0, 0), (0, LR_PAD - 2 * GLA_RANK))).astype(BF16)
    wa = jnp.zeros((LR_PAD, 2 * QK), F32)
    wa = wa.at[0:GLA_RANK, 0:QK].set(gla_w_alpha[0])
    wa = wa.at[GLA_RANK:2 * GLA_RANK, QK:2 * QK].set(gla_w_alpha[1])
    ba = jnp.concatenate([gla_b_alpha[0], gla_b_alpha[1]]).reshape(1, 2 * QK)
    return wqk, wz, wvt, wlr, wa.astype(BF16), ba


def kernel(x_prompt, x_sample, c, state_ret, state_gla, c_ctx, w_mod, b_mod, w_in, ret_log_decay,
           gla_w_alpha, gla_b_alpha, gla_norm_w, w_out, final_norm_w):
    assert w_mod.shape[0] == 1, "single-layer model"
    b_ctx, t_ctx, _ = x_prompt.shape
    b_dec, t_dec, _ = x_sample.shape
    assert t_ctx % SCAN_CHUNK == 0 and t_dec % TOKEN_BLOCK == 0 and TOKEN_BLOCK % t_ctx == 0
    assert 1 + b_dec <= MOD_ROWS

    cond = jnp.concatenate(
        [c_ctx[None, :], c, jnp.zeros((MOD_ROWS - 1 - b_dec, D_MODEL), F32)], axis=0)
    mod = _modulation(cond, w_mod[0], b_mod[0]).reshape(MOD_ROWS, 1, 3 * D_MODEL)

    weights = _prep_weights(w_in[0], gla_w_alpha[0], gla_b_alpha[0])
    wout = w_out[0].astype(BF16)
    gw = gla_norm_w[0].reshape(1, DV)
    fw = final_norm_w.reshape(1, D_MODEL)
    ld = ret_log_decay[0]

    per_blk = TOKEN_BLOCK // t_ctx
    xp = x_prompt.reshape(b_ctx // per_blk, TOKEN_BLOCK, D_MODEL)
    ctx_row = lambda b: 0
    qk, z, vt, lg = _inproj(xp, mod, ctx_row, weights, None)
    qk = qk.reshape(b_ctx, t_ctx, 4 * QK)
    vt = vt.reshape(b_ctx, t_ctx // SCAN_CHUNK, 2 * WV, SCAN_CHUNK)
    lg = lg.reshape(b_ctx, t_ctx, 2 * QK)
    o_r, new_ret = _scan(False, qk, vt, None, ld, None, True, CTX_SEQ_BLOCK, 1)
    o_g, new_gla = _scan(True, qk, vt, lg, None, None, True, CTX_SEQ_BLOCK, 1)
    y_prompt = _outproj(xp, mod, ctx_row, o_r.reshape(xp.shape[0], TOKEN_BLOCK, WV),
                        o_g.reshape(xp.shape[0], TOKEN_BLOCK, WV), z, gw, fw, wout)
    y_prompt = y_prompt.reshape(b_ctx, t_ctx, D_MODEL)

    dec_row = lambda b: b + 1
    rope = _rope_tables(t_dec)
    qk, z, vt, lg = _inproj(x_sample, mod, dec_row, weights, rope)
    (o_r,) = _scan(False, qk, vt, None, ld, state_ret, False, 1, DEC_UNROLL)
    (o_g,) = _scan(True, qk, vt, lg, None, state_gla, False, 1, DEC_UNROLL)
    y_sample = _outproj(x_sample, mod, dec_row, o_r, o_g, z, gw, fw, wout)

    return (y_prompt, y_sample, new_ret, new_gla)
```

```python
import functools

import jax
import jax.numpy as jnp
from jax import lax
from jax.experimental import pallas as pl
from jax.experimental.pallas import tpu as pltpu

F32 = jnp.float32
BF16 = jnp.bfloat16

D_MODEL = 1024
N_HEADS = 4
DK = 64
DV = 128
QK = N_HEADS * DK
WV = N_HEADS * DV
GLA_RANK = 16
GLA_TAU = 16.0
EPS = 1e-6
GRID_W = 64
ROPE_BASE = 10000.0

LANES = 128
SCAN_CHUNK = 128
CTX_SEQ_BLOCK = 4
DEC_UNROLL = 4
TOKEN_BLOCK = 512
INPROJ_SUB = 256
OUTPROJ_SUB = 256
MOD_ROWS = 16
MOD_COL_BLOCK = 512
LR_PAD = LANES
VMEM_LIMIT = 48 * 1024 * 1024

_NT = (((1,), (1,)), ((), ()))
_TN = (((0,), (0,)), ((), ()))


def _dot(a, b):
    return jnp.dot(a, b, preferred_element_type=F32)


def _dot_nt(a, b):
    return lax.dot_general(a, b, _NT, preferred_element_type=F32)


def _silu(x):
    return x * jax.nn.sigmoid(x)


def _mod_kernel(c_ref, w_ref, b_ref, o_ref):
    s = _silu(c_ref[...]).astype(BF16)
    o_ref[...] = _dot(s, w_ref[...].astype(BF16)) + b_ref[...]


def _modulation(cond, w_mod, b_mod):
    n_col = w_mod.shape[1]
    return pl.pallas_call(
        _mod_kernel,
        grid=(n_col // MOD_COL_BLOCK,),
        in_specs=[
            pl.BlockSpec((MOD_ROWS, D_MODEL), lambda j: (0, 0)),
            pl.BlockSpec((D_MODEL, MOD_COL_BLOCK), lambda j: (0, j)),
            pl.BlockSpec((1, MOD_COL_BLOCK), lambda j: (0, j)),
        ],
        out_specs=pl.BlockSpec((MOD_ROWS, MOD_COL_BLOCK), lambda j: (0, j)),
        out_shape=jax.ShapeDtypeStruct((MOD_ROWS, n_col), F32),
        compiler_params=pltpu.CompilerParams(vmem_limit_bytes=VMEM_LIMIT),
        name="modulation",
    )(cond, w_mod, b_mod.reshape(1, n_col))


def _rope_tile(t, cos, sin_signed):
    lane = lax.broadcasted_iota(jnp.int32, t.shape, 1)
    first_half = (lane & 32) == 0
    swapped = jnp.where(first_half, pltpu.roll(t, LANES - 32, 1), pltpu.roll(t, 32, 1))
    return t * cos + swapped * sin_signed


def _inproj_kernel(use_rope, x_ref, mod_ref, wqk_ref, wz_ref, wvt_ref, wlr_ref, wa_ref, ba_ref,
                   *rest):
    if use_rope:
        cos_ref, sin_ref, qk_ref, z_ref, vt_ref, lg_ref = rest
    else:
        qk_ref, z_ref, vt_ref, lg_ref = rest
    shift = mod_ref[0, :, 0:D_MODEL]
    scale1 = 1.0 + mod_ref[0, :, D_MODEL:2 * D_MODEL]

    for sub in range(TOKEN_BLOCK // INPROJ_SUB):
        rows = slice(sub * INPROJ_SUB, (sub + 1) * INPROJ_SUB)
        x = x_ref[0, rows, :]
        inv = lax.rsqrt(jnp.mean(x * x, axis=-1, keepdims=True) + EPS)
        hb = ((x * inv) * scale1 + shift).astype(BF16)

        lr_t = _dot_nt(wlr_ref[...], hb).astype(BF16)

        for half in range(2):
            res = _dot(hb, wqk_ref[:, half * 2 * QK:(half + 1) * 2 * QK])
            for t in range(4):
                tile = res[:, t * LANES:(t + 1) * LANES]
                is_q = t < 2
                if (half == 0 and not is_q) or (half == 1 and is_q):
                    tile = tile * (DK ** -0.5)
                if use_rope and half == 0:
                    tile = _rope_tile(tile, cos_ref[rows, :], sin_ref[rows, :])
                col = half * 2 * QK + t * LANES
                qk_ref[0, rows, col:col + LANES] = tile.astype(BF16)

        logit = lax.dot_general(lr_t, wa_ref[...], _TN,
                                preferred_element_type=F32) + ba_ref[...]
        log_sig = jnp.minimum(logit, 0.0) - jnp.log1p(jnp.exp(-jnp.abs(logit)))
        lg_ref[0, rows, :] = log_sig * (1.0 / GLA_TAU)

        for half in range(2):
            z_ref[0, rows, half * WV:(half + 1) * WV] = _silu(
                _dot(hb, wz_ref[:, half * WV:(half + 1) * WV])).astype(BF16)

        for half in range(2):
            vt = _dot_nt(wvt_ref[half * WV:(half + 1) * WV, :], hb)
            for j in range(INPROJ_SUB // SCAN_CHUNK):
                jj = sub * (INPROJ_SUB // SCAN_CHUNK) + j
                vt_ref[0, jj, half * WV:(half + 1) * WV, :] = vt[:, j * SCAN_CHUNK:(j + 1) * SCAN_CHUNK].astype(BF16)


def _inproj(x, mod, mod_row_fn, weights, rope):
    B, T, _ = x.shape
    wqk, wz, wvt, wlr, wa, ba = weights
    nt = T // TOKEN_BLOCK
    cpb = TOKEN_BLOCK // SCAN_CHUNK
    const = lambda b, t: (0, 0)
    in_specs = [
        pl.BlockSpec((1, TOKEN_BLOCK, D_MODEL), lambda b, t: (b, t, 0)),
        pl.BlockSpec((1, 1, 3 * D_MODEL), lambda b, t: (mod_row_fn(b), 0, 0)),
        pl.BlockSpec(wqk.shape, const),
        pl.BlockSpec(wz.shape, const),
        pl.BlockSpec(wvt.shape, const),
        pl.BlockSpec(wlr.shape, const),
        pl.BlockSpec(wa.shape, const),
        pl.BlockSpec(ba.shape, const),
    ]
    args = [x, mod, wqk, wz, wvt, wlr, wa, ba]
    if rope is not None:
        in_specs += [pl.BlockSpec((TOKEN_BLOCK, LANES), lambda b, t: (t, 0))] * 2
        args += list(rope)
    out_shape = (
        jax.ShapeDtypeStruct((B, T, 4 * QK), BF16),
        jax.ShapeDtypeStruct((B, T, 2 * WV), BF16),
        jax.ShapeDtypeStruct((B, T // SCAN_CHUNK, 2 * WV, SCAN_CHUNK), BF16),
        jax.ShapeDtypeStruct((B, T, 2 * QK), F32),
    )
    out_specs = (
        pl.BlockSpec((1, TOKEN_BLOCK, 4 * QK), lambda b, t: (b, t, 0)),
        pl.BlockSpec((1, TOKEN_BLOCK, 2 * WV), lambda b, t: (b, t, 0)),
        pl.BlockSpec((1, cpb, 2 * WV, SCAN_CHUNK), lambda b, t: (b, t, 0, 0)),
        pl.BlockSpec((1, TOKEN_BLOCK, 2 * QK), lambda b, t: (b, t, 0)),
    )
    return pl.pallas_call(
        functools.partial(_inproj_kernel, rope is not None),
        grid=(B, nt),
        in_specs=in_specs,
        out_specs=out_specs,
        out_shape=out_shape,
        compiler_params=pltpu.CompilerParams(
            dimension_semantics=("parallel", "parallel"), vmem_limit_bytes=VMEM_LIMIT),
        name="inproj_rope" if rope is not None else "inproj",
    )(*args)


def _scan_kernel(is_gla, has_init, want_state, n_chunks, seq_blk, unroll, *refs):
    refs = list(refs)
    q_ref, k_ref, vt_ref = refs[:3]
    pos = 3
    if is_gla:
        lgf_ref, lgb_ref = refs[pos:pos + 2]
        pos += 2
    else:
        ld_ref = refs[pos]
        pos += 1
    if has_init:
        s0_ref = refs[pos]
        pos += 1
    o_ref = refs[pos]
    pos += 1
    if want_state:
        sn_ref = refs[pos]
        pos += 1
    st_ref, oacc_ref = refs[pos:pos + 2]

    C = SCAN_CHUNK
    mid = C // 2
    pair = pl.program_id(1)
    lane = lax.broadcasted_iota(jnp.int32, (1, LANES), 1)
    head_mask = (lane < DK, lane >= DK)
    ri = lax.broadcasted_iota(jnp.int32, (C, C), 0)
    ci = lax.broadcasted_iota(jnp.int32, (C, C), 1)
    tpos = lax.broadcasted_iota(jnp.int32, (C, 1), 0).astype(F32)
    tri = ((ri >= ci), (ci >= ri))

    if has_init:
        for s in range(seq_blk):
            for d in range(2):
                s0 = s0_ref[s, 0, d]
                both = jnp.concatenate([s0[0], s0[1]], axis=0).T
                st_ref[s, d, 0:DV] = both
                st_ref[s, d, DV:2 * DV] = both
    else:
        st_ref[...] = jnp.zeros(st_ref.shape, F32)

    if is_gla:
        tri_ones = [t.astype(BF16) for t in tri]
    else:
        q_mul, k_mul, ret_decay, score_mul = [], [], [], []
        for d in range(2):
            ld0 = ld_ref[d, 2 * pair]
            ld1 = ld_ref[d, 2 * pair + 1]
            ldr = jnp.where(lane < DK, ld0, ld1)
            if d == 0:
                q_mul.append(jnp.exp(ldr * (tpos + 1.0)))
                k_mul.append(jnp.exp(ldr * (C - 1.0 - tpos)))
                dist = (ri - ci).astype(F32)
            else:
                q_mul.append(jnp.exp(ldr * (C - tpos)))
                k_mul.append(jnp.exp(ldr * tpos))
                dist = (ci - ri).astype(F32)
            ret_decay.append(jnp.exp(ldr * float(C)))
            score_mul.append([jnp.where(tri[d], jnp.exp(ldh * dist), 0.0) for ldh in (ld0, ld1)])

    zero = jnp.zeros((), BF16)
    groups = [(s, d) for s in range(seq_blk) for d in range(2)]

    def run_block(n0, accumulate):
        chains = {}
        for s, d in groups:
            for u in range(unroll):
                n = n0 + u
                c = n if d == 0 else n_chunks - 1 - n
                t = dict(c=c, rows=pl.ds(pl.multiple_of(c * C, C), C))
                t["q"] = q_ref[s, t["rows"], :]
                t["k"] = k_ref[s, t["rows"], :]
                if is_gla:
                    lg = (lgf_ref if d == 0 else lgb_ref)[s, t["rows"], :]
                    hi = lg.astype(BF16)
                    lo = (lg - hi.astype(F32)).astype(BF16)
                    bb = _dot(tri_ones[d], jnp.concatenate([hi, lo], axis=1))
                    t["b"] = bb[:, :LANES] + bb[:, LANES:]
                chains[s, d, u] = t

        for (s, d, u), t in chains.items():
            q, k = t["q"], t["k"]
            if is_gla:
                b = t["b"]
                r = b[mid - 1:mid] if d == 0 else b[mid:mid + 1]
                bl = b[C - 1:C] if d == 0 else b[0:1]
                qs = q.astype(F32) * jnp.exp(b - r)
                ks = k.astype(F32) * jnp.exp(r - b)
                q_sc = qs.astype(BF16)
                k_sc = ks.astype(BF16)
                q_dec = (qs * jnp.exp(r)).astype(BF16)
                t["k_dec"] = (ks * jnp.exp(bl - r)).astype(BF16)
                t["decay"] = jnp.exp(bl)
            else:
                q_sc = q
                k_sc = k
                q_dec = (q.astype(F32) * q_mul[d]).astype(BF16)
                t["k_dec"] = (k.astype(F32) * k_mul[d]).astype(BF16)
                t["decay"] = ret_decay[d]
            t["q_dec"] = [jnp.where(head_mask[hh], q_dec, zero) for hh in range(2)]
            q_heads = jnp.concatenate([jnp.where(head_mask[hh], q_sc, zero) for hh in range(2)], axis=0)
            t["sc"] = _dot_nt(q_heads, k_sc)

        for (s, d, u), t in chains.items():
            t["vt"] = vt_ref[s, t["c"]]
            t["kv"] = _dot(t["vt"], t["k_dec"])

        for (s, d, u), t in chains.items():
            t["scb"] = []
            for hh in range(2):
                sc = t["sc"][hh * C:(hh + 1) * C]
                if is_gla:
                    sc = jnp.where(tri[d], sc, 0.0)
                else:
                    sc = sc * score_mul[d][hh]
                t["scb"].append(sc.astype(BF16))

        state = {(s, d): st_ref[s, d] for s, d in groups}
        for u in range(unroll):
            for s, d in groups:
                t = chains[s, d, u]
                st = state[s, d]
                stb = st.astype(BF16)
                t["o"] = []
                for hh in range(2):
                    hrows = slice(hh * DV, (hh + 1) * DV)
                    lhs = jnp.concatenate([t["scb"][hh], t["q_dec"][hh]], axis=1)
                    rhs = jnp.concatenate([t["vt"][hrows], stb[hrows]], axis=1)
                    t["o"].append(_dot_nt(lhs, rhs))
                state[s, d] = st * t["decay"] + t["kv"]
        for key, st in state.items():
            st_ref[key] = st

        for (s, d, u), t in chains.items():
            for hh in range(2):
                cols = slice(hh * DV, (hh + 1) * DV)
                if accumulate:
                    o_ref[s, t["rows"], cols] = (oacc_ref[s, t["rows"], cols] + t["o"][hh]).astype(BF16)
                else:
                    oacc_ref[s, t["rows"], cols] = t["o"][hh]

    def loop_body(accumulate, base):
        def body(m, carry):
            run_block(base + m * unroll, accumulate)
            return carry
        return body

    half = n_chunks // 2
    assert half % unroll == 0
    lax.fori_loop(0, half // unroll, loop_body(False, 0), 0)
    lax.fori_loop(0, half // unroll, loop_body(True, half), 0)

    if want_state:
        for s in range(seq_blk):
            for d in range(2):
                for hh in range(2):
                    sn_ref[s, 0, d, hh] = st_ref[s, d, hh * DV:(hh + 1) * DV].T[hh * DK:(hh + 1) * DK, :]


def _scan(is_gla, qk, vt, lg, ld, s0, want_state, seq_blk, unroll):
    B, T, _ = qk.shape
    n_chunks = T // SCAN_CHUNK
    assert n_chunks % 2 == 0 and B % seq_blk == 0
    sb = seq_blk
    q_blk = 4 if is_gla else 0
    k_blk = q_blk + 2
    v_blk = 2 if is_gla else 0
    in_specs = [
        pl.BlockSpec((sb, T, LANES), lambda b, p: (b, 0, q_blk + p)),
        pl.BlockSpec((sb, T, LANES), lambda b, p: (b, 0, k_blk + p)),
        pl.BlockSpec((sb, n_chunks, 2 * DV, SCAN_CHUNK), lambda b, p: (b, 0, v_blk + p, 0)),
    ]
    args = [qk, qk, vt]
    if is_gla:
        in_specs += [
            pl.BlockSpec((sb, T, LANES), lambda b, p: (b, 0, p)),
            pl.BlockSpec((sb, T, LANES), lambda b, p: (b, 0, 2 + p)),
        ]
        args += [lg, lg]
    else:
        in_specs += [pl.BlockSpec(memory_space=pltpu.SMEM)]
        args += [ld]
    state_spec = pl.BlockSpec((sb, 1, 2, 2, DK, DV), lambda b, p: (b, 0, 0, p, 0, 0))
    if s0 is not None:
        in_specs += [state_spec]
        args += [s0]
    out_shape = [jax.ShapeDtypeStruct((B, T, WV), BF16)]
    out_specs = [pl.BlockSpec((sb, T, 2 * DV), lambda b, p: (b, 0, p))]
    if want_state:
        out_shape += [jax.ShapeDtypeStruct((B, 1, 2, N_HEADS, DK, DV), F32)]
        out_specs += [state_spec]
    name = ("gla" if is_gla else "ret") + ("_init" if s0 is not None else "") + "_scan"
    return pl.pallas_call(
        functools.partial(_scan_kernel, is_gla, s0 is not None, want_state, n_chunks, sb, unroll),
        grid=(B // sb, 2),
        in_specs=in_specs,
        out_specs=out_specs,
        out_shape=out_shape,
        scratch_shapes=[pltpu.VMEM((sb, 2, 2 * DV, LANES), F32),
                        pltpu.VMEM((sb, T, 2 * DV), F32)],
        compiler_params=pltpu.CompilerParams(
            dimension_semantics=("arbitrary", "arbitrary"), vmem_limit_bytes=VMEM_LIMIT),
        name=name,
    )(*args)


def _outproj_kernel(x_ref, mod_ref, or_ref, og_ref, zr_ref, zg_ref, gw_ref, fw_ref, wout_ref, y_ref):
    gate = mod_ref[0, :, 2 * D_MODEL:3 * D_MODEL]
    gw = gw_ref[...]
    fw = fw_ref[...]
    for sub in range(TOKEN_BLOCK // OUTPROJ_SUB):
        rows = slice(sub * OUTPROJ_SUB, (sub + 1) * OUTPROJ_SUB)
        parts = []
        for h in range(N_HEADS):
            cols = slice(h * DV, (h + 1) * DV)
            t = or_ref[0, rows, cols].astype(F32)
            mu = jnp.mean(t, axis=-1, keepdims=True)
            dlt = t - mu
            var = jnp.mean(dlt * dlt, axis=-1, keepdims=True)
            n = dlt * lax.rsqrt(var + EPS)
            parts.append((n * zr_ref[0, rows, cols].astype(F32)).astype(BF16))
        for h in range(N_HEADS):
            cols = slice(h * DV, (h + 1) * DV)
            t = og_ref[0, rows, cols].astype(F32)
            n = t * lax.rsqrt(jnp.mean(t * t, axis=-1, keepdims=True) + EPS) * gw
            parts.append((n * zg_ref[0, rows, cols].astype(F32)).astype(BF16))
        mixed = jnp.concatenate(parts, axis=-1)
        out = _dot(mixed, wout_ref[...])
        y = x_ref[0, rows, :] + gate * out
        yn = y * lax.rsqrt(jnp.mean(y * y, axis=-1, keepdims=True) + EPS)
        y_ref[0, rows, :] = yn * fw


def _outproj(x, mod, mod_row_fn, o_r, o_g, z, gw, fw, wout):
    B, T, _ = x.shape
    nt = T // TOKEN_BLOCK
    const = lambda b, t: (0, 0)
    return pl.pallas_call(
        _outproj_kernel,
        grid=(B, nt),
        in_specs=[
            pl.BlockSpec((1, TOKEN_BLOCK, D_MODEL), lambda b, t: (b, t, 0)),
            pl.BlockSpec((1, 1, 3 * D_MODEL), lambda b, t: (mod_row_fn(b), 0, 0)),
            pl.BlockSpec((1, TOKEN_BLOCK, WV), lambda b, t: (b, t, 0)),
            pl.BlockSpec((1, TOKEN_BLOCK, WV), lambda b, t: (b, t, 0)),
            pl.BlockSpec((1, TOKEN_BLOCK, WV), lambda b, t: (b, t, 0)),
            pl.BlockSpec((1, TOKEN_BLOCK, WV), lambda b, t: (b, t, 1)),
            pl.BlockSpec((1, DV), const),
            pl.BlockSpec((1, D_MODEL), const),
            pl.BlockSpec(wout.shape, const),
        ],
        out_specs=pl.BlockSpec((1, TOKEN_BLOCK, D_MODEL), lambda b, t: (b, t, 0)),
        out_shape=jax.ShapeDtypeStruct((B, T, D_MODEL), F32),
        compiler_params=pltpu.CompilerParams(
            dimension_semantics=("parallel", "parallel"), vmem_limit_bytes=VMEM_LIMIT),
        name="outproj",
    )(x, mod, o_r, o_g, z, z, gw, fw, wout)


def _rope_tables(n_tokens):
    rows = n_tokens // GRID_W
    rr, cc = jnp.meshgrid(jnp.arange(rows), jnp.arange(GRID_W), indexing="ij")
    rr = rr.reshape(-1).astype(F32)
    cc = cc.reshape(-1).astype(F32)
    n_freq = DK // 4
    inv = ROPE_BASE ** (-jnp.arange(n_freq, dtype=F32) / n_freq)
    ang = jnp.concatenate([rr[:, None] * inv, cc[:, None] * inv], axis=-1)
    cos, sin = jnp.cos(ang), jnp.sin(ang)
    cos_t = jnp.tile(jnp.concatenate([cos, cos], axis=-1), (1, LANES // DK))
    sin_t = jnp.tile(jnp.concatenate([-sin, sin], axis=-1), (1, LANES // DK))
    return cos_t, sin_t


def _prep_weights(w_in, gla_w_alpha, gla_b_alpha):
    q_r, k_r = w_in[:, 0:QK], w_in[:, QK:2 * QK]
    o = 2 * QK
    v_r, z_r = w_in[:, o:o + WV], w_in[:, o + WV:o + 2 * WV]
    o += 2 * WV
    q_g, k_g = w_in[:, o:o + QK], w_in[:, o + QK:o + 2 * QK]
    o += 2 * QK
    v_g, z_g = w_in[:, o:o + WV], w_in[:, o + WV:o + 2 * WV]
    o += 2 * WV
    lr = w_in[:, o:o + 2 * GLA_RANK]
    wqk = jnp.concatenate([q_r, k_r, q_g, k_g], axis=1).astype(BF16)
    wz = jnp.concatenate([z_r, z_g], axis=1).astype(BF16)
    wvt = jnp.concatenate([v_r, v_g], axis=1).T.astype(BF16)
    wlr = lr.T.astype(BF16)
    wa = jnp.zeros((2 * GLA_RANK, 2 * QK), F32)
    wa = wa.at[0:GLA_RANK, 0:QK].set(gla_w_alpha[0])
    wa = wa.at[GLA_RANK:2 * GLA_RANK, QK:2 * QK].set(gla_w_alpha[1])
    ba = jnp.concatenate([gla_b_alpha[0], gla_b_alpha[1]]).reshape(1, 2 * QK)
    return wqk, wz, wvt, wlr, wa.astype(BF16), ba


def kernel(x_prompt, x_sample, c, state_ret, state_gla, c_ctx, w_mod, b_mod, w_in, ret_log_decay,
           gla_w_alpha, gla_b_alpha, gla_norm_w, w_out, final_norm_w):
    assert w_mod.shape[0] == 1, "single-layer model"
    b_ctx, t_ctx, _ = x_prompt.shape
    b_dec, t_dec, _ = x_sample.shape
    assert t_ctx % SCAN_CHUNK == 0 and t_dec % TOKEN_BLOCK == 0 and TOKEN_BLOCK % t_ctx == 0
    assert 1 + b_dec <= MOD_ROWS

    cond = jnp.concatenate(
        [c_ctx[None, :], c, jnp.zeros((MOD_ROWS - 1 - b_dec, D_MODEL), F32)], axis=0)
    mod = _modulation(cond, w_mod[0], b_mod[0]).reshape(MOD_ROWS, 1, 3 * D_MODEL)

    weights = _prep_weights(w_in[0], gla_w_alpha[0], gla_b_alpha[0])
    wout = w_out[0].astype(BF16)
    gw = gla_norm_w[0].reshape(1, DV)
    fw = final_norm_w.reshape(1, D_MODEL)
    ld = ret_log_decay[0]

    per_blk = TOKEN_BLOCK // t_ctx
    xp = x_prompt.reshape(b_ctx // per_blk, TOKEN_BLOCK, D_MODEL)
    ctx_row = lambda b: 0
    qk, z, vt, lg = _inproj(xp, mod, ctx_row, weights, None)
    qk = qk.reshape(b_ctx, t_ctx, 4 * QK)
    vt = vt.reshape(b_ctx, t_ctx // SCAN_CHUNK, 2 * WV, SCAN_CHUNK)
    lg = lg.reshape(b_ctx, t_ctx, 2 * QK)
    o_r, new_ret = _scan(False, qk, vt, None, ld, None, True, CTX_SEQ_BLOCK, 1)
    o_g, new_gla = _scan(True, qk, vt, lg, None, None, True, CTX_SEQ_BLOCK, 1)
    y_prompt = _outproj(xp, mod, ctx_row, o_r.reshape(xp.shape[0], TOKEN_BLOCK, WV),
                        o_g.reshape(xp.shape[0], TOKEN_BLOCK, WV), z, gw, fw, wout)
    y_prompt = y_prompt.reshape(b_ctx, t_ctx, D_MODEL)

    dec_row = lambda b: b + 1
    rope = _rope_tables(t_dec)
    qk, z, vt, lg = _inproj(x_sample, mod, dec_row, weights, rope)
    (o_r,) = _scan(False, qk, vt, None, ld, state_ret, False, 1, DEC_UNROLL)
    (o_g,) = _scan(True, qk, vt, lg, None, state_gla, False, 1, DEC_UNROLL)
    y_sample = _outproj(x_sample, mod, dec_row, o_r, o_g, z, gw, fw, wout)

    return (y_prompt, y_sample, new_ret, new_gla)
```

```python
import functools

import jax
import jax.numpy as jnp
from jax import lax
from jax.experimental import pallas as pl
from jax.experimental.pallas import tpu as pltpu

F32 = jnp.float32
BF16 = jnp.bfloat16

D_MODEL = 1024
N_HEADS = 4
DK = 64
DV = 128
QK = N_HEADS * DK
WV = N_HEADS * DV
GLA_RANK = 16
GLA_TAU = 16.0
EPS = 1e-6
GRID_W = 64
ROPE_BASE = 10000.0

LANES = 128
SCAN_CHUNK = 128
CTX_SEQ_BLOCK = 4
DEC_UNROLL = 4
TOKEN_BLOCK = 512
INPROJ_SUB = 256
OUTPROJ_SUB = 256
MOD_ROWS = 16
MOD_COL_BLOCK = 512
LR_PAD = LANES
VMEM_LIMIT = 48 * 1024 * 1024

_NT = (((1,), (1,)), ((), ()))
_TN = (((0,), (0,)), ((), ()))


def _dot(a, b):
    return jnp.dot(a, b, preferred_element_type=F32)


def _dot_nt(a, b):
    return lax.dot_general(a, b, _NT, preferred_element_type=F32)


def _silu(x):
    return x * jax.nn.sigmoid(x)


def _mod_kernel(c_ref, w_ref, b_ref, o_ref):
    s = _silu(c_ref[...]).astype(BF16)
    o_ref[...] = _dot(s, w_ref[...].astype(BF16)) + b_ref[...]


def _modulation(cond, w_mod, b_mod):
    n_col = w_mod.shape[1]
    return pl.pallas_call(
        _mod_kernel,
        grid=(n_col // MOD_COL_BLOCK,),
        in_specs=[
            pl.BlockSpec((MOD_ROWS, D_MODEL), lambda j: (0, 0)),
            pl.BlockSpec((D_MODEL, MOD_COL_BLOCK), lambda j: (0, j)),
            pl.BlockSpec((1, MOD_COL_BLOCK), lambda j: (0, j)),
        ],
        out_specs=pl.BlockSpec((MOD_ROWS, MOD_COL_BLOCK), lambda j: (0, j)),
        out_shape=jax.ShapeDtypeStruct((MOD_ROWS, n_col), F32),
        compiler_params=pltpu.CompilerParams(vmem_limit_bytes=VMEM_LIMIT),
        name="modulation",
    )(cond, w_mod, b_mod.reshape(1, n_col))


def _rope_tile(t, cos, sin_signed):
    lane = lax.broadcasted_iota(jnp.int32, t.shape, 1)
    first_half = (lane & 32) == 0
    swapped = jnp.where(first_half, pltpu.roll(t, LANES - 32, 1), pltpu.roll(t, 32, 1))
    return t * cos + swapped * sin_signed


def _inproj_kernel(use_rope, x_ref, mod_ref, wqk_ref, wz_ref, wvt_ref, wlr_ref, wa_ref, ba_ref,
                   *rest):
    if use_rope:
        cos_ref, sin_ref, qk_ref, z_ref, vt_ref, lg_ref = rest
    else:
        qk_ref, z_ref, vt_ref, lg_ref = rest
    shift = mod_ref[0, :, 0:D_MODEL]
    scale1 = 1.0 + mod_ref[0, :, D_MODEL:2 * D_MODEL]

    for sub in range(TOKEN_BLOCK // INPROJ_SUB):
        rows = slice(sub * INPROJ_SUB, (sub + 1) * INPROJ_SUB)
        x = x_ref[0, rows, :]
        inv = lax.rsqrt(jnp.mean(x * x, axis=-1, keepdims=True) + EPS)
        hb = ((x * inv) * scale1 + shift).astype(BF16)

        lr_t = _dot_nt(wlr_ref[...], hb).astype(BF16)

        for half in range(2):
            res = _dot(hb, wqk_ref[:, half * 2 * QK:(half + 1) * 2 * QK])
            for t in range(4):
                tile = res[:, t * LANES:(t + 1) * LANES]
                is_q = t < 2
                if (half == 0 and not is_q) or (half == 1 and is_q):
                    tile = tile * (DK ** -0.5)
                if use_rope and half == 0:
                    tile = _rope_tile(tile, cos_ref[rows, :], sin_ref[rows, :])
                col = half * 2 * QK + t * LANES
                qk_ref[0, rows, col:col + LANES] = tile.astype(BF16)

        logit = lax.dot_general(lr_t, wa_ref[...], _TN,
                                preferred_element_type=F32) + ba_ref[...]
        log_sig = jnp.minimum(logit, 0.0) - jnp.log1p(jnp.exp(-jnp.abs(logit)))
        lg_ref[0, rows, :] = log_sig * (1.0 / GLA_TAU)

        for half in range(2):
            z_ref[0, rows, half * WV:(half + 1) * WV] = _silu(
                _dot(hb, wz_ref[:, half * WV:(half + 1) * WV])).astype(BF16)

        for half in range(2):
            vt = _dot_nt(wvt_ref[half * WV:(half + 1) * WV, :], hb)
            for j in range(INPROJ_SUB // SCAN_CHUNK):
                jj = sub * (INPROJ_SUB // SCAN_CHUNK) + j
                vt_ref[0, jj, half * WV:(half + 1) * WV, :] = vt[:, j * SCAN_CHUNK:(j + 1) * SCAN_CHUNK].astype(BF16)


def _inproj(x, mod, mod_row_fn, weights, rope):
    B, T, _ = x.shape
    wqk, wz, wvt, wlr, wa, ba = weights
    nt = T // TOKEN_BLOCK
    cpb = TOKEN_BLOCK // SCAN_CHUNK
    const = lambda b, t: (0, 0)
    in_specs = [
        pl.BlockSpec((1, TOKEN_BLOCK, D_MODEL), lambda b, t: (b, t, 0)),
        pl.BlockSpec((1, 1, 3 * D_MODEL), lambda b, t: (mod_row_fn(b), 0, 0)),
        pl.BlockSpec(wqk.shape, const),
        pl.BlockSpec(wz.shape, const),
        pl.BlockSpec(wvt.shape, const),
        pl.BlockSpec(wlr.shape, const),
        pl.BlockSpec(wa.shape, const),
        pl.BlockSpec(ba.shape, const),
    ]
    args = [x, mod, wqk, wz, wvt, wlr, wa, ba]
    if rope is not None:
        in_specs += [pl.BlockSpec((TOKEN_BLOCK, LANES), lambda b, t: (t, 0))] * 2
        args += list(rope)
    out_shape = (
        jax.ShapeDtypeStruct((B, T, 4 * QK), BF16),
        jax.ShapeDtypeStruct((B, T, 2 * WV), BF16),
        jax.ShapeDtypeStruct((B, T // SCAN_CHUNK, 2 * WV, SCAN_CHUNK), BF16),
        jax.ShapeDtypeStruct((B, T, 2 * QK), F32),
    )
    out_specs = (
        pl.BlockSpec((1, TOKEN_BLOCK, 4 * QK), lambda b, t: (b, t, 0)),
        pl.BlockSpec((1, TOKEN_BLOCK, 2 * WV), lambda b, t: (b, t, 0)),
        pl.BlockSpec((1, cpb, 2 * WV, SCAN_CHUNK), lambda b, t: (b, t, 0, 0)),
        pl.BlockSpec((1, TOKEN_BLOCK, 2 * QK), lambda b, t: (b, t, 0)),
    )
    return pl.pallas_call(
        functools.partial(_inproj_kernel, rope is not None),
        grid=(B, nt),
        in_specs=in_specs,
        out_specs=out_specs,
        out_shape=out_shape,
        compiler_params=pltpu.CompilerParams(
            dimension_semantics=("parallel", "parallel"), vmem_limit_bytes=VMEM_LIMIT),
        name="inproj_rope" if rope is not None else "inproj",
    )(*args)


def _scan_kernel(is_gla, has_init, want_state, n_chunks, seq_blk, unroll, *refs):
    refs = list(refs)
    q_ref, k_ref, vt_ref, g_ref = refs[:4]
    pos = 4
    if is_gla:
        lgf_ref, lgb_ref, gw_ref = refs[pos:pos + 3]
        pos += 3
    else:
        ld_ref = refs[pos]
        pos += 1
    if has_init:
        s0_ref = refs[pos]
        pos += 1
    o_ref = refs[pos]
    pos += 1
    if want_state:
        sn_ref = refs[pos]
        pos += 1
    st_ref, oacc_ref = refs[pos:pos + 2]

    C = SCAN_CHUNK
    mid = C // 2
    pair = pl.program_id(1)
    lane = lax.broadcasted_iota(jnp.int32, (1, LANES), 1)
    head_mask = (lane < DK, lane >= DK)
    ri = lax.broadcasted_iota(jnp.int32, (C, C), 0)
    ci = lax.broadcasted_iota(jnp.int32, (C, C), 1)
    tpos = lax.broadcasted_iota(jnp.int32, (C, 1), 0).astype(F32)
    tri = ((ri >= ci), (ci >= ri))

    if has_init:
        for s in range(seq_blk):
            for d in range(2):
                s0 = s0_ref[s, 0, d]
                both = jnp.concatenate([s0[0], s0[1]], axis=0).T
                st_ref[s, d, 0:DV] = both
                st_ref[s, d, DV:2 * DV] = both
    else:
        st_ref[...] = jnp.zeros(st_ref.shape, F32)

    if is_gla:
        tri_ones = [t.astype(BF16) for t in tri]
    else:
        q_mul, k_mul, ret_decay, score_mul = [], [], [], []
        for d in range(2):
            ld0 = ld_ref[d, 2 * pair]
            ld1 = ld_ref[d, 2 * pair + 1]
            ldr = jnp.where(lane < DK, ld0, ld1)
            if d == 0:
                q_mul.append(jnp.exp(ldr * (tpos + 1.0)))
                k_mul.append(jnp.exp(ldr * (C - 1.0 - tpos)))
                dist = (ri - ci).astype(F32)
            else:
                q_mul.append(jnp.exp(ldr * (C - tpos)))
                k_mul.append(jnp.exp(ldr * tpos))
                dist = (ci - ri).astype(F32)
            ret_decay.append(jnp.exp(ldr * float(C)))
            score_mul.append([jnp.where(tri[d], jnp.exp(ldh * dist), 0.0) for ldh in (ld0, ld1)])

    zero = jnp.zeros((), BF16)
    groups = [(s, d) for s in range(seq_blk) for d in range(2)]

    def run_block(n0, accumulate):
        chains = {}
        for s, d in groups:
            for u in range(unroll):
                n = n0 + u
                c = n if d == 0 else n_chunks - 1 - n
                t = dict(c=c, rows=pl.ds(pl.multiple_of(c * C, C), C))
                t["q"] = q_ref[s, t["rows"], :]
                t["k"] = k_ref[s, t["rows"], :]
                if is_gla:
                    lg = (lgf_ref if d == 0 else lgb_ref)[s, t["rows"], :]
                    hi = lg.astype(BF16)
                    lo = (lg - hi.astype(F32)).astype(BF16)
                    bb = _dot(tri_ones[d], jnp.concatenate([hi, lo], axis=1))
                    t["b"] = bb[:, :LANES] + bb[:, LANES:]
                chains[s, d, u] = t

        for (s, d, u), t in chains.items():
            q, k = t["q"], t["k"]
            if is_gla:
                b = t["b"]
                r = b[mid - 1:mid] if d == 0 else b[mid:mid + 1]
                bl = b[C - 1:C] if d == 0 else b[0:1]
                qs = q.astype(F32) * jnp.exp(b - r)
                ks = k.astype(F32) * jnp.exp(r - b)
                q_sc = qs.astype(BF16)
                k_sc = ks.astype(BF16)
                q_dec = (qs * jnp.exp(r)).astype(BF16)
                t["k_dec"] = (ks * jnp.exp(bl - r)).astype(BF16)
                t["decay"] = jnp.exp(bl)
            else:
                q_sc = q
                k_sc = k
                q_dec = (q.astype(F32) * q_mul[d]).astype(BF16)
                t["k_dec"] = (k.astype(F32) * k_mul[d]).astype(BF16)
                t["decay"] = ret_decay[d]
            t["q_dec"] = [jnp.where(head_mask[hh], q_dec, zero) for hh in range(2)]
            q_heads = jnp.concatenate([jnp.where(head_mask[hh], q_sc, zero) for hh in range(2)], axis=0)
            t["sc"] = _dot_nt(q_heads, k_sc)

        for (s, d, u), t in chains.items():
            t["vt"] = vt_ref[s, t["c"]]
            t["kv"] = _dot(t["vt"], t["k_dec"])

        for (s, d, u), t in chains.items():
            t["scb"] = []
            for hh in range(2):
                sc = t["sc"][hh * C:(hh + 1) * C]
                if is_gla:
                    sc = jnp.where(tri[d], sc, 0.0)
                else:
                    sc = sc * score_mul[d][hh]
                t["scb"].append(sc.astype(BF16))

        state = {(s, d): st_ref[s, d] for s, d in groups}
        for u in range(unroll):
            for s, d in groups:
                t = chains[s, d, u]
                st = state[s, d]
                stb = st.astype(BF16)
                t["o"] = []
                for hh in range(2):
                    hrows = slice(hh * DV, (hh + 1) * DV)
                    lhs = jnp.concatenate([t["scb"][hh], t["q_dec"][hh]], axis=1)
                    rhs = jnp.concatenate([t["vt"][hrows], stb[hrows]], axis=1)
                    t["o"].append(_dot_nt(lhs, rhs))
                state[s, d] = st * t["decay"] + t["kv"]
        for key, st in state.items():
            st_ref[key] = st

        for (s, d, u), t in chains.items():
            for hh in range(2):
                cols = slice(hh * DV, (hh + 1) * DV)
                if accumulate:
                    of = oacc_ref[s, t["rows"], cols] + t["o"][hh]
                    if is_gla:
                        nrm = of * lax.rsqrt(jnp.mean(of * of, axis=-1, keepdims=True) + EPS) * gw_ref[...]
                    else:
                        dlt = of - jnp.mean(of, axis=-1, keepdims=True)
                        nrm = dlt * lax.rsqrt(jnp.mean(dlt * dlt, axis=-1, keepdims=True) + EPS)
                    o_ref[s, t["rows"], cols] = (nrm * g_ref[s, t["rows"], cols].astype(F32)).astype(BF16)
                else:
                    oacc_ref[s, t["rows"], cols] = t["o"][hh]

    def loop_body(accumulate, base):
        def body(m, carry):
            run_block(base + m * unroll, accumulate)
            return carry
        return body

    half = n_chunks // 2
    assert half % unroll == 0
    lax.fori_loop(0, half // unroll, loop_body(False, 0), 0)
    lax.fori_loop(0, half // unroll, loop_body(True, half), 0)

    if want_state:
        for s in range(seq_blk):
            for d in range(2):
                for hh in range(2):
                    sn_ref[s, 0, d, hh] = st_ref[s, d, hh * DV:(hh + 1) * DV].T[hh * DK:(hh + 1) * DK, :]


def _scan(is_gla, qk, vt, gz, lg, gw, ld, s0, want_state, seq_blk, unroll):
    B, T, _ = qk.shape
    n_chunks = T // SCAN_CHUNK
    assert n_chunks % 2 == 0 and B % seq_blk == 0
    sb = seq_blk
    q_blk = 4 if is_gla else 0
    k_blk = q_blk + 2
    v_blk = 2 if is_gla else 0
    in_specs = [
        pl.BlockSpec((sb, T, LANES), lambda b, p: (b, 0, q_blk + p)),
        pl.BlockSpec((sb, T, LANES), lambda b, p: (b, 0, k_blk + p)),
        pl.BlockSpec((sb, n_chunks, 2 * DV, SCAN_CHUNK), lambda b, p: (b, 0, v_blk + p, 0)),
        pl.BlockSpec((sb, T, 2 * DV), lambda b, p: (b, 0, v_blk + p)),
    ]
    args = [qk, qk, vt, gz]
    if is_gla:
        in_specs += [
            pl.BlockSpec((sb, T, LANES), lambda b, p: (b, 0, p)),
            pl.BlockSpec((sb, T, LANES), lambda b, p: (b, 0, 2 + p)),
            pl.BlockSpec((1, DV), lambda b, p: (0, 0)),
        ]
        args += [lg, lg, gw]
    else:
        in_specs += [pl.BlockSpec(memory_space=pltpu.SMEM)]
        args += [ld]
    state_spec = pl.BlockSpec((sb, 1, 2, 2, DK, DV), lambda b, p: (b, 0, 0, p, 0, 0))
    if s0 is not None:
        in_specs += [state_spec]
        args += [s0]
    out_shape = [jax.ShapeDtypeStruct((B, T, WV), BF16)]
    out_specs = [pl.BlockSpec((sb, T, 2 * DV), lambda b, p: (b, 0, p))]
    if want_state:
        out_shape += [jax.ShapeDtypeStruct((B, 1, 2, N_HEADS, DK, DV), F32)]
        out_specs += [state_spec]
    name = ("gla" if is_gla else "ret") + ("_init" if s0 is not None else "") + "_scan"
    return pl.pallas_call(
        functools.partial(_scan_kernel, is_gla, s0 is not None, want_state, n_chunks, sb, unroll),
        grid=(B // sb, 2),
        in_specs=in_specs,
        out_specs=out_specs,
        out_shape=out_shape,
        scratch_shapes=[pltpu.VMEM((sb, 2, 2 * DV, LANES), F32),
                        pltpu.VMEM((sb, T, 2 * DV), F32)],
        compiler_params=pltpu.CompilerParams(
            dimension_semantics=("arbitrary", "arbitrary"), vmem_limit_bytes=VMEM_LIMIT),
        name=name,
    )(*args)


def _outproj_kernel(x_ref, mod_ref, mr_ref, mg_ref, fw_ref, wout_ref, y_ref):
    gate = mod_ref[0, :, 2 * D_MODEL:3 * D_MODEL]
    fw = fw_ref[...]
    for sub in range(TOKEN_BLOCK // OUTPROJ_SUB):
        rows = slice(sub * OUTPROJ_SUB, (sub + 1) * OUTPROJ_SUB)
        out = _dot(mr_ref[0, rows, :], wout_ref[0:WV, :]) + _dot(mg_ref[0, rows, :], wout_ref[WV:2 * WV, :])
        y = x_ref[0, rows, :] + gate * out
        yn = y * lax.rsqrt(jnp.mean(y * y, axis=-1, keepdims=True) + EPS)
        y_ref[0, rows, :] = yn * fw


def _outproj(x, mod, mod_row_fn, m_r, m_g, fw, wout):
    B, T, _ = x.shape
    nt = T // TOKEN_BLOCK
    const = lambda b, t: (0, 0)
    return pl.pallas_call(
        _outproj_kernel,
        grid=(B, nt),
        in_specs=[
            pl.BlockSpec((1, TOKEN_BLOCK, D_MODEL), lambda b, t: (b, t, 0)),
            pl.BlockSpec((1, 1, 3 * D_MODEL), lambda b, t: (mod_row_fn(b), 0, 0)),
            pl.BlockSpec((1, TOKEN_BLOCK, WV), lambda b, t: (b, t, 0)),
            pl.BlockSpec((1, TOKEN_BLOCK, WV), lambda b, t: (b, t, 0)),
            pl.BlockSpec((1, D_MODEL), const),
            pl.BlockSpec(wout.shape, const),
        ],
        out_specs=pl.BlockSpec((1, TOKEN_BLOCK, D_MODEL), lambda b, t: (b, t, 0)),
        out_shape=jax.ShapeDtypeStruct((B, T, D_MODEL), F32),
        compiler_params=pltpu.CompilerParams(
            dimension_semantics=("parallel", "parallel"), vmem_limit_bytes=VMEM_LIMIT),
        name="outproj",
    )(x, mod, m_r, m_g, fw, wout)


def _rope_tables(n_tokens):
    rows = n_tokens // GRID_W
    rr, cc = jnp.meshgrid(jnp.arange(rows), jnp.arange(GRID_W), indexing="ij")
    rr = rr.reshape(-1).astype(F32)
    cc = cc.reshape(-1).astype(F32)
    n_freq = DK // 4
    inv = ROPE_BASE ** (-jnp.arange(n_freq, dtype=F32) / n_freq)
    ang = jnp.concatenate([rr[:, None] * inv, cc[:, None] * inv], axis=-1)
    cos, sin = jnp.cos(ang), jnp.sin(ang)
    cos_t = jnp.tile(jnp.concatenate([cos, cos], axis=-1), (1, LANES // DK))
    sin_t = jnp.tile(jnp.concatenate([-sin, sin], axis=-1), (1, LANES // DK))
    return cos_t, sin_t


def _prep_weights(w_in, gla_w_alpha, gla_b_alpha):
    q_r, k_r = w_in[:, 0:QK], w_in[:, QK:2 * QK]
    o = 2 * QK
    v_r, z_r = w_in[:, o:o + WV], w_in[:, o + WV:o + 2 * WV]
    o += 2 * WV
    q_g, k_g = w_in[:, o:o + QK], w_in[:, o + QK:o + 2 * QK]
    o += 2 * QK
    v_g, z_g = w_in[:, o:o + WV], w_in[:, o + WV:o + 2 * WV]
    o += 2 * WV
    lr = w_in[:, o:o + 2 * GLA_RANK]
    wqk = jnp.concatenate([q_r, k_r, q_g, k_g], axis=1).astype(BF16)
    wz = jnp.concatenate([z_r, z_g], axis=1).astype(BF16)
    wvt = jnp.concatenate([v_r, v_g], axis=1).T.astype(BF16)
    wlr = lr.T.astype(BF16)
    wa = jnp.zeros((2 * GLA_RANK, 2 * QK), F32)
    wa = wa.at[0:GLA_RANK, 0:QK].set(gla_w_alpha[0])
    wa = wa.at[GLA_RANK:2 * GLA_RANK, QK:2 * QK].set(gla_w_alpha[1])
    ba = jnp.concatenate([gla_b_alpha[0], gla_b_alpha[1]]).reshape(1, 2 * QK)
    return wqk, wz, wvt, wlr, wa.astype(BF16), ba


def kernel(x_prompt, x_sample, c, state_ret, state_gla, c_ctx, w_mod, b_mod, w_in, ret_log_decay,
           gla_w_alpha, gla_b_alpha, gla_norm_w, w_out, final_norm_w):
    assert w_mod.shape[0] == 1, "single-layer model"
    b_ctx, t_ctx, _ = x_prompt.shape
    b_dec, t_dec, _ = x_sample.shape
    assert t_ctx % SCAN_CHUNK == 0 and t_dec % TOKEN_BLOCK == 0 and TOKEN_BLOCK % t_ctx == 0
    assert 1 + b_dec <= MOD_ROWS

    cond = jnp.concatenate(
        [c_ctx[None, :], c, jnp.zeros((MOD_ROWS - 1 - b_dec, D_MODEL), F32)], axis=0)
    mod = _modulation(cond, w_mod[0], b_mod[0]).reshape(MOD_ROWS, 1, 3 * D_MODEL)

    weights = _prep_weights(w_in[0], gla_w_alpha[0], gla_b_alpha[0])
    wout = w_out[0].astype(BF16)
    gw = gla_norm_w[0].reshape(1, DV)
    fw = final_norm_w.reshape(1, D_MODEL)
    ld = ret_log_decay[0]

    per_blk = TOKEN_BLOCK // t_ctx
    xp = x_prompt.reshape(b_ctx // per_blk, TOKEN_BLOCK, D_MODEL)
    ctx_row = lambda b: 0
    qk, gz, vt, lg = _inproj(xp, mod, ctx_row, weights, None)
    qk = qk.reshape(b_ctx, t_ctx, 4 * QK)
    gz = gz.reshape(b_ctx, t_ctx, 2 * WV)
    vt = vt.reshape(b_ctx, t_ctx // SCAN_CHUNK, 2 * WV, SCAN_CHUNK)
    lg = lg.reshape(b_ctx, t_ctx, 2 * QK)
    m_r, new_ret = _scan(False, qk, vt, gz, None, None, ld, None, True, CTX_SEQ_BLOCK, 1)
    m_g, new_gla = _scan(True, qk, vt, gz, lg, gw, None, None, True, CTX_SEQ_BLOCK, 1)
    y_prompt = _outproj(xp, mod, ctx_row, m_r.reshape(xp.shape[0], TOKEN_BLOCK, WV),
                        m_g.reshape(xp.shape[0], TOKEN_BLOCK, WV), fw, wout)
    y_prompt = y_prompt.reshape(b_ctx, t_ctx, D_MODEL)

    dec_row = lambda b: b + 1
    rope = _rope_tables(t_dec)
    qk, gz, vt, lg = _inproj(x_sample, mod, dec_row, weights, rope)
    (m_r,) = _scan(False, qk, vt, gz, None, None, ld, state_ret, False, 1, DEC_UNROLL)
    (m_g,) = _scan(True, qk, vt, gz, lg, gw, None, state_gla, False, 1, DEC_UNROLL)
    y_sample = _outproj(x_sample, mod, dec_row, m_r, m_g, fw, wout)

    return (y_prompt, y_sample, new_ret, new_gla)
```

```python
import functools

import jax
import jax.numpy as jnp
from jax import lax
from jax.experimental import pallas as pl
from jax.experimental.pallas import tpu as pltpu

F32 = jnp.float32
BF16 = jnp.bfloat16

D_MODEL = 1024
N_HEADS = 4
DK = 64
DV = 128
QK = N_HEADS * DK
WV = N_HEADS * DV
GLA_RANK = 16
GLA_TAU = 16.0
EPS = 1e-6
GRID_W = 64
ROPE_BASE = 10000.0

LANES = 128
SCAN_CHUNK = 128
CTX_SEQ_BLOCK = 4
DEC_UNROLL = 4
TOKEN_BLOCK = 1024
INPROJ_SUB = 256
OUTPROJ_SUB = 256
MOD_ROWS = 16
MOD_COL_BLOCK = 512
VMEM_LIMIT = 48 * 1024 * 1024

_NT = (((1,), (1,)), ((), ()))
_TN = (((0,), (0,)), ((), ()))


def _dot(a, b):
    return jnp.dot(a, b, preferred_element_type=F32)


def _dot_nt(a, b):
    return lax.dot_general(a, b, _NT, preferred_element_type=F32)


def _silu(x):
    return x * jax.nn.sigmoid(x)


def _mod_kernel(c_ref, w_ref, b_ref, o_ref):
    s = _silu(c_ref[...]).astype(BF16)
    o_ref[...] = _dot(s, w_ref[...].astype(BF16)) + b_ref[...]


def _modulation(cond, w_mod, b_mod):
    n_col = w_mod.shape[1]
    return pl.pallas_call(
        _mod_kernel,
        grid=(n_col // MOD_COL_BLOCK,),
        in_specs=[
            pl.BlockSpec((MOD_ROWS, D_MODEL), lambda j: (0, 0)),
            pl.BlockSpec((D_MODEL, MOD_COL_BLOCK), lambda j: (0, j)),
            pl.BlockSpec((1, MOD_COL_BLOCK), lambda j: (0, j)),
        ],
        out_specs=pl.BlockSpec((MOD_ROWS, MOD_COL_BLOCK), lambda j: (0, j)),
        out_shape=jax.ShapeDtypeStruct((MOD_ROWS, n_col), F32),
        compiler_params=pltpu.CompilerParams(vmem_limit_bytes=VMEM_LIMIT),
        name="modulation",
    )(cond, w_mod, b_mod.reshape(1, n_col))


def _rope_tile(t, cos, sin_signed):
    lane = lax.broadcasted_iota(jnp.int32, t.shape, 1)
    first_half = (lane & 32) == 0
    swapped = jnp.where(first_half, pltpu.roll(t, LANES - 32, 1), pltpu.roll(t, 32, 1))
    return t * cos + swapped * sin_signed


def _inproj_kernel(use_rope, x_ref, mod_ref, wqk_ref, wz_ref, wvt_ref, wlr_ref, wa_ref, ba_ref,
                   *rest):
    if use_rope:
        cos_ref, sin_ref, qk_ref, z_ref, vt_ref, lg_ref = rest
    else:
        qk_ref, z_ref, vt_ref, lg_ref = rest
    shift = mod_ref[0, :, 0:D_MODEL]
    scale1 = 1.0 + mod_ref[0, :, D_MODEL:2 * D_MODEL]

    for sub in range(TOKEN_BLOCK // INPROJ_SUB):
        rows = slice(sub * INPROJ_SUB, (sub + 1) * INPROJ_SUB)
        x = x_ref[0, rows, :]
        inv = lax.rsqrt(jnp.mean(x * x, axis=-1, keepdims=True) + EPS)
        hb = ((x * inv) * scale1 + shift).astype(BF16)

        lr_t = _dot_nt(wlr_ref[...], hb).astype(BF16)

        res = _dot(hb, wqk_ref[...])
        for half in range(2):
            for t in range(4):
                col = half * 2 * QK + t * LANES
                tile = res[:, col:col + LANES]
                is_q = t < 2
                if (half == 0 and not is_q) or (half == 1 and is_q):
                    tile = tile * (DK ** -0.5)
                if use_rope and half == 0:
                    tile = _rope_tile(tile, cos_ref[rows, :], sin_ref[rows, :])
                qk_ref[0, rows, col:col + LANES] = tile.astype(BF16)

        logit = lax.dot_general(lr_t, wa_ref[...], _TN,
                                preferred_element_type=F32) + ba_ref[...]
        log_sig = jnp.minimum(logit, 0.0) - jnp.log1p(jnp.exp(-jnp.abs(logit)))
        lg_ref[0, rows, :] = log_sig * (1.0 / GLA_TAU)

        z_ref[0, rows, :] = _silu(_dot(hb, wz_ref[...])).astype(BF16)

        vt = _dot_nt(wvt_ref[...], hb)
        for j in range(INPROJ_SUB // SCAN_CHUNK):
            jj = sub * (INPROJ_SUB // SCAN_CHUNK) + j
            vt_ref[0, jj] = vt[:, j * SCAN_CHUNK:(j + 1) * SCAN_CHUNK].astype(BF16)


def _inproj(x, mod, mod_row_fn, weights, rope):
    B, T, _ = x.shape
    wqk, wz, wvt, wlr, wa, ba = weights
    nt = T // TOKEN_BLOCK
    cpb = TOKEN_BLOCK // SCAN_CHUNK
    const = lambda b, t: (0, 0)
    in_specs = [
        pl.BlockSpec((1, TOKEN_BLOCK, D_MODEL), lambda b, t: (b, t, 0)),
        pl.BlockSpec((1, 1, 3 * D_MODEL), lambda b, t: (mod_row_fn(b), 0, 0)),
        pl.BlockSpec(wqk.shape, const),
        pl.BlockSpec(wz.shape, const),
        pl.BlockSpec(wvt.shape, const),
        pl.BlockSpec(wlr.shape, const),
        pl.BlockSpec(wa.shape, const),
        pl.BlockSpec(ba.shape, const),
    ]
    args = [x, mod, wqk, wz, wvt, wlr, wa, ba]
    if rope is not None:
        in_specs += [pl.BlockSpec((TOKEN_BLOCK, LANES), lambda b, t: (t, 0))] * 2
        args += list(rope)
    out_shape = (
        jax.ShapeDtypeStruct((B, T, 4 * QK), BF16),
        jax.ShapeDtypeStruct((B, T, 2 * WV), BF16),
        jax.ShapeDtypeStruct((B, T // SCAN_CHUNK, 2 * WV, SCAN_CHUNK), BF16),
        jax.ShapeDtypeStruct((B, T, 2 * QK), F32),
    )
    out_specs = (
        pl.BlockSpec((1, TOKEN_BLOCK, 4 * QK), lambda b, t: (b, t, 0)),
        pl.BlockSpec((1, TOKEN_BLOCK, 2 * WV), lambda b, t: (b, t, 0)),
        pl.BlockSpec((1, cpb, 2 * WV, SCAN_CHUNK), lambda b, t: (b, t, 0, 0)),
        pl.BlockSpec((1, TOKEN_BLOCK, 2 * QK), lambda b, t: (b, t, 0)),
    )
    return pl.pallas_call(
        functools.partial(_inproj_kernel, rope is not None),
        grid=(B, nt),
        in_specs=in_specs,
        out_specs=out_specs,
        out_shape=out_shape,
        compiler_params=pltpu.CompilerParams(
            dimension_semantics=("parallel", "parallel"), vmem_limit_bytes=VMEM_LIMIT),
        name="inproj_rope" if rope is not None else "inproj",
    )(*args)


def _scan_kernel(is_gla, has_init, want_state, n_chunks, seq_blk, unroll, *refs):
    refs = list(refs)
    q_ref, k_ref, vt_ref = refs[:3]
    pos = 3
    if is_gla:
        lgf_ref, lgb_ref = refs[pos:pos + 2]
        pos += 2
    else:
        ld_ref = refs[pos]
        pos += 1
    if has_init:
        s0_ref = refs[pos]
        pos += 1
    o_ref = refs[pos]
    pos += 1
    if want_state:
        sn_ref = refs[pos]
        pos += 1
    st_ref, oacc_ref = refs[pos:pos + 2]

    C = SCAN_CHUNK
    mid = C // 2
    pair = pl.program_id(1)
    lane = lax.broadcasted_iota(jnp.int32, (1, LANES), 1)
    head_mask = (lane < DK, lane >= DK)
    ri = lax.broadcasted_iota(jnp.int32, (C, C), 0)
    ci = lax.broadcasted_iota(jnp.int32, (C, C), 1)
    tpos = lax.broadcasted_iota(jnp.int32, (C, 1), 0).astype(F32)
    tri = ((ri >= ci), (ci >= ri))

    if has_init:
        for s in range(seq_blk):
            for d in range(2):
                s0 = s0_ref[s, 0, d]
                both = jnp.concatenate([s0[0], s0[1]], axis=0).T
                st_ref[s, d, 0:DV] = both
                st_ref[s, d, DV:2 * DV] = both
    else:
        st_ref[...] = jnp.zeros(st_ref.shape, F32)

    if is_gla:
        tri_ones = [t.astype(BF16) for t in tri]
    else:
        q_mul, k_mul, ret_decay, score_mul = [], [], [], []
        for d in range(2):
            ld0 = ld_ref[d, 2 * pair]
            ld1 = ld_ref[d, 2 * pair + 1]
            ldr = jnp.where(lane < DK, ld0, ld1)
            if d == 0:
                q_mul.append(jnp.exp(ldr * (tpos + 1.0)))
                k_mul.append(jnp.exp(ldr * (C - 1.0 - tpos)))
                dist = (ri - ci).astype(F32)
            else:
                q_mul.append(jnp.exp(ldr * (C - tpos)))
                k_mul.append(jnp.exp(ldr * tpos))
                dist = (ci - ri).astype(F32)
            ret_decay.append(jnp.exp(ldr * float(C)))
            score_mul.append([jnp.where(tri[d], jnp.exp(ldh * dist), 0.0) for ldh in (ld0, ld1)])

    zero = jnp.zeros((), BF16)
    groups = [(s, d) for s in range(seq_blk) for d in range(2)]

    def run_block(n0, accumulate):
        chains = {}
        for s, d in groups:
            for u in range(unroll):
                n = n0 + u
                c = n if d == 0 else n_chunks - 1 - n
                t = dict(c=c, rows=pl.ds(pl.multiple_of(c * C, C), C))
                t["q"] = q_ref[s, t["rows"], :]
                t["k"] = k_ref[s, t["rows"], :]
                if is_gla:
                    lg = (lgf_ref if d == 0 else lgb_ref)[s, t["rows"], :]
                    hi = lg.astype(BF16)
                    lo = (lg - hi.astype(F32)).astype(BF16)
                    bb = _dot(tri_ones[d], jnp.concatenate([hi, lo], axis=1))
                    t["b"] = bb[:, :LANES] + bb[:, LANES:]
                chains[s, d, u] = t

        for (s, d, u), t in chains.items():
            q, k = t["q"], t["k"]
            if is_gla:
                b = t["b"]
                r = b[mid - 1:mid] if d == 0 else b[mid:mid + 1]
                bl = b[C - 1:C] if d == 0 else b[0:1]
                qs = q.astype(F32) * jnp.exp(b - r)
                ks = k.astype(F32) * jnp.exp(r - b)
                q_sc = qs.astype(BF16)
                k_sc = ks.astype(BF16)
                q_dec = (qs * jnp.exp(r)).astype(BF16)
                t["k_dec"] = (ks * jnp.exp(bl - r)).astype(BF16)
                t["decay"] = jnp.exp(bl)
            else:
                q_sc = q
                k_sc = k
                q_dec = (q.astype(F32) * q_mul[d]).astype(BF16)
                t["k_dec"] = (k.astype(F32) * k_mul[d]).astype(BF16)
                t["decay"] = ret_decay[d]
            t["q_dec"] = [jnp.where(head_mask[hh], q_dec, zero) for hh in range(2)]
            k_heads = jnp.concatenate([jnp.where(head_mask[hh], k_sc, zero) for hh in range(2)], axis=0)
            t["sc"] = _dot_nt(q_sc, k_heads)

        for (s, d, u), t in chains.items():
            t["vt"] = vt_ref[s, t["c"]]
            t["kv"] = _dot(t["vt"], t["k_dec"])

        for (s, d, u), t in chains.items():
            t["scb"] = []
            for hh in range(2):
                sc = t["sc"][:, hh * C:(hh + 1) * C]
                if is_gla:
                    sc = jnp.where(tri[d], sc, 0.0)
                else:
                    sc = sc * score_mul[d][hh]
                t["scb"].append(sc.astype(BF16))

        state = {(s, d): st_ref[s, d] for s, d in groups}
        for u in range(unroll):
            for s, d in groups:
                t = chains[s, d, u]
                st = state[s, d]
                stb = st.astype(BF16)
                t["o"] = []
                for hh in range(2):
                    hrows = slice(hh * DV, (hh + 1) * DV)
                    lhs = jnp.concatenate([t["scb"][hh], t["q_dec"][hh]], axis=1)
                    rhs = jnp.concatenate([t["vt"][hrows], stb[hrows]], axis=1)
                    t["o"].append(_dot_nt(lhs, rhs))
                state[s, d] = st * t["decay"] + t["kv"]
        for key, st in state.items():
            st_ref[key] = st

        for (s, d, u), t in chains.items():
            for hh in range(2):
                cols = slice(hh * DV, (hh + 1) * DV)
                if accumulate:
                    o_ref[s, t["rows"], cols] = (oacc_ref[s, t["rows"], cols] + t["o"][hh]).astype(BF16)
                else:
                    oacc_ref[s, t["rows"], cols] = t["o"][hh]

    def loop_body(accumulate, base):
        def body(m, carry):
            run_block(base + m * unroll, accumulate)
            return carry
        return body

    half = n_chunks // 2
    assert half % unroll == 0
    lax.fori_loop(0, half // unroll, loop_body(False, 0), 0)
    lax.fori_loop(0, half // unroll, loop_body(True, half), 0)

    if want_state:
        for s in range(seq_blk):
            for d in range(2):
                for hh in range(2):
                    sn_ref[s, 0, d, hh] = st_ref[s, d, hh * DV:(hh + 1) * DV].T[hh * DK:(hh + 1) * DK, :]


def _scan(is_gla, qk, vt, lg, ld, s0, want_state, seq_blk, unroll):
    B, T, _ = qk.shape
    n_chunks = T // SCAN_CHUNK
    assert n_chunks % 2 == 0 and B % seq_blk == 0
    sb = seq_blk
    q_blk = 4 if is_gla else 0
    k_blk = q_blk + 2
    v_blk = 2 if is_gla else 0
    in_specs = [
        pl.BlockSpec((sb, T, LANES), lambda b, p: (b, 0, q_blk + p)),
        pl.BlockSpec((sb, T, LANES), lambda b, p: (b, 0, k_blk + p)),
        pl.BlockSpec((sb, n_chunks, 2 * DV, SCAN_CHUNK), lambda b, p: (b, 0, v_blk + p, 0)),
    ]
    args = [qk, qk, vt]
    if is_gla:
        in_specs += [
            pl.BlockSpec((sb, T, LANES), lambda b, p: (b, 0, p)),
            pl.BlockSpec((sb, T, LANES), lambda b, p: (b, 0, 2 + p)),
        ]
        args += [lg, lg]
    else:
        in_specs += [pl.BlockSpec(memory_space=pltpu.SMEM)]
        args += [ld]
    state_spec = pl.BlockSpec((sb, 1, 2, 2, DK, DV), lambda b, p: (b, 0, 0, p, 0, 0))
    if s0 is not None:
        in_specs += [state_spec]
        args += [s0]
    out_shape = [jax.ShapeDtypeStruct((B, T, WV), BF16)]
    out_specs = [pl.BlockSpec((sb, T, 2 * DV), lambda b, p: (b, 0, p))]
    if want_state:
        out_shape += [jax.ShapeDtypeStruct((B, 1, 2, N_HEADS, DK, DV), F32)]
        out_specs += [state_spec]
    name = ("gla" if is_gla else "ret") + ("_init" if s0 is not None else "") + "_scan"
    return pl.pallas_call(
        functools.partial(_scan_kernel, is_gla, s0 is not None, want_state, n_chunks, sb, unroll),
        grid=(B // sb, 2),
        in_specs=in_specs,
        out_specs=out_specs,
        out_shape=out_shape,
        scratch_shapes=[pltpu.VMEM((sb, 2, 2 * DV, LANES), F32),
                        pltpu.VMEM((sb, T, 2 * DV), F32)],
        compiler_params=pltpu.CompilerParams(
            dimension_semantics=("arbitrary", "arbitrary"), vmem_limit_bytes=VMEM_LIMIT),
        name=name,
    )(*args)


def _outproj_kernel(x_ref, mod_ref, or_ref, og_ref, zr_ref, zg_ref, gw_ref, fw_ref, wout_ref, y_ref):
    gate = mod_ref[0, :, 2 * D_MODEL:3 * D_MODEL]
    gw = gw_ref[...]
    fw = fw_ref[...]
    for sub in range(TOKEN_BLOCK // OUTPROJ_SUB):
        rows = slice(sub * OUTPROJ_SUB, (sub + 1) * OUTPROJ_SUB)
        parts = []
        for h in range(N_HEADS):
            cols = slice(h * DV, (h + 1) * DV)
            t = or_ref[0, rows, cols].astype(F32)
            mu = jnp.mean(t, axis=-1, keepdims=True)
            dlt = t - mu
            var = jnp.mean(dlt * dlt, axis=-1, keepdims=True)
            n = dlt * lax.rsqrt(var + EPS)
            parts.append((n * zr_ref[0, rows, cols].astype(F32)).astype(BF16))
        for h in range(N_HEADS):
            cols = slice(h * DV, (h + 1) * DV)
            t = og_ref[0, rows, cols].astype(F32)
            n = t * lax.rsqrt(jnp.mean(t * t, axis=-1, keepdims=True) + EPS) * gw
            parts.append((n * zg_ref[0, rows, cols].astype(F32)).astype(BF16))
        mixed = jnp.concatenate(parts, axis=-1)
        out = _dot(mixed, wout_ref[...])
        y = x_ref[0, rows, :] + gate * out
        yn = y * lax.rsqrt(jnp.mean(y * y, axis=-1, keepdims=True) + EPS)
        y_ref[0, rows, :] = yn * fw


def _outproj(x, mod, mod_row_fn, o_r, o_g, gz, gw, fw, wout):
    B, T, _ = x.shape
    nt = T // TOKEN_BLOCK
    const = lambda b, t: (0, 0)
    return pl.pallas_call(
        _outproj_kernel,
        grid=(B, nt),
        in_specs=[
            pl.BlockSpec((1, TOKEN_BLOCK, D_MODEL), lambda b, t: (b, t, 0)),
            pl.BlockSpec((1, 1, 3 * D_MODEL), lambda b, t: (mod_row_fn(b), 0, 0)),
            pl.BlockSpec((1, TOKEN_BLOCK, WV), lambda b, t: (b, t, 0)),
            pl.BlockSpec((1, TOKEN_BLOCK, WV), lambda b, t: (b, t, 0)),
            pl.BlockSpec((1, TOKEN_BLOCK, WV), lambda b, t: (b, t, 0)),
            pl.BlockSpec((1, TOKEN_BLOCK, WV), lambda b, t: (b, t, 1)),
            pl.BlockSpec((1, DV), const),
            pl.BlockSpec((1, D_MODEL), const),
            pl.BlockSpec(wout.shape, const),
        ],
        out_specs=pl.BlockSpec((1, TOKEN_BLOCK, D_MODEL), lambda b, t: (b, t, 0)),
        out_shape=jax.ShapeDtypeStruct((B, T, D_MODEL), F32),
        compiler_params=pltpu.CompilerParams(
            dimension_semantics=("parallel", "parallel"), vmem_limit_bytes=VMEM_LIMIT),
        name="outproj",
    )(x, mod, o_r, o_g, gz, gz, gw, fw, wout)


def _rope_tables(n_tokens):
    rows = n_tokens // GRID_W
    rr, cc = jnp.meshgrid(jnp.arange(rows), jnp.arange(GRID_W), indexing="ij")
    rr = rr.reshape(-1).astype(F32)
    cc = cc.reshape(-1).astype(F32)
    n_freq = DK // 4
    inv = ROPE_BASE ** (-jnp.arange(n_freq, dtype=F32) / n_freq)
    ang = jnp.concatenate([rr[:, None] * inv, cc[:, None] * inv], axis=-1)
    cos, sin = jnp.cos(ang), jnp.sin(ang)
    cos_t = jnp.tile(jnp.concatenate([cos, cos], axis=-1), (1, LANES // DK))
    sin_t = jnp.tile(jnp.concatenate([-sin, sin], axis=-1), (1, LANES // DK))
    return cos_t, sin_t


def _prep_weights(w_in, gla_w_alpha, gla_b_alpha):
    q_r, k_r = w_in[:, 0:QK], w_in[:, QK:2 * QK]
    o = 2 * QK
    v_r, z_r = w_in[:, o:o + WV], w_in[:, o + WV:o + 2 * WV]
    o += 2 * WV
    q_g, k_g = w_in[:, o:o + QK], w_in[:, o + QK:o + 2 * QK]
    o += 2 * QK
    v_g, z_g = w_in[:, o:o + WV], w_in[:, o + WV:o + 2 * WV]
    o += 2 * WV
    lr = w_in[:, o:o + 2 * GLA_RANK]
    wqk = jnp.concatenate([q_r, k_r, q_g, k_g], axis=1).astype(BF16)
    wz = jnp.concatenate([z_r, z_g], axis=1).astype(BF16)
    wvt = jnp.concatenate([v_r, v_g], axis=1).T.astype(BF16)
    wlr = lr.T.astype(BF16)
    wa = jnp.zeros((2 * GLA_RANK, 2 * QK), F32)
    wa = wa.at[0:GLA_RANK, 0:QK].set(gla_w_alpha[0])
    wa = wa.at[GLA_RANK:2 * GLA_RANK, QK:2 * QK].set(gla_w_alpha[1])
    ba = jnp.concatenate([gla_b_alpha[0], gla_b_alpha[1]]).reshape(1, 2 * QK)
    return wqk, wz, wvt, wlr, wa.astype(BF16), ba


def kernel(x_prompt, x_sample, c, state_ret, state_gla, c_ctx, w_mod, b_mod, w_in, ret_log_decay,
           gla_w_alpha, gla_b_alpha, gla_norm_w, w_out, final_norm_w):
    assert w_mod.shape[0] == 1, "single-layer model"
    b_ctx, t_ctx, _ = x_prompt.shape
    b_dec, t_dec, _ = x_sample.shape
    assert t_ctx % SCAN_CHUNK == 0 and t_dec % TOKEN_BLOCK == 0 and TOKEN_BLOCK % t_ctx == 0
    assert 1 + b_dec <= MOD_ROWS

    cond = jnp.concatenate(
        [c_ctx[None, :], c, jnp.zeros((MOD_ROWS - 1 - b_dec, D_MODEL), F32)], axis=0)
    mod = _modulation(cond, w_mod[0], b_mod[0]).reshape(MOD_ROWS, 1, 3 * D_MODEL)

    weights = _prep_weights(w_in[0], gla_w_alpha[0], gla_b_alpha[0])
    wout = w_out[0].astype(BF16)
    gw = gla_norm_w[0].reshape(1, DV)
    fw = final_norm_w.reshape(1, D_MODEL)
    ld = ret_log_decay[0]

    per_blk = TOKEN_BLOCK // t_ctx
    xp = x_prompt.reshape(b_ctx // per_blk, TOKEN_BLOCK, D_MODEL)
    ctx_row = lambda b: 0
    qk, gz, vt, lg = _inproj(xp, mod, ctx_row, weights, None)
    qk = qk.reshape(b_ctx, t_ctx, 4 * QK)
    vt = vt.reshape(b_ctx, t_ctx // SCAN_CHUNK, 2 * WV, SCAN_CHUNK)
    lg = lg.reshape(b_ctx, t_ctx, 2 * QK)
    o_r, new_ret = _scan(False, qk, vt, None, ld, None, True, CTX_SEQ_BLOCK, 1)
    o_g, new_gla = _scan(True, qk, vt, lg, None, None, True, CTX_SEQ_BLOCK, 1)
    y_prompt = _outproj(xp, mod, ctx_row, o_r.reshape(xp.shape[0], TOKEN_BLOCK, WV),
                        o_g.reshape(xp.shape[0], TOKEN_BLOCK, WV), gz, gw, fw, wout)
    y_prompt = y_prompt.reshape(b_ctx, t_ctx, D_MODEL)

    dec_row = lambda b: b + 1
    rope = _rope_tables(t_dec)
    qk, gz, vt, lg = _inproj(x_sample, mod, dec_row, weights, rope)
    (o_r,) = _scan(False, qk, vt, None, ld, state_ret, False, 1, DEC_UNROLL)
    (o_g,) = _scan(True, qk, vt, lg, None, state_gla, False, 1, DEC_UNROLL)
    y_sample = _outproj(x_sample, mod, dec_row, o_r, o_g, gz, gw, fw, wout)

    return (y_prompt, y_sample, new_ret, new_gla)
```

```python
import functools

import jax
import jax.numpy as jnp
from jax import lax
from jax.experimental import pallas as pl
from jax.experimental.pallas import tpu as pltpu

F32 = jnp.float32
BF16 = jnp.bfloat16

D_MODEL = 1024
N_HEADS = 4
DK = 64
DV = 128
QK = N_HEADS * DK
WV = N_HEADS * DV
GLA_RANK = 16
GLA_TAU = 16.0
EPS = 1e-6
GRID_W = 64
ROPE_BASE = 10000.0

LANES = 128
SCAN_CHUNK = 128
CTX_SEQ_BLOCK = 4
DEC_UNROLL = 2
TOKEN_BLOCK = 1024
INPROJ_SUB = 256
OUTPROJ_SUB = 256
MOD_ROWS = 16
MOD_COL_BLOCK = 512
VMEM_LIMIT = 48 * 1024 * 1024

_NT = (((1,), (1,)), ((), ()))
_TN = (((0,), (0,)), ((), ()))


def _dot(a, b):
    return jnp.dot(a, b, preferred_element_type=F32)


def _dot_nt(a, b):
    return lax.dot_general(a, b, _NT, preferred_element_type=F32)


def _silu(x):
    return x * jax.nn.sigmoid(x)


def _mod_kernel(c_ref, w_ref, b_ref, o_ref):
    s = _silu(c_ref[...]).astype(BF16)
    o_ref[...] = _dot(s, w_ref[...].astype(BF16)) + b_ref[...]


def _modulation(cond, w_mod, b_mod):
    n_col = w_mod.shape[1]
    return pl.pallas_call(
        _mod_kernel,
        grid=(n_col // MOD_COL_BLOCK,),
        in_specs=[
            pl.BlockSpec((MOD_ROWS, D_MODEL), lambda j: (0, 0)),
            pl.BlockSpec((D_MODEL, MOD_COL_BLOCK), lambda j: (0, j)),
            pl.BlockSpec((1, MOD_COL_BLOCK), lambda j: (0, j)),
        ],
        out_specs=pl.BlockSpec((MOD_ROWS, MOD_COL_BLOCK), lambda j: (0, j)),
        out_shape=jax.ShapeDtypeStruct((MOD_ROWS, n_col), F32),
        compiler_params=pltpu.CompilerParams(vmem_limit_bytes=VMEM_LIMIT),
        name="modulation",
    )(cond, w_mod, b_mod.reshape(1, n_col))


def _rope_tile(t, cos, sin_signed):
    lane = lax.broadcasted_iota(jnp.int32, t.shape, 1)
    first_half = (lane & 32) == 0
    swapped = jnp.where(first_half, pltpu.roll(t, LANES - 32, 1), pltpu.roll(t, 32, 1))
    return t * cos + swapped * sin_signed


def _inproj_kernel(use_rope, x_ref, mod_ref, wqk_ref, wz_ref, wvt_ref, wlr_ref, wa_ref, ba_ref,
                   *rest):
    if use_rope:
        cos_ref, sin_ref, qk_ref, z_ref, vt_ref, lg_ref = rest
    else:
        qk_ref, z_ref, vt_ref, lg_ref = rest
    shift = mod_ref[0, :, 0:D_MODEL]
    scale1 = 1.0 + mod_ref[0, :, D_MODEL:2 * D_MODEL]

    for sub in range(TOKEN_BLOCK // INPROJ_SUB):
        rows = slice(sub * INPROJ_SUB, (sub + 1) * INPROJ_SUB)
        x = x_ref[0, rows, :]
        inv = lax.rsqrt(jnp.mean(x * x, axis=-1, keepdims=True) + EPS)
        hb = ((x * inv) * scale1 + shift).astype(BF16)

        lr_t = _dot_nt(wlr_ref[...], hb).astype(BF16)

        res = _dot(hb, wqk_ref[...])
        for half in range(2):
            for t in range(4):
                col = half * 2 * QK + t * LANES
                tile = res[:, col:col + LANES]
                is_q = t < 2
                if (half == 0 and not is_q) or (half == 1 and is_q):
                    tile = tile * (DK ** -0.5)
                if use_rope and half == 0:
                    tile = _rope_tile(tile, cos_ref[rows, :], sin_ref[rows, :])
                qk_ref[0, rows, col:col + LANES] = tile.astype(BF16)

        logit = lax.dot_general(lr_t, wa_ref[...], _TN,
                                preferred_element_type=F32) + ba_ref[...]
        log_sig = jnp.minimum(logit, 0.0) - jnp.log1p(jnp.exp(-jnp.abs(logit)))
        lg_ref[0, rows, :] = log_sig * (1.0 / GLA_TAU)

        z_ref[0, rows, :] = _silu(_dot(hb, wz_ref[...])).astype(BF16)

        vt = _dot_nt(wvt_ref[...], hb)
        for j in range(INPROJ_SUB // SCAN_CHUNK):
            jj = sub * (INPROJ_SUB // SCAN_CHUNK) + j
            vt_ref[0, jj] = vt[:, j * SCAN_CHUNK:(j + 1) * SCAN_CHUNK].astype(BF16)


def _inproj(x, mod, mod_row_fn, weights, rope):
    B, T, _ = x.shape
    wqk, wz, wvt, wlr, wa, ba = weights
    nt = T // TOKEN_BLOCK
    cpb = TOKEN_BLOCK // SCAN_CHUNK
    const = lambda b, t: (0, 0)
    in_specs = [
        pl.BlockSpec((1, TOKEN_BLOCK, D_MODEL), lambda b, t: (b, t, 0)),
        pl.BlockSpec((1, 1, 3 * D_MODEL), lambda b, t: (mod_row_fn(b), 0, 0)),
        pl.BlockSpec(wqk.shape, const),
        pl.BlockSpec(wz.shape, const),
        pl.BlockSpec(wvt.shape, const),
        pl.BlockSpec(wlr.shape, const),
        pl.BlockSpec(wa.shape, const),
        pl.BlockSpec(ba.shape, const),
    ]
    args = [x, mod, wqk, wz, wvt, wlr, wa, ba]
    if rope is not None:
        in_specs += [pl.BlockSpec((TOKEN_BLOCK, LANES), lambda b, t: (t, 0))] * 2
        args += list(rope)
    out_shape = (
        jax.ShapeDtypeStruct((B, T, 4 * QK), BF16),
        jax.ShapeDtypeStruct((B, T, 2 * WV), BF16),
        jax.ShapeDtypeStruct((B, T // SCAN_CHUNK, 2 * WV, SCAN_CHUNK), BF16),
        jax.ShapeDtypeStruct((B, T, 2 * QK), F32),
    )
    out_specs = (
        pl.BlockSpec((1, TOKEN_BLOCK, 4 * QK), lambda b, t: (b, t, 0)),
        pl.BlockSpec((1, TOKEN_BLOCK, 2 * WV), lambda b, t: (b, t, 0)),
        pl.BlockSpec((1, cpb, 2 * WV, SCAN_CHUNK), lambda b, t: (b, t, 0, 0)),
        pl.BlockSpec((1, TOKEN_BLOCK, 2 * QK), lambda b, t: (b, t, 0)),
    )
    return pl.pallas_call(
        functools.partial(_inproj_kernel, rope is not None),
        grid=(B, nt),
        in_specs=in_specs,
        out_specs=out_specs,
        out_shape=out_shape,
        compiler_params=pltpu.CompilerParams(
            dimension_semantics=("parallel", "parallel"), vmem_limit_bytes=VMEM_LIMIT),
        name="inproj_rope" if rope is not None else "inproj",
    )(*args)


def _scan_kernel(is_gla, has_init, want_state, n_chunks, seq_blk, unroll, *refs):
    refs = list(refs)
    q_ref, k_ref, vt_ref = refs[:3]
    pos = 3
    if is_gla:
        lgf_ref, lgb_ref = refs[pos:pos + 2]
        pos += 2
    else:
        ld_ref = refs[pos]
        pos += 1
    if has_init:
        s0_ref = refs[pos]
        pos += 1
    o_ref = refs[pos]
    pos += 1
    if want_state:
        sn_ref = refs[pos]
        pos += 1
    st_ref, oacc_ref = refs[pos:pos + 2]

    C = SCAN_CHUNK
    mid = C // 2
    n_pairs = N_HEADS // 2
    lane = lax.broadcasted_iota(jnp.int32, (1, LANES), 1)
    head_mask = (lane < DK, lane >= DK)
    ri = lax.broadcasted_iota(jnp.int32, (C, C), 0)
    ci = lax.broadcasted_iota(jnp.int32, (C, C), 1)
    tpos = lax.broadcasted_iota(jnp.int32, (C, 1), 0).astype(F32)
    tri = ((ri >= ci), (ci >= ri))

    if has_init:
        for s in range(seq_blk):
            for p in range(n_pairs):
                for d in range(2):
                    s0 = s0_ref[s, 0, d]
                    both = jnp.concatenate([s0[2 * p], s0[2 * p + 1]], axis=0).T
                    st_ref[s, p, d, 0:DV] = both
                    st_ref[s, p, d, DV:2 * DV] = both
    else:
        st_ref[...] = jnp.zeros(st_ref.shape, F32)

    if is_gla:
        tri_ones = [t.astype(BF16) for t in tri]
    else:
        q_mul, k_mul, ret_decay, score_mul = {}, {}, {}, {}
        for p in range(n_pairs):
            for d in range(2):
                ld0 = ld_ref[d, 2 * p]
                ld1 = ld_ref[d, 2 * p + 1]
                ldr = jnp.where(lane < DK, ld0, ld1)
                if d == 0:
                    q_mul[p, d] = jnp.exp(ldr * (tpos + 1.0))
                    k_mul[p, d] = jnp.exp(ldr * (C - 1.0 - tpos))
                    dist = (ri - ci).astype(F32)
                else:
                    q_mul[p, d] = jnp.exp(ldr * (C - tpos))
                    k_mul[p, d] = jnp.exp(ldr * tpos)
                    dist = (ci - ri).astype(F32)
                ret_decay[p, d] = jnp.exp(ldr * float(C))
                score_mul[p, d] = [jnp.where(tri[d], jnp.exp(ldh * dist), 0.0) for ldh in (ld0, ld1)]

    zero = jnp.zeros((), BF16)
    groups = [(s, p, d) for s in range(seq_blk) for p in range(n_pairs) for d in range(2)]

    def run_block(n0, accumulate):
        chains = {}
        for s, p, d in groups:
            ptile = slice(p * LANES, (p + 1) * LANES)
            for u in range(unroll):
                n = n0 + u
                c = n if d == 0 else n_chunks - 1 - n
                t = dict(c=c, rows=pl.ds(pl.multiple_of(c * C, C), C))
                t["q"] = q_ref[s, t["rows"], ptile]
                t["k"] = k_ref[s, t["rows"], ptile]
                if is_gla:
                    lg = (lgf_ref if d == 0 else lgb_ref)[s, t["rows"], ptile]
                    hi = lg.astype(BF16)
                    lo = (lg - hi.astype(F32)).astype(BF16)
                    bb = _dot(tri_ones[d], jnp.concatenate([hi, lo], axis=1))
                    t["b"] = bb[:, :LANES] + bb[:, LANES:]
                chains[s, p, d, u] = t

        for (s, p, d, u), t in chains.items():
            q, k = t["q"], t["k"]
            if is_gla:
                b = t["b"]
                r = b[mid - 1:mid] if d == 0 else b[mid:mid + 1]
                bl = b[C - 1:C] if d == 0 else b[0:1]
                qs = q.astype(F32) * jnp.exp(b - r)
                ks = k.astype(F32) * jnp.exp(r - b)
                q_sc = qs.astype(BF16)
                k_sc = ks.astype(BF16)
                q_dec = (qs * jnp.exp(r)).astype(BF16)
                t["k_dec"] = (ks * jnp.exp(bl - r)).astype(BF16)
                t["decay"] = jnp.exp(bl)
            else:
                q_sc = q
                k_sc = k
                q_dec = (q.astype(F32) * q_mul[p, d]).astype(BF16)
                t["k_dec"] = (k.astype(F32) * k_mul[p, d]).astype(BF16)
                t["decay"] = ret_decay[p, d]
            t["q_dec"] = [jnp.where(head_mask[hh], q_dec, zero) for hh in range(2)]
            k_heads = jnp.concatenate([jnp.where(head_mask[hh], k_sc, zero) for hh in range(2)], axis=0)
            t["sc"] = _dot_nt(q_sc, k_heads)

        for (s, p, d, u), t in chains.items():
            t["vt"] = vt_ref[s, t["c"], p * 2 * DV:(p + 1) * 2 * DV, :]
            t["kv"] = _dot(t["vt"], t["k_dec"])

        for (s, p, d, u), t in chains.items():
            t["scb"] = []
            for hh in range(2):
                sc = t["sc"][:, hh * C:(hh + 1) * C]
                if is_gla:
                    sc = jnp.where(tri[d], sc, 0.0)
                else:
                    sc = sc * score_mul[p, d][hh]
                t["scb"].append(sc.astype(BF16))

        state = {g: st_ref[g] for g in groups}
        for u in range(unroll):
            for g in groups:
                t = chains[g + (u,)]
                st = state[g]
                stb = st.astype(BF16)
                t["o"] = []
                for hh in range(2):
                    hrows = slice(hh * DV, (hh + 1) * DV)
                    lhs = jnp.concatenate([t["scb"][hh], t["q_dec"][hh]], axis=1)
                    rhs = jnp.concatenate([t["vt"][hrows], stb[hrows]], axis=1)
                    t["o"].append(_dot_nt(lhs, rhs))
                state[g] = st * t["decay"] + t["kv"]
        for key, st in state.items():
            st_ref[key] = st

        for (s, p, d, u), t in chains.items():
            for hh in range(2):
                cols = slice((2 * p + hh) * DV, (2 * p + hh + 1) * DV)
                if accumulate:
                    o_ref[s, t["rows"], cols] = (oacc_ref[s, t["rows"], cols] + t["o"][hh]).astype(BF16)
                else:
                    oacc_ref[s, t["rows"], cols] = t["o"][hh]

    def loop_body(accumulate, base):
        def body(m, carry):
            run_block(base + m * unroll, accumulate)
            return carry
        return body

    half = n_chunks // 2
    assert half % unroll == 0
    lax.fori_loop(0, half // unroll, loop_body(False, 0), 0)
    lax.fori_loop(0, half // unroll, loop_body(True, half), 0)

    if want_state:
        for s, p, d in groups:
            for hh in range(2):
                sn_ref[s, 0, d, 2 * p + hh] = st_ref[s, p, d, hh * DV:(hh + 1) * DV].T[hh * DK:(hh + 1) * DK, :]


def _scan(is_gla, qk, vt, lg, ld, s0, want_state, seq_blk, unroll):
    B, T, _ = qk.shape
    n_chunks = T // SCAN_CHUNK
    assert n_chunks % 2 == 0 and B % seq_blk == 0
    sb = seq_blk
    grp = 1 if is_gla else 0
    in_specs = [
        pl.BlockSpec((sb, T, QK), lambda b: (b, 0, 2 * grp)),
        pl.BlockSpec((sb, T, QK), lambda b: (b, 0, 2 * grp + 1)),
        pl.BlockSpec((sb, n_chunks, WV, SCAN_CHUNK), lambda b: (b, 0, grp, 0)),
    ]
    args = [qk, qk, vt]
    if is_gla:
        in_specs += [
            pl.BlockSpec((sb, T, QK), lambda b: (b, 0, 0)),
            pl.BlockSpec((sb, T, QK), lambda b: (b, 0, 1)),
        ]
        args += [lg, lg]
    else:
        in_specs += [pl.BlockSpec(memory_space=pltpu.SMEM)]
        args += [ld]
    state_spec = pl.BlockSpec((sb, 1, 2, N_HEADS, DK, DV), lambda b: (b, 0, 0, 0, 0, 0))
    if s0 is not None:
        in_specs += [state_spec]
        args += [s0]
    out_shape = [jax.ShapeDtypeStruct((B, T, WV), BF16)]
    out_specs = [pl.BlockSpec((sb, T, WV), lambda b: (b, 0, 0))]
    if want_state:
        out_shape += [jax.ShapeDtypeStruct((B, 1, 2, N_HEADS, DK, DV), F32)]
        out_specs += [state_spec]
    name = ("gla" if is_gla else "ret") + ("_init" if s0 is not None else "") + "_scan"
    return pl.pallas_call(
        functools.partial(_scan_kernel, is_gla, s0 is not None, want_state, n_chunks, sb, unroll),
        grid=(B // sb,),
        in_specs=in_specs,
        out_specs=out_specs,
        out_shape=out_shape,
        scratch_shapes=[pltpu.VMEM((sb, N_HEADS // 2, 2, 2 * DV, LANES), F32),
                        pltpu.VMEM((sb, T, WV), F32)],
        compiler_params=pltpu.CompilerParams(
            dimension_semantics=("arbitrary",), vmem_limit_bytes=VMEM_LIMIT),
        name=name,
    )(*args)


def _outproj_kernel(x_ref, mod_ref, or_ref, og_ref, zr_ref, zg_ref, gw_ref, fw_ref, wout_ref, y_ref):
    gate = mod_ref[0, :, 2 * D_MODEL:3 * D_MODEL]
    gw = gw_ref[...]
    fw = fw_ref[...]
    for sub in range(TOKEN_BLOCK // OUTPROJ_SUB):
        rows = slice(sub * OUTPROJ_SUB, (sub + 1) * OUTPROJ_SUB)
        parts = []
        for h in range(N_HEADS):
            cols = slice(h * DV, (h + 1) * DV)
            t = or_ref[0, rows, cols].astype(F32)
            mu = jnp.mean(t, axis=-1, keepdims=True)
            dlt = t - mu
            var = jnp.mean(dlt * dlt, axis=-1, keepdims=True)
            n = dlt * lax.rsqrt(var + EPS)
            parts.append((n * zr_ref[0, rows, cols].astype(F32)).astype(BF16))
        for h in range(N_HEADS):
            cols = slice(h * DV, (h + 1) * DV)
            t = og_ref[0, rows, cols].astype(F32)
            n = t * lax.rsqrt(jnp.mean(t * t, axis=-1, keepdims=True) + EPS) * gw
            parts.append((n * zg_ref[0, rows, cols].astype(F32)).astype(BF16))
        mixed = jnp.concatenate(parts, axis=-1)
        out = _dot(mixed, wout_ref[...])
        y = x_ref[0, rows, :] + gate * out
        yn = y * lax.rsqrt(jnp.mean(y * y, axis=-1, keepdims=True) + EPS)
        y_ref[0, rows, :] = yn * fw


def _outproj(x, mod, mod_row_fn, o_r, o_g, gz, gw, fw, wout):
    B, T, _ = x.shape
    nt = T // TOKEN_BLOCK
    const = lambda b, t: (0, 0)
    return pl.pallas_call(
        _outproj_kernel,
        grid=(B, nt),
        in_specs=[
            pl.BlockSpec((1, TOKEN_BLOCK, D_MODEL), lambda b, t: (b, t, 0)),
            pl.BlockSpec((1, 1, 3 * D_MODEL), lambda b, t: (mod_row_fn(b), 0, 0)),
            pl.BlockSpec((1, TOKEN_BLOCK, WV), lambda b, t: (b, t, 0)),
            pl.BlockSpec((1, TOKEN_BLOCK, WV), lambda b, t: (b, t, 0)),
            pl.BlockSpec((1, TOKEN_BLOCK, WV), lambda b, t: (b, t, 0)),
            pl.BlockSpec((1, TOKEN_BLOCK, WV), lambda b, t: (b, t, 1)),
            pl.BlockSpec((1, DV), const),
            pl.BlockSpec((1, D_MODEL), const),
            pl.BlockSpec(wout.shape, const),
        ],
        out_specs=pl.BlockSpec((1, TOKEN_BLOCK, D_MODEL), lambda b, t: (b, t, 0)),
        out_shape=jax.ShapeDtypeStruct((B, T, D_MODEL), F32),
        compiler_params=pltpu.CompilerParams(
            dimension_semantics=("parallel", "parallel"), vmem_limit_bytes=VMEM_LIMIT),
        name="outproj",
    )(x, mod, o_r, o_g, gz, gz, gw, fw, wout)


def _rope_tables(n_tokens):
    rows = n_tokens // GRID_W
    rr, cc = jnp.meshgrid(jnp.arange(rows), jnp.arange(GRID_W), indexing="ij")
    rr = rr.reshape(-1).astype(F32)
    cc = cc.reshape(-1).astype(F32)
    n_freq = DK // 4
    inv = ROPE_BASE ** (-jnp.arange(n_freq, dtype=F32) / n_freq)
    ang = jnp.concatenate([rr[:, None] * inv, cc[:, None] * inv], axis=-1)
    cos, sin = jnp.cos(ang), jnp.sin(ang)
    cos_t = jnp.tile(jnp.concatenate([cos, cos], axis=-1), (1, LANES // DK))
    sin_t = jnp.tile(jnp.concatenate([-sin, sin], axis=-1), (1, LANES // DK))
    return cos_t, sin_t


def _prep_weights(w_in, gla_w_alpha, gla_b_alpha):
    q_r, k_r = w_in[:, 0:QK], w_in[:, QK:2 * QK]
    o = 2 * QK
    v_r, z_r = w_in[:, o:o + WV], w_in[:, o + WV:o + 2 * WV]
    o += 2 * WV
    q_g, k_g = w_in[:, o:o + QK], w_in[:, o + QK:o + 2 * QK]
    o += 2 * QK
    v_g, z_g = w_in[:, o:o + WV], w_in[:, o + WV:o + 2 * WV]
    o += 2 * WV
    lr = w_in[:, o:o + 2 * GLA_RANK]
    wqk = jnp.concatenate([q_r, k_r, q_g, k_g], axis=1).astype(BF16)
    wz = jnp.concatenate([z_r, z_g], axis=1).astype(BF16)
    wvt = jnp.concatenate([v_r, v_g], axis=1).T.astype(BF16)
    wlr = lr.T.astype(BF16)
    wa = jnp.zeros((2 * GLA_RANK, 2 * QK), F32)
    wa = wa.at[0:GLA_RANK, 0:QK].set(gla_w_alpha[0])
    wa = wa.at[GLA_RANK:2 * GLA_RANK, QK:2 * QK].set(gla_w_alpha[1])
    ba = jnp.concatenate([gla_b_alpha[0], gla_b_alpha[1]]).reshape(1, 2 * QK)
    return wqk, wz, wvt, wlr, wa.astype(BF16), ba


def kernel(x_prompt, x_sample, c, state_ret, state_gla, c_ctx, w_mod, b_mod, w_in, ret_log_decay,
           gla_w_alpha, gla_b_alpha, gla_norm_w, w_out, final_norm_w):
    assert w_mod.shape[0] == 1, "single-layer model"
    b_ctx, t_ctx, _ = x_prompt.shape
    b_dec, t_dec, _ = x_sample.shape
    assert t_ctx % SCAN_CHUNK == 0 and t_dec % TOKEN_BLOCK == 0 and TOKEN_BLOCK % t_ctx == 0
    assert 1 + b_dec <= MOD_ROWS

    cond = jnp.concatenate(
        [c_ctx[None, :], c, jnp.zeros((MOD_ROWS - 1 - b_dec, D_MODEL), F32)], axis=0)
    mod = _modulation(cond, w_mod[0], b_mod[0]).reshape(MOD_ROWS, 1, 3 * D_MODEL)

    weights = _prep_weights(w_in[0], gla_w_alpha[0], gla_b_alpha[0])
    wout = w_out[0].astype(BF16)
    gw = gla_norm_w[0].reshape(1, DV)
    fw = final_norm_w.reshape(1, D_MODEL)
    ld = ret_log_decay[0]

    per_blk = TOKEN_BLOCK // t_ctx
    xp = x_prompt.reshape(b_ctx // per_blk, TOKEN_BLOCK, D_MODEL)
    ctx_row = lambda b: 0
    qk, gz, vt, lg = _inproj(xp, mod, ctx_row, weights, None)
    qk = qk.reshape(b_ctx, t_ctx, 4 * QK)
    vt = vt.reshape(b_ctx, t_ctx // SCAN_CHUNK, 2 * WV, SCAN_CHUNK)
    lg = lg.reshape(b_ctx, t_ctx, 2 * QK)
    o_r, new_ret = _scan(False, qk, vt, None, ld, None, True, CTX_SEQ_BLOCK, 1)
    o_g, new_gla = _scan(True, qk, vt, lg, None, None, True, CTX_SEQ_BLOCK, 1)
    y_prompt = _outproj(xp, mod, ctx_row, o_r.reshape(xp.shape[0], TOKEN_BLOCK, WV),
                        o_g.reshape(xp.shape[0], TOKEN_BLOCK, WV), gz, gw, fw, wout)
    y_prompt = y_prompt.reshape(b_ctx, t_ctx, D_MODEL)

    dec_row = lambda b: b + 1
    rope = _rope_tables(t_dec)
    qk, gz, vt, lg = _inproj(x_sample, mod, dec_row, weights, rope)
    (o_r,) = _scan(False, qk, vt, None, ld, state_ret, False, 1, DEC_UNROLL)
    (o_g,) = _scan(True, qk, vt, lg, None, state_gla, False, 1, DEC_UNROLL)
    y_sample = _outproj(x_sample, mod, dec_row, o_r, o_g, gz, gw, fw, wout)

    return (y_prompt, y_sample, new_ret, new_gla)
```

```python
import functools

import jax
import jax.numpy as jnp
from jax import lax
from jax.experimental import pallas as pl
from jax.experimental.pallas import tpu as pltpu

F32 = jnp.float32
BF16 = jnp.bfloat16

D_MODEL = 1024
N_HEADS = 4
DK = 64
DV = 128
QK = N_HEADS * DK
WV = N_HEADS * DV
GLA_RANK = 16
GLA_TAU = 16.0
EPS = 1e-6
GRID_W = 64
ROPE_BASE = 10000.0

LANES = 128
SCAN_CHUNK = 128
CTX_SEQ_BLOCK = 4
DEC_UNROLL = 4
TOKEN_BLOCK = 1024
INPROJ_SUB = 256
OUTPROJ_SUB = 256
MOD_ROWS = 16
MOD_COL_BLOCK = 512
VMEM_LIMIT = 48 * 1024 * 1024

_NT = (((1,), (1,)), ((), ()))
_TN = (((0,), (0,)), ((), ()))


def _dot(a, b):
    return jnp.dot(a, b, preferred_element_type=F32)


def _dot_nt(a, b):
    return lax.dot_general(a, b, _NT, preferred_element_type=F32)


def _silu(x):
    return x * jax.nn.sigmoid(x)


def _mod_kernel(c_ref, w_ref, b_ref, o_ref):
    s = _silu(c_ref[...]).astype(BF16)
    o_ref[...] = _dot(s, w_ref[...].astype(BF16)) + b_ref[...]


def _modulation(cond, w_mod, b_mod):
    n_col = w_mod.shape[1]
    return pl.pallas_call(
        _mod_kernel,
        grid=(n_col // MOD_COL_BLOCK,),
        in_specs=[
            pl.BlockSpec((MOD_ROWS, D_MODEL), lambda j: (0, 0)),
            pl.BlockSpec((D_MODEL, MOD_COL_BLOCK), lambda j: (0, j)),
            pl.BlockSpec((1, MOD_COL_BLOCK), lambda j: (0, j)),
        ],
        out_specs=pl.BlockSpec((MOD_ROWS, MOD_COL_BLOCK), lambda j: (0, j)),
        out_shape=jax.ShapeDtypeStruct((MOD_ROWS, n_col), F32),
        compiler_params=pltpu.CompilerParams(vmem_limit_bytes=VMEM_LIMIT),
        name="modulation",
    )(cond, w_mod, b_mod.reshape(1, n_col))


def _rope_tile(t, cos, sin_signed):
    lane = lax.broadcasted_iota(jnp.int32, t.shape, 1)
    first_half = (lane & 32) == 0
    swapped = jnp.where(first_half, pltpu.roll(t, LANES - 32, 1), pltpu.roll(t, 32, 1))
    return t * cos + swapped * sin_signed


def _inproj_kernel(use_rope, x_ref, mod_ref, wqk_ref, wz_ref, wvt_ref, wlr_ref, wa_ref, ba_ref,
                   *rest):
    if use_rope:
        cos_ref, sin_ref, qk_ref, z_ref, vt_ref, lg_ref = rest
    else:
        qk_ref, z_ref, vt_ref, lg_ref = rest
    shift = mod_ref[0, :, 0:D_MODEL]
    scale1 = 1.0 + mod_ref[0, :, D_MODEL:2 * D_MODEL]

    for sub in range(TOKEN_BLOCK // INPROJ_SUB):
        rows = slice(sub * INPROJ_SUB, (sub + 1) * INPROJ_SUB)
        x = x_ref[0, rows, :]
        inv = lax.rsqrt(jnp.mean(x * x, axis=-1, keepdims=True) + EPS)
        hb = ((x * inv) * scale1 + shift).astype(BF16)

        lr_t = _dot_nt(wlr_ref[...], hb).astype(BF16)

        res = _dot(hb, wqk_ref[...])
        for half in range(2):
            for t in range(4):
                col = half * 2 * QK + t * LANES
                tile = res[:, col:col + LANES]
                is_q = t < 2
                if (half == 0 and not is_q) or (half == 1 and is_q):
                    tile = tile * (DK ** -0.5)
                if use_rope and half == 0:
                    tile = _rope_tile(tile, cos_ref[rows, :], sin_ref[rows, :])
                qk_ref[0, rows, col:col + LANES] = tile.astype(BF16)

        logit = lax.dot_general(lr_t, wa_ref[...], _TN,
                                preferred_element_type=F32) + ba_ref[...]
        log_sig = jnp.minimum(logit, 0.0) - jnp.log1p(jnp.exp(-jnp.abs(logit)))
        lg_ref[0, rows, :] = log_sig * (1.0 / GLA_TAU)

        z_ref[0, rows, :] = _silu(_dot(hb, wz_ref[...])).astype(BF16)

        vt = _dot_nt(wvt_ref[...], hb)
        for j in range(INPROJ_SUB // SCAN_CHUNK):
            jj = sub * (INPROJ_SUB // SCAN_CHUNK) + j
            vt_ref[0, jj] = vt[:, j * SCAN_CHUNK:(j + 1) * SCAN_CHUNK].astype(BF16)


def _inproj(x, mod, mod_row_fn, weights, rope):
    B, T, _ = x.shape
    wqk, wz, wvt, wlr, wa, ba = weights
    nt = T // TOKEN_BLOCK
    cpb = TOKEN_BLOCK // SCAN_CHUNK
    const = lambda b, t: (0, 0)
    in_specs = [
        pl.BlockSpec((1, TOKEN_BLOCK, D_MODEL), lambda b, t: (b, t, 0)),
        pl.BlockSpec((1, 1, 3 * D_MODEL), lambda b, t: (mod_row_fn(b), 0, 0)),
        pl.BlockSpec(wqk.shape, const),
        pl.BlockSpec(wz.shape, const),
        pl.BlockSpec(wvt.shape, const),
        pl.BlockSpec(wlr.shape, const),
        pl.BlockSpec(wa.shape, const),
        pl.BlockSpec(ba.shape, const),
    ]
    args = [x, mod, wqk, wz, wvt, wlr, wa, ba]
    if rope is not None:
        in_specs += [pl.BlockSpec((TOKEN_BLOCK, LANES), lambda b, t: (t, 0))] * 2
        args += list(rope)
    out_shape = (
        jax.ShapeDtypeStruct((B, T, 4 * QK), BF16),
        jax.ShapeDtypeStruct((B, T, 2 * WV), BF16),
        jax.ShapeDtypeStruct((B, T // SCAN_CHUNK, 2 * WV, SCAN_CHUNK), BF16),
        jax.ShapeDtypeStruct((B, T, 2 * QK), F32),
    )
    out_specs = (
        pl.BlockSpec((1, TOKEN_BLOCK, 4 * QK), lambda b, t: (b, t, 0)),
        pl.BlockSpec((1, TOKEN_BLOCK, 2 * WV), lambda b, t: (b, t, 0)),
        pl.BlockSpec((1, cpb, 2 * WV, SCAN_CHUNK), lambda b, t: (b, t, 0, 0)),
        pl.BlockSpec((1, TOKEN_BLOCK, 2 * QK), lambda b, t: (b, t, 0)),
    )
    return pl.pallas_call(
        functools.partial(_inproj_kernel, rope is not None),
        grid=(B, nt),
        in_specs=in_specs,
        out_specs=out_specs,
        out_shape=out_shape,
        compiler_params=pltpu.CompilerParams(
            dimension_semantics=("parallel", "parallel"), vmem_limit_bytes=VMEM_LIMIT),
        name="inproj_rope" if rope is not None else "inproj",
    )(*args)


def _scan_kernel(is_gla, has_init, want_state, n_chunks, seq_blk, unroll, *refs):
    refs = list(refs)
    q_ref, k_ref, vt_ref = refs[:3]
    pos = 3
    if is_gla:
        lgf_ref, lgb_ref = refs[pos:pos + 2]
        pos += 2
    else:
        ld_ref = refs[pos]
        pos += 1
    if has_init:
        s0_ref = refs[pos]
        pos += 1
    o_ref = refs[pos]
    pos += 1
    if want_state:
        sn_ref = refs[pos]
        pos += 1
    st_ref, oacc_ref = refs[pos:pos + 2]

    C = SCAN_CHUNK
    mid = C // 2
    n_pairs = N_HEADS // 2
    lane = lax.broadcasted_iota(jnp.int32, (1, LANES), 1)
    head_mask = (lane < DK, lane >= DK)
    ri = lax.broadcasted_iota(jnp.int32, (C, C), 0)
    ci = lax.broadcasted_iota(jnp.int32, (C, C), 1)
    tpos = lax.broadcasted_iota(jnp.int32, (C, 1), 0).astype(F32)
    tri = ((ri >= ci), (ci >= ri))

    if has_init:
        for s in range(seq_blk):
            for p in range(n_pairs):
                for d in range(2):
                    s0 = s0_ref[s, 0, d]
                    both = jnp.concatenate([s0[2 * p], s0[2 * p + 1]], axis=0).T
                    st_ref[s, p, d, 0:DV] = both
                    st_ref[s, p, d, DV:2 * DV] = both
    else:
        st_ref[...] = jnp.zeros(st_ref.shape, F32)

    if is_gla:
        tri_ones = [t.astype(BF16) for t in tri]
    else:
        q_mul, k_mul, ret_decay, score_mul = {}, {}, {}, {}
        for p in range(n_pairs):
            for d in range(2):
                ld0 = ld_ref[d, 2 * p]
                ld1 = ld_ref[d, 2 * p + 1]
                ldr = jnp.where(lane < DK, ld0, ld1)
                if d == 0:
                    q_mul[p, d] = jnp.exp(ldr * (tpos + 1.0))
                    k_mul[p, d] = jnp.exp(ldr * (C - 1.0 - tpos))
                    dist = (ri - ci).astype(F32)
                else:
                    q_mul[p, d] = jnp.exp(ldr * (C - tpos))
                    k_mul[p, d] = jnp.exp(ldr * tpos)
                    dist = (ci - ri).astype(F32)
                ret_decay[p, d] = jnp.exp(ldr * float(C))
                score_mul[p, d] = [jnp.where(tri[d], jnp.exp(ldh * dist), 0.0) for ldh in (ld0, ld1)]

    zero = jnp.zeros((), BF16)
    groups = [(s, p, d) for s in range(seq_blk) for p in range(n_pairs) for d in range(2)]

    def run_block(n0, accumulate):
        chains = {}
        for s, p, d in groups:
            ptile = slice(p * LANES, (p + 1) * LANES)
            for u in range(unroll):
                n = n0 + u
                c = n if d == 0 else n_chunks - 1 - n
                t = dict(c=c, rows=pl.ds(pl.multiple_of(c * C, C), C))
                t["q"] = q_ref[s, t["rows"], ptile]
                t["k"] = k_ref[s, t["rows"], ptile]
                if is_gla:
                    lg = (lgf_ref if d == 0 else lgb_ref)[s, t["rows"], ptile]
                    hi = lg.astype(BF16)
                    lo = (lg - hi.astype(F32)).astype(BF16)
                    bb = _dot(tri_ones[d], jnp.concatenate([hi, lo], axis=1))
                    t["b"] = bb[:, :LANES] + bb[:, LANES:]
                chains[s, p, d, u] = t

        for (s, p, d, u), t in chains.items():
            q, k = t["q"], t["k"]
            if is_gla:
                b = t["b"]
                r = b[mid - 1:mid] if d == 0 else b[mid:mid + 1]
                bl = b[C - 1:C] if d == 0 else b[0:1]
                qs = q.astype(F32) * jnp.exp(b - r)
                ks = k.astype(F32) * jnp.exp(r - b)
                q_sc = qs.astype(BF16)
                k_sc = ks.astype(BF16)
                q_dec = (qs * jnp.exp(r)).astype(BF16)
                t["k_dec"] = (ks * jnp.exp(bl - r)).astype(BF16)
                t["decay"] = jnp.exp(bl)
            else:
                q_sc = q
                k_sc = k
                q_dec = (q.astype(F32) * q_mul[p, d]).astype(BF16)
                t["k_dec"] = (k.astype(F32) * k_mul[p, d]).astype(BF16)
                t["decay"] = ret_decay[p, d]
            t["q_dec"] = [jnp.where(head_mask[hh], q_dec, zero) for hh in range(2)]
            k_heads = jnp.concatenate([jnp.where(head_mask[hh], k_sc, zero) for hh in range(2)], axis=0)
            t["sc"] = _dot_nt(q_sc, k_heads)

        for (s, p, d, u), t in chains.items():
            t["vt"] = vt_ref[s, t["c"], p * 2 * DV:(p + 1) * 2 * DV, :]
            t["kv"] = _dot(t["vt"], t["k_dec"])

        for (s, p, d, u), t in chains.items():
            t["scb"] = []
            for hh in range(2):
                sc = t["sc"][:, hh * C:(hh + 1) * C]
                if is_gla:
                    sc = jnp.where(tri[d], sc, 0.0)
                else:
                    sc = sc * score_mul[p, d][hh]
                t["scb"].append(sc.astype(BF16))

        state = {g: st_ref[g] for g in groups}
        for u in range(unroll):
            for g in groups:
                t = chains[g + (u,)]
                st = state[g]
                stb = st.astype(BF16)
                t["o"] = []
                for hh in range(2):
                    hrows = slice(hh * DV, (hh + 1) * DV)
                    lhs = jnp.concatenate([t["scb"][hh], t["q_dec"][hh]], axis=1)
                    rhs = jnp.concatenate([t["vt"][hrows], stb[hrows]], axis=1)
                    t["o"].append(_dot_nt(lhs, rhs))
                state[g] = st * t["decay"] + t["kv"]
        for key, st in state.items():
            st_ref[key] = st

        for (s, p, d, u), t in chains.items():
            for hh in range(2):
                cols = slice((2 * p + hh) * DV, (2 * p + hh + 1) * DV)
                if accumulate:
                    o_ref[s, t["rows"], cols] = (oacc_ref[s, t["rows"], cols] + t["o"][hh]).astype(BF16)
                else:
                    oacc_ref[s, t["rows"], cols] = t["o"][hh]

    def loop_body(accumulate, base):
        def body(m, carry):
            run_block(base + m * unroll, accumulate)
            return carry
        return body

    half = n_chunks // 2
    assert half % unroll == 0
    lax.fori_loop(0, half // unroll, loop_body(False, 0), 0)
    lax.fori_loop(0, half // unroll, loop_body(True, half), 0)

    if want_state:
        for s, p, d in groups:
            for hh in range(2):
                sn_ref[s, 0, d, 2 * p + hh] = st_ref[s, p, d, hh * DV:(hh + 1) * DV].T[hh * DK:(hh + 1) * DK, :]


def _scan(is_gla, qk, vt, lg, ld, s0, want_state, seq_blk, unroll):
    B, T, _ = qk.shape
    n_chunks = T // SCAN_CHUNK
    assert n_chunks % 2 == 0 and B % seq_blk == 0
    sb = seq_blk
    grp = 1 if is_gla else 0
    in_specs = [
        pl.BlockSpec((sb, T, QK), lambda b: (b, 0, 2 * grp)),
        pl.BlockSpec((sb, T, QK), lambda b: (b, 0, 2 * grp + 1)),
        pl.BlockSpec((sb, n_chunks, WV, SCAN_CHUNK), lambda b: (b, 0, grp, 0)),
    ]
    args = [qk, qk, vt]
    if is_gla:
        in_specs += [
            pl.BlockSpec((sb, T, QK), lambda b: (b, 0, 0)),
            pl.BlockSpec((sb, T, QK), lambda b: (b, 0, 1)),
        ]
        args += [lg, lg]
    else:
        in_specs += [pl.BlockSpec(memory_space=pltpu.SMEM)]
        args += [ld]
    state_spec = pl.BlockSpec((sb, 1, 2, N_HEADS, DK, DV), lambda b: (b, 0, 0, 0, 0, 0))
    if s0 is not None:
        in_specs += [state_spec]
        args += [s0]
    out_shape = [jax.ShapeDtypeStruct((B, T, WV), BF16)]
    out_specs = [pl.BlockSpec((sb, T, WV), lambda b: (b, 0, 0))]
    if want_state:
        out_shape += [jax.ShapeDtypeStruct((B, 1, 2, N_HEADS, DK, DV), F32)]
        out_specs += [state_spec]
    name = ("gla" if is_gla else "ret") + ("_init" if s0 is not None else "") + "_scan"
    return pl.pallas_call(
        functools.partial(_scan_kernel, is_gla, s0 is not None, want_state, n_chunks, sb, unroll),
        grid=(B // sb,),
        in_specs=in_specs,
        out_specs=out_specs,
        out_shape=out_shape,
        scratch_shapes=[pltpu.VMEM((sb, N_HEADS // 2, 2, 2 * DV, LANES), F32),
                        pltpu.VMEM((sb, T, WV), F32)],
        compiler_params=pltpu.CompilerParams(
            dimension_semantics=("arbitrary",), vmem_limit_bytes=VMEM_LIMIT),
        name=name,
    )(*args)


def _outproj_kernel(x_ref, mod_ref, or_ref, og_ref, zr_ref, zg_ref, gw_ref, fw_ref, wout_ref, y_ref):
    gate = mod_ref[0, :, 2 * D_MODEL:3 * D_MODEL]
    gw = gw_ref[...]
    fw = fw_ref[...]
    for sub in range(TOKEN_BLOCK // OUTPROJ_SUB):
        rows = slice(sub * OUTPROJ_SUB, (sub + 1) * OUTPROJ_SUB)
        parts = []
        for h in range(N_HEADS):
            cols = slice(h * DV, (h + 1) * DV)
            t = or_ref[0, rows, cols].astype(F32)
            mu = jnp.mean(t, axis=-1, keepdims=True)
            dlt = t - mu
            var = jnp.mean(dlt * dlt, axis=-1, keepdims=True)
            n = dlt * lax.rsqrt(var + EPS)
            parts.append((n * zr_ref[0, rows, cols].astype(F32)).astype(BF16))
        for h in range(N_HEADS):
            cols = slice(h * DV, (h + 1) * DV)
            t = og_ref[0, rows, cols].astype(F32)
            n = t * lax.rsqrt(jnp.mean(t * t, axis=-1, keepdims=True) + EPS) * gw
            parts.append((n * zg_ref[0, rows, cols].astype(F32)).astype(BF16))
        mixed = jnp.concatenate(parts, axis=-1)
        out = _dot(mixed, wout_ref[...])
        y = x_ref[0, rows, :] + gate * out
        yn = y * lax.rsqrt(jnp.mean(y * y, axis=-1, keepdims=True) + EPS)
        y_ref[0, rows, :] = yn * fw


def _outproj(x, mod, mod_row_fn, o_r, o_g, gz, gw, fw, wout):
    B, T, _ = x.shape
    nt = T // TOKEN_BLOCK
    const = lambda b, t: (0, 0)
    return pl.pallas_call(
        _outproj_kernel,
        grid=(B, nt),
        in_specs=[
            pl.BlockSpec((1, TOKEN_BLOCK, D_MODEL), lambda b, t: (b, t, 0)),
            pl.BlockSpec((1, 1, 3 * D_MODEL), lambda b, t: (mod_row_fn(b), 0, 0)),
            pl.BlockSpec((1, TOKEN_BLOCK, WV), lambda b, t: (b, t, 0)),
            pl.BlockSpec((1, TOKEN_BLOCK, WV), lambda b, t: (b, t, 0)),
            pl.BlockSpec((1, TOKEN_BLOCK, WV), lambda b, t: (b, t, 0)),
            pl.BlockSpec((1, TOKEN_BLOCK, WV), lambda b, t: (b, t, 1)),
            pl.BlockSpec((1, DV), const),
            pl.BlockSpec((1, D_MODEL), const),
            pl.BlockSpec(wout.shape, const),
        ],
        out_specs=pl.BlockSpec((1, TOKEN_BLOCK, D_MODEL), lambda b, t: (b, t, 0)),
        out_shape=jax.ShapeDtypeStruct((B, T, D_MODEL), F32),
        compiler_params=pltpu.CompilerParams(
            dimension_semantics=("parallel", "parallel"), vmem_limit_bytes=VMEM_LIMIT),
        name="outproj",
    )(x, mod, o_r, o_g, gz, gz, gw, fw, wout)


def _rope_tables(n_tokens):
    rows = n_tokens // GRID_W
    rr, cc = jnp.meshgrid(jnp.arange(rows), jnp.arange(GRID_W), indexing="ij")
    rr = rr.reshape(-1).astype(F32)
    cc = cc.reshape(-1).astype(F32)
    n_freq = DK // 4
    inv = ROPE_BASE ** (-jnp.arange(n_freq, dtype=F32) / n_freq)
    ang = jnp.concatenate([rr[:, None] * inv, cc[:, None] * inv], axis=-1)
    cos, sin = jnp.cos(ang), jnp.sin(ang)
    cos_t = jnp.tile(jnp.concatenate([cos, cos], axis=-1), (1, LANES // DK))
    sin_t = jnp.tile(jnp.concatenate([-sin, sin], axis=-1), (1, LANES // DK))
    return cos_t, sin_t


def _prep_weights(w_in, gla_w_alpha, gla_b_alpha):
    q_r, k_r = w_in[:, 0:QK], w_in[:, QK:2 * QK]
    o = 2 * QK
    v_r, z_r = w_in[:, o:o + WV], w_in[:, o + WV:o + 2 * WV]
    o += 2 * WV
    q_g, k_g = w_in[:, o:o + QK], w_in[:, o + QK:o + 2 * QK]
    o += 2 * QK
    v_g, z_g = w_in[:, o:o + WV], w_in[:, o + WV:o + 2 * WV]
    o += 2 * WV
    lr = w_in[:, o:o + 2 * GLA_RANK]
    wqk = jnp.concatenate([q_r, k_r, q_g, k_g], axis=1).astype(BF16)
    wz = jnp.concatenate([z_r, z_g], axis=1).astype(BF16)
    wvt = jnp.concatenate([v_r, v_g], axis=1).T.astype(BF16)
    wlr = lr.T.astype(BF16)
    wa = jnp.zeros((2 * GLA_RANK, 2 * QK), F32)
    wa = wa.at[0:GLA_RANK, 0:QK].set(gla_w_alpha[0])
    wa = wa.at[GLA_RANK:2 * GLA_RANK, QK:2 * QK].set(gla_w_alpha[1])
    ba = jnp.concatenate([gla_b_alpha[0], gla_b_alpha[1]]).reshape(1, 2 * QK)
    return wqk, wz, wvt, wlr, wa.astype(BF16), ba


def kernel(x_prompt, x_sample, c, state_ret, state_gla, c_ctx, w_mod, b_mod, w_in, ret_log_decay,
           gla_w_alpha, gla_b_alpha, gla_norm_w, w_out, final_norm_w):
    assert w_mod.shape[0] == 1, "single-layer model"
    b_ctx, t_ctx, _ = x_prompt.shape
    b_dec, t_dec, _ = x_sample.shape
    assert t_ctx % SCAN_CHUNK == 0 and t_dec % TOKEN_BLOCK == 0 and TOKEN_BLOCK % t_ctx == 0
    assert 1 + b_dec <= MOD_ROWS

    cond = jnp.concatenate(
        [c_ctx[None, :], c, jnp.zeros((MOD_ROWS - 1 - b_dec, D_MODEL), F32)], axis=0)
    mod = _modulation(cond, w_mod[0], b_mod[0]).reshape(MOD_ROWS, 1, 3 * D_MODEL)

    weights = _prep_weights(w_in[0], gla_w_alpha[0], gla_b_alpha[0])
    wout = w_out[0].astype(BF16)
    gw = gla_norm_w[0].reshape(1, DV)
    fw = final_norm_w.reshape(1, D_MODEL)
    ld = ret_log_decay[0]

    per_blk = TOKEN_BLOCK // t_ctx
    xp = x_prompt.reshape(b_ctx // per_blk, TOKEN_BLOCK, D_MODEL)
    ctx_row = lambda b: 0
    qk, gz, vt, lg = _inproj(xp, mod, ctx_row, weights, None)
    qk = qk.reshape(b_ctx, t_ctx, 4 * QK)
    vt = vt.reshape(b_ctx, t_ctx // SCAN_CHUNK, 2 * WV, SCAN_CHUNK)
    lg = lg.reshape(b_ctx, t_ctx, 2 * QK)
    o_r, new_ret = _scan(False, qk, vt, None, ld, None, True, CTX_SEQ_BLOCK, 1)
    o_g, new_gla = _scan(True, qk, vt, lg, None, None, True, CTX_SEQ_BLOCK, 1)
    y_prompt = _outproj(xp, mod, ctx_row, o_r.reshape(xp.shape[0], TOKEN_BLOCK, WV),
                        o_g.reshape(xp.shape[0], TOKEN_BLOCK, WV), gz, gw, fw, wout)
    y_prompt = y_prompt.reshape(b_ctx, t_ctx, D_MODEL)

    dec_row = lambda b: b + 1
    rope = _rope_tables(t_dec)
    qk, gz, vt, lg = _inproj(x_sample, mod, dec_row, weights, rope)
    (o_r,) = _scan(False, qk, vt, None, ld, state_ret, False, 1, DEC_UNROLL)
    (o_g,) = _scan(True, qk, vt, lg, None, state_gla, False, 1, DEC_UNROLL)
    y_sample = _outproj(x_sample, mod, dec_row, o_r, o_g, gz, gw, fw, wout)

    return (y_prompt, y_sample, new_ret, new_gla)
```

```python
import functools

import jax
import jax.numpy as jnp
from jax import lax
from jax.experimental import pallas as pl
from jax.experimental.pallas import tpu as pltpu

F32 = jnp.float32
BF16 = jnp.bfloat16

D_MODEL = 1024
N_HEADS = 4
DK = 64
DV = 128
QK = N_HEADS * DK
WV = N_HEADS * DV
GLA_RANK = 16
GLA_TAU = 16.0
EPS = 1e-6
GRID_W = 64
ROPE_BASE = 10000.0

LANES = 128
SCAN_CHUNK = 128
CTX_SEQ_BLOCK = 4
DEC_UNROLL = 4
TOKEN_BLOCK = 1024
INPROJ_SUB = 256
OUTPROJ_SUB = 256
MOD_ROWS = 16
MOD_COL_BLOCK = 512
VMEM_LIMIT = 48 * 1024 * 1024

_NT = (((1,), (1,)), ((), ()))
_TN = (((0,), (0,)), ((), ()))


def _dot(a, b):
    return jnp.dot(a, b, preferred_element_type=F32)


def _dot_nt(a, b):
    return lax.dot_general(a, b, _NT, preferred_element_type=F32)


def _silu(x):
    return x * jax.nn.sigmoid(x)


def _mod_kernel(c_ref, w_ref, b_ref, o_ref):
    s = _silu(c_ref[...]).astype(BF16)
    o_ref[...] = _dot(s, w_ref[...].astype(BF16)) + b_ref[...]


def _modulation(cond, w_mod, b_mod):
    n_col = w_mod.shape[1]
    return pl.pallas_call(
        _mod_kernel,
        grid=(n_col // MOD_COL_BLOCK,),
        in_specs=[
            pl.BlockSpec((MOD_ROWS, D_MODEL), lambda j: (0, 0)),
            pl.BlockSpec((D_MODEL, MOD_COL_BLOCK), lambda j: (0, j)),
            pl.BlockSpec((1, MOD_COL_BLOCK), lambda j: (0, j)),
        ],
        out_specs=pl.BlockSpec((MOD_ROWS, MOD_COL_BLOCK), lambda j: (0, j)),
        out_shape=jax.ShapeDtypeStruct((MOD_ROWS, n_col), F32),
        compiler_params=pltpu.CompilerParams(vmem_limit_bytes=VMEM_LIMIT),
        name="modulation",
    )(cond, w_mod, b_mod.reshape(1, n_col))


def _rope_tile(t, cos, sin_signed):
    lane = lax.broadcasted_iota(jnp.int32, t.shape, 1)
    first_half = (lane & 32) == 0
    swapped = jnp.where(first_half, pltpu.roll(t, LANES - 32, 1), pltpu.roll(t, 32, 1))
    return t * cos + swapped * sin_signed


def _inproj_kernel(use_rope, x_ref, mod_ref, wqk_ref, wz_ref, wvt_ref, wa_ref, ba_ref, *rest):
    if use_rope:
        cos_ref, sin_ref, qk_ref, z_ref, vt_ref, lg_ref = rest
    else:
        qk_ref, z_ref, vt_ref, lg_ref = rest
    shift = mod_ref[0, :, 0:D_MODEL]
    scale1 = 1.0 + mod_ref[0, :, D_MODEL:2 * D_MODEL]

    def gate_logs(lr_t, rows):
        logit = lax.dot_general(lr_t, wa_ref[...], _TN, preferred_element_type=F32) + ba_ref[...]
        log_sig = jnp.minimum(logit, 0.0) - jnp.log(1.0 + jnp.exp(-jnp.abs(logit)))
        lg_ref[0, rows, :] = log_sig * (1.0 / GLA_TAU)

    for sub in range(TOKEN_BLOCK // INPROJ_SUB):
        rows = slice(sub * INPROJ_SUB, (sub + 1) * INPROJ_SUB)
        x = x_ref[0, rows, :]
        inv = lax.rsqrt(jnp.mean(x * x, axis=-1, keepdims=True) + EPS)
        hb = ((x * inv) * scale1 + shift).astype(BF16)

        lr_t = _dot_nt(wvt_ref[2 * WV:, :], hb).astype(BF16)

        res = _dot(hb, wqk_ref[...])
        for half in range(2):
            for t in range(4):
                col = half * 2 * QK + t * LANES
                tile = res[:, col:col + LANES]
                is_q = t < 2
                if (half == 0 and not is_q) or (half == 1 and is_q):
                    tile = tile * (DK ** -0.5)
                if use_rope and half == 0:
                    tile = _rope_tile(tile, cos_ref[rows, :], sin_ref[rows, :])
                qk_ref[0, rows, col:col + LANES] = tile.astype(BF16)

        gate_logs(lr_t, rows)

        z_ref[0, rows, :] = _silu(_dot(hb, wz_ref[...])).astype(BF16)

        vt = _dot_nt(wvt_ref[0:2 * WV, :], hb)
        for j in range(INPROJ_SUB // SCAN_CHUNK):
            jj = sub * (INPROJ_SUB // SCAN_CHUNK) + j
            vt_ref[0, jj] = vt[:, j * SCAN_CHUNK:(j + 1) * SCAN_CHUNK].astype(BF16)


def _inproj(x, mod, mod_row_fn, weights, rope):
    B, T, _ = x.shape
    wqk, wz, wvt, wa, ba = weights
    nt = T // TOKEN_BLOCK
    cpb = TOKEN_BLOCK // SCAN_CHUNK
    const = lambda b, t: (0, 0)
    in_specs = [
        pl.BlockSpec((1, TOKEN_BLOCK, D_MODEL), lambda b, t: (b, t, 0)),
        pl.BlockSpec((1, 1, 3 * D_MODEL), lambda b, t: (mod_row_fn(b), 0, 0)),
        pl.BlockSpec(wqk.shape, const),
        pl.BlockSpec(wz.shape, const),
        pl.BlockSpec(wvt.shape, const),
        pl.BlockSpec(wa.shape, const),
        pl.BlockSpec(ba.shape, const),
    ]
    args = [x, mod, wqk, wz, wvt, wa, ba]
    if rope is not None:
        in_specs += [pl.BlockSpec((TOKEN_BLOCK, LANES), lambda b, t: (t, 0))] * 2
        args += list(rope)
    out_shape = (
        jax.ShapeDtypeStruct((B, T, 4 * QK), BF16),
        jax.ShapeDtypeStruct((B, T, 2 * WV), BF16),
        jax.ShapeDtypeStruct((B, T // SCAN_CHUNK, 2 * WV, SCAN_CHUNK), BF16),
        jax.ShapeDtypeStruct((B, T, 2 * QK), F32),
    )
    out_specs = (
        pl.BlockSpec((1, TOKEN_BLOCK, 4 * QK), lambda b, t: (b, t, 0)),
        pl.BlockSpec((1, TOKEN_BLOCK, 2 * WV), lambda b, t: (b, t, 0)),
        pl.BlockSpec((1, cpb, 2 * WV, SCAN_CHUNK), lambda b, t: (b, t, 0, 0)),
        pl.BlockSpec((1, TOKEN_BLOCK, 2 * QK), lambda b, t: (b, t, 0)),
    )
    return pl.pallas_call(
        functools.partial(_inproj_kernel, rope is not None),
        grid=(B, nt),
        in_specs=in_specs,
        out_specs=out_specs,
        out_shape=out_shape,
        compiler_params=pltpu.CompilerParams(
            dimension_semantics=("parallel", "parallel"), vmem_limit_bytes=VMEM_LIMIT),
        name="inproj_rope" if rope is not None else "inproj",
    )(*args)


def _scan_kernel(is_gla, has_init, want_state, n_chunks, seq_blk, unroll, *refs):
    refs = list(refs)
    q_ref, k_ref, vt_ref = refs[:3]
    pos = 3
    if is_gla:
        lgf_ref, lgb_ref = refs[pos:pos + 2]
        pos += 2
    else:
        ld_ref = refs[pos]
        pos += 1
    if has_init:
        s0_ref = refs[pos]
        pos += 1
    o_ref = refs[pos]
    pos += 1
    if want_state:
        sn_ref = refs[pos]
        pos += 1
    st_ref, oacc_ref = refs[pos:pos + 2]

    C = SCAN_CHUNK
    mid = C // 2
    n_pairs = N_HEADS // 2
    lane = lax.broadcasted_iota(jnp.int32, (1, LANES), 1)
    head_mask = (lane < DK, lane >= DK)
    ri = lax.broadcasted_iota(jnp.int32, (C, C), 0)
    ci = lax.broadcasted_iota(jnp.int32, (C, C), 1)
    tpos = lax.broadcasted_iota(jnp.int32, (C, 1), 0).astype(F32)
    tri = ((ri >= ci), (ci >= ri))

    if has_init:
        for s in range(seq_blk):
            for p in range(n_pairs):
                for d in range(2):
                    s0 = s0_ref[s, 0, d]
                    both = jnp.concatenate([s0[2 * p], s0[2 * p + 1]], axis=0).T
                    st_ref[s, p, d, 0:DV] = both
                    st_ref[s, p, d, DV:2 * DV] = both
    else:
        st_ref[...] = jnp.zeros(st_ref.shape, F32)

    if is_gla:
        tri_ones = [t.astype(BF16) for t in tri]
    else:
        q_mul, k_mul, ret_decay, score_mul = {}, {}, {}, {}
        for p in range(n_pairs):
            for d in range(2):
                ld0 = ld_ref[d, 2 * p]
                ld1 = ld_ref[d, 2 * p + 1]
                ldr = jnp.where(lane < DK, ld0, ld1)
                if d == 0:
                    q_mul[p, d] = jnp.exp(ldr * (tpos + 1.0))
                    k_mul[p, d] = jnp.exp(ldr * (C - 1.0 - tpos))
                    dist = (ri - ci).astype(F32)
                else:
                    q_mul[p, d] = jnp.exp(ldr * (C - tpos))
                    k_mul[p, d] = jnp.exp(ldr * tpos)
                    dist = (ci - ri).astype(F32)
                ret_decay[p, d] = jnp.exp(ldr * float(C))
                score_mul[p, d] = [jnp.where(tri[d], jnp.exp(ldh * dist), 0.0) for ldh in (ld0, ld1)]

    zero = jnp.zeros((), BF16)
    groups = [(s, p, d) for s in range(seq_blk) for p in range(n_pairs) for d in range(2)]

    def run_block(n0, accumulate):
        chains = {}
        for s, p, d in groups:
            ptile = slice(p * LANES, (p + 1) * LANES)
            for u in range(unroll):
                n = n0 + u
                c = n if d == 0 else n_chunks - 1 - n
                t = dict(c=c, rows=pl.ds(pl.multiple_of(c * C, C), C))
                t["q"] = q_ref[s, t["rows"], ptile]
                t["k"] = k_ref[s, t["rows"], ptile]
                if is_gla:
                    lg = (lgf_ref if d == 0 else lgb_ref)[s, t["rows"], ptile]
                    hi = lg.astype(BF16)
                    lo = (lg - hi.astype(F32)).astype(BF16)
                    bb = _dot(tri_ones[d], jnp.concatenate([hi, lo], axis=1))
                    t["b"] = bb[:, :LANES] + bb[:, LANES:]
                chains[s, p, d, u] = t

        for (s, p, d, u), t in chains.items():
            q, k = t["q"], t["k"]
            if is_gla:
                b = t["b"]
                r = b[mid - 1:mid] if d == 0 else b[mid:mid + 1]
                bl = b[C - 1:C] if d == 0 else b[0:1]
                qs = q.astype(F32) * jnp.exp(b - r)
                ks = k.astype(F32) * jnp.exp(r - b)
                q_sc = qs.astype(BF16)
                k_sc = ks.astype(BF16)
                q_dec = (qs * jnp.exp(r)).astype(BF16)
                t["k_dec"] = (ks * jnp.exp(bl - r)).astype(BF16)
                t["decay"] = jnp.exp(bl)
            else:
                q_sc = q
                k_sc = k
                q_dec = (q.astype(F32) * q_mul[p, d]).astype(BF16)
                t["k_dec"] = (k.astype(F32) * k_mul[p, d]).astype(BF16)
                t["decay"] = ret_decay[p, d]
            t["q_dec"] = [jnp.where(head_mask[hh], q_dec, zero) for hh in range(2)]
            k_heads = jnp.concatenate([jnp.where(head_mask[hh], k_sc, zero) for hh in range(2)], axis=0)
            t["sc"] = _dot_nt(q_sc, k_heads)

        for (s, p, d, u), t in chains.items():
            t["vt"] = vt_ref[s, t["c"], p * 2 * DV:(p + 1) * 2 * DV, :]
            t["kv"] = _dot(t["vt"], t["k_dec"])

        for (s, p, d, u), t in chains.items():
            t["scb"] = []
            for hh in range(2):
                sc = t["sc"][:, hh * C:(hh + 1) * C]
                if is_gla:
                    sc = jnp.where(tri[d], sc, 0.0)
                else:
                    sc = sc * score_mul[p, d][hh]
                t["scb"].append(sc.astype(BF16))

        state = {g: st_ref[g] for g in groups}
        for u in range(unroll):
            for g in groups:
                t = chains[g + (u,)]
                st = state[g]
                stb = st.astype(BF16)
                t["o"] = []
                for hh in range(2):
                    hrows = slice(hh * DV, (hh + 1) * DV)
                    lhs = jnp.concatenate([t["scb"][hh], t["q_dec"][hh]], axis=1)
                    rhs = jnp.concatenate([t["vt"][hrows], stb[hrows]], axis=1)
                    t["o"].append(_dot_nt(lhs, rhs))
                state[g] = st * t["decay"] + t["kv"]
        for key, st in state.items():
            st_ref[key] = st

        for (s, p, d, u), t in chains.items():
            for hh in range(2):
                cols = slice((2 * p + hh) * DV, (2 * p + hh + 1) * DV)
                if accumulate:
                    o_ref[s, t["rows"], cols] = (oacc_ref[s, t["rows"], cols] + t["o"][hh]).astype(BF16)
                else:
                    oacc_ref[s, t["rows"], cols] = t["o"][hh]

    def loop_body(accumulate, base):
        def body(m, carry):
            run_block(base + m * unroll, accumulate)
            return carry
        return body

    half = n_chunks // 2
    assert half % unroll == 0
    lax.fori_loop(0, half // unroll, loop_body(False, 0), 0)
    lax.fori_loop(0, half // unroll, loop_body(True, half), 0)

    if want_state:
        for s, p, d in groups:
            for hh in range(2):
                sn_ref[s, 0, d, 2 * p + hh] = st_ref[s, p, d, hh * DV:(hh + 1) * DV].T[hh * DK:(hh + 1) * DK, :]


def _scan(is_gla, qk, vt, lg, ld, s0, want_state, seq_blk, unroll):
    B, T, _ = qk.shape
    n_chunks = T // SCAN_CHUNK
    assert n_chunks % 2 == 0 and B % seq_blk == 0
    sb = seq_blk
    grp = 1 if is_gla else 0
    in_specs = [
        pl.BlockSpec((sb, T, QK), lambda b: (b, 0, 2 * grp)),
        pl.BlockSpec((sb, T, QK), lambda b: (b, 0, 2 * grp + 1)),
        pl.BlockSpec((sb, n_chunks, WV, SCAN_CHUNK), lambda b: (b, 0, grp, 0)),
    ]
    args = [qk, qk, vt]
    if is_gla:
        in_specs += [
            pl.BlockSpec((sb, T, QK), lambda b: (b, 0, 0)),
            pl.BlockSpec((sb, T, QK), lambda b: (b, 0, 1)),
        ]
        args += [lg, lg]
    else:
        in_specs += [pl.BlockSpec(memory_space=pltpu.SMEM)]
        args += [ld]
    state_spec = pl.BlockSpec((sb, 1, 2, N_HEADS, DK, DV), lambda b: (b, 0, 0, 0, 0, 0))
    if s0 is not None:
        in_specs += [state_spec]
        args += [s0]
    out_shape = [jax.ShapeDtypeStruct((B, T, WV), BF16)]
    out_specs = [pl.BlockSpec((sb, T, WV), lambda b: (b, 0, 0))]
    if want_state:
        out_shape += [jax.ShapeDtypeStruct((B, 1, 2, N_HEADS, DK, DV), F32)]
        out_specs += [state_spec]
    name = ("gla" if is_gla else "ret") + ("_init" if s0 is not None else "") + "_scan"
    return pl.pallas_call(
        functools.partial(_scan_kernel, is_gla, s0 is not None, want_state, n_chunks, sb, unroll),
        grid=(B // sb,),
        in_specs=in_specs,
        out_specs=out_specs,
        out_shape=out_shape,
        scratch_shapes=[pltpu.VMEM((sb, N_HEADS // 2, 2, 2 * DV, LANES), F32),
                        pltpu.VMEM((sb, T, WV), F32)],
        compiler_params=pltpu.CompilerParams(
            dimension_semantics=("arbitrary",), vmem_limit_bytes=VMEM_LIMIT),
        name=name,
    )(*args)


def _outproj_kernel(x_ref, mod_ref, or_ref, og_ref, gz_ref, gw_ref, fw_ref, wout_ref, y_ref):
    gate = mod_ref[0, :, 2 * D_MODEL:3 * D_MODEL]
    gw = gw_ref[...]
    fw = fw_ref[...]
    for sub in range(TOKEN_BLOCK // OUTPROJ_SUB):
        rows = slice(sub * OUTPROJ_SUB, (sub + 1) * OUTPROJ_SUB)
        parts = []
        for h in range(N_HEADS):
            cols = slice(h * DV, (h + 1) * DV)
            t = or_ref[0, rows, cols].astype(F32)
            mu = jnp.mean(t, axis=-1, keepdims=True)
            dlt = t - mu
            var = jnp.mean(dlt * dlt, axis=-1, keepdims=True)
            n = dlt * lax.rsqrt(var + EPS)
            parts.append((n * gz_ref[0, rows, cols].astype(F32)).astype(BF16))
        for h in range(N_HEADS):
            cols = slice(h * DV, (h + 1) * DV)
            t = og_ref[0, rows, cols].astype(F32)
            n = t * lax.rsqrt(jnp.mean(t * t, axis=-1, keepdims=True) + EPS) * gw
            gcols = slice(WV + h * DV, WV + (h + 1) * DV)
            parts.append((n * gz_ref[0, rows, gcols].astype(F32)).astype(BF16))
        mixed = jnp.concatenate(parts, axis=-1)
        out = _dot(mixed, wout_ref[...])
        y = x_ref[0, rows, :] + gate * out
        yn = y * lax.rsqrt(jnp.mean(y * y, axis=-1, keepdims=True) + EPS)
        y_ref[0, rows, :] = yn * fw


def _outproj(x, mod, mod_row_fn, o_r, o_g, gz, gw, fw, wout):
    B, T, _ = x.shape
    nt = T // TOKEN_BLOCK
    const = lambda b, t: (0, 0)
    return pl.pallas_call(
        _outproj_kernel,
        grid=(B, nt),
        in_specs=[
            pl.BlockSpec((1, TOKEN_BLOCK, D_MODEL), lambda b, t: (b, t, 0)),
            pl.BlockSpec((1, 1, 3 * D_MODEL), lambda b, t: (mod_row_fn(b), 0, 0)),
            pl.BlockSpec((1, TOKEN_BLOCK, WV), lambda b, t: (b, t, 0)),
            pl.BlockSpec((1, TOKEN_BLOCK, WV), lambda b, t: (b, t, 0)),
            pl.BlockSpec((1, TOKEN_BLOCK, 2 * WV), lambda b, t: (b, t, 0)),
            pl.BlockSpec((1, DV), const),
            pl.BlockSpec((1, D_MODEL), const),
            pl.BlockSpec(wout.shape, const),
        ],
        out_specs=pl.BlockSpec((1, TOKEN_BLOCK, D_MODEL), lambda b, t: (b, t, 0)),
        out_shape=jax.ShapeDtypeStruct((B, T, D_MODEL), F32),
        compiler_params=pltpu.CompilerParams(
            dimension_semantics=("parallel", "parallel"), vmem_limit_bytes=VMEM_LIMIT),
        name="outproj",
    )(x, mod, o_r, o_g, gz, gw, fw, wout)


def _rope_tables(n_tokens):
    rows = n_tokens // GRID_W
    rr, cc = jnp.meshgrid(jnp.arange(rows), jnp.arange(GRID_W), indexing="ij")
    rr = rr.reshape(-1).astype(F32)
    cc = cc.reshape(-1).astype(F32)
    n_freq = DK // 4
    inv = ROPE_BASE ** (-jnp.arange(n_freq, dtype=F32) / n_freq)
    ang = jnp.concatenate([rr[:, None] * inv, cc[:, None] * inv], axis=-1)
    cos, sin = jnp.cos(ang), jnp.sin(ang)
    cos_t = jnp.tile(jnp.concatenate([cos, cos], axis=-1), (1, LANES // DK))
    sin_t = jnp.tile(jnp.concatenate([-sin, sin], axis=-1), (1, LANES // DK))
    return cos_t, sin_t


def _prep_weights(w_in, gla_w_alpha, gla_b_alpha):
    q_r, k_r = w_in[:, 0:QK], w_in[:, QK:2 * QK]
    o = 2 * QK
    v_r, z_r = w_in[:, o:o + WV], w_in[:, o + WV:o + 2 * WV]
    o += 2 * WV
    q_g, k_g = w_in[:, o:o + QK], w_in[:, o + QK:o + 2 * QK]
    o += 2 * QK
    v_g, z_g = w_in[:, o:o + WV], w_in[:, o + WV:o + 2 * WV]
    o += 2 * WV
    lr = w_in[:, o:o + 2 * GLA_RANK]
    wqk = jnp.concatenate([q_r, k_r, q_g, k_g], axis=1).astype(BF16)
    wz = jnp.concatenate([z_r, z_g], axis=1).astype(BF16)
    wvt = jnp.concatenate([v_r, v_g, lr], axis=1).T.astype(BF16)
    wa = jnp.zeros((2 * GLA_RANK, 2 * QK), F32)
    wa = wa.at[0:GLA_RANK, 0:QK].set(gla_w_alpha[0])
    wa = wa.at[GLA_RANK:2 * GLA_RANK, QK:2 * QK].set(gla_w_alpha[1])
    ba = jnp.concatenate([gla_b_alpha[0], gla_b_alpha[1]]).reshape(1, 2 * QK)
    return wqk, wz, wvt, wa.astype(BF16), ba


def kernel(x_prompt, x_sample, c, state_ret, state_gla, c_ctx, w_mod, b_mod, w_in, ret_log_decay,
           gla_w_alpha, gla_b_alpha, gla_norm_w, w_out, final_norm_w):
    assert w_mod.shape[0] == 1, "single-layer model"
    b_ctx, t_ctx, _ = x_prompt.shape
    b_dec, t_dec, _ = x_sample.shape
    assert t_ctx % SCAN_CHUNK == 0 and t_dec % TOKEN_BLOCK == 0 and TOKEN_BLOCK % t_ctx == 0
    assert 1 + b_dec <= MOD_ROWS

    cond = jnp.concatenate(
        [c_ctx[None, :], c, jnp.zeros((MOD_ROWS - 1 - b_dec, D_MODEL), F32)], axis=0)
    mod = _modulation(cond, w_mod[0], b_mod[0]).reshape(MOD_ROWS, 1, 3 * D_MODEL)

    weights = _prep_weights(w_in[0], gla_w_alpha[0], gla_b_alpha[0])
    wout = w_out[0].astype(BF16)
    gw = gla_norm_w[0].reshape(1, DV)
    fw = final_norm_w.reshape(1, D_MODEL)
    ld = ret_log_decay[0]

    per_blk = TOKEN_BLOCK // t_ctx
    xp = x_prompt.reshape(b_ctx // per_blk, TOKEN_BLOCK, D_MODEL)
    ctx_row = lambda b: 0
    qk, gz, vt, lg = _inproj(xp, mod, ctx_row, weights, None)
    qk = qk.reshape(b_ctx, t_ctx, 4 * QK)
    vt = vt.reshape(b_ctx, t_ctx // SCAN_CHUNK, 2 * WV, SCAN_CHUNK)
    lg = lg.reshape(b_ctx, t_ctx, 2 * QK)
    o_r, new_ret = _scan(False, qk, vt, None, ld, None, True, CTX_SEQ_BLOCK, 1)
    o_g, new_gla = _scan(True, qk, vt, lg, None, None, True, CTX_SEQ_BLOCK, 1)
    y_prompt = _outproj(xp, mod, ctx_row, o_r.reshape(xp.shape[0], TOKEN_BLOCK, WV),
                        o_g.reshape(xp.shape[0], TOKEN_BLOCK, WV), gz, gw, fw, wout)
    y_prompt = y_prompt.reshape(b_ctx, t_ctx, D_MODEL)

    dec_row = lambda b: b + 1
    rope = _rope_tables(t_dec)
    qk, gz, vt, lg = _inproj(x_sample, mod, dec_row, weights, rope)
    (o_r,) = _scan(False, qk, vt, None, ld, state_ret, False, 1, DEC_UNROLL)
    (o_g,) = _scan(True, qk, vt, lg, None, state_gla, False, 1, DEC_UNROLL)
    y_sample = _outproj(x_sample, mod, dec_row, o_r, o_g, gz, gw, fw, wout)

    return (y_prompt, y_sample, new_ret, new_gla)
```

```python
import functools

import jax
import jax.numpy as jnp
import numpy as np
from jax import lax
from jax.experimental import pallas as pl
from jax.experimental.pallas import tpu as pltpu

F32 = jnp.float32
BF16 = jnp.bfloat16

D_MODEL = 1024
N_HEADS = 4
DK = 64
DV = 128
QK = N_HEADS * DK
WV = N_HEADS * DV
GLA_RANK = 16
GLA_TAU = 16.0
EPS = 1e-6
GRID_W = 64
ROPE_BASE = 10000.0

LANES = 128
SCAN_CHUNK = 128
CTX_SEQ_BLOCK = 4
DEC_UNROLL = 4
TOKEN_BLOCK = 1024
INPROJ_SUB = 256
OUTPROJ_SUB = 256
MOD_ROWS = 16
MOD_COL_BLOCK = 512
OUT_TOKEN_BLOCK = 1024
VMEM_LIMIT = 48 * 1024 * 1024
OUT_VMEM_LIMIT = VMEM_LIMIT

_NT = (((1,), (1,)), ((), ()))
_TN = (((0,), (0,)), ((), ()))


def _dot(a, b):
    return jnp.dot(a, b, preferred_element_type=F32)


def _dot_nt(a, b):
    return lax.dot_general(a, b, _NT, preferred_element_type=F32)


def _silu(x):
    return x * jax.nn.sigmoid(x)


def _mod_kernel(c_ref, w_ref, b_ref, o_ref):
    s = _silu(c_ref[...]).astype(BF16)
    o_ref[...] = _dot(s, w_ref[...].astype(BF16)) + b_ref[...]


def _modulation(cond, w_mod, b_mod):
    n_col = w_mod.shape[1]
    return pl.pallas_call(
        _mod_kernel,
        grid=(n_col // MOD_COL_BLOCK,),
        in_specs=[
            pl.BlockSpec((MOD_ROWS, D_MODEL), lambda j: (0, 0)),
            pl.BlockSpec((D_MODEL, MOD_COL_BLOCK), lambda j: (0, j)),
            pl.BlockSpec((1, MOD_COL_BLOCK), lambda j: (0, j)),
        ],
        out_specs=pl.BlockSpec((MOD_ROWS, MOD_COL_BLOCK), lambda j: (0, j)),
        out_shape=jax.ShapeDtypeStruct((MOD_ROWS, n_col), F32),
        compiler_params=pltpu.CompilerParams(vmem_limit_bytes=VMEM_LIMIT),
        name="modulation",
    )(cond, w_mod, b_mod.reshape(1, n_col))


def _rope_tile(t, cos, sin_signed):
    lane = lax.broadcasted_iota(jnp.int32, t.shape, 1)
    first_half = (lane & 32) == 0
    swapped = jnp.where(first_half, pltpu.roll(t, LANES - 32, 1), pltpu.roll(t, 32, 1))
    return t * cos + swapped * sin_signed


def _inproj_kernel(use_rope, x_ref, mod_ref, wqk_ref, wz_ref, wvt_ref, wa_ref, ba_ref, *rest):
    if use_rope:
        cos_ref, sin_ref, qk_ref, z_ref, vt_ref, lg_ref = rest
    else:
        qk_ref, z_ref, vt_ref, lg_ref = rest
    shift = mod_ref[0, :, 0:D_MODEL]
    scale1 = 1.0 + mod_ref[0, :, D_MODEL:2 * D_MODEL]

    def gate_logs(lr_t, rows):
        logit = lax.dot_general(lr_t, wa_ref[...], _TN, preferred_element_type=F32) + ba_ref[...]
        log_sig = jnp.minimum(logit, 0.0) - jnp.log(1.0 + jnp.exp(-jnp.abs(logit)))
        lg_ref[0, rows, :] = log_sig * (1.0 / GLA_TAU)

    for sub in range(TOKEN_BLOCK // INPROJ_SUB):
        rows = slice(sub * INPROJ_SUB, (sub + 1) * INPROJ_SUB)
        x = x_ref[0, rows, :]
        inv = lax.rsqrt(jnp.mean(x * x, axis=-1, keepdims=True) + EPS)
        hb = ((x * inv) * scale1 + shift).astype(BF16)

        lr_t = _dot_nt(wvt_ref[2 * WV:, :], hb).astype(BF16)

        for half in range(2):
            res = _dot(hb, wqk_ref[:, half * 2 * QK:(half + 1) * 2 * QK])
            if use_rope and half == 0:
                for t in range(4):
                    tile = res[:, t * LANES:(t + 1) * LANES]
                    if t >= 2:
                        tile = tile * (DK ** -0.5)
                    tile = _rope_tile(tile, cos_ref[rows, :], sin_ref[rows, :])
                    qk_ref[0, rows, t * LANES:(t + 1) * LANES] = tile.astype(BF16)
            else:
                lane = lax.broadcasted_iota(jnp.int32, (1, 2 * QK), 1)
                scaled = (lane >= QK) if half == 0 else (lane < QK)
                col_scale = jnp.where(scaled, DK ** -0.5, 1.0)
                qk_ref[0, rows, half * 2 * QK:(half + 1) * 2 * QK] = (res * col_scale).astype(BF16)

        gate_logs(lr_t, rows)

        z_ref[0, rows, :] = _silu(_dot(hb, wz_ref[...])).astype(BF16)

        vt = _dot_nt(wvt_ref[0:2 * WV, :], hb)
        for j in range(INPROJ_SUB // SCAN_CHUNK):
            jj = sub * (INPROJ_SUB // SCAN_CHUNK) + j
            vt_ref[0, jj] = vt[:, j * SCAN_CHUNK:(j + 1) * SCAN_CHUNK].astype(BF16)


def _inproj(x, mod, mod_row_fn, weights, rope):
    B, T, _ = x.shape
    wqk, wz, wvt, wa, ba = weights
    nt = T // TOKEN_BLOCK
    cpb = TOKEN_BLOCK // SCAN_CHUNK
    const = lambda b, t: (0, 0)
    in_specs = [
        pl.BlockSpec((1, TOKEN_BLOCK, D_MODEL), lambda b, t: (b, t, 0)),
        pl.BlockSpec((1, 1, 3 * D_MODEL), lambda b, t: (mod_row_fn(b), 0, 0)),
        pl.BlockSpec(wqk.shape, const),
        pl.BlockSpec(wz.shape, const),
        pl.BlockSpec(wvt.shape, const),
        pl.BlockSpec(wa.shape, const),
        pl.BlockSpec(ba.shape, const),
    ]
    args = [x, mod, wqk, wz, wvt, wa, ba]
    if rope is not None:
        in_specs += [pl.BlockSpec((TOKEN_BLOCK, LANES), lambda b, t: (t, 0))] * 2
        args += list(rope)
    out_shape = (
        jax.ShapeDtypeStruct((B, T, 4 * QK), BF16),
        jax.ShapeDtypeStruct((B, T, 2 * WV), BF16),
        jax.ShapeDtypeStruct((B, T // SCAN_CHUNK, 2 * WV, SCAN_CHUNK), BF16),
        jax.ShapeDtypeStruct((B, T, 2 * QK), F32),
    )
    out_specs = (
        pl.BlockSpec((1, TOKEN_BLOCK, 4 * QK), lambda b, t: (b, t, 0)),
        pl.BlockSpec((1, TOKEN_BLOCK, 2 * WV), lambda b, t: (b, t, 0)),
        pl.BlockSpec((1, cpb, 2 * WV, SCAN_CHUNK), lambda b, t: (b, t, 0, 0)),
        pl.BlockSpec((1, TOKEN_BLOCK, 2 * QK), lambda b, t: (b, t, 0)),
    )
    return pl.pallas_call(
        functools.partial(_inproj_kernel, rope is not None),
        grid=(B, nt),
        in_specs=in_specs,
        out_specs=out_specs,
        out_shape=out_shape,
        compiler_params=pltpu.CompilerParams(
            dimension_semantics=("parallel", "parallel"), vmem_limit_bytes=VMEM_LIMIT),
        name="inproj_rope" if rope is not None else "inproj",
    )(*args)


def _scan_kernel(is_gla, has_init, want_state, n_chunks, seq_blk, unroll, *refs):
    refs = list(refs)
    q_ref, k_ref, vt_ref = refs[:3]
    pos = 3
    if is_gla:
        lgf_ref, lgb_ref = refs[pos:pos + 2]
        pos += 2
    else:
        ld_ref = refs[pos]
        pos += 1
    if has_init:
        s0_ref = refs[pos]
        pos += 1
    o_ref = refs[pos]
    pos += 1
    if want_state:
        sn_ref = refs[pos]
        pos += 1
    st_ref, oacc_ref = refs[pos:pos + 2]

    C = SCAN_CHUNK
    mid = C // 2
    n_pairs = N_HEADS // 2
    lane = lax.broadcasted_iota(jnp.int32, (1, LANES), 1)
    head_mask = (lane < DK, lane >= DK)
    ri = lax.broadcasted_iota(jnp.int32, (C, C), 0)
    ci = lax.broadcasted_iota(jnp.int32, (C, C), 1)
    tpos = lax.broadcasted_iota(jnp.int32, (C, 1), 0).astype(F32)
    tri = ((ri >= ci), (ci >= ri))

    if has_init:
        for s in range(seq_blk):
            for p in range(n_pairs):
                for d in range(2):
                    s0 = s0_ref[s, 0, d]
                    both = jnp.concatenate([s0[2 * p], s0[2 * p + 1]], axis=0).T
                    st_ref[s, p, d, 0:DV] = both
                    st_ref[s, p, d, DV:2 * DV] = both
    else:
        st_ref[...] = jnp.zeros(st_ref.shape, F32)

    if is_gla:
        tri_ones = [t.astype(BF16) for t in tri]
    else:
        q_mul, k_mul, ret_decay, score_mul = {}, {}, {}, {}
        for p in range(n_pairs):
            for d in range(2):
                ld0 = ld_ref[d, 2 * p]
                ld1 = ld_ref[d, 2 * p + 1]
                ldr = jnp.where(lane < DK, ld0, ld1)
                if d == 0:
                    q_mul[p, d] = jnp.exp(ldr * (tpos + 1.0))
                    k_mul[p, d] = jnp.exp(ldr * (C - 1.0 - tpos))
                    dist = (ri - ci).astype(F32)
                else:
                    q_mul[p, d] = jnp.exp(ldr * (C - tpos))
                    k_mul[p, d] = jnp.exp(ldr * tpos)
                    dist = (ci - ri).astype(F32)
                ret_decay[p, d] = jnp.exp(ldr * float(C))
                score_mul[p, d] = [jnp.where(tri[d], jnp.exp(ldh * dist), 0.0) for ldh in (ld0, ld1)]

    zero = jnp.zeros((), BF16)
    groups = [(s, p, d) for s in range(seq_blk) for p in range(n_pairs) for d in range(2)]

    def run_block(n0, accumulate):
        chains = {}
        for s, p, d in groups:
            ptile = slice(p * LANES, (p + 1) * LANES)
            for u in range(unroll):
                n = n0 + u
                c = n if d == 0 else n_chunks - 1 - n
                t = dict(c=c, rows=pl.ds(pl.multiple_of(c * C, C), C))
                t["q"] = q_ref[s, t["rows"], ptile]
                t["k"] = k_ref[s, t["rows"], ptile]
                if is_gla:
                    lg = (lgf_ref if d == 0 else lgb_ref)[s, t["rows"], ptile]
                    hi = lg.astype(BF16)
                    lo = (lg - hi.astype(F32)).astype(BF16)
                    bb = _dot(tri_ones[d], jnp.concatenate([hi, lo], axis=1))
                    t["b"] = bb[:, :LANES] + bb[:, LANES:]
                chains[s, p, d, u] = t

        for (s, p, d, u), t in chains.items():
            q, k = t["q"], t["k"]
            if is_gla:
                b = t["b"]
                r = b[mid - 1:mid] if d == 0 else b[mid:mid + 1]
                bl = b[C - 1:C] if d == 0 else b[0:1]
                qs = q.astype(F32) * jnp.exp(b - r)
                ks = k.astype(F32) * jnp.exp(r - b)
                q_sc = qs.astype(BF16)
                k_sc = ks.astype(BF16)
                q_dec = (qs * jnp.exp(r)).astype(BF16)
                t["k_dec"] = (ks * jnp.exp(bl - r)).astype(BF16)
                t["decay"] = jnp.exp(bl)
            else:
                q_sc = q
                k_sc = k
                q_dec = (q.astype(F32) * q_mul[p, d]).astype(BF16)
                t["k_dec"] = (k.astype(F32) * k_mul[p, d]).astype(BF16)
                t["decay"] = ret_decay[p, d]
            t["q_dec"] = [jnp.where(head_mask[hh], q_dec, zero) for hh in range(2)]
            k_heads = jnp.concatenate([jnp.where(head_mask[hh], k_sc, zero) for hh in range(2)], axis=0)
            t["sc"] = _dot_nt(q_sc, k_heads)

        for (s, p, d, u), t in chains.items():
            t["vt"] = vt_ref[s, t["c"], p * 2 * DV:(p + 1) * 2 * DV, :]
            t["kv"] = _dot(t["vt"], t["k_dec"])

        for (s, p, d, u), t in chains.items():
            t["scb"] = []
            for hh in range(2):
                sc = t["sc"][:, hh * C:(hh + 1) * C]
                if is_gla:
                    sc = jnp.where(tri[d], sc, 0.0)
                else:
                    sc = sc * score_mul[p, d][hh]
                t["scb"].append(sc.astype(BF16))

        state = {g: st_ref[g] for g in groups}
        for u in range(unroll):
            for g in groups:
                t = chains[g + (u,)]
                st = state[g]
                stb = st.astype(BF16)
                t["o"] = []
                for hh in range(2):
                    hrows = slice(hh * DV, (hh + 1) * DV)
                    lhs = jnp.concatenate([t["scb"][hh], t["q_dec"][hh]], axis=1)
                    rhs = jnp.concatenate([t["vt"][hrows], stb[hrows]], axis=1)
                    t["o"].append(_dot_nt(lhs, rhs))
                state[g] = st * t["decay"] + t["kv"]
        for key, st in state.items():
            st_ref[key] = st

        for (s, p, d, u), t in chains.items():
            for hh in range(2):
                cols = slice((2 * p + hh) * DV, (2 * p + hh + 1) * DV)
                if accumulate:
                    o_ref[s, t["rows"], cols] = (oacc_ref[s, t["rows"], cols] + t["o"][hh]).astype(BF16)
                else:
                    oacc_ref[s, t["rows"], cols] = t["o"][hh]

    def loop_body(accumulate, base):
        def body(m, carry):
            run_block(base + m * unroll, accumulate)
            return carry
        return body

    half = n_chunks // 2
    assert half % unroll == 0
    lax.fori_loop(0, half // unroll, loop_body(False, 0), 0)
    lax.fori_loop(0, half // unroll, loop_body(True, half), 0)

    if want_state:
        for s, p, d in groups:
            for hh in range(2):
                sn_ref[s, 0, d, 2 * p + hh] = st_ref[s, p, d, hh * DV:(hh + 1) * DV].T[hh * DK:(hh + 1) * DK, :]


def _scan(is_gla, qk, vt, lg, ld, s0, want_state, seq_blk, unroll):
    B, T, _ = qk.shape
    n_chunks = T // SCAN_CHUNK
    assert n_chunks % 2 == 0 and B % seq_blk == 0
    sb = seq_blk
    grp = 1 if is_gla else 0
    in_specs = [
        pl.BlockSpec((sb, T, QK), lambda b: (b, 0, 2 * grp)),
        pl.BlockSpec((sb, T, QK), lambda b: (b, 0, 2 * grp + 1)),
        pl.BlockSpec((sb, n_chunks, WV, SCAN_CHUNK), lambda b: (b, 0, grp, 0)),
    ]
    args = [qk, qk, vt]
    if is_gla:
        in_specs += [
            pl.BlockSpec((sb, T, QK), lambda b: (b, 0, 0)),
            pl.BlockSpec((sb, T, QK), lambda b: (b, 0, 1)),
        ]
        args += [lg, lg]
    else:
        in_specs += [pl.BlockSpec(memory_space=pltpu.SMEM)]
        args += [ld]
    state_spec = pl.BlockSpec((sb, 1, 2, N_HEADS, DK, DV), lambda b: (b, 0, 0, 0, 0, 0))
    if s0 is not None:
        in_specs += [state_spec]
        args += [s0]
    out_shape = [jax.ShapeDtypeStruct((B, T, WV), BF16)]
    out_specs = [pl.BlockSpec((sb, T, WV), lambda b: (b, 0, 0))]
    if want_state:
        out_shape += [jax.ShapeDtypeStruct((B, 1, 2, N_HEADS, DK, DV), F32)]
        out_specs += [state_spec]
    name = ("gla" if is_gla else "ret") + ("_init" if s0 is not None else "") + "_scan"
    return pl.pallas_call(
        functools.partial(_scan_kernel, is_gla, s0 is not None, want_state, n_chunks, sb, unroll),
        grid=(B // sb,),
        in_specs=in_specs,
        out_specs=out_specs,
        out_shape=out_shape,
        scratch_shapes=[pltpu.VMEM((sb, N_HEADS // 2, 2, 2 * DV, LANES), F32),
                        pltpu.VMEM((sb, T, WV), F32)],
        compiler_params=pltpu.CompilerParams(
            dimension_semantics=("arbitrary",), vmem_limit_bytes=VMEM_LIMIT),
        name=name,
    )(*args)


def _outproj_kernel(x_ref, mod_ref, or_ref, og_ref, gz_ref, gw_ref, fw_ref, wout_ref, y_ref):
    gate = mod_ref[0, :, 2 * D_MODEL:3 * D_MODEL]
    gw = gw_ref[...]
    fw = fw_ref[...]
    for sub in range(OUT_TOKEN_BLOCK // OUTPROJ_SUB):
        rows = slice(sub * OUTPROJ_SUB, (sub + 1) * OUTPROJ_SUB)
        parts = []
        for h in range(N_HEADS):
            cols = slice(h * DV, (h + 1) * DV)
            t = or_ref[0, rows, cols].astype(F32)
            mu = jnp.mean(t, axis=-1, keepdims=True)
            dlt = t - mu
            var = jnp.mean(dlt * dlt, axis=-1, keepdims=True)
            n = dlt * lax.rsqrt(var + EPS)
            parts.append((n * gz_ref[0, rows, cols].astype(F32)).astype(BF16))
        for h in range(N_HEADS):
            cols = slice(h * DV, (h + 1) * DV)
            t = og_ref[0, rows, cols].astype(F32)
            n = t * lax.rsqrt(jnp.mean(t * t, axis=-1, keepdims=True) + EPS) * gw
            gcols = slice(WV + h * DV, WV + (h + 1) * DV)
            parts.append((n * gz_ref[0, rows, gcols].astype(F32)).astype(BF16))
        mixed = jnp.concatenate(parts, axis=-1)
        out = _dot(mixed, wout_ref[...])
        y = x_ref[0, rows, :] + gate * out
        yn = y * lax.rsqrt(jnp.mean(y * y, axis=-1, keepdims=True) + EPS)
        y_ref[0, rows, :] = yn * fw


def _outproj(x, mod, mod_row_fn, o_r, o_g, gz, gw, fw, wout):
    n_tok = x.shape[0] * x.shape[1]
    B = n_tok // OUT_TOKEN_BLOCK
    x, o_r, o_g, gz = (a.reshape(B, OUT_TOKEN_BLOCK, a.shape[-1]) for a in (x, o_r, o_g, gz))
    const = lambda b: (0, 0)
    blk = lambda width: pl.BlockSpec((1, OUT_TOKEN_BLOCK, width), lambda b: (b, 0, 0))
    return pl.pallas_call(
        _outproj_kernel,
        grid=(B,),
        in_specs=[
            blk(D_MODEL),
            pl.BlockSpec((1, 1, 3 * D_MODEL), lambda b: (mod_row_fn(b), 0, 0)),
            blk(WV),
            blk(WV),
            blk(2 * WV),
            pl.BlockSpec((1, DV), const),
            pl.BlockSpec((1, D_MODEL), const),
            pl.BlockSpec(wout.shape, const),
        ],
        out_specs=blk(D_MODEL),
        out_shape=jax.ShapeDtypeStruct((B, OUT_TOKEN_BLOCK, D_MODEL), F32),
        compiler_params=pltpu.CompilerParams(
            dimension_semantics=("parallel",), vmem_limit_bytes=OUT_VMEM_LIMIT),
        name="outproj",
    )(x, mod, o_r, o_g, gz, gw, fw, wout)


def _rope_tables(n_tokens):
    rows = n_tokens // GRID_W
    rr, cc = np.meshgrid(np.arange(rows), np.arange(GRID_W), indexing="ij")
    rr = rr.reshape(-1).astype(np.float32)
    cc = cc.reshape(-1).astype(np.float32)
    n_freq = DK // 4
    inv = np.float32(ROPE_BASE) ** (-np.arange(n_freq, dtype=np.float32) / np.float32(n_freq))
    ang = np.concatenate([rr[:, None] * inv, cc[:, None] * inv], axis=-1)
    cos = np.cos(ang.astype(np.float64)).astype(np.float32)
    sin = np.sin(ang.astype(np.float64)).astype(np.float32)
    cos_t = np.tile(np.concatenate([cos, cos], axis=-1), (1, LANES // DK))
    sin_t = np.tile(np.concatenate([-sin, sin], axis=-1), (1, LANES // DK))
    return jnp.asarray(cos_t), jnp.asarray(sin_t)


def _prep_weights(w_in, gla_w_alpha, gla_b_alpha):
    q_r, k_r = w_in[:, 0:QK], w_in[:, QK:2 * QK]
    o = 2 * QK
    v_r, z_r = w_in[:, o:o + WV], w_in[:, o + WV:o + 2 * WV]
    o += 2 * WV
    q_g, k_g = w_in[:, o:o + QK], w_in[:, o + QK:o + 2 * QK]
    o += 2 * QK
    v_g, z_g = w_in[:, o:o + WV], w_in[:, o + WV:o + 2 * WV]
    o += 2 * WV
    lr = w_in[:, o:o + 2 * GLA_RANK]
    wqk = jnp.concatenate([q_r, k_r, q_g, k_g], axis=1).astype(BF16)
    wz = jnp.concatenate([z_r, z_g], axis=1).astype(BF16)
    wvt = jnp.concatenate([v_r, v_g, lr], axis=1).T.astype(BF16)
    wa = jnp.zeros((2 * GLA_RANK, 2 * QK), F32)
    wa = wa.at[0:GLA_RANK, 0:QK].set(gla_w_alpha[0])
    wa = wa.at[GLA_RANK:2 * GLA_RANK, QK:2 * QK].set(gla_w_alpha[1])
    ba = jnp.concatenate([gla_b_alpha[0], gla_b_alpha[1]]).reshape(1, 2 * QK)
    return wqk, wz, wvt, wa.astype(BF16), ba


def kernel(x_prompt, x_sample, c, state_ret, state_gla, c_ctx, w_mod, b_mod, w_in, ret_log_decay,
           gla_w_alpha, gla_b_alpha, gla_norm_w, w_out, final_norm_w):
    assert w_mod.shape[0] == 1, "single-layer model"
    b_ctx, t_ctx, _ = x_prompt.shape
    b_dec, t_dec, _ = x_sample.shape
    assert t_ctx % SCAN_CHUNK == 0 and t_dec % TOKEN_BLOCK == 0 and TOKEN_BLOCK % t_ctx == 0
    assert 1 + b_dec <= MOD_ROWS

    cond = jnp.concatenate(
        [c_ctx[None, :], c, jnp.zeros((MOD_ROWS - 1 - b_dec, D_MODEL), F32)], axis=0)
    mod = _modulation(cond, w_mod[0], b_mod[0]).reshape(MOD_ROWS, 1, 3 * D_MODEL)

    weights = _prep_weights(w_in[0], gla_w_alpha[0], gla_b_alpha[0])
    wout = w_out[0].astype(BF16)
    gw = gla_norm_w[0].reshape(1, DV)
    fw = final_norm_w.reshape(1, D_MODEL)
    ld = ret_log_decay[0]

    per_blk = TOKEN_BLOCK // t_ctx
    xp = x_prompt.reshape(b_ctx // per_blk, TOKEN_BLOCK, D_MODEL)
    ctx_row = lambda b: 0
    qk, gz, vt, lg = _inproj(xp, mod, ctx_row, weights, None)
    qk = qk.reshape(b_ctx, t_ctx, 4 * QK)
    vt = vt.reshape(b_ctx, t_ctx // SCAN_CHUNK, 2 * WV, SCAN_CHUNK)
    lg = lg.reshape(b_ctx, t_ctx, 2 * QK)
    o_r, new_ret = _scan(False, qk, vt, None, ld, None, True, CTX_SEQ_BLOCK, 1)
    o_g, new_gla = _scan(True, qk, vt, lg, None, None, True, CTX_SEQ_BLOCK, 1)
    y_prompt = _outproj(xp, mod, ctx_row, o_r, o_g, gz, gw, fw, wout)
    y_prompt = y_prompt.reshape(b_ctx, t_ctx, D_MODEL)

    dec_row = lambda b: b + 1
    rope = _rope_tables(t_dec)
    qk, gz, vt, lg = _inproj(x_sample, mod, dec_row, weights, rope)
    (o_r,) = _scan(False, qk, vt, None, ld, state_ret, False, 1, DEC_UNROLL)
    (o_g,) = _scan(True, qk, vt, lg, None, state_gla, False, 1, DEC_UNROLL)
    per_seq = t_dec // OUT_TOKEN_BLOCK
    y_sample = _outproj(x_sample, mod, lambda b: b // per_seq + 1, o_r, o_g, gz, gw, fw, wout)
    y_sample = y_sample.reshape(b_dec, t_dec, D_MODEL)

    return (y_prompt, y_sample, new_ret, new_gla)
```

```python
import functools

import jax
import jax.numpy as jnp
import numpy as np
from jax import lax
from jax.experimental import pallas as pl
from jax.experimental.pallas import tpu as pltpu

F32 = jnp.float32
BF16 = jnp.bfloat16

D_MODEL = 1024
N_HEADS = 4
DK = 64
DV = 128
QK = N_HEADS * DK
WV = N_HEADS * DV
GLA_RANK = 16
GLA_TAU = 16.0
EPS = 1e-6
GRID_W = 64
ROPE_BASE = 10000.0

LANES = 128
SCAN_CHUNK = 128
CTX_SEQ_BLOCK = 8
DEC_UNROLL = 8
TOKEN_BLOCK = 1024
INPROJ_SUB = 256
OUTPROJ_SUB = 256
MOD_ROWS = 16
MOD_COL_BLOCK = 512
OUT_TOKEN_BLOCK = 1024
VMEM_LIMIT = 48 * 1024 * 1024
OUT_VMEM_LIMIT = VMEM_LIMIT

_NT = (((1,), (1,)), ((), ()))
_TN = (((0,), (0,)), ((), ()))


def _dot(a, b):
    return jnp.dot(a, b, preferred_element_type=F32)


def _dot_nt(a, b):
    return lax.dot_general(a, b, _NT, preferred_element_type=F32)


def _silu(x):
    return x * jax.nn.sigmoid(x)


def _mod_kernel(c_ref, w_ref, b_ref, o_ref):
    s = _silu(c_ref[...]).astype(BF16)
    o_ref[...] = _dot(s, w_ref[...].astype(BF16)) + b_ref[...]


def _modulation(cond, w_mod, b_mod):
    n_col = w_mod.shape[1]
    return pl.pallas_call(
        _mod_kernel,
        grid=(n_col // MOD_COL_BLOCK,),
        in_specs=[
            pl.BlockSpec((MOD_ROWS, D_MODEL), lambda j: (0, 0)),
            pl.BlockSpec((D_MODEL, MOD_COL_BLOCK), lambda j: (0, j)),
            pl.BlockSpec((1, MOD_COL_BLOCK), lambda j: (0, j)),
        ],
        out_specs=pl.BlockSpec((MOD_ROWS, MOD_COL_BLOCK), lambda j: (0, j)),
        out_shape=jax.ShapeDtypeStruct((MOD_ROWS, n_col), F32),
        compiler_params=pltpu.CompilerParams(vmem_limit_bytes=VMEM_LIMIT),
        name="modulation",
    )(cond, w_mod, b_mod.reshape(1, n_col))


def _rope_tile(t, cos, sin_signed):
    lane = lax.broadcasted_iota(jnp.int32, t.shape, 1)
    first_half = (lane & 32) == 0
    swapped = jnp.where(first_half, pltpu.roll(t, LANES - 32, 1), pltpu.roll(t, 32, 1))
    return t * cos + swapped * sin_signed


def _inproj_kernel(use_rope, x_ref, mod_ref, wqk_ref, wz_ref, wvt_ref, wa_ref, ba_ref, *rest):
    if use_rope:
        cos_ref, sin_ref, qk_ref, z_ref, vt_ref, lg_ref = rest
    else:
        qk_ref, z_ref, vt_ref, lg_ref = rest
    shift = mod_ref[0, :, 0:D_MODEL]
    scale1 = 1.0 + mod_ref[0, :, D_MODEL:2 * D_MODEL]

    def gate_logs(lr_t, rows):
        logit = lax.dot_general(lr_t, wa_ref[...], _TN, preferred_element_type=F32) + ba_ref[...]
        log_sig = jnp.minimum(logit, 0.0) - jnp.log(1.0 + jnp.exp(-jnp.abs(logit)))
        lg_ref[0, rows, :] = log_sig * (1.0 / GLA_TAU)

    for sub in range(TOKEN_BLOCK // INPROJ_SUB):
        rows = slice(sub * INPROJ_SUB, (sub + 1) * INPROJ_SUB)
        x = x_ref[0, rows, :]
        inv = lax.rsqrt(jnp.mean(x * x, axis=-1, keepdims=True) + EPS)
        hb = ((x * inv) * scale1 + shift).astype(BF16)

        lr_t = _dot_nt(wvt_ref[2 * WV:, :], hb).astype(BF16)

        for half in range(2):
            res = _dot(hb, wqk_ref[:, half * 2 * QK:(half + 1) * 2 * QK])
            if use_rope and half == 0:
                for t in range(4):
                    tile = res[:, t * LANES:(t + 1) * LANES]
                    if t >= 2:
                        tile = tile * (DK ** -0.5)
                    tile = _rope_tile(tile, cos_ref[rows, :], sin_ref[rows, :])
                    qk_ref[0, rows, t * LANES:(t + 1) * LANES] = tile.astype(BF16)
            else:
                lane = lax.broadcasted_iota(jnp.int32, (1, 2 * QK), 1)
                scaled = (lane >= QK) if half == 0 else (lane < QK)
                col_scale = jnp.where(scaled, DK ** -0.5, 1.0)
                qk_ref[0, rows, half * 2 * QK:(half + 1) * 2 * QK] = (res * col_scale).astype(BF16)

        gate_logs(lr_t, rows)

        z_ref[0, rows, :] = _silu(_dot(hb, wz_ref[...])).astype(BF16)

        vt = _dot_nt(wvt_ref[0:2 * WV, :], hb)
        for j in range(INPROJ_SUB // SCAN_CHUNK):
            jj = sub * (INPROJ_SUB // SCAN_CHUNK) + j
            vt_ref[0, jj] = vt[:, j * SCAN_CHUNK:(j + 1) * SCAN_CHUNK].astype(BF16)


def _inproj(x, mod, mod_row_fn, weights, rope):
    B, T, _ = x.shape
    wqk, wz, wvt, wa, ba = weights
    nt = T // TOKEN_BLOCK
    cpb = TOKEN_BLOCK // SCAN_CHUNK
    const = lambda b, t: (0, 0)
    in_specs = [
        pl.BlockSpec((1, TOKEN_BLOCK, D_MODEL), lambda b, t: (b, t, 0)),
        pl.BlockSpec((1, 1, 3 * D_MODEL), lambda b, t: (mod_row_fn(b), 0, 0)),
        pl.BlockSpec(wqk.shape, const),
        pl.BlockSpec(wz.shape, const),
        pl.BlockSpec(wvt.shape, const),
        pl.BlockSpec(wa.shape, const),
        pl.BlockSpec(ba.shape, const),
    ]
    args = [x, mod, wqk, wz, wvt, wa, ba]
    if rope is not None:
        in_specs += [pl.BlockSpec((TOKEN_BLOCK, LANES), lambda b, t: (t, 0))] * 2
        args += list(rope)
    out_shape = (
        jax.ShapeDtypeStruct((B, T, 4 * QK), BF16),
        jax.ShapeDtypeStruct((B, T, 2 * WV), BF16),
        jax.ShapeDtypeStruct((B, T // SCAN_CHUNK, 2 * WV, SCAN_CHUNK), BF16),
        jax.ShapeDtypeStruct((B, T, 2 * QK), F32),
    )
    out_specs = (
        pl.BlockSpec((1, TOKEN_BLOCK, 4 * QK), lambda b, t: (b, t, 0)),
        pl.BlockSpec((1, TOKEN_BLOCK, 2 * WV), lambda b, t: (b, t, 0)),
        pl.BlockSpec((1, cpb, 2 * WV, SCAN_CHUNK), lambda b, t: (b, t, 0, 0)),
        pl.BlockSpec((1, TOKEN_BLOCK, 2 * QK), lambda b, t: (b, t, 0)),
    )
    return pl.pallas_call(
        functools.partial(_inproj_kernel, rope is not None),
        grid=(B, nt),
        in_specs=in_specs,
        out_specs=out_specs,
        out_shape=out_shape,
        compiler_params=pltpu.CompilerParams(
            dimension_semantics=("parallel", "parallel"), vmem_limit_bytes=VMEM_LIMIT),
        name="inproj_rope" if rope is not None else "inproj",
    )(*args)


def _scan_kernel(is_gla, has_init, want_state, n_chunks, seq_blk, unroll, *refs):
    refs = list(refs)
    q_ref, k_ref, vt_ref = refs[:3]
    pos = 3
    if is_gla:
        lgf_ref, lgb_ref = refs[pos:pos + 2]
        pos += 2
    else:
        ld_ref = refs[pos]
        pos += 1
    if has_init:
        s0_ref = refs[pos]
        pos += 1
    o_ref = refs[pos]
    pos += 1
    if want_state:
        sn_ref = refs[pos]
        pos += 1
    st_ref, oacc_ref = refs[pos:pos + 2]

    C = SCAN_CHUNK
    mid = C // 2
    n_pairs = N_HEADS // 2
    lane = lax.broadcasted_iota(jnp.int32, (1, LANES), 1)
    head_mask = (lane < DK, lane >= DK)
    ri = lax.broadcasted_iota(jnp.int32, (C, C), 0)
    ci = lax.broadcasted_iota(jnp.int32, (C, C), 1)
    tpos = lax.broadcasted_iota(jnp.int32, (C, 1), 0).astype(F32)
    tri = ((ri >= ci), (ci >= ri))

    if has_init:
        for s in range(seq_blk):
            for p in range(n_pairs):
                for d in range(2):
                    s0 = s0_ref[s, 0, d]
                    both = jnp.concatenate([s0[2 * p], s0[2 * p + 1]], axis=0).T
                    st_ref[s, p, d, 0:DV] = both
                    st_ref[s, p, d, DV:2 * DV] = both
    else:
        st_ref[...] = jnp.zeros(st_ref.shape, F32)

    if is_gla:
        tri_ones = [t.astype(BF16) for t in tri]
    else:
        q_mul, k_mul, ret_decay, score_mul = {}, {}, {}, {}
        for p in range(n_pairs):
            for d in range(2):
                ld0 = ld_ref[d, 2 * p]
                ld1 = ld_ref[d, 2 * p + 1]
                ldr = jnp.where(lane < DK, ld0, ld1)
                if d == 0:
                    q_mul[p, d] = jnp.exp(ldr * (tpos + 1.0))
                    k_mul[p, d] = jnp.exp(ldr * (C - 1.0 - tpos))
                    dist = (ri - ci).astype(F32)
                else:
                    q_mul[p, d] = jnp.exp(ldr * (C - tpos))
                    k_mul[p, d] = jnp.exp(ldr * tpos)
                    dist = (ci - ri).astype(F32)
                ret_decay[p, d] = jnp.exp(ldr * float(C))
                score_mul[p, d] = [jnp.where(tri[d], jnp.exp(ldh * dist), 0.0) for ldh in (ld0, ld1)]

    zero = jnp.zeros((), BF16)
    groups = [(s, p, d) for s in range(seq_blk) for p in range(n_pairs) for d in range(2)]

    def run_block(n0, accumulate):
        chains = {}
        for s, p, d in groups:
            ptile = slice(p * LANES, (p + 1) * LANES)
            for u in range(unroll):
                n = n0 + u
                c = n if d == 0 else n_chunks - 1 - n
                t = dict(c=c, rows=pl.ds(pl.multiple_of(c * C, C), C))
                t["q"] = q_ref[s, t["rows"], ptile]
                t["k"] = k_ref[s, t["rows"], ptile]
                if is_gla:
                    lg = (lgf_ref if d == 0 else lgb_ref)[s, t["rows"], ptile]
                    hi = lg.astype(BF16)
                    lo = (lg - hi.astype(F32)).astype(BF16)
                    bb = _dot(tri_ones[d], jnp.concatenate([hi, lo], axis=1))
                    t["b"] = bb[:, :LANES] + bb[:, LANES:]
                chains[s, p, d, u] = t

        for (s, p, d, u), t in chains.items():
            q, k = t["q"], t["k"]
            if is_gla:
                b = t["b"]
                r = b[mid - 1:mid] if d == 0 else b[mid:mid + 1]
                bl = b[C - 1:C] if d == 0 else b[0:1]
                qs = q.astype(F32) * jnp.exp(b - r)
                ks = k.astype(F32) * jnp.exp(r - b)
                q_sc = qs.astype(BF16)
                k_sc = ks.astype(BF16)
                q_dec = (qs * jnp.exp(r)).astype(BF16)
                t["k_dec"] = (ks * jnp.exp(bl - r)).astype(BF16)
                t["decay"] = jnp.exp(bl)
            else:
                q_sc = q
                k_sc = k
                q_dec = (q.astype(F32) * q_mul[p, d]).astype(BF16)
                t["k_dec"] = (k.astype(F32) * k_mul[p, d]).astype(BF16)
                t["decay"] = ret_decay[p, d]
            t["q_dec"] = [jnp.where(head_mask[hh], q_dec, zero) for hh in range(2)]
            k_heads = jnp.concatenate([jnp.where(head_mask[hh], k_sc, zero) for hh in range(2)], axis=0)
            t["sc"] = _dot_nt(q_sc, k_heads)

        for (s, p, d, u), t in chains.items():
            t["vt"] = vt_ref[s, t["c"], p * 2 * DV:(p + 1) * 2 * DV, :]
            t["kv"] = _dot(t["vt"], t["k_dec"])

        for (s, p, d, u), t in chains.items():
            t["scb"] = []
            for hh in range(2):
                sc = t["sc"][:, hh * C:(hh + 1) * C]
                if is_gla:
                    sc = jnp.where(tri[d], sc, 0.0)
                else:
                    sc = sc * score_mul[p, d][hh]
                t["scb"].append(sc.astype(BF16))

        state = {g: st_ref[g] for g in groups}
        for u in range(unroll):
            for g in groups:
                t = chains[g + (u,)]
                st = state[g]
                stb = st.astype(BF16)
                t["o"] = []
                for hh in range(2):
                    hrows = slice(hh * DV, (hh + 1) * DV)
                    lhs = jnp.concatenate([t["scb"][hh], t["q_dec"][hh]], axis=1)
                    rhs = jnp.concatenate([t["vt"][hrows], stb[hrows]], axis=1)
                    t["o"].append(_dot_nt(lhs, rhs))
                state[g] = st * t["decay"] + t["kv"]
        for key, st in state.items():
            st_ref[key] = st

        for (s, p, d, u), t in chains.items():
            for hh in range(2):
                cols = slice((2 * p + hh) * DV, (2 * p + hh + 1) * DV)
                if accumulate:
                    o_ref[s, t["rows"], cols] = (oacc_ref[s, t["rows"], cols] + t["o"][hh]).astype(BF16)
                else:
                    oacc_ref[s, t["rows"], cols] = t["o"][hh]

    def loop_body(accumulate, base):
        def body(m, carry):
            run_block(base + m * unroll, accumulate)
            return carry
        return body

    half = n_chunks // 2
    assert half % unroll == 0
    lax.fori_loop(0, half // unroll, loop_body(False, 0), 0)
    lax.fori_loop(0, half // unroll, loop_body(True, half), 0)

    if want_state:
        for s, p, d in groups:
            for hh in range(2):
                sn_ref[s, 0, d, 2 * p + hh] = st_ref[s, p, d, hh * DV:(hh + 1) * DV].T[hh * DK:(hh + 1) * DK, :]


def _scan(is_gla, qk, vt, lg, ld, s0, want_state, seq_blk, unroll):
    B, T, _ = qk.shape
    n_chunks = T // SCAN_CHUNK
    assert n_chunks % 2 == 0 and B % seq_blk == 0
    sb = seq_blk
    grp = 1 if is_gla else 0
    in_specs = [
        pl.BlockSpec((sb, T, QK), lambda b: (b, 0, 2 * grp)),
        pl.BlockSpec((sb, T, QK), lambda b: (b, 0, 2 * grp + 1)),
        pl.BlockSpec((sb, n_chunks, WV, SCAN_CHUNK), lambda b: (b, 0, grp, 0)),
    ]
    args = [qk, qk, vt]
    if is_gla:
        in_specs += [
            pl.BlockSpec((sb, T, QK), lambda b: (b, 0, 0)),
            pl.BlockSpec((sb, T, QK), lambda b: (b, 0, 1)),
        ]
        args += [lg, lg]
    else:
        in_specs += [pl.BlockSpec(memory_space=pltpu.SMEM)]
        args += [ld]
    state_spec = pl.BlockSpec((sb, 1, 2, N_HEADS, DK, DV), lambda b: (b, 0, 0, 0, 0, 0))
    if s0 is not None:
        in_specs += [state_spec]
        args += [s0]
    out_shape = [jax.ShapeDtypeStruct((B, T, WV), BF16)]
    out_specs = [pl.BlockSpec((sb, T, WV), lambda b: (b, 0, 0))]
    if want_state:
        out_shape += [jax.ShapeDtypeStruct((B, 1, 2, N_HEADS, DK, DV), F32)]
        out_specs += [state_spec]
    name = ("gla" if is_gla else "ret") + ("_init" if s0 is not None else "") + "_scan"
    return pl.pallas_call(
        functools.partial(_scan_kernel, is_gla, s0 is not None, want_state, n_chunks, sb, unroll),
        grid=(B // sb,),
        in_specs=in_specs,
        out_specs=out_specs,
        out_shape=out_shape,
        scratch_shapes=[pltpu.VMEM((sb, N_HEADS // 2, 2, 2 * DV, LANES), F32),
                        pltpu.VMEM((sb, T, WV), F32)],
        compiler_params=pltpu.CompilerParams(
            dimension_semantics=("arbitrary",), vmem_limit_bytes=VMEM_LIMIT),
        name=name,
    )(*args)


def _outproj_kernel(x_ref, mod_ref, or_ref, og_ref, gz_ref, gw_ref, fw_ref, wout_ref, y_ref):
    gate = mod_ref[0, :, 2 * D_MODEL:3 * D_MODEL]
    gw = gw_ref[...]
    fw = fw_ref[...]
    for sub in range(OUT_TOKEN_BLOCK // OUTPROJ_SUB):
        rows = slice(sub * OUTPROJ_SUB, (sub + 1) * OUTPROJ_SUB)
        parts = []
        for h in range(N_HEADS):
            cols = slice(h * DV, (h + 1) * DV)
            t = or_ref[0, rows, cols].astype(F32)
            mu = jnp.mean(t, axis=-1, keepdims=True)
            dlt = t - mu
            var = jnp.mean(dlt * dlt, axis=-1, keepdims=True)
            n = dlt * lax.rsqrt(var + EPS)
            parts.append((n * gz_ref[0, rows, cols].astype(F32)).astype(BF16))
        for h in range(N_HEADS):
            cols = slice(h * DV, (h + 1) * DV)
            t = og_ref[0, rows, cols].astype(F32)
            n = t * lax.rsqrt(jnp.mean(t * t, axis=-1, keepdims=True) + EPS) * gw
            gcols = slice(WV + h * DV, WV + (h + 1) * DV)
            parts.append((n * gz_ref[0, rows, gcols].astype(F32)).astype(BF16))
        mixed = jnp.concatenate(parts, axis=-1)
        out = _dot(mixed, wout_ref[...])
        y = x_ref[0, rows, :] + gate * out
        yn = y * lax.rsqrt(jnp.mean(y * y, axis=-1, keepdims=True) + EPS)
        y_ref[0, rows, :] = yn * fw


def _outproj(x, mod, mod_row_fn, o_r, o_g, gz, gw, fw, wout):
    n_tok = x.shape[0] * x.shape[1]
    B = n_tok // OUT_TOKEN_BLOCK
    x, o_r, o_g, gz = (a.reshape(B, OUT_TOKEN_BLOCK, a.shape[-1]) for a in (x, o_r, o_g, gz))
    const = lambda b: (0, 0)
    blk = lambda width: pl.BlockSpec((1, OUT_TOKEN_BLOCK, width), lambda b: (b, 0, 0))
    return pl.pallas_call(
        _outproj_kernel,
        grid=(B,),
        in_specs=[
            blk(D_MODEL),
            pl.BlockSpec((1, 1, 3 * D_MODEL), lambda b: (mod_row_fn(b), 0, 0)),
            blk(WV),
            blk(WV),
            blk(2 * WV),
            pl.BlockSpec((1, DV), const),
            pl.BlockSpec((1, D_MODEL), const),
            pl.BlockSpec(wout.shape, const),
        ],
        out_specs=blk(D_MODEL),
        out_shape=jax.ShapeDtypeStruct((B, OUT_TOKEN_BLOCK, D_MODEL), F32),
        compiler_params=pltpu.CompilerParams(
            dimension_semantics=("parallel",), vmem_limit_bytes=OUT_VMEM_LIMIT),
        name="outproj",
    )(x, mod, o_r, o_g, gz, gw, fw, wout)


def _rope_tables(n_tokens):
    rows = n_tokens // GRID_W
    rr, cc = np.meshgrid(np.arange(rows), np.arange(GRID_W), indexing="ij")
    rr = rr.reshape(-1).astype(np.float32)
    cc = cc.reshape(-1).astype(np.float32)
    n_freq = DK // 4
    inv = np.float32(ROPE_BASE) ** (-np.arange(n_freq, dtype=np.float32) / np.float32(n_freq))
    ang = np.concatenate([rr[:, None] * inv, cc[:, None] * inv], axis=-1)
    cos = np.cos(ang.astype(np.float64)).astype(np.float32)
    sin = np.sin(ang.astype(np.float64)).astype(np.float32)
    cos_t = np.tile(np.concatenate([cos, cos], axis=-1), (1, LANES // DK))
    sin_t = np.tile(np.concatenate([-sin, sin], axis=-1), (1, LANES // DK))
    return jnp.asarray(cos_t), jnp.asarray(sin_t)


def _prep_weights(w_in, gla_w_alpha, gla_b_alpha):
    q_r, k_r = w_in[:, 0:QK], w_in[:, QK:2 * QK]
    o = 2 * QK
    v_r, z_r = w_in[:, o:o + WV], w_in[:, o + WV:o + 2 * WV]
    o += 2 * WV
    q_g, k_g = w_in[:, o:o + QK], w_in[:, o + QK:o + 2 * QK]
    o += 2 * QK
    v_g, z_g = w_in[:, o:o + WV], w_in[:, o + WV:o + 2 * WV]
    o += 2 * WV
    lr = w_in[:, o:o + 2 * GLA_RANK]
    wqk = jnp.concatenate([q_r, k_r, q_g, k_g], axis=1).astype(BF16)
    wz = jnp.concatenate([z_r, z_g], axis=1).astype(BF16)
    wvt = jnp.concatenate([v_r, v_g, lr], axis=1).T.astype(BF16)
    wa = jnp.zeros((2 * GLA_RANK, 2 * QK), F32)
    wa = wa.at[0:GLA_RANK, 0:QK].set(gla_w_alpha[0])
    wa = wa.at[GLA_RANK:2 * GLA_RANK, QK:2 * QK].set(gla_w_alpha[1])
    ba = jnp.concatenate([gla_b_alpha[0], gla_b_alpha[1]]).reshape(1, 2 * QK)
    return wqk, wz, wvt, wa.astype(BF16), ba


def kernel(x_prompt, x_sample, c, state_ret, state_gla, c_ctx, w_mod, b_mod, w_in, ret_log_decay,
           gla_w_alpha, gla_b_alpha, gla_norm_w, w_out, final_norm_w):
    assert w_mod.shape[0] == 1, "single-layer model"
    b_ctx, t_ctx, _ = x_prompt.shape
    b_dec, t_dec, _ = x_sample.shape
    assert t_ctx % SCAN_CHUNK == 0 and t_dec % TOKEN_BLOCK == 0 and TOKEN_BLOCK % t_ctx == 0
    assert 1 + b_dec <= MOD_ROWS

    cond = jnp.concatenate(
        [c_ctx[None, :], c, jnp.zeros((MOD_ROWS - 1 - b_dec, D_MODEL), F32)], axis=0)
    mod = _modulation(cond, w_mod[0], b_mod[0]).reshape(MOD_ROWS, 1, 3 * D_MODEL)

    weights = _prep_weights(w_in[0], gla_w_alpha[0], gla_b_alpha[0])
    wout = w_out[0].astype(BF16)
    gw = gla_norm_w[0].reshape(1, DV)
    fw = final_norm_w.reshape(1, D_MODEL)
    ld = ret_log_decay[0]

    per_blk = TOKEN_BLOCK // t_ctx
    xp = x_prompt.reshape(b_ctx // per_blk, TOKEN_BLOCK, D_MODEL)
    ctx_row = lambda b: 0
    qk, gz, vt, lg = _inproj(xp, mod, ctx_row, weights, None)
    qk = qk.reshape(b_ctx, t_ctx, 4 * QK)
    vt = vt.reshape(b_ctx, t_ctx // SCAN_CHUNK, 2 * WV, SCAN_CHUNK)
    lg = lg.reshape(b_ctx, t_ctx, 2 * QK)
    o_r, new_ret = _scan(False, qk, vt, None, ld, None, True, CTX_SEQ_BLOCK, 1)
    o_g, new_gla = _scan(True, qk, vt, lg, None, None, True, CTX_SEQ_BLOCK, 1)
    y_prompt = _outproj(xp, mod, ctx_row, o_r, o_g, gz, gw, fw, wout)
    y_prompt = y_prompt.reshape(b_ctx, t_ctx, D_MODEL)

    dec_row = lambda b: b + 1
    rope = _rope_tables(t_dec)
    qk, gz, vt, lg = _inproj(x_sample, mod, dec_row, weights, rope)
    (o_r,) = _scan(False, qk, vt, None, ld, state_ret, False, 1, DEC_UNROLL)
    (o_g,) = _scan(True, qk, vt, lg, None, state_gla, False, 1, DEC_UNROLL)
    per_seq = t_dec // OUT_TOKEN_BLOCK
    y_sample = _outproj(x_sample, mod, lambda b: b // per_seq + 1, o_r, o_g, gz, gw, fw, wout)
    y_sample = y_sample.reshape(b_dec, t_dec, D_MODEL)

    return (y_prompt, y_sample, new_ret, new_gla)
```

```python
import functools

import jax
import jax.numpy as jnp
import numpy as np
from jax import lax
from jax.experimental import pallas as pl
from jax.experimental.pallas import tpu as pltpu

F32 = jnp.float32
BF16 = jnp.bfloat16

D_MODEL = 1024
N_HEADS = 4
DK = 64
DV = 128
QK = N_HEADS * DK
WV = N_HEADS * DV
GLA_RANK = 16
GLA_TAU = 16.0
EPS = 1e-6
GRID_W = 64
ROPE_BASE = 10000.0

LANES = 128
SCAN_CHUNK = 128
CTX_SEQ_BLOCK = 8
DEC_UNROLL = 8
TOKEN_BLOCK = 1024
INPROJ_SUB = 256
OUTPROJ_SUB = 256
MOD_ROWS = 16
MOD_COL_BLOCK = 512
OUT_TOKEN_BLOCK = 1024
VMEM_LIMIT = 48 * 1024 * 1024
OUT_VMEM_LIMIT = VMEM_LIMIT

_NT = (((1,), (1,)), ((), ()))
_TN = (((0,), (0,)), ((), ()))


def _dot(a, b):
    return jnp.dot(a, b, preferred_element_type=F32)


def _dot_nt(a, b):
    return lax.dot_general(a, b, _NT, preferred_element_type=F32)


def _silu(x):
    return x * jax.nn.sigmoid(x)


def _mod_kernel(c_ref, w_ref, b_ref, o_ref):
    s = _silu(c_ref[...]).astype(BF16)
    o_ref[...] = _dot(s, w_ref[...].astype(BF16)) + b_ref[...]


def _modulation(cond, w_mod, b_mod):
    n_col = w_mod.shape[1]
    return pl.pallas_call(
        _mod_kernel,
        grid=(n_col // MOD_COL_BLOCK,),
        in_specs=[
            pl.BlockSpec((MOD_ROWS, D_MODEL), lambda j: (0, 0)),
            pl.BlockSpec((D_MODEL, MOD_COL_BLOCK), lambda j: (0, j)),
            pl.BlockSpec((1, MOD_COL_BLOCK), lambda j: (0, j)),
        ],
        out_specs=pl.BlockSpec((MOD_ROWS, MOD_COL_BLOCK), lambda j: (0, j)),
        out_shape=jax.ShapeDtypeStruct((MOD_ROWS, n_col), F32),
        compiler_params=pltpu.CompilerParams(vmem_limit_bytes=VMEM_LIMIT),
        name="modulation",
    )(cond, w_mod, b_mod.reshape(1, n_col))


def _rope_tile(t, cos, sin_signed):
    lane = lax.broadcasted_iota(jnp.int32, t.shape, 1)
    first_half = (lane & 32) == 0
    swapped = jnp.where(first_half, pltpu.roll(t, LANES - 32, 1), pltpu.roll(t, 32, 1))
    return t * cos + swapped * sin_signed


def _outproj_rows(rows, x_ref, mod_ref, or_ref, og_ref, gz_ref, gw_ref, fw_ref, wout_ref, y_ref):
    gate = mod_ref[0, :, 2 * D_MODEL:3 * D_MODEL]
    gw = gw_ref[...]
    parts = []
    for h in range(N_HEADS):
        cols = slice(h * DV, (h + 1) * DV)
        t = or_ref[0, rows, cols].astype(F32)
        mu = jnp.mean(t, axis=-1, keepdims=True)
        dlt = t - mu
        var = jnp.mean(dlt * dlt, axis=-1, keepdims=True)
        n = dlt * lax.rsqrt(var + EPS)
        parts.append((n * gz_ref[0, rows, cols].astype(F32)).astype(BF16))
    for h in range(N_HEADS):
        cols = slice(h * DV, (h + 1) * DV)
        t = og_ref[0, rows, cols].astype(F32)
        n = t * lax.rsqrt(jnp.mean(t * t, axis=-1, keepdims=True) + EPS) * gw
        gcols = slice(WV + h * DV, WV + (h + 1) * DV)
        parts.append((n * gz_ref[0, rows, gcols].astype(F32)).astype(BF16))
    mixed = jnp.concatenate(parts, axis=-1)
    out = _dot(mixed, wout_ref[...])
    y = x_ref[0, rows, :] + gate * out
    yn = y * lax.rsqrt(jnp.mean(y * y, axis=-1, keepdims=True) + EPS)
    y_ref[0, rows, :] = yn * fw_ref[...]


def _inproj_kernel(use_rope, x_ref, mod_ref, wqk_ref, wz_ref, wvt_ref, wa_ref, ba_ref, *rest):
    if use_rope:
        cos_ref, sin_ref, qk_ref, z_ref, vt_ref, lg_ref = rest
    else:
        qk_ref, z_ref, vt_ref, lg_ref = rest
    shift = mod_ref[0, :, 0:D_MODEL]
    scale1 = 1.0 + mod_ref[0, :, D_MODEL:2 * D_MODEL]

    def gate_logs(lr_t, rows):
        logit = lax.dot_general(lr_t, wa_ref[...], _TN, preferred_element_type=F32) + ba_ref[...]
        log_sig = jnp.minimum(logit, 0.0) - jnp.log(1.0 + jnp.exp(-jnp.abs(logit)))
        lg_ref[0, rows, :] = log_sig * (1.0 / GLA_TAU)

    for sub in range(TOKEN_BLOCK // INPROJ_SUB):
        rows = slice(sub * INPROJ_SUB, (sub + 1) * INPROJ_SUB)
        x = x_ref[0, rows, :]
        inv = lax.rsqrt(jnp.mean(x * x, axis=-1, keepdims=True) + EPS)
        hb = ((x * inv) * scale1 + shift).astype(BF16)

        lr_t = _dot_nt(wvt_ref[2 * WV:, :], hb).astype(BF16)

        for half in range(2):
            res = _dot(hb, wqk_ref[:, half * 2 * QK:(half + 1) * 2 * QK])
            if use_rope and half == 0:
                for t in range(4):
                    tile = res[:, t * LANES:(t + 1) * LANES]
                    if t >= 2:
                        tile = tile * (DK ** -0.5)
                    tile = _rope_tile(tile, cos_ref[rows, :], sin_ref[rows, :])
                    qk_ref[0, rows, t * LANES:(t + 1) * LANES] = tile.astype(BF16)
            else:
                lane = lax.broadcasted_iota(jnp.int32, (1, 2 * QK), 1)
                scaled = (lane >= QK) if half == 0 else (lane < QK)
                col_scale = jnp.where(scaled, DK ** -0.5, 1.0)
                qk_ref[0, rows, half * 2 * QK:(half + 1) * 2 * QK] = (res * col_scale).astype(BF16)

        gate_logs(lr_t, rows)

        z_ref[0, rows, :] = _silu(_dot(hb, wz_ref[...])).astype(BF16)

        vt = _dot_nt(wvt_ref[0:2 * WV, :], hb)
        for j in range(INPROJ_SUB // SCAN_CHUNK):
            jj = sub * (INPROJ_SUB // SCAN_CHUNK) + j
            vt_ref[0, jj] = vt[:, j * SCAN_CHUNK:(j + 1) * SCAN_CHUNK].astype(BF16)


def _inproj(x, mod, mod_row_fn, weights, rope):
    B, T, _ = x.shape
    wqk, wz, wvt, wa, ba = weights
    nt = T // TOKEN_BLOCK
    cpb = TOKEN_BLOCK // SCAN_CHUNK
    const = lambda b, t: (0, 0)
    in_specs = [
        pl.BlockSpec((1, TOKEN_BLOCK, D_MODEL), lambda b, t: (b, t, 0)),
        pl.BlockSpec((1, 1, 3 * D_MODEL), lambda b, t: (mod_row_fn(b), 0, 0)),
        pl.BlockSpec(wqk.shape, const),
        pl.BlockSpec(wz.shape, const),
        pl.BlockSpec(wvt.shape, const),
        pl.BlockSpec(wa.shape, const),
        pl.BlockSpec(ba.shape, const),
    ]
    args = [x, mod, wqk, wz, wvt, wa, ba]
    if rope is not None:
        in_specs += [pl.BlockSpec((TOKEN_BLOCK, LANES), lambda b, t: (t, 0))] * 2
        args += list(rope)
    out_shape = [
        jax.ShapeDtypeStruct((B, T, 4 * QK), BF16),
        jax.ShapeDtypeStruct((B, T, 2 * WV), BF16),
        jax.ShapeDtypeStruct((B, T // SCAN_CHUNK, 2 * WV, SCAN_CHUNK), BF16),
        jax.ShapeDtypeStruct((B, T, 2 * QK), F32),
    ]
    out_specs = [
        pl.BlockSpec((1, TOKEN_BLOCK, 4 * QK), lambda b, t: (b, t, 0)),
        pl.BlockSpec((1, TOKEN_BLOCK, 2 * WV), lambda b, t: (b, t, 0)),
        pl.BlockSpec((1, cpb, 2 * WV, SCAN_CHUNK), lambda b, t: (b, t, 0, 0)),
        pl.BlockSpec((1, TOKEN_BLOCK, 2 * QK), lambda b, t: (b, t, 0)),
    ]
    return pl.pallas_call(
        functools.partial(_inproj_kernel, rope is not None),
        grid=(B, nt),
        in_specs=in_specs,
        out_specs=out_specs,
        out_shape=out_shape,
        compiler_params=pltpu.CompilerParams(
            dimension_semantics=("parallel", "parallel"), vmem_limit_bytes=VMEM_LIMIT),
        name="inproj_rope" if rope is not None else "inproj",
    )(*args)


def _scan_kernel(is_gla, has_init, want_state, n_chunks, seq_blk, unroll, *refs):
    refs = list(refs)
    q_ref, k_ref, vt_ref = refs[:3]
    pos = 3
    if is_gla:
        lgf_ref, lgb_ref = refs[pos:pos + 2]
        pos += 2
    else:
        ld_ref = refs[pos]
        pos += 1
    if has_init:
        s0_ref = refs[pos]
        pos += 1
    o_ref = refs[pos]
    pos += 1
    if want_state:
        sn_ref = refs[pos]
        pos += 1
    st_ref, oacc_ref = refs[pos:pos + 2]

    C = SCAN_CHUNK
    mid = C // 2
    n_pairs = N_HEADS // 2
    lane = lax.broadcasted_iota(jnp.int32, (1, LANES), 1)
    head_mask = (lane < DK, lane >= DK)
    ri = lax.broadcasted_iota(jnp.int32, (C, C), 0)
    ci = lax.broadcasted_iota(jnp.int32, (C, C), 1)
    tpos = lax.broadcasted_iota(jnp.int32, (C, 1), 0).astype(F32)
    tri = ((ri >= ci), (ci >= ri))

    if has_init:
        for s in range(seq_blk):
            for p in range(n_pairs):
                for d in range(2):
                    s0 = s0_ref[s, 0, d]
                    both = jnp.concatenate([s0[2 * p], s0[2 * p + 1]], axis=0).T
                    st_ref[s, p, d, 0:DV] = both
                    st_ref[s, p, d, DV:2 * DV] = both
    else:
        st_ref[...] = jnp.zeros(st_ref.shape, F32)

    if is_gla:
        tri_ones = [t.astype(BF16) for t in tri]
    else:
        q_mul, k_mul, ret_decay, score_mul = {}, {}, {}, {}
        for p in range(n_pairs):
            for d in range(2):
                ld0 = ld_ref[d, 2 * p]
                ld1 = ld_ref[d, 2 * p + 1]
                ldr = jnp.where(lane < DK, ld0, ld1)
                if d == 0:
                    q_mul[p, d] = jnp.exp(ldr * (tpos + 1.0))
                    k_mul[p, d] = jnp.exp(ldr * (C - 1.0 - tpos))
                    dist = (ri - ci).astype(F32)
                else:
                    q_mul[p, d] = jnp.exp(ldr * (C - tpos))
                    k_mul[p, d] = jnp.exp(ldr * tpos)
                    dist = (ci - ri).astype(F32)
                ret_decay[p, d] = jnp.exp(ldr * float(C))
                score_mul[p, d] = [jnp.where(tri[d], jnp.exp(ldh * dist), 0.0) for ldh in (ld0, ld1)]
            score_mul[p] = [score_mul[p, 0][hh] + score_mul[p, 1][hh] for hh in range(2)]

    def has_intra(d):
        return is_gla or d == 0

    zero = jnp.zeros((), BF16)
    groups = [(s, p, d) for s in range(seq_blk) for p in range(n_pairs) for d in range(2)]

    def run_block(n0, accumulate):
        chains = {}
        for s, p, d in groups:
            ptile = slice(p * LANES, (p + 1) * LANES)
            for u in range(unroll):
                n = n0 + u
                c = n if d == 0 else n_chunks - 1 - n
                t = dict(c=c, rows=pl.ds(pl.multiple_of(c * C, C), C))
                t["q"] = q_ref[s, t["rows"], ptile]
                t["k"] = k_ref[s, t["rows"], ptile]
                if is_gla:
                    lg = (lgf_ref if d == 0 else lgb_ref)[s, t["rows"], ptile]
                    hi = lg.astype(BF16)
                    lo = (lg - hi.astype(F32)).astype(BF16)
                    bb = _dot(tri_ones[d], jnp.concatenate([hi, lo], axis=1))
                    t["b"] = bb[:, :LANES] + bb[:, LANES:]
                chains[s, p, d, u] = t

        for (s, p, d, u), t in chains.items():
            q, k = t["q"], t["k"]
            if is_gla:
                b = t["b"]
                r = b[mid - 1:mid] if d == 0 else b[mid:mid + 1]
                bl = b[C - 1:C] if d == 0 else b[0:1]
                qs = q.astype(F32) * jnp.exp(b - r)
                ks = k.astype(F32) * jnp.exp(r - b)
                q_sc = qs.astype(BF16)
                k_sc = ks.astype(BF16)
                q_dec = (qs * jnp.exp(r)).astype(BF16)
                t["k_dec"] = (ks * jnp.exp(bl - r)).astype(BF16)
                t["decay"] = jnp.exp(bl)
            else:
                q_sc = q
                k_sc = k
                q_dec = (q.astype(F32) * q_mul[p, d]).astype(BF16)
                t["k_dec"] = (k.astype(F32) * k_mul[p, d]).astype(BF16)
                t["decay"] = ret_decay[p, d]
            t["q_dec"] = [jnp.where(head_mask[hh], q_dec, zero) for hh in range(2)]
            if has_intra(d):
                k_heads = jnp.concatenate([jnp.where(head_mask[hh], k_sc, zero) for hh in range(2)], axis=0)
                t["sc"] = _dot_nt(q_sc, k_heads)

        for (s, p, d, u), t in chains.items():
            t["vt"] = vt_ref[s, t["c"], p * 2 * DV:(p + 1) * 2 * DV, :]
            t["kv"] = _dot(t["vt"], t["k_dec"])

        for (s, p, d, u), t in chains.items():
            t["scb"] = []
            if not has_intra(d):
                continue
            for hh in range(2):
                sc = t["sc"][:, hh * C:(hh + 1) * C]
                if is_gla:
                    sc = jnp.where(tri[d], sc, 0.0)
                else:
                    sc = sc * score_mul[p][hh]
                t["scb"].append(sc.astype(BF16))

        state = {g: st_ref[g] for g in groups}
        for u in range(unroll):
            for g in groups:
                t = chains[g + (u,)]
                st = state[g]
                stb = st.astype(BF16)
                t["o"] = []
                for hh in range(2):
                    hrows = slice(hh * DV, (hh + 1) * DV)
                    if has_intra(g[2]):
                        lhs = jnp.concatenate([t["scb"][hh], t["q_dec"][hh]], axis=1)
                        rhs = jnp.concatenate([t["vt"][hrows], stb[hrows]], axis=1)
                    else:
                        lhs, rhs = t["q_dec"][hh], stb[hrows]
                    t["o"].append(_dot_nt(lhs, rhs))
                state[g] = st * t["decay"] + t["kv"]
        for key, st in state.items():
            st_ref[key] = st

        for (s, p, d, u), t in chains.items():
            for hh in range(2):
                cols = slice((2 * p + hh) * DV, (2 * p + hh + 1) * DV)
                if accumulate:
                    o_ref[s, t["rows"], cols] = (oacc_ref[s, t["rows"], cols] + t["o"][hh]).astype(BF16)
                else:
                    oacc_ref[s, t["rows"], cols] = t["o"][hh]

    def loop_body(accumulate, base):
        def body(m, carry):
            run_block(base + m * unroll, accumulate)
            return carry
        return body

    half = n_chunks // 2
    assert half % unroll == 0
    lax.fori_loop(0, half // unroll, loop_body(False, 0), 0)
    lax.fori_loop(0, half // unroll, loop_body(True, half), 0)

    if want_state:
        for s, p, d in groups:
            for hh in range(2):
                sn_ref[s, 0, d, 2 * p + hh] = st_ref[s, p, d, hh * DV:(hh + 1) * DV].T[hh * DK:(hh + 1) * DK, :]


def _scan(is_gla, qk, vt, lg, ld, s0, want_state, seq_blk, unroll):
    B, T, _ = qk.shape
    n_chunks = T // SCAN_CHUNK
    assert n_chunks % 2 == 0 and B % seq_blk == 0
    sb = seq_blk
    grp = 1 if is_gla else 0
    in_specs = [
        pl.BlockSpec((sb, T, QK), lambda b: (b, 0, 2 * grp)),
        pl.BlockSpec((sb, T, QK), lambda b: (b, 0, 2 * grp + 1)),
        pl.BlockSpec((sb, n_chunks, WV, SCAN_CHUNK), lambda b: (b, 0, grp, 0)),
    ]
    args = [qk, qk, vt]
    if is_gla:
        in_specs += [
            pl.BlockSpec((sb, T, QK), lambda b: (b, 0, 0)),
            pl.BlockSpec((sb, T, QK), lambda b: (b, 0, 1)),
        ]
        args += [lg, lg]
    else:
        in_specs += [pl.BlockSpec(memory_space=pltpu.SMEM)]
        args += [ld]
    state_spec = pl.BlockSpec((sb, 1, 2, N_HEADS, DK, DV), lambda b: (b, 0, 0, 0, 0, 0))
    if s0 is not None:
        in_specs += [state_spec]
        args += [s0]
    out_shape = [jax.ShapeDtypeStruct((B, T, WV), BF16)]
    out_specs = [pl.BlockSpec((sb, T, WV), lambda b: (b, 0, 0))]
    if want_state:
        out_shape += [jax.ShapeDtypeStruct((B, 1, 2, N_HEADS, DK, DV), F32)]
        out_specs += [state_spec]
    name = ("gla" if is_gla else "ret") + ("_init" if s0 is not None else "") + "_scan"
    return pl.pallas_call(
        functools.partial(_scan_kernel, is_gla, s0 is not None, want_state, n_chunks, sb, unroll),
        grid=(B // sb,),
        in_specs=in_specs,
        out_specs=out_specs,
        out_shape=out_shape,
        scratch_shapes=[pltpu.VMEM((sb, N_HEADS // 2, 2, 2 * DV, LANES), F32),
                        pltpu.VMEM((sb, T, WV), F32)],
        compiler_params=pltpu.CompilerParams(
            dimension_semantics=("arbitrary",), vmem_limit_bytes=VMEM_LIMIT),
        name=name,
    )(*args)


def _outproj_kernel(*refs):
    for sub in range(OUT_TOKEN_BLOCK // OUTPROJ_SUB):
        _outproj_rows(slice(sub * OUTPROJ_SUB, (sub + 1) * OUTPROJ_SUB), *refs)


def _outproj(x, mod, mod_row_fn, o_r, o_g, gz, gw, fw, wout):
    n_tok = x.shape[0] * x.shape[1]
    B = n_tok // OUT_TOKEN_BLOCK
    x, o_r, o_g, gz = (a.reshape(B, OUT_TOKEN_BLOCK, a.shape[-1]) for a in (x, o_r, o_g, gz))
    const = lambda b: (0, 0)
    blk = lambda width: pl.BlockSpec((1, OUT_TOKEN_BLOCK, width), lambda b: (b, 0, 0))
    return pl.pallas_call(
        _outproj_kernel,
        grid=(B,),
        in_specs=[
            blk(D_MODEL),
            pl.BlockSpec((1, 1, 3 * D_MODEL), lambda b: (mod_row_fn(b), 0, 0)),
            blk(WV),
            blk(WV),
            blk(2 * WV),
            pl.BlockSpec((1, DV), const),
            pl.BlockSpec((1, D_MODEL), const),
            pl.BlockSpec(wout.shape, const),
        ],
        out_specs=blk(D_MODEL),
        out_shape=jax.ShapeDtypeStruct((B, OUT_TOKEN_BLOCK, D_MODEL), F32),
        compiler_params=pltpu.CompilerParams(
            dimension_semantics=("parallel",), vmem_limit_bytes=OUT_VMEM_LIMIT),
        name="outproj",
    )(x, mod, o_r, o_g, gz, gw, fw, wout)


def _rope_tables(n_tokens):
    rows = n_tokens // GRID_W
    rr, cc = np.meshgrid(np.arange(rows), np.arange(GRID_W), indexing="ij")
    rr = rr.reshape(-1).astype(np.float32)
    cc = cc.reshape(-1).astype(np.float32)
    n_freq = DK // 4
    inv = np.float32(ROPE_BASE) ** (-np.arange(n_freq, dtype=np.float32) / np.float32(n_freq))
    ang = np.concatenate([rr[:, None] * inv, cc[:, None] * inv], axis=-1)
    cos = np.cos(ang.astype(np.float64)).astype(np.float32)
    sin = np.sin(ang.astype(np.float64)).astype(np.float32)
    cos_t = np.tile(np.concatenate([cos, cos], axis=-1), (1, LANES // DK))
    sin_t = np.tile(np.concatenate([-sin, sin], axis=-1), (1, LANES // DK))
    return jnp.asarray(cos_t), jnp.asarray(sin_t)


def _prep_weights(w_in, gla_w_alpha, gla_b_alpha):
    q_r, k_r = w_in[:, 0:QK], w_in[:, QK:2 * QK]
    o = 2 * QK
    v_r, z_r = w_in[:, o:o + WV], w_in[:, o + WV:o + 2 * WV]
    o += 2 * WV
    q_g, k_g = w_in[:, o:o + QK], w_in[:, o + QK:o + 2 * QK]
    o += 2 * QK
    v_g, z_g = w_in[:, o:o + WV], w_in[:, o + WV:o + 2 * WV]
    o += 2 * WV
    lr = w_in[:, o:o + 2 * GLA_RANK]
    wqk = jnp.concatenate([q_r, k_r, q_g, k_g], axis=1).astype(BF16)
    wz = jnp.concatenate([z_r, z_g], axis=1).astype(BF16)
    wvt = jnp.concatenate([v_r, v_g, lr], axis=1).T.astype(BF16)
    wa = jnp.zeros((2 * GLA_RANK, 2 * QK), F32)
    wa = wa.at[0:GLA_RANK, 0:QK].set(gla_w_alpha[0])
    wa = wa.at[GLA_RANK:2 * GLA_RANK, QK:2 * QK].set(gla_w_alpha[1])
    ba = jnp.concatenate([gla_b_alpha[0], gla_b_alpha[1]]).reshape(1, 2 * QK)
    return wqk, wz, wvt, wa.astype(BF16), ba


def kernel(x_prompt, x_sample, c, state_ret, state_gla, c_ctx, w_mod, b_mod, w_in, ret_log_decay,
           gla_w_alpha, gla_b_alpha, gla_norm_w, w_out, final_norm_w):
    assert w_mod.shape[0] == 1, "single-layer model"
    b_ctx, t_ctx, _ = x_prompt.shape
    b_dec, t_dec, _ = x_sample.shape
    assert t_ctx % SCAN_CHUNK == 0 and t_dec % TOKEN_BLOCK == 0 and TOKEN_BLOCK % t_ctx == 0
    assert 1 + b_dec <= MOD_ROWS

    cond = jnp.concatenate(
        [c_ctx[None, :], c, jnp.zeros((MOD_ROWS - 1 - b_dec, D_MODEL), F32)], axis=0)
    mod = _modulation(cond, w_mod[0], b_mod[0]).reshape(MOD_ROWS, 1, 3 * D_MODEL)

    weights = _prep_weights(w_in[0], gla_w_alpha[0], gla_b_alpha[0])
    wout = w_out[0].astype(BF16)
    gw = gla_norm_w[0].reshape(1, DV)
    fw = final_norm_w.reshape(1, D_MODEL)
    ld = ret_log_decay[0]

    per_blk = TOKEN_BLOCK // t_ctx
    xp = x_prompt.reshape(b_ctx // per_blk, TOKEN_BLOCK, D_MODEL)
    ctx_row = lambda b: 0
    qk, gz, vt, lg = _inproj(xp, mod, ctx_row, weights, None)
    qk = qk.reshape(b_ctx, t_ctx, 4 * QK)
    vt = vt.reshape(b_ctx, t_ctx // SCAN_CHUNK, 2 * WV, SCAN_CHUNK)
    lg = lg.reshape(b_ctx, t_ctx, 2 * QK)
    o_r, new_ret = _scan(False, qk, vt, None, ld, None, True, CTX_SEQ_BLOCK, 1)
    o_g, new_gla = _scan(True, qk, vt, lg, None, None, True, CTX_SEQ_BLOCK, 1)
    y_prompt = _outproj(xp, mod, ctx_row, o_r, o_g, gz, gw, fw, wout)
    y_prompt = y_prompt.reshape(b_ctx, t_ctx, D_MODEL)

    dec_row = lambda b: b + 1
    rope = _rope_tables(t_dec)
    qk, gz, vt, lg = _inproj(x_sample, mod, dec_row, weights, rope)
    (o_r,) = _scan(False, qk, vt, None, ld, state_ret, False, 1, DEC_UNROLL)
    (o_g,) = _scan(True, qk, vt, lg, None, state_gla, False, 1, DEC_UNROLL)
    per_seq = t_dec // OUT_TOKEN_BLOCK
    y_sample = _outproj(x_sample, mod, lambda b: b // per_seq + 1, o_r, o_g, gz, gw, fw, wout)
    y_sample = y_sample.reshape(b_dec, t_dec, D_MODEL)

    return (y_prompt, y_sample, new_ret, new_gla)
```

```python
import functools

import jax
import jax.numpy as jnp
import numpy as np
from jax import lax
from jax.experimental import pallas as pl
from jax.experimental.pallas import tpu as pltpu

F32 = jnp.float32
BF16 = jnp.bfloat16

D_MODEL = 1024
N_HEADS = 4
DK = 64
DV = 128
QK = N_HEADS * DK
WV = N_HEADS * DV
GLA_RANK = 16
GLA_TAU = 16.0
EPS = 1e-6
GRID_W = 64
ROPE_BASE = 10000.0

LANES = 128
SCAN_CHUNK = 128
CTX_SEQ_BLOCK = 8
DEC_UNROLL = 8
RET_PIPE_GROUP = 2
GLA_PIPE_GROUP = 32
TOKEN_BLOCK = 1024
INPROJ_SUB = 256
OUTPROJ_SUB = 256
MOD_ROWS = 16
MOD_COL_BLOCK = 512
OUT_TOKEN_BLOCK = 1024
VMEM_LIMIT = 48 * 1024 * 1024
OUT_VMEM_LIMIT = VMEM_LIMIT

_NT = (((1,), (1,)), ((), ()))
_TN = (((0,), (0,)), ((), ()))


def _dot(a, b):
    return jnp.dot(a, b, preferred_element_type=F32)


def _dot_nt(a, b):
    return lax.dot_general(a, b, _NT, preferred_element_type=F32)


def _silu(x):
    return x * jax.nn.sigmoid(x)


def _mod_kernel(c_ref, w_ref, b_ref, o_ref):
    s = _silu(c_ref[...]).astype(BF16)
    o_ref[...] = _dot(s, w_ref[...].astype(BF16)) + b_ref[...]


def _modulation(cond, w_mod, b_mod):
    n_col = w_mod.shape[1]
    return pl.pallas_call(
        _mod_kernel,
        grid=(n_col // MOD_COL_BLOCK,),
        in_specs=[
            pl.BlockSpec((MOD_ROWS, D_MODEL), lambda j: (0, 0)),
            pl.BlockSpec((D_MODEL, MOD_COL_BLOCK), lambda j: (0, j)),
            pl.BlockSpec((1, MOD_COL_BLOCK), lambda j: (0, j)),
        ],
        out_specs=pl.BlockSpec((MOD_ROWS, MOD_COL_BLOCK), lambda j: (0, j)),
        out_shape=jax.ShapeDtypeStruct((MOD_ROWS, n_col), F32),
        compiler_params=pltpu.CompilerParams(vmem_limit_bytes=VMEM_LIMIT),
        name="modulation",
    )(cond, w_mod, b_mod.reshape(1, n_col))


def _rope_tile(t, cos, sin_signed):
    lane = lax.broadcasted_iota(jnp.int32, t.shape, 1)
    first_half = (lane & 32) == 0
    swapped = jnp.where(first_half, pltpu.roll(t, LANES - 32, 1), pltpu.roll(t, 32, 1))
    return t * cos + swapped * sin_signed


def _outproj_rows(rows, x_ref, mod_ref, or_ref, og_ref, gz_ref, gw_ref, fw_ref, wout_ref, y_ref):
    gate = mod_ref[0, :, 2 * D_MODEL:3 * D_MODEL]
    gw = gw_ref[...]
    parts = []
    for h in range(N_HEADS):
        cols = slice(h * DV, (h + 1) * DV)
        t = or_ref[0, rows, cols].astype(F32)
        mu = jnp.mean(t, axis=-1, keepdims=True)
        dlt = t - mu
        var = jnp.mean(dlt * dlt, axis=-1, keepdims=True)
        n = dlt * lax.rsqrt(var + EPS)
        parts.append((n * gz_ref[0, rows, cols].astype(F32)).astype(BF16))
    for h in range(N_HEADS):
        cols = slice(h * DV, (h + 1) * DV)
        t = og_ref[0, rows, cols].astype(F32)
        n = t * lax.rsqrt(jnp.mean(t * t, axis=-1, keepdims=True) + EPS) * gw
        gcols = slice(WV + h * DV, WV + (h + 1) * DV)
        parts.append((n * gz_ref[0, rows, gcols].astype(F32)).astype(BF16))
    mixed = jnp.concatenate(parts, axis=-1)
    out = _dot(mixed, wout_ref[...])
    y = x_ref[0, rows, :] + gate * out
    yn = y * lax.rsqrt(jnp.mean(y * y, axis=-1, keepdims=True) + EPS)
    y_ref[0, rows, :] = yn * fw_ref[...]


def _inproj_kernel(use_rope, x_ref, mod_ref, wqk_ref, wz_ref, wvt_ref, wa_ref, ba_ref, *rest):
    if use_rope:
        cos_ref, sin_ref, qk_ref, z_ref, vt_ref, lg_ref = rest
    else:
        qk_ref, z_ref, vt_ref, lg_ref = rest
    shift = mod_ref[0, :, 0:D_MODEL]
    scale1 = 1.0 + mod_ref[0, :, D_MODEL:2 * D_MODEL]

    def gate_logs(lr_t, rows):
        logit = lax.dot_general(lr_t, wa_ref[...], _TN, preferred_element_type=F32) + ba_ref[...]
        log_sig = jnp.minimum(logit, 0.0) - jnp.log(1.0 + jnp.exp(-jnp.abs(logit)))
        lg_ref[0, rows, :] = log_sig * (1.0 / GLA_TAU)

    for sub in range(TOKEN_BLOCK // INPROJ_SUB):
        rows = slice(sub * INPROJ_SUB, (sub + 1) * INPROJ_SUB)
        x = x_ref[0, rows, :]
        inv = lax.rsqrt(jnp.mean(x * x, axis=-1, keepdims=True) + EPS)
        hb = ((x * inv) * scale1 + shift).astype(BF16)

        lr_t = _dot_nt(wvt_ref[2 * WV:, :], hb).astype(BF16)

        for half in range(2):
            res = _dot(hb, wqk_ref[:, half * 2 * QK:(half + 1) * 2 * QK])
            if use_rope and half == 0:
                for t in range(4):
                    tile = res[:, t * LANES:(t + 1) * LANES]
                    if t >= 2:
                        tile = tile * (DK ** -0.5)
                    tile = _rope_tile(tile, cos_ref[rows, :], sin_ref[rows, :])
                    qk_ref[0, rows, t * LANES:(t + 1) * LANES] = tile.astype(BF16)
            else:
                lane = lax.broadcasted_iota(jnp.int32, (1, 2 * QK), 1)
                scaled = (lane >= QK) if half == 0 else (lane < QK)
                col_scale = jnp.where(scaled, DK ** -0.5, 1.0)
                qk_ref[0, rows, half * 2 * QK:(half + 1) * 2 * QK] = (res * col_scale).astype(BF16)

        gate_logs(lr_t, rows)

        z_ref[0, rows, :] = _silu(_dot(hb, wz_ref[...])).astype(BF16)

        vt = _dot_nt(wvt_ref[0:2 * WV, :], hb)
        for j in range(INPROJ_SUB // SCAN_CHUNK):
            jj = sub * (INPROJ_SUB // SCAN_CHUNK) + j
            vt_ref[0, jj] = vt[:, j * SCAN_CHUNK:(j + 1) * SCAN_CHUNK].astype(BF16)


def _inproj(x, mod, mod_row_fn, weights, rope):
    B, T, _ = x.shape
    wqk, wz, wvt, wa, ba = weights
    nt = T // TOKEN_BLOCK
    cpb = TOKEN_BLOCK // SCAN_CHUNK
    const = lambda b, t: (0, 0)
    in_specs = [
        pl.BlockSpec((1, TOKEN_BLOCK, D_MODEL), lambda b, t: (b, t, 0)),
        pl.BlockSpec((1, 1, 3 * D_MODEL), lambda b, t: (mod_row_fn(b), 0, 0)),
        pl.BlockSpec(wqk.shape, const),
        pl.BlockSpec(wz.shape, const),
        pl.BlockSpec(wvt.shape, const),
        pl.BlockSpec(wa.shape, const),
        pl.BlockSpec(ba.shape, const),
    ]
    args = [x, mod, wqk, wz, wvt, wa, ba]
    if rope is not None:
        in_specs += [pl.BlockSpec((TOKEN_BLOCK, LANES), lambda b, t: (t, 0))] * 2
        args += list(rope)
    out_shape = [
        jax.ShapeDtypeStruct((B, T, 4 * QK), BF16),
        jax.ShapeDtypeStruct((B, T, 2 * WV), BF16),
        jax.ShapeDtypeStruct((B, T // SCAN_CHUNK, 2 * WV, SCAN_CHUNK), BF16),
        jax.ShapeDtypeStruct((B, T, 2 * QK), F32),
    ]
    out_specs = [
        pl.BlockSpec((1, TOKEN_BLOCK, 4 * QK), lambda b, t: (b, t, 0)),
        pl.BlockSpec((1, TOKEN_BLOCK, 2 * WV), lambda b, t: (b, t, 0)),
        pl.BlockSpec((1, cpb, 2 * WV, SCAN_CHUNK), lambda b, t: (b, t, 0, 0)),
        pl.BlockSpec((1, TOKEN_BLOCK, 2 * QK), lambda b, t: (b, t, 0)),
    ]
    return pl.pallas_call(
        functools.partial(_inproj_kernel, rope is not None),
        grid=(B, nt),
        in_specs=in_specs,
        out_specs=out_specs,
        out_shape=out_shape,
        compiler_params=pltpu.CompilerParams(
            dimension_semantics=("parallel", "parallel"), vmem_limit_bytes=VMEM_LIMIT),
        name="inproj_rope" if rope is not None else "inproj",
    )(*args)


def _scan_kernel(is_gla, has_init, want_state, n_chunks, seq_blk, unroll, *refs):
    refs = list(refs)
    q_ref, k_ref, vt_ref = refs[:3]
    pos = 3
    if is_gla:
        lgf_ref, lgb_ref = refs[pos:pos + 2]
        pos += 2
    else:
        ld_ref = refs[pos]
        pos += 1
    if has_init:
        s0_ref = refs[pos]
        pos += 1
    o_ref = refs[pos]
    pos += 1
    if want_state:
        sn_ref = refs[pos]
        pos += 1
    st_ref, oacc_ref = refs[pos:pos + 2]

    C = SCAN_CHUNK
    mid = C // 2
    n_pairs = N_HEADS // 2
    lane = lax.broadcasted_iota(jnp.int32, (1, LANES), 1)
    head_mask = (lane < DK, lane >= DK)
    ri = lax.broadcasted_iota(jnp.int32, (C, C), 0)
    ci = lax.broadcasted_iota(jnp.int32, (C, C), 1)
    tpos = lax.broadcasted_iota(jnp.int32, (C, 1), 0).astype(F32)
    tri = ((ri >= ci), (ci >= ri))

    if has_init:
        for s in range(seq_blk):
            for p in range(n_pairs):
                for d in range(2):
                    s0 = s0_ref[s, 0, d]
                    both = jnp.concatenate([s0[2 * p], s0[2 * p + 1]], axis=0).T
                    st_ref[s, p, d, 0:DV] = both
                    st_ref[s, p, d, DV:2 * DV] = both
    else:
        st_ref[...] = jnp.zeros(st_ref.shape, F32)

    if is_gla:
        tri_ones = [t.astype(BF16) for t in tri]
    else:
        q_mul, k_mul, ret_decay, score_mul = {}, {}, {}, {}
        for p in range(n_pairs):
            for d in range(2):
                ld0 = ld_ref[d, 2 * p]
                ld1 = ld_ref[d, 2 * p + 1]
                ldr = jnp.where(lane < DK, ld0, ld1)
                if d == 0:
                    q_mul[p, d] = jnp.exp(ldr * (tpos + 1.0))
                    k_mul[p, d] = jnp.exp(ldr * (C - 1.0 - tpos))
                    dist = (ri - ci).astype(F32)
                else:
                    q_mul[p, d] = jnp.exp(ldr * (C - tpos))
                    k_mul[p, d] = jnp.exp(ldr * tpos)
                    dist = (ci - ri).astype(F32)
                ret_decay[p, d] = jnp.exp(ldr * float(C))
                score_mul[p, d] = [jnp.where(tri[d], jnp.exp(ldh * dist), 0.0) for ldh in (ld0, ld1)]
            score_mul[p] = [score_mul[p, 0][hh] + score_mul[p, 1][hh] for hh in range(2)]

    def has_intra(d):
        return is_gla or d == 0

    zero = jnp.zeros((), BF16)
    groups = [(s, p, d) for s in range(seq_blk) for p in range(n_pairs) for d in range(2)]

    def run_block(n0, accumulate):
        keys = [g + (u,) for u in range(unroll) for g in groups]
        chains = {}

        def stage_load(key):
            s, p, d, u = key
            ptile = slice(p * LANES, (p + 1) * LANES)
            n = n0 + u
            c = n if d == 0 else n_chunks - 1 - n
            t = dict(c=c, rows=pl.ds(pl.multiple_of(c * C, C), C))
            t["q"] = q_ref[s, t["rows"], ptile]
            t["k"] = k_ref[s, t["rows"], ptile]
            if is_gla:
                lg = (lgf_ref if d == 0 else lgb_ref)[s, t["rows"], ptile]
                hi = lg.astype(BF16)
                lo = (lg - hi.astype(F32)).astype(BF16)
                bb = _dot(tri_ones[d], jnp.concatenate([hi, lo], axis=1))
                t["b"] = bb[:, :LANES] + bb[:, LANES:]
            chains[key] = t

        def stage_scores(key):
            s, p, d, u = key
            t = chains[key]
            q, k = t.pop("q"), t.pop("k")
            if is_gla:
                b = t.pop("b")
                r = b[mid - 1:mid] if d == 0 else b[mid:mid + 1]
                bl = b[C - 1:C] if d == 0 else b[0:1]
                qs = q.astype(F32) * jnp.exp(b - r)
                ks = k.astype(F32) * jnp.exp(r - b)
                q_sc = qs.astype(BF16)
                k_sc = ks.astype(BF16)
                q_dec = (qs * jnp.exp(r)).astype(BF16)
                t["k_dec"] = (ks * jnp.exp(bl - r)).astype(BF16)
                t["decay"] = jnp.exp(bl)
            else:
                q_sc = q
                k_sc = k
                q_dec = (q.astype(F32) * q_mul[p, d]).astype(BF16)
                t["k_dec"] = (k.astype(F32) * k_mul[p, d]).astype(BF16)
                t["decay"] = ret_decay[p, d]
            t["q_dec"] = [jnp.where(head_mask[hh], q_dec, zero) for hh in range(2)]
            if has_intra(d):
                k_heads = jnp.concatenate([jnp.where(head_mask[hh], k_sc, zero) for hh in range(2)], axis=0)
                t["sc"] = _dot_nt(q_sc, k_heads)

        def stage_kv(key):
            s, p, d, u = key
            t = chains[key]
            t["vt"] = vt_ref[s, t["c"], p * 2 * DV:(p + 1) * 2 * DV, :]
            t["kv"] = _dot(t["vt"], t.pop("k_dec"))
            t["scb"] = []
            if has_intra(d):
                sc_all = t.pop("sc")
                for hh in range(2):
                    sc = sc_all[:, hh * C:(hh + 1) * C]
                    if is_gla:
                        sc = jnp.where(tri[d], sc, 0.0)
                    else:
                        sc = sc * score_mul[p][hh]
                    t["scb"].append(sc.astype(BF16))

        def stage_out(key):
            s, p, d, u = key
            t = chains[key]
            st = st_ref[s, p, d]
            stb = st.astype(BF16)
            t["o"] = []
            for hh in range(2):
                hrows = slice(hh * DV, (hh + 1) * DV)
                if has_intra(d):
                    lhs = jnp.concatenate([t["scb"][hh], t["q_dec"][hh]], axis=1)
                    rhs = jnp.concatenate([t["vt"][hrows], stb[hrows]], axis=1)
                else:
                    lhs, rhs = t["q_dec"][hh], stb[hrows]
                t["o"].append(_dot_nt(lhs, rhs))
            st_ref[s, p, d] = st * t.pop("decay") + t.pop("kv")

        def stage_store(key):
            s, p, d, u = key
            t = chains.pop(key)
            for hh in range(2):
                cols = slice((2 * p + hh) * DV, (2 * p + hh + 1) * DV)
                if accumulate:
                    o_ref[s, t["rows"], cols] = (oacc_ref[s, t["rows"], cols] + t["o"][hh]).astype(BF16)
                else:
                    oacc_ref[s, t["rows"], cols] = t["o"][hh]

        stages = (stage_load, stage_scores, stage_kv, stage_out, stage_store)
        grp = GLA_PIPE_GROUP if is_gla else RET_PIPE_GROUP
        key_groups = [keys[i:i + grp] for i in range(0, len(keys), grp)]
        for step in range(len(key_groups) + len(stages) - 1):
            for k in reversed(range(len(stages))):
                if 0 <= step - k < len(key_groups):
                    for key in key_groups[step - k]:
                        stages[k](key)

    def loop_body(accumulate, base):
        def body(m, carry):
            run_block(base + m * unroll, accumulate)
            return carry
        return body

    half = n_chunks // 2
    assert half % unroll == 0
    lax.fori_loop(0, half // unroll, loop_body(False, 0), 0)
    lax.fori_loop(0, half // unroll, loop_body(True, half), 0)

    if want_state:
        for s, p, d in groups:
            for hh in range(2):
                sn_ref[s, 0, d, 2 * p + hh] = st_ref[s, p, d, hh * DV:(hh + 1) * DV].T[hh * DK:(hh + 1) * DK, :]


def _scan(is_gla, qk, vt, lg, ld, s0, want_state, seq_blk, unroll):
    B, T, _ = qk.shape
    n_chunks = T // SCAN_CHUNK
    assert n_chunks % 2 == 0 and B % seq_blk == 0
    sb = seq_blk
    grp = 1 if is_gla else 0
    in_specs = [
        pl.BlockSpec((sb, T, QK), lambda b: (b, 0, 2 * grp)),
        pl.BlockSpec((sb, T, QK), lambda b: (b, 0, 2 * grp + 1)),
        pl.BlockSpec((sb, n_chunks, WV, SCAN_CHUNK), lambda b: (b, 0, grp, 0)),
    ]
    args = [qk, qk, vt]
    if is_gla:
        in_specs += [
            pl.BlockSpec((sb, T, QK), lambda b: (b, 0, 0)),
            pl.BlockSpec((sb, T, QK), lambda b: (b, 0, 1)),
        ]
        args += [lg, lg]
    else:
        in_specs += [pl.BlockSpec(memory_space=pltpu.SMEM)]
        args += [ld]
    state_spec = pl.BlockSpec((sb, 1, 2, N_HEADS, DK, DV), lambda b: (b, 0, 0, 0, 0, 0))
    if s0 is not None:
        in_specs += [state_spec]
        args += [s0]
    out_shape = [jax.ShapeDtypeStruct((B, T, WV), BF16)]
    out_specs = [pl.BlockSpec((sb, T, WV), lambda b: (b, 0, 0))]
    if want_state:
        out_shape += [jax.ShapeDtypeStruct((B, 1, 2, N_HEADS, DK, DV), F32)]
        out_specs += [state_spec]
    name = ("gla" if is_gla else "ret") + ("_init" if s0 is not None else "") + "_scan"
    return pl.pallas_call(
        functools.partial(_scan_kernel, is_gla, s0 is not None, want_state, n_chunks, sb, unroll),
        grid=(B // sb,),
        in_specs=in_specs,
        out_specs=out_specs,
        out_shape=out_shape,
        scratch_shapes=[pltpu.VMEM((sb, N_HEADS // 2, 2, 2 * DV, LANES), F32),
                        pltpu.VMEM((sb, T, WV), F32)],
        compiler_params=pltpu.CompilerParams(
            dimension_semantics=("arbitrary",), vmem_limit_bytes=VMEM_LIMIT),
        name=name,
    )(*args)


def _outproj_kernel(*refs):
    for sub in range(OUT_TOKEN_BLOCK // OUTPROJ_SUB):
        _outproj_rows(slice(sub * OUTPROJ_SUB, (sub + 1) * OUTPROJ_SUB), *refs)


def _outproj(x, mod, mod_row_fn, o_r, o_g, gz, gw, fw, wout):
    n_tok = x.shape[0] * x.shape[1]
    B = n_tok // OUT_TOKEN_BLOCK
    x, o_r, o_g, gz = (a.reshape(B, OUT_TOKEN_BLOCK, a.shape[-1]) for a in (x, o_r, o_g, gz))
    const = lambda b: (0, 0)
    blk = lambda width: pl.BlockSpec((1, OUT_TOKEN_BLOCK, width), lambda b: (b, 0, 0))
    return pl.pallas_call(
        _outproj_kernel,
        grid=(B,),
        in_specs=[
            blk(D_MODEL),
            pl.BlockSpec((1, 1, 3 * D_MODEL), lambda b: (mod_row_fn(b), 0, 0)),
            blk(WV),
            blk(WV),
            blk(2 * WV),
            pl.BlockSpec((1, DV), const),
            pl.BlockSpec((1, D_MODEL), const),
            pl.BlockSpec(wout.shape, const),
        ],
        out_specs=blk(D_MODEL),
        out_shape=jax.ShapeDtypeStruct((B, OUT_TOKEN_BLOCK, D_MODEL), F32),
        compiler_params=pltpu.CompilerParams(
            dimension_semantics=("parallel",), vmem_limit_bytes=OUT_VMEM_LIMIT),
        name="outproj",
    )(x, mod, o_r, o_g, gz, gw, fw, wout)


def _rope_tables(n_tokens):
    rows = n_tokens // GRID_W
    rr, cc = np.meshgrid(np.arange(rows), np.arange(GRID_W), indexing="ij")
    rr = rr.reshape(-1).astype(np.float32)
    cc = cc.reshape(-1).astype(np.float32)
    n_freq = DK // 4
    inv = np.float32(ROPE_BASE) ** (-np.arange(n_freq, dtype=np.float32) / np.float32(n_freq))
    ang = np.concatenate([rr[:, None] * inv, cc[:, None] * inv], axis=-1)
    cos = np.cos(ang.astype(np.float64)).astype(np.float32)
    sin = np.sin(ang.astype(np.float64)).astype(np.float32)
    cos_t = np.tile(np.concatenate([cos, cos], axis=-1), (1, LANES // DK))
    sin_t = np.tile(np.concatenate([-sin, sin], axis=-1), (1, LANES // DK))
    return jnp.asarray(cos_t), jnp.asarray(sin_t)


def _prep_weights(w_in, gla_w_alpha, gla_b_alpha):
    q_r, k_r = w_in[:, 0:QK], w_in[:, QK:2 * QK]
    o = 2 * QK
    v_r, z_r = w_in[:, o:o + WV], w_in[:, o + WV:o + 2 * WV]
    o += 2 * WV
    q_g, k_g = w_in[:, o:o + QK], w_in[:, o + QK:o + 2 * QK]
    o += 2 * QK
    v_g, z_g = w_in[:, o:o + WV], w_in[:, o + WV:o + 2 * WV]
    o += 2 * WV
    lr = w_in[:, o:o + 2 * GLA_RANK]
    wqk = jnp.concatenate([q_r, k_r, q_g, k_g], axis=1).astype(BF16)
    wz = jnp.concatenate([z_r, z_g], axis=1).astype(BF16)
    wvt = jnp.concatenate([v_r, v_g, lr], axis=1).T.astype(BF16)
    wa = jnp.zeros((2 * GLA_RANK, 2 * QK), F32)
    wa = wa.at[0:GLA_RANK, 0:QK].set(gla_w_alpha[0])
    wa = wa.at[GLA_RANK:2 * GLA_RANK, QK:2 * QK].set(gla_w_alpha[1])
    ba = jnp.concatenate([gla_b_alpha[0], gla_b_alpha[1]]).reshape(1, 2 * QK)
    return wqk, wz, wvt, wa.astype(BF16), ba


def kernel(x_prompt, x_sample, c, state_ret, state_gla, c_ctx, w_mod, b_mod, w_in, ret_log_decay,
           gla_w_alpha, gla_b_alpha, gla_norm_w, w_out, final_norm_w):
    assert w_mod.shape[0] == 1, "single-layer model"
    b_ctx, t_ctx, _ = x_prompt.shape
    b_dec, t_dec, _ = x_sample.shape
    assert t_ctx % SCAN_CHUNK == 0 and t_dec % TOKEN_BLOCK == 0 and TOKEN_BLOCK % t_ctx == 0
    assert 1 + b_dec <= MOD_ROWS

    cond = jnp.concatenate(
        [c_ctx[None, :], c, jnp.zeros((MOD_ROWS - 1 - b_dec, D_MODEL), F32)], axis=0)
    mod = _modulation(cond, w_mod[0], b_mod[0]).reshape(MOD_ROWS, 1, 3 * D_MODEL)

    weights = _prep_weights(w_in[0], gla_w_alpha[0], gla_b_alpha[0])
    wout = w_out[0].astype(BF16)
    gw = gla_norm_w[0].reshape(1, DV)
    fw = final_norm_w.reshape(1, D_MODEL)
    ld = ret_log_decay[0]

    per_blk = TOKEN_BLOCK // t_ctx
    xp = x_prompt.reshape(b_ctx // per_blk, TOKEN_BLOCK, D_MODEL)
    ctx_row = lambda b: 0
    qk, gz, vt, lg = _inproj(xp, mod, ctx_row, weights, None)
    qk = qk.reshape(b_ctx, t_ctx, 4 * QK)
    vt = vt.reshape(b_ctx, t_ctx // SCAN_CHUNK, 2 * WV, SCAN_CHUNK)
    lg = lg.reshape(b_ctx, t_ctx, 2 * QK)
    o_r, new_ret = _scan(False, qk, vt, None, ld, None, True, CTX_SEQ_BLOCK, 1)
    o_g, new_gla = _scan(True, qk, vt, lg, None, None, True, CTX_SEQ_BLOCK, 1)
    y_prompt = _outproj(xp, mod, ctx_row, o_r, o_g, gz, gw, fw, wout)
    y_prompt = y_prompt.reshape(b_ctx, t_ctx, D_MODEL)

    dec_row = lambda b: b + 1
    rope = _rope_tables(t_dec)
    qk, gz, vt, lg = _inproj(x_sample, mod, dec_row, weights, rope)
    (o_r,) = _scan(False, qk, vt, None, ld, state_ret, False, 1, DEC_UNROLL)
    (o_g,) = _scan(True, qk, vt, lg, None, state_gla, False, 1, DEC_UNROLL)
    per_seq = t_dec // OUT_TOKEN_BLOCK
    y_sample = _outproj(x_sample, mod, lambda b: b // per_seq + 1, o_r, o_g, gz, gw, fw, wout)
    y_sample = y_sample.reshape(b_dec, t_dec, D_MODEL)

    return (y_prompt, y_sample, new_ret, new_gla)
```

```python
import functools

import jax
import jax.numpy as jnp
import numpy as np
from jax import lax
from jax.experimental import pallas as pl
from jax.experimental.pallas import tpu as pltpu

F32 = jnp.float32
BF16 = jnp.bfloat16

D_MODEL = 1024
N_HEADS = 4
DK = 64
DV = 128
QK = N_HEADS * DK
WV = N_HEADS * DV
GLA_RANK = 16
GLA_TAU = 16.0
EPS = 1e-6
GRID_W = 64
ROPE_BASE = 10000.0

LANES = 128
SCAN_CHUNK = 128
CTX_SEQ_BLOCK = 8
DEC_UNROLL = 8
RET_PIPE_GROUP = 2
GLA_PIPE_GROUP = 8
TOKEN_BLOCK = 1024
INPROJ_SUB = 256
OUTPROJ_SUB = 256
MOD_ROWS = 16
MOD_COL_BLOCK = 512
OUT_TOKEN_BLOCK = 1024
VMEM_LIMIT = 48 * 1024 * 1024
OUT_VMEM_LIMIT = VMEM_LIMIT

_NT = (((1,), (1,)), ((), ()))
_TN = (((0,), (0,)), ((), ()))


def _dot(a, b):
    return jnp.dot(a, b, preferred_element_type=F32)


def _dot_nt(a, b):
    return lax.dot_general(a, b, _NT, preferred_element_type=F32)


def _silu(x):
    return x * jax.nn.sigmoid(x)


def _mod_kernel(c_ref, w_ref, b_ref, o_ref):
    s = _silu(c_ref[...]).astype(BF16)
    o_ref[...] = _dot(s, w_ref[...].astype(BF16)) + b_ref[...]


def _modulation(cond, w_mod, b_mod):
    n_col = w_mod.shape[1]
    return pl.pallas_call(
        _mod_kernel,
        grid=(n_col // MOD_COL_BLOCK,),
        in_specs=[
            pl.BlockSpec((MOD_ROWS, D_MODEL), lambda j: (0, 0)),
            pl.BlockSpec((D_MODEL, MOD_COL_BLOCK), lambda j: (0, j)),
            pl.BlockSpec((1, MOD_COL_BLOCK), lambda j: (0, j)),
        ],
        out_specs=pl.BlockSpec((MOD_ROWS, MOD_COL_BLOCK), lambda j: (0, j)),
        out_shape=jax.ShapeDtypeStruct((MOD_ROWS, n_col), F32),
        compiler_params=pltpu.CompilerParams(vmem_limit_bytes=VMEM_LIMIT),
        name="modulation",
    )(cond, w_mod, b_mod.reshape(1, n_col))


def _rope_tile(t, cos, sin_signed):
    lane = lax.broadcasted_iota(jnp.int32, t.shape, 1)
    first_half = (lane & 32) == 0
    swapped = jnp.where(first_half, pltpu.roll(t, LANES - 32, 1), pltpu.roll(t, 32, 1))
    return t * cos + swapped * sin_signed


def _outproj_rows(rows, x_ref, mod_ref, or_ref, og_ref, gz_ref, gw_ref, fw_ref, wout_ref, y_ref):
    gate = mod_ref[0, :, 2 * D_MODEL:3 * D_MODEL]
    gw = gw_ref[...]
    parts = []
    for h in range(N_HEADS):
        cols = slice(h * DV, (h + 1) * DV)
        t = or_ref[0, rows, cols].astype(F32)
        mu = jnp.mean(t, axis=-1, keepdims=True)
        dlt = t - mu
        var = jnp.mean(dlt * dlt, axis=-1, keepdims=True)
        n = dlt * lax.rsqrt(var + EPS)
        parts.append((n * gz_ref[0, rows, cols].astype(F32)).astype(BF16))
    for h in range(N_HEADS):
        cols = slice(h * DV, (h + 1) * DV)
        t = og_ref[0, rows, cols].astype(F32)
        n = t * lax.rsqrt(jnp.mean(t * t, axis=-1, keepdims=True) + EPS) * gw
        gcols = slice(WV + h * DV, WV + (h + 1) * DV)
        parts.append((n * gz_ref[0, rows, gcols].astype(F32)).astype(BF16))
    mixed = jnp.concatenate(parts, axis=-1)
    out = _dot(mixed, wout_ref[...])
    y = x_ref[0, rows, :] + gate * out
    yn = y * lax.rsqrt(jnp.mean(y * y, axis=-1, keepdims=True) + EPS)
    y_ref[0, rows, :] = yn * fw_ref[...]


def _inproj_kernel(use_rope, x_ref, mod_ref, wqk_ref, wz_ref, wvt_ref, wa_ref, ba_ref, *rest):
    if use_rope:
        cos_ref, sin_ref, qk_ref, z_ref, vt_ref, lg_ref = rest
    else:
        qk_ref, z_ref, vt_ref, lg_ref = rest
    shift = mod_ref[0, :, 0:D_MODEL]
    scale1 = 1.0 + mod_ref[0, :, D_MODEL:2 * D_MODEL]

    def gate_logs(lr_t, rows):
        logit = lax.dot_general(lr_t, wa_ref[...], _TN, preferred_element_type=F32) + ba_ref[...]
        log_sig = jnp.minimum(logit, 0.0) - jnp.log(1.0 + jnp.exp(-jnp.abs(logit)))
        lg_ref[0, rows, :] = log_sig * (1.0 / GLA_TAU)

    for sub in range(TOKEN_BLOCK // INPROJ_SUB):
        rows = slice(sub * INPROJ_SUB, (sub + 1) * INPROJ_SUB)
        x = x_ref[0, rows, :]
        inv = lax.rsqrt(jnp.mean(x * x, axis=-1, keepdims=True) + EPS)
        hb = ((x * inv) * scale1 + shift).astype(BF16)

        lr_t = _dot_nt(wvt_ref[2 * WV:, :], hb).astype(BF16)

        for half in range(2):
            res = _dot(hb, wqk_ref[:, half * 2 * QK:(half + 1) * 2 * QK])
            if use_rope and half == 0:
                for t in range(4):
                    tile = res[:, t * LANES:(t + 1) * LANES]
                    if t >= 2:
                        tile = tile * (DK ** -0.5)
                    tile = _rope_tile(tile, cos_ref[rows, :], sin_ref[rows, :])
                    qk_ref[0, rows, t * LANES:(t + 1) * LANES] = tile.astype(BF16)
            else:
                lane = lax.broadcasted_iota(jnp.int32, (1, 2 * QK), 1)
                scaled = (lane >= QK) if half == 0 else (lane < QK)
                col_scale = jnp.where(scaled, DK ** -0.5, 1.0)
                qk_ref[0, rows, half * 2 * QK:(half + 1) * 2 * QK] = (res * col_scale).astype(BF16)

        gate_logs(lr_t, rows)

        z_ref[0, rows, :] = _silu(_dot(hb, wz_ref[...])).astype(BF16)

        vt = _dot_nt(wvt_ref[0:2 * WV, :], hb)
        for j in range(INPROJ_SUB // SCAN_CHUNK):
            jj = sub * (INPROJ_SUB // SCAN_CHUNK) + j
            vt_ref[0, jj] = vt[:, j * SCAN_CHUNK:(j + 1) * SCAN_CHUNK].astype(BF16)


def _inproj(x, mod, mod_row_fn, weights, rope):
    B, T, _ = x.shape
    wqk, wz, wvt, wa, ba = weights
    nt = T // TOKEN_BLOCK
    cpb = TOKEN_BLOCK // SCAN_CHUNK
    const = lambda b, t: (0, 0)
    in_specs = [
        pl.BlockSpec((1, TOKEN_BLOCK, D_MODEL), lambda b, t: (b, t, 0)),
        pl.BlockSpec((1, 1, 3 * D_MODEL), lambda b, t: (mod_row_fn(b), 0, 0)),
        pl.BlockSpec(wqk.shape, const),
        pl.BlockSpec(wz.shape, const),
        pl.BlockSpec(wvt.shape, const),
        pl.BlockSpec(wa.shape, const),
        pl.BlockSpec(ba.shape, const),
    ]
    args = [x, mod, wqk, wz, wvt, wa, ba]
    if rope is not None:
        in_specs += [pl.BlockSpec((TOKEN_BLOCK, LANES), lambda b, t: (t, 0))] * 2
        args += list(rope)
    out_shape = [
        jax.ShapeDtypeStruct((B, T, 4 * QK), BF16),
        jax.ShapeDtypeStruct((B, T, 2 * WV), BF16),
        jax.ShapeDtypeStruct((B, T // SCAN_CHUNK, 2 * WV, SCAN_CHUNK), BF16),
        jax.ShapeDtypeStruct((B, T, 2 * QK), F32),
    ]
    out_specs = [
        pl.BlockSpec((1, TOKEN_BLOCK, 4 * QK), lambda b, t: (b, t, 0)),
        pl.BlockSpec((1, TOKEN_BLOCK, 2 * WV), lambda b, t: (b, t, 0)),
        pl.BlockSpec((1, cpb, 2 * WV, SCAN_CHUNK), lambda b, t: (b, t, 0, 0)),
        pl.BlockSpec((1, TOKEN_BLOCK, 2 * QK), lambda b, t: (b, t, 0)),
    ]
    return pl.pallas_call(
        functools.partial(_inproj_kernel, rope is not None),
        grid=(B, nt),
        in_specs=in_specs,
        out_specs=out_specs,
        out_shape=out_shape,
        compiler_params=pltpu.CompilerParams(
            dimension_semantics=("parallel", "parallel"), vmem_limit_bytes=VMEM_LIMIT),
        name="inproj_rope" if rope is not None else "inproj",
    )(*args)


def _scan_kernel(is_gla, has_init, want_state, n_chunks, seq_blk, unroll, *refs):
    refs = list(refs)
    q_ref, k_ref, vt_ref = refs[:3]
    pos = 3
    if is_gla:
        lgf_ref, lgb_ref = refs[pos:pos + 2]
        pos += 2
    else:
        ld_ref = refs[pos]
        pos += 1
    if has_init:
        s0_ref = refs[pos]
        pos += 1
    o_ref = refs[pos]
    pos += 1
    if want_state:
        sn_ref = refs[pos]
        pos += 1
    st_ref, oacc_ref = refs[pos:pos + 2]

    C = SCAN_CHUNK
    mid = C // 2
    n_pairs = N_HEADS // 2
    lane = lax.broadcasted_iota(jnp.int32, (1, LANES), 1)
    head_mask = (lane < DK, lane >= DK)
    ri = lax.broadcasted_iota(jnp.int32, (C, C), 0)
    ci = lax.broadcasted_iota(jnp.int32, (C, C), 1)
    tpos = lax.broadcasted_iota(jnp.int32, (C, 1), 0).astype(F32)
    tri = ((ri >= ci), (ci >= ri))

    if has_init:
        for s in range(seq_blk):
            for p in range(n_pairs):
                for d in range(2):
                    s0 = s0_ref[s, 0, d]
                    both = jnp.concatenate([s0[2 * p], s0[2 * p + 1]], axis=0).T
                    st_ref[s, p, d, 0:DV] = both
                    st_ref[s, p, d, DV:2 * DV] = both
    else:
        st_ref[...] = jnp.zeros(st_ref.shape, F32)

    if is_gla:
        tri_ones = [t.astype(BF16) for t in tri]
    else:
        q_mul, k_mul, ret_decay, score_mul = {}, {}, {}, {}
        for p in range(n_pairs):
            for d in range(2):
                ld0 = ld_ref[d, 2 * p]
                ld1 = ld_ref[d, 2 * p + 1]
                ldr = jnp.where(lane < DK, ld0, ld1)
                if d == 0:
                    q_mul[p, d] = jnp.exp(ldr * (tpos + 1.0))
                    k_mul[p, d] = jnp.exp(ldr * (C - 1.0 - tpos))
                    dist = (ri - ci).astype(F32)
                else:
                    q_mul[p, d] = jnp.exp(ldr * (C - tpos))
                    k_mul[p, d] = jnp.exp(ldr * tpos)
                    dist = (ci - ri).astype(F32)
                ret_decay[p, d] = jnp.exp(ldr * float(C))
                score_mul[p, d] = [jnp.where(tri[d], jnp.exp(ldh * dist), 0.0) for ldh in (ld0, ld1)]
            score_mul[p] = [score_mul[p, 0][hh] + score_mul[p, 1][hh] for hh in range(2)]

    def has_intra(d):
        return is_gla or d == 0

    zero = jnp.zeros((), BF16)
    groups = [(s, p, d) for s in range(seq_blk) for p in range(n_pairs) for d in range(2)]

    def run_block(n0, accumulate):
        keys = [g + (u,) for u in range(unroll) for g in groups]
        chains = {}

        def stage_load(key):
            s, p, d, u = key
            ptile = slice(p * LANES, (p + 1) * LANES)
            n = n0 + u
            c = n if d == 0 else n_chunks - 1 - n
            t = dict(c=c, rows=pl.ds(pl.multiple_of(c * C, C), C))
            t["q"] = q_ref[s, t["rows"], ptile]
            t["k"] = k_ref[s, t["rows"], ptile]
            if is_gla:
                lg = (lgf_ref if d == 0 else lgb_ref)[s, t["rows"], ptile]
                hi = lg.astype(BF16)
                lo = (lg - hi.astype(F32)).astype(BF16)
                bb = _dot(tri_ones[d], jnp.concatenate([hi, lo], axis=1))
                t["b"] = bb[:, :LANES] + bb[:, LANES:]
            chains[key] = t

        def stage_scores(key):
            s, p, d, u = key
            t = chains[key]
            q, k = t.pop("q"), t.pop("k")
            if is_gla:
                b = t.pop("b")
                r = b[mid - 1:mid] if d == 0 else b[mid:mid + 1]
                bl = b[C - 1:C] if d == 0 else b[0:1]
                qs = q.astype(F32) * jnp.exp(b - r)
                ks = k.astype(F32) * jnp.exp(r - b)
                q_sc = qs.astype(BF16)
                k_sc = ks.astype(BF16)
                q_dec = (qs * jnp.exp(r)).astype(BF16)
                t["k_dec"] = (ks * jnp.exp(bl - r)).astype(BF16)
                t["decay"] = jnp.exp(bl)
            else:
                q_sc = q
                k_sc = k
                q_dec = (q.astype(F32) * q_mul[p, d]).astype(BF16)
                t["k_dec"] = (k.astype(F32) * k_mul[p, d]).astype(BF16)
                t["decay"] = ret_decay[p, d]
            t["q_dec"] = [jnp.where(head_mask[hh], q_dec, zero) for hh in range(2)]
            if has_intra(d):
                k_heads = jnp.concatenate([jnp.where(head_mask[hh], k_sc, zero) for hh in range(2)], axis=0)
                t["sc"] = _dot_nt(q_sc, k_heads)

        def stage_kv(key):
            s, p, d, u = key
            t = chains[key]
            t["vt"] = vt_ref[s, t["c"], p * 2 * DV:(p + 1) * 2 * DV, :]
            t["kv"] = _dot(t["vt"], t.pop("k_dec"))
            t["scb"] = []
            if has_intra(d):
                sc_all = t.pop("sc")
                for hh in range(2):
                    sc = sc_all[:, hh * C:(hh + 1) * C]
                    if is_gla:
                        sc = jnp.where(tri[d], sc, 0.0)
                    else:
                        sc = sc * score_mul[p][hh]
                    t["scb"].append(sc.astype(BF16))

        def stage_out(key):
            s, p, d, u = key
            t = chains[key]
            st = st_ref[s, p, d]
            stb = st.astype(BF16)
            t["o"] = []
            for hh in range(2):
                hrows = slice(hh * DV, (hh + 1) * DV)
                if has_intra(d):
                    lhs = jnp.concatenate([t["scb"][hh], t["q_dec"][hh]], axis=1)
                    rhs = jnp.concatenate([t["vt"][hrows], stb[hrows]], axis=1)
                else:
                    lhs, rhs = t["q_dec"][hh], stb[hrows]
                t["o"].append(_dot_nt(lhs, rhs))
            st_ref[s, p, d] = st * t.pop("decay") + t.pop("kv")

        def stage_store(key):
            s, p, d, u = key
            t = chains.pop(key)
            for hh in range(2):
                cols = slice((2 * p + hh) * DV, (2 * p + hh + 1) * DV)
                if accumulate:
                    o_ref[s, t["rows"], cols] = (oacc_ref[s, t["rows"], cols] + t["o"][hh]).astype(BF16)
                else:
                    oacc_ref[s, t["rows"], cols] = t["o"][hh]

        stages = (stage_load, stage_scores, stage_kv, stage_out, stage_store)
        grp = GLA_PIPE_GROUP if is_gla else RET_PIPE_GROUP
        key_groups = [keys[i:i + grp] for i in range(0, len(keys), grp)]
        for step in range(len(key_groups) + len(stages) - 1):
            for k in reversed(range(len(stages))):
                if 0 <= step - k < len(key_groups):
                    for key in key_groups[step - k]:
                        stages[k](key)

    def loop_body(accumulate, base):
        def body(m, carry):
            run_block(base + m * unroll, accumulate)
            return carry
        return body

    half = n_chunks // 2
    assert half % unroll == 0
    lax.fori_loop(0, half // unroll, loop_body(False, 0), 0)
    lax.fori_loop(0, half // unroll, loop_body(True, half), 0)

    if want_state:
        for s, p, d in groups:
            for hh in range(2):
                sn_ref[s, 0, d, 2 * p + hh] = st_ref[s, p, d, hh * DV:(hh + 1) * DV].T[hh * DK:(hh + 1) * DK, :]


def _scan(is_gla, qk, vt, lg, ld, s0, want_state, seq_blk, unroll):
    B, T, _ = qk.shape
    n_chunks = T // SCAN_CHUNK
    assert n_chunks % 2 == 0 and B % seq_blk == 0
    sb = seq_blk
    grp = 1 if is_gla else 0
    in_specs = [
        pl.BlockSpec((sb, T, QK), lambda b: (b, 0, 2 * grp)),
        pl.BlockSpec((sb, T, QK), lambda b: (b, 0, 2 * grp + 1)),
        pl.BlockSpec((sb, n_chunks, WV, SCAN_CHUNK), lambda b: (b, 0, grp, 0)),
    ]
    args = [qk, qk, vt]
    if is_gla:
        in_specs += [
            pl.BlockSpec((sb, T, QK), lambda b: (b, 0, 0)),
            pl.BlockSpec((sb, T, QK), lambda b: (b, 0, 1)),
        ]
        args += [lg, lg]
    else:
        in_specs += [pl.BlockSpec(memory_space=pltpu.SMEM)]
        args += [ld]
    state_spec = pl.BlockSpec((sb, 1, 2, N_HEADS, DK, DV), lambda b: (b, 0, 0, 0, 0, 0))
    if s0 is not None:
        in_specs += [state_spec]
        args += [s0]
    out_shape = [jax.ShapeDtypeStruct((B, T, WV), BF16)]
    out_specs = [pl.BlockSpec((sb, T, WV), lambda b: (b, 0, 0))]
    if want_state:
        out_shape += [jax.ShapeDtypeStruct((B, 1, 2, N_HEADS, DK, DV), F32)]
        out_specs += [state_spec]
    name = ("gla" if is_gla else "ret") + ("_init" if s0 is not None else "") + "_scan"
    return pl.pallas_call(
        functools.partial(_scan_kernel, is_gla, s0 is not None, want_state, n_chunks, sb, unroll),
        grid=(B // sb,),
        in_specs=in_specs,
        out_specs=out_specs,
        out_shape=out_shape,
        scratch_shapes=[pltpu.VMEM((sb, N_HEADS // 2, 2, 2 * DV, LANES), F32),
                        pltpu.VMEM((sb, T, WV), F32)],
        compiler_params=pltpu.CompilerParams(
            dimension_semantics=("arbitrary",), vmem_limit_bytes=VMEM_LIMIT),
        name=name,
    )(*args)


def _outproj_kernel(*refs):
    for sub in range(OUT_TOKEN_BLOCK // OUTPROJ_SUB):
        _outproj_rows(slice(sub * OUTPROJ_SUB, (sub + 1) * OUTPROJ_SUB), *refs)


def _outproj(x, mod, mod_row_fn, o_r, o_g, gz, gw, fw, wout):
    n_tok = x.shape[0] * x.shape[1]
    B = n_tok // OUT_TOKEN_BLOCK
    x, o_r, o_g, gz = (a.reshape(B, OUT_TOKEN_BLOCK, a.shape[-1]) for a in (x, o_r, o_g, gz))
    const = lambda b: (0, 0)
    blk = lambda width: pl.BlockSpec((1, OUT_TOKEN_BLOCK, width), lambda b: (b, 0, 0))
    return pl.pallas_call(
        _outproj_kernel,
        grid=(B,),
        in_specs=[
            blk(D_MODEL),
            pl.BlockSpec((1, 1, 3 * D_MODEL), lambda b: (mod_row_fn(b), 0, 0)),
            blk(WV),
            blk(WV),
            blk(2 * WV),
            pl.BlockSpec((1, DV), const),
            pl.BlockSpec((1, D_MODEL), const),
            pl.BlockSpec(wout.shape, const),
        ],
        out_specs=blk(D_MODEL),
        out_shape=jax.ShapeDtypeStruct((B, OUT_TOKEN_BLOCK, D_MODEL), F32),
        compiler_params=pltpu.CompilerParams(
            dimension_semantics=("parallel",), vmem_limit_bytes=OUT_VMEM_LIMIT),
        name="outproj",
    )(x, mod, o_r, o_g, gz, gw, fw, wout)


def _rope_tables(n_tokens):
    rows = n_tokens // GRID_W
    rr, cc = np.meshgrid(np.arange(rows), np.arange(GRID_W), indexing="ij")
    rr = rr.reshape(-1).astype(np.float32)
    cc = cc.reshape(-1).astype(np.float32)
    n_freq = DK // 4
    inv = np.float32(ROPE_BASE) ** (-np.arange(n_freq, dtype=np.float32) / np.float32(n_freq))
    ang = np.concatenate([rr[:, None] * inv, cc[:, None] * inv], axis=-1)
    cos = np.cos(ang.astype(np.float64)).astype(np.float32)
    sin = np.sin(ang.astype(np.float64)).astype(np.float32)
    cos_t = np.tile(np.concatenate([cos, cos], axis=-1), (1, LANES // DK))
    sin_t = np.tile(np.concatenate([-sin, sin], axis=-1), (1, LANES // DK))
    return jnp.asarray(cos_t), jnp.asarray(sin_t)


def _prep_weights(w_in, gla_w_alpha, gla_b_alpha):
    q_r, k_r = w_in[:, 0:QK], w_in[:, QK:2 * QK]
    o = 2 * QK
    v_r, z_r = w_in[:, o:o + WV], w_in[:, o + WV:o + 2 * WV]
    o += 2 * WV
    q_g, k_g = w_in[:, o:o + QK], w_in[:, o + QK:o + 2 * QK]
    o += 2 * QK
    v_g, z_g = w_in[:, o:o + WV], w_in[:, o + WV:o + 2 * WV]
    o += 2 * WV
    lr = w_in[:, o:o + 2 * GLA_RANK]
    wqk = jnp.concatenate([q_r, k_r, q_g, k_g], axis=1).astype(BF16)
    wz = jnp.concatenate([z_r, z_g], axis=1).astype(BF16)
    wvt = jnp.concatenate([v_r, v_g, lr], axis=1).T.astype(BF16)
    wa = jnp.zeros((2 * GLA_RANK, 2 * QK), F32)
    wa = wa.at[0:GLA_RANK, 0:QK].set(gla_w_alpha[0])
    wa = wa.at[GLA_RANK:2 * GLA_RANK, QK:2 * QK].set(gla_w_alpha[1])
    ba = jnp.concatenate([gla_b_alpha[0], gla_b_alpha[1]]).reshape(1, 2 * QK)
    return wqk, wz, wvt, wa.astype(BF16), ba


def kernel(x_prompt, x_sample, c, state_ret, state_gla, c_ctx, w_mod, b_mod, w_in, ret_log_decay,
           gla_w_alpha, gla_b_alpha, gla_norm_w, w_out, final_norm_w):
    assert w_mod.shape[0] == 1, "single-layer model"
    b_ctx, t_ctx, _ = x_prompt.shape
    b_dec, t_dec, _ = x_sample.shape
    assert t_ctx % SCAN_CHUNK == 0 and t_dec % TOKEN_BLOCK == 0 and TOKEN_BLOCK % t_ctx == 0
    assert 1 + b_dec <= MOD_ROWS

    cond = jnp.concatenate(
        [c_ctx[None, :], c, jnp.zeros((MOD_ROWS - 1 - b_dec, D_MODEL), F32)], axis=0)
    mod = _modulation(cond, w_mod[0], b_mod[0]).reshape(MOD_ROWS, 1, 3 * D_MODEL)

    weights = _prep_weights(w_in[0], gla_w_alpha[0], gla_b_alpha[0])
    wout = w_out[0].astype(BF16)
    gw = gla_norm_w[0].reshape(1, DV)
    fw = final_norm_w.reshape(1, D_MODEL)
    ld = ret_log_decay[0]

    per_blk = TOKEN_BLOCK // t_ctx
    xp = x_prompt.reshape(b_ctx // per_blk, TOKEN_BLOCK, D_MODEL)
    ctx_row = lambda b: 0
    qk, gz, vt, lg = _inproj(xp, mod, ctx_row, weights, None)
    qk = qk.reshape(b_ctx, t_ctx, 4 * QK)
    vt = vt.reshape(b_ctx, t_ctx // SCAN_CHUNK, 2 * WV, SCAN_CHUNK)
    lg = lg.reshape(b_ctx, t_ctx, 2 * QK)
    o_r, new_ret = _scan(False, qk, vt, None, ld, None, True, CTX_SEQ_BLOCK, 1)
    o_g, new_gla = _scan(True, qk, vt, lg, None, None, True, CTX_SEQ_BLOCK, 1)
    y_prompt = _outproj(xp, mod, ctx_row, o_r, o_g, gz, gw, fw, wout)
    y_prompt = y_prompt.reshape(b_ctx, t_ctx, D_MODEL)

    dec_row = lambda b: b + 1
    rope = _rope_tables(t_dec)
    qk, gz, vt, lg = _inproj(x_sample, mod, dec_row, weights, rope)
    (o_r,) = _scan(False, qk, vt, None, ld, state_ret, False, 1, DEC_UNROLL)
    (o_g,) = _scan(True, qk, vt, lg, None, state_gla, False, 1, DEC_UNROLL)
    per_seq = t_dec // OUT_TOKEN_BLOCK
    y_sample = _outproj(x_sample, mod, lambda b: b // per_seq + 1, o_r, o_g, gz, gw, fw, wout)
    y_sample = y_sample.reshape(b_dec, t_dec, D_MODEL)

    return (y_prompt, y_sample, new_ret, new_gla)
```

```python
import functools

import jax
import jax.numpy as jnp
import numpy as np
from jax import lax
from jax.experimental import pallas as pl
from jax.experimental.pallas import tpu as pltpu

F32 = jnp.float32
BF16 = jnp.bfloat16

D_MODEL = 1024
N_HEADS = 4
DK = 64
DV = 128
QK = N_HEADS * DK
WV = N_HEADS * DV
GLA_RANK = 16
GLA_TAU = 16.0
EPS = 1e-6
GRID_W = 64
ROPE_BASE = 10000.0

LANES = 128
SCAN_CHUNK = 128
CTX_SEQ_BLOCK = 8
DEC_UNROLL = 8
RET_PIPE_GROUP = 2
GLA_PIPE_GROUP = 8
TOKEN_BLOCK = 1024
INPROJ_SUB = 256
OUTPROJ_SUB = 128
MOD_ROWS = 16
MOD_COL_BLOCK = 512
OUT_TOKEN_BLOCK = 1024
VMEM_LIMIT = 48 * 1024 * 1024
OUT_VMEM_LIMIT = VMEM_LIMIT

_NT = (((1,), (1,)), ((), ()))
_TN = (((0,), (0,)), ((), ()))


def _dot(a, b):
    return jnp.dot(a, b, preferred_element_type=F32)


def _dot_nt(a, b):
    return lax.dot_general(a, b, _NT, preferred_element_type=F32)


def _silu(x):
    return x * jax.nn.sigmoid(x)


def _mod_kernel(c_ref, w_ref, b_ref, o_ref):
    s = _silu(c_ref[...]).astype(BF16)
    o_ref[...] = _dot(s, w_ref[...].astype(BF16)) + b_ref[...]


def _modulation(cond, w_mod, b_mod):
    n_col = w_mod.shape[1]
    return pl.pallas_call(
        _mod_kernel,
        grid=(n_col // MOD_COL_BLOCK,),
        in_specs=[
            pl.BlockSpec((MOD_ROWS, D_MODEL), lambda j: (0, 0)),
            pl.BlockSpec((D_MODEL, MOD_COL_BLOCK), lambda j: (0, j)),
            pl.BlockSpec((1, MOD_COL_BLOCK), lambda j: (0, j)),
        ],
        out_specs=pl.BlockSpec((MOD_ROWS, MOD_COL_BLOCK), lambda j: (0, j)),
        out_shape=jax.ShapeDtypeStruct((MOD_ROWS, n_col), F32),
        compiler_params=pltpu.CompilerParams(vmem_limit_bytes=VMEM_LIMIT),
        name="modulation",
    )(cond, w_mod, b_mod.reshape(1, n_col))


def _rope_tile(t, cos, sin_signed):
    lane = lax.broadcasted_iota(jnp.int32, t.shape, 1)
    first_half = (lane & 32) == 0
    swapped = jnp.where(first_half, pltpu.roll(t, LANES - 32, 1), pltpu.roll(t, 32, 1))
    return t * cos + swapped * sin_signed


def _outproj_rows(rows, x_ref, mod_ref, or_ref, og_ref, gz_ref, gw_ref, fw_ref, wout_ref, y_ref):
    gate = mod_ref[0, :, 2 * D_MODEL:3 * D_MODEL]
    gw = gw_ref[...]
    parts = []
    for h in range(N_HEADS):
        cols = slice(h * DV, (h + 1) * DV)
        t = or_ref[0, rows, cols].astype(F32)
        mu = jnp.mean(t, axis=-1, keepdims=True)
        dlt = t - mu
        var = jnp.mean(dlt * dlt, axis=-1, keepdims=True)
        n = dlt * lax.rsqrt(var + EPS)
        parts.append((n * gz_ref[0, rows, cols].astype(F32)).astype(BF16))
    for h in range(N_HEADS):
        cols = slice(h * DV, (h + 1) * DV)
        t = og_ref[0, rows, cols].astype(F32)
        n = t * lax.rsqrt(jnp.mean(t * t, axis=-1, keepdims=True) + EPS) * gw
        gcols = slice(WV + h * DV, WV + (h + 1) * DV)
        parts.append((n * gz_ref[0, rows, gcols].astype(F32)).astype(BF16))
    mixed = jnp.concatenate(parts, axis=-1)
    out = _dot(mixed, wout_ref[...])
    y = x_ref[0, rows, :] + gate * out
    yn = y * lax.rsqrt(jnp.mean(y * y, axis=-1, keepdims=True) + EPS)
    y_ref[0, rows, :] = yn * fw_ref[...]


def _inproj_kernel(use_rope, x_ref, mod_ref, wqk_ref, wz_ref, wvt_ref, wa_ref, ba_ref, *rest):
    if use_rope:
        cos_ref, sin_ref, qk_ref, z_ref, vt_ref, lg_ref = rest
    else:
        qk_ref, z_ref, vt_ref, lg_ref = rest
    shift = mod_ref[0, :, 0:D_MODEL]
    scale1 = 1.0 + mod_ref[0, :, D_MODEL:2 * D_MODEL]

    def gate_logs(lr_t, rows):
        logit = lax.dot_general(lr_t, wa_ref[...], _TN, preferred_element_type=F32) + ba_ref[...]
        log_sig = jnp.minimum(logit, 0.0) - jnp.log(1.0 + jnp.exp(-jnp.abs(logit)))
        lg_ref[0, rows, :] = log_sig * (1.0 / GLA_TAU)

    for sub in range(TOKEN_BLOCK // INPROJ_SUB):
        rows = slice(sub * INPROJ_SUB, (sub + 1) * INPROJ_SUB)
        x = x_ref[0, rows, :]
        inv = lax.rsqrt(jnp.mean(x * x, axis=-1, keepdims=True) + EPS)
        hb = ((x * inv) * scale1 + shift).astype(BF16)

        lr_t = _dot_nt(wvt_ref[2 * WV:, :], hb).astype(BF16)

        for half in range(2):
            res = _dot(hb, wqk_ref[:, half * 2 * QK:(half + 1) * 2 * QK])
            if use_rope and half == 0:
                for t in range(4):
                    tile = res[:, t * LANES:(t + 1) * LANES]
                    if t >= 2:
                        tile = tile * (DK ** -0.5)
                    tile = _rope_tile(tile, cos_ref[rows, :], sin_ref[rows, :])
                    qk_ref[0, rows, t * LANES:(t + 1) * LANES] = tile.astype(BF16)
            else:
                lane = lax.broadcasted_iota(jnp.int32, (1, 2 * QK), 1)
                scaled = (lane >= QK) if half == 0 else (lane < QK)
                col_scale = jnp.where(scaled, DK ** -0.5, 1.0)
                qk_ref[0, rows, half * 2 * QK:(half + 1) * 2 * QK] = (res * col_scale).astype(BF16)

        gate_logs(lr_t, rows)

        z_ref[0, rows, :] = _silu(_dot(hb, wz_ref[...])).astype(BF16)

        vt = _dot_nt(wvt_ref[0:2 * WV, :], hb)
        for j in range(INPROJ_SUB // SCAN_CHUNK):
            jj = sub * (INPROJ_SUB // SCAN_CHUNK) + j
            vt_ref[0, jj] = vt[:, j * SCAN_CHUNK:(j + 1) * SCAN_CHUNK].astype(BF16)


def _inproj(x, mod, mod_row_fn, weights, rope):
    B, T, _ = x.shape
    wqk, wz, wvt, wa, ba = weights
    nt = T // TOKEN_BLOCK
    cpb = TOKEN_BLOCK // SCAN_CHUNK
    const = lambda b, t: (0, 0)
    in_specs = [
        pl.BlockSpec((1, TOKEN_BLOCK, D_MODEL), lambda b, t: (b, t, 0)),
        pl.BlockSpec((1, 1, 3 * D_MODEL), lambda b, t: (mod_row_fn(b), 0, 0)),
        pl.BlockSpec(wqk.shape, const),
        pl.BlockSpec(wz.shape, const),
        pl.BlockSpec(wvt.shape, const),
        pl.BlockSpec(wa.shape, const),
        pl.BlockSpec(ba.shape, const),
    ]
    args = [x, mod, wqk, wz, wvt, wa, ba]
    if rope is not None:
        in_specs += [pl.BlockSpec((TOKEN_BLOCK, LANES), lambda b, t: (t, 0))] * 2
        args += list(rope)
    out_shape = [
        jax.ShapeDtypeStruct((B, T, 4 * QK), BF16),
        jax.ShapeDtypeStruct((B, T, 2 * WV), BF16),
        jax.ShapeDtypeStruct((B, T // SCAN_CHUNK, 2 * WV, SCAN_CHUNK), BF16),
        jax.ShapeDtypeStruct((B, T, 2 * QK), F32),
    ]
    out_specs = [
        pl.BlockSpec((1, TOKEN_BLOCK, 4 * QK), lambda b, t: (b, t, 0)),
        pl.BlockSpec((1, TOKEN_BLOCK, 2 * WV), lambda b, t: (b, t, 0)),
        pl.BlockSpec((1, cpb, 2 * WV, SCAN_CHUNK), lambda b, t: (b, t, 0, 0)),
        pl.BlockSpec((1, TOKEN_BLOCK, 2 * QK), lambda b, t: (b, t, 0)),
    ]
    return pl.pallas_call(
        functools.partial(_inproj_kernel, rope is not None),
        grid=(B, nt),
        in_specs=in_specs,
        out_specs=out_specs,
        out_shape=out_shape,
        compiler_params=pltpu.CompilerParams(
            dimension_semantics=("parallel", "parallel"), vmem_limit_bytes=VMEM_LIMIT),
        name="inproj_rope" if rope is not None else "inproj",
    )(*args)


def _scan_kernel(is_gla, has_init, want_state, n_chunks, seq_blk, unroll, *refs):
    refs = list(refs)
    q_ref, k_ref, vt_ref = refs[:3]
    pos = 3
    if is_gla:
        lgf_ref, lgb_ref = refs[pos:pos + 2]
        pos += 2
    else:
        ld_ref = refs[pos]
        pos += 1
    if has_init:
        s0_ref = refs[pos]
        pos += 1
    o_ref = refs[pos]
    pos += 1
    if want_state:
        sn_ref = refs[pos]
        pos += 1
    st_ref, oacc_ref = refs[pos:pos + 2]

    C = SCAN_CHUNK
    mid = C // 2
    n_pairs = N_HEADS // 2
    lane = lax.broadcasted_iota(jnp.int32, (1, LANES), 1)
    head_mask = (lane < DK, lane >= DK)
    ri = lax.broadcasted_iota(jnp.int32, (C, C), 0)
    ci = lax.broadcasted_iota(jnp.int32, (C, C), 1)
    tpos = lax.broadcasted_iota(jnp.int32, (C, 1), 0).astype(F32)
    tri = ((ri >= ci), (ci >= ri))

    if has_init:
        for s in range(seq_blk):
            for p in range(n_pairs):
                for d in range(2):
                    s0 = s0_ref[s, 0, d]
                    both = jnp.concatenate([s0[2 * p], s0[2 * p + 1]], axis=0).T
                    st_ref[s, p, d, 0:DV] = both
                    st_ref[s, p, d, DV:2 * DV] = both
    else:
        st_ref[...] = jnp.zeros(st_ref.shape, F32)

    if is_gla:
        tri_ones = [t.astype(BF16) for t in tri]
    else:
        q_mul, k_mul, ret_decay, score_mul = {}, {}, {}, {}
        for p in range(n_pairs):
            for d in range(2):
                ld0 = ld_ref[d, 2 * p]
                ld1 = ld_ref[d, 2 * p + 1]
                ldr = jnp.where(lane < DK, ld0, ld1)
                if d == 0:
                    q_mul[p, d] = jnp.exp(ldr * (tpos + 1.0))
                    k_mul[p, d] = jnp.exp(ldr * (C - 1.0 - tpos))
                    dist = (ri - ci).astype(F32)
                else:
                    q_mul[p, d] = jnp.exp(ldr * (C - tpos))
                    k_mul[p, d] = jnp.exp(ldr * tpos)
                    dist = (ci - ri).astype(F32)
                ret_decay[p, d] = jnp.exp(ldr * float(C))
                score_mul[p, d] = [jnp.where(tri[d], jnp.exp(ldh * dist), 0.0) for ldh in (ld0, ld1)]
            score_mul[p] = [score_mul[p, 0][hh] + score_mul[p, 1][hh] for hh in range(2)]

    def has_intra(d):
        return is_gla or d == 0

    zero = jnp.zeros((), BF16)
    groups = [(s, p, d) for s in range(seq_blk) for p in range(n_pairs) for d in range(2)]

    def run_block(n0, accumulate):
        keys = [g + (u,) for u in range(unroll) for g in groups]
        chains = {}

        def stage_load(key):
            s, p, d, u = key
            ptile = slice(p * LANES, (p + 1) * LANES)
            n = n0 + u
            c = n if d == 0 else n_chunks - 1 - n
            t = dict(c=c, rows=pl.ds(pl.multiple_of(c * C, C), C))
            t["q"] = q_ref[s, t["rows"], ptile]
            t["k"] = k_ref[s, t["rows"], ptile]
            if is_gla:
                lg = (lgf_ref if d == 0 else lgb_ref)[s, t["rows"], ptile]
                hi = lg.astype(BF16)
                lo = (lg - hi.astype(F32)).astype(BF16)
                bb = _dot(tri_ones[d], jnp.concatenate([hi, lo], axis=1))
                t["b"] = bb[:, :LANES] + bb[:, LANES:]
            chains[key] = t

        def stage_scores(key):
            s, p, d, u = key
            t = chains[key]
            q, k = t.pop("q"), t.pop("k")
            if is_gla:
                b = t.pop("b")
                r = b[mid - 1:mid] if d == 0 else b[mid:mid + 1]
                bl = b[C - 1:C] if d == 0 else b[0:1]
                qs = q.astype(F32) * jnp.exp(b - r)
                ks = k.astype(F32) * jnp.exp(r - b)
                q_sc = qs.astype(BF16)
                k_sc = ks.astype(BF16)
                q_dec = (qs * jnp.exp(r)).astype(BF16)
                t["k_dec"] = (ks * jnp.exp(bl - r)).astype(BF16)
                t["decay"] = jnp.exp(bl)
            else:
                q_sc = q
                k_sc = k
                q_dec = (q.astype(F32) * q_mul[p, d]).astype(BF16)
                t["k_dec"] = (k.astype(F32) * k_mul[p, d]).astype(BF16)
                t["decay"] = ret_decay[p, d]
            t["q_dec"] = [jnp.where(head_mask[hh], q_dec, zero) for hh in range(2)]
            if has_intra(d):
                k_heads = jnp.concatenate([jnp.where(head_mask[hh], k_sc, zero) for hh in range(2)], axis=0)
                t["sc"] = _dot_nt(q_sc, k_heads)

        def stage_kv(key):
            s, p, d, u = key
            t = chains[key]
            t["vt"] = vt_ref[s, t["c"], p * 2 * DV:(p + 1) * 2 * DV, :]
            t["kv"] = _dot(t["vt"], t.pop("k_dec"))
            t["scb"] = []
            if has_intra(d):
                sc_all = t.pop("sc")
                for hh in range(2):
                    sc = sc_all[:, hh * C:(hh + 1) * C]
                    if is_gla:
                        sc = jnp.where(tri[d], sc, 0.0)
                    else:
                        sc = sc * score_mul[p][hh]
                    t["scb"].append(sc.astype(BF16))

        def stage_out(key):
            s, p, d, u = key
            t = chains[key]
            st = st_ref[s, p, d]
            stb = st.astype(BF16)
            t["o"] = []
            for hh in range(2):
                hrows = slice(hh * DV, (hh + 1) * DV)
                if has_intra(d):
                    lhs = jnp.concatenate([t["scb"][hh], t["q_dec"][hh]], axis=1)
                    rhs = jnp.concatenate([t["vt"][hrows], stb[hrows]], axis=1)
                else:
                    lhs, rhs = t["q_dec"][hh], stb[hrows]
                t["o"].append(_dot_nt(lhs, rhs))
            st_ref[s, p, d] = st * t.pop("decay") + t.pop("kv")

        def stage_store(key):
            s, p, d, u = key
            t = chains.pop(key)
            for hh in range(2):
                cols = slice((2 * p + hh) * DV, (2 * p + hh + 1) * DV)
                if accumulate:
                    o_ref[s, t["rows"], cols] = (oacc_ref[s, t["rows"], cols] + t["o"][hh]).astype(BF16)
                else:
                    oacc_ref[s, t["rows"], cols] = t["o"][hh]

        stages = (stage_load, stage_scores, stage_kv, stage_out, stage_store)
        grp = GLA_PIPE_GROUP if is_gla else RET_PIPE_GROUP
        key_groups = [keys[i:i + grp] for i in range(0, len(keys), grp)]
        for step in range(len(key_groups) + len(stages) - 1):
            for k in reversed(range(len(stages))):
                if 0 <= step - k < len(key_groups):
                    for key in key_groups[step - k]:
                        stages[k](key)

    def loop_body(accumulate, base):
        def body(m, carry):
            run_block(base + m * unroll, accumulate)
            return carry
        return body

    half = n_chunks // 2
    assert half % unroll == 0
    lax.fori_loop(0, half // unroll, loop_body(False, 0), 0)
    lax.fori_loop(0, half // unroll, loop_body(True, half), 0)

    if want_state:
        for s, p, d in groups:
            for hh in range(2):
                sn_ref[s, 0, d, 2 * p + hh] = st_ref[s, p, d, hh * DV:(hh + 1) * DV].T[hh * DK:(hh + 1) * DK, :]


def _scan(is_gla, qk, vt, lg, ld, s0, want_state, seq_blk, unroll):
    B, T, _ = qk.shape
    n_chunks = T // SCAN_CHUNK
    assert n_chunks % 2 == 0 and B % seq_blk == 0
    sb = seq_blk
    grp = 1 if is_gla else 0
    in_specs = [
        pl.BlockSpec((sb, T, QK), lambda b: (b, 0, 2 * grp)),
        pl.BlockSpec((sb, T, QK), lambda b: (b, 0, 2 * grp + 1)),
        pl.BlockSpec((sb, n_chunks, WV, SCAN_CHUNK), lambda b: (b, 0, grp, 0)),
    ]
    args = [qk, qk, vt]
    if is_gla:
        in_specs += [
            pl.BlockSpec((sb, T, QK), lambda b: (b, 0, 0)),
            pl.BlockSpec((sb, T, QK), lambda b: (b, 0, 1)),
        ]
        args += [lg, lg]
    else:
        in_specs += [pl.BlockSpec(memory_space=pltpu.SMEM)]
        args += [ld]
    state_spec = pl.BlockSpec((sb, 1, 2, N_HEADS, DK, DV), lambda b: (b, 0, 0, 0, 0, 0))
    if s0 is not None:
        in_specs += [state_spec]
        args += [s0]
    out_shape = [jax.ShapeDtypeStruct((B, T, WV), BF16)]
    out_specs = [pl.BlockSpec((sb, T, WV), lambda b: (b, 0, 0))]
    if want_state:
        out_shape += [jax.ShapeDtypeStruct((B, 1, 2, N_HEADS, DK, DV), F32)]
        out_specs += [state_spec]
    name = ("gla" if is_gla else "ret") + ("_init" if s0 is not None else "") + "_scan"
    return pl.pallas_call(
        functools.partial(_scan_kernel, is_gla, s0 is not None, want_state, n_chunks, sb, unroll),
        grid=(B // sb,),
        in_specs=in_specs,
        out_specs=out_specs,
        out_shape=out_shape,
        scratch_shapes=[pltpu.VMEM((sb, N_HEADS // 2, 2, 2 * DV, LANES), F32),
                        pltpu.VMEM((sb, T, WV), F32)],
        compiler_params=pltpu.CompilerParams(
            dimension_semantics=("arbitrary",), vmem_limit_bytes=VMEM_LIMIT),
        name=name,
    )(*args)


def _outproj_kernel(*refs):
    for sub in range(OUT_TOKEN_BLOCK // OUTPROJ_SUB):
        _outproj_rows(slice(sub * OUTPROJ_SUB, (sub + 1) * OUTPROJ_SUB), *refs)


def _outproj(x, mod, mod_row_fn, o_r, o_g, gz, gw, fw, wout):
    n_tok = x.shape[0] * x.shape[1]
    B = n_tok // OUT_TOKEN_BLOCK
    x, o_r, o_g, gz = (a.reshape(B, OUT_TOKEN_BLOCK, a.shape[-1]) for a in (x, o_r, o_g, gz))
    const = lambda b: (0, 0)
    blk = lambda width: pl.BlockSpec((1, OUT_TOKEN_BLOCK, width), lambda b: (b, 0, 0))
    return pl.pallas_call(
        _outproj_kernel,
        grid=(B,),
        in_specs=[
            blk(D_MODEL),
            pl.BlockSpec((1, 1, 3 * D_MODEL), lambda b: (mod_row_fn(b), 0, 0)),
            blk(WV),
            blk(WV),
            blk(2 * WV),
            pl.BlockSpec((1, DV), const),
            pl.BlockSpec((1, D_MODEL), const),
            pl.BlockSpec(wout.shape, const),
        ],
        out_specs=blk(D_MODEL),
        out_shape=jax.ShapeDtypeStruct((B, OUT_TOKEN_BLOCK, D_MODEL), F32),
        compiler_params=pltpu.CompilerParams(
            dimension_semantics=("parallel",), vmem_limit_bytes=OUT_VMEM_LIMIT),
        name="outproj",
    )(x, mod, o_r, o_g, gz, gw, fw, wout)


def _rope_tables(n_tokens):
    rows = n_tokens // GRID_W
    rr, cc = np.meshgrid(np.arange(rows), np.arange(GRID_W), indexing="ij")
    rr = rr.reshape(-1).astype(np.float32)
    cc = cc.reshape(-1).astype(np.float32)
    n_freq = DK // 4
    inv = np.float32(ROPE_BASE) ** (-np.arange(n_freq, dtype=np.float32) / np.float32(n_freq))
    ang = np.concatenate([rr[:, None] * inv, cc[:, None] * inv], axis=-1)
    cos = np.cos(ang.astype(np.float64)).astype(np.float32)
    sin = np.sin(ang.astype(np.float64)).astype(np.float32)
    cos_t = np.tile(np.concatenate([cos, cos], axis=-1), (1, LANES // DK))
    sin_t = np.tile(np.concatenate([-sin, sin], axis=-1), (1, LANES // DK))
    return jnp.asarray(cos_t), jnp.asarray(sin_t)


def _prep_weights(w_in, gla_w_alpha, gla_b_alpha):
    q_r, k_r = w_in[:, 0:QK], w_in[:, QK:2 * QK]
    o = 2 * QK
    v_r, z_r = w_in[:, o:o + WV], w_in[:, o + WV:o + 2 * WV]
    o += 2 * WV
    q_g, k_g = w_in[:, o:o + QK], w_in[:, o + QK:o + 2 * QK]
    o += 2 * QK
    v_g, z_g = w_in[:, o:o + WV], w_in[:, o + WV:o + 2 * WV]
    o += 2 * WV
    lr = w_in[:, o:o + 2 * GLA_RANK]
    wqk = jnp.concatenate([q_r, k_r, q_g, k_g], axis=1).astype(BF16)
    wz = jnp.concatenate([z_r, z_g], axis=1).astype(BF16)
    wvt = jnp.concatenate([v_r, v_g, lr], axis=1).T.astype(BF16)
    wa = jnp.zeros((2 * GLA_RANK, 2 * QK), F32)
    wa = wa.at[0:GLA_RANK, 0:QK].set(gla_w_alpha[0])
    wa = wa.at[GLA_RANK:2 * GLA_RANK, QK:2 * QK].set(gla_w_alpha[1])
    ba = jnp.concatenate([gla_b_alpha[0], gla_b_alpha[1]]).reshape(1, 2 * QK)
    return wqk, wz, wvt, wa.astype(BF16), ba


def kernel(x_prompt, x_sample, c, state_ret, state_gla, c_ctx, w_mod, b_mod, w_in, ret_log_decay,
           gla_w_alpha, gla_b_alpha, gla_norm_w, w_out, final_norm_w):
    assert w_mod.shape[0] == 1, "single-layer model"
    b_ctx, t_ctx, _ = x_prompt.shape
    b_dec, t_dec, _ = x_sample.shape
    assert t_ctx % SCAN_CHUNK == 0 and t_dec % TOKEN_BLOCK == 0 and TOKEN_BLOCK % t_ctx == 0
    assert 1 + b_dec <= MOD_ROWS

    cond = jnp.concatenate(
        [c_ctx[None, :], c, jnp.zeros((MOD_ROWS - 1 - b_dec, D_MODEL), F32)], axis=0)
    mod = _modulation(cond, w_mod[0], b_mod[0]).reshape(MOD_ROWS, 1, 3 * D_MODEL)

    weights = _prep_weights(w_in[0], gla_w_alpha[0], gla_b_alpha[0])
    wout = w_out[0].astype(BF16)
    gw = gla_norm_w[0].reshape(1, DV)
    fw = final_norm_w.reshape(1, D_MODEL)
    ld = ret_log_decay[0]

    per_blk = TOKEN_BLOCK // t_ctx
    xp = x_prompt.reshape(b_ctx // per_blk, TOKEN_BLOCK, D_MODEL)
    ctx_row = lambda b: 0
    qk, gz, vt, lg = _inproj(xp, mod, ctx_row, weights, None)
    qk = qk.reshape(b_ctx, t_ctx, 4 * QK)
    vt = vt.reshape(b_ctx, t_ctx // SCAN_CHUNK, 2 * WV, SCAN_CHUNK)
    lg = lg.reshape(b_ctx, t_ctx, 2 * QK)
    o_r, new_ret = _scan(False, qk, vt, None, ld, None, True, CTX_SEQ_BLOCK, 1)
    o_g, new_gla = _scan(True, qk, vt, lg, None, None, True, CTX_SEQ_BLOCK, 1)
    y_prompt = _outproj(xp, mod, ctx_row, o_r, o_g, gz, gw, fw, wout)
    y_prompt = y_prompt.reshape(b_ctx, t_ctx, D_MODEL)

    dec_row = lambda b: b + 1
    rope = _rope_tables(t_dec)
    qk, gz, vt, lg = _inproj(x_sample, mod, dec_row, weights, rope)
    (o_r,) = _scan(False, qk, vt, None, ld, state_ret, False, 1, DEC_UNROLL)
    (o_g,) = _scan(True, qk, vt, lg, None, state_gla, False, 1, DEC_UNROLL)
    per_seq = t_dec // OUT_TOKEN_BLOCK
    y_sample = _outproj(x_sample, mod, lambda b: b // per_seq + 1, o_r, o_g, gz, gw, fw, wout)
    y_sample = y_sample.reshape(b_dec, t_dec, D_MODEL)

    return (y_prompt, y_sample, new_ret, new_gla)
```

```python
import functools

import jax
import jax.numpy as jnp
import numpy as np
from jax import lax
from jax.experimental import pallas as pl
from jax.experimental.pallas import tpu as pltpu

F32 = jnp.float32
BF16 = jnp.bfloat16

D_MODEL = 1024
N_HEADS = 4
DK = 64
DV = 128
QK = N_HEADS * DK
WV = N_HEADS * DV
GLA_RANK = 16
GLA_TAU = 16.0
EPS = 1e-6
GRID_W = 64
ROPE_BASE = 10000.0

LANES = 128
SCAN_CHUNK = 128
CTX_SEQ_BLOCK = 8
DEC_UNROLL = 8
RET_PIPE_GROUP = 2
GLA_PIPE_GROUP = 4
TOKEN_BLOCK = 1024
INPROJ_SUB = 256
OUTPROJ_SUB = 256
MOD_ROWS = 16
MOD_COL_BLOCK = 512
OUT_TOKEN_BLOCK = 1024
VMEM_LIMIT = 48 * 1024 * 1024
OUT_VMEM_LIMIT = VMEM_LIMIT

_NT = (((1,), (1,)), ((), ()))
_TN = (((0,), (0,)), ((), ()))


def _dot(a, b):
    return jnp.dot(a, b, preferred_element_type=F32)


def _dot_nt(a, b):
    return lax.dot_general(a, b, _NT, preferred_element_type=F32)


def _silu(x):
    return x * jax.nn.sigmoid(x)


def _mod_kernel(c_ref, w_ref, b_ref, o_ref):
    s = _silu(c_ref[...]).astype(BF16)
    o_ref[...] = _dot(s, w_ref[...].astype(BF16)) + b_ref[...]


def _modulation(cond, w_mod, b_mod):
    n_col = w_mod.shape[1]
    return pl.pallas_call(
        _mod_kernel,
        grid=(n_col // MOD_COL_BLOCK,),
        in_specs=[
            pl.BlockSpec((MOD_ROWS, D_MODEL), lambda j: (0, 0)),
            pl.BlockSpec((D_MODEL, MOD_COL_BLOCK), lambda j: (0, j)),
            pl.BlockSpec((1, MOD_COL_BLOCK), lambda j: (0, j)),
        ],
        out_specs=pl.BlockSpec((MOD_ROWS, MOD_COL_BLOCK), lambda j: (0, j)),
        out_shape=jax.ShapeDtypeStruct((MOD_ROWS, n_col), F32),
        compiler_params=pltpu.CompilerParams(vmem_limit_bytes=VMEM_LIMIT),
        name="modulation",
    )(cond, w_mod, b_mod.reshape(1, n_col))


def _rope_tile(t, cos, sin_signed):
    lane = lax.broadcasted_iota(jnp.int32, t.shape, 1)
    first_half = (lane & 32) == 0
    swapped = jnp.where(first_half, pltpu.roll(t, LANES - 32, 1), pltpu.roll(t, 32, 1))
    return t * cos + swapped * sin_signed


def _outproj_rows(rows, x_ref, mod_ref, or_ref, og_ref, gz_ref, gw_ref, fw_ref, wout_ref, y_ref):
    gate = mod_ref[0, :, 2 * D_MODEL:3 * D_MODEL]
    gw = gw_ref[...]
    parts = []
    for h in range(N_HEADS):
        cols = slice(h * DV, (h + 1) * DV)
        t = or_ref[0, rows, cols].astype(F32)
        mu = jnp.mean(t, axis=-1, keepdims=True)
        dlt = t - mu
        var = jnp.mean(dlt * dlt, axis=-1, keepdims=True)
        n = dlt * lax.rsqrt(var + EPS)
        parts.append((n * gz_ref[0, rows, cols].astype(F32)).astype(BF16))
    for h in range(N_HEADS):
        cols = slice(h * DV, (h + 1) * DV)
        t = og_ref[0, rows, cols].astype(F32)
        n = t * lax.rsqrt(jnp.mean(t * t, axis=-1, keepdims=True) + EPS) * gw
        gcols = slice(WV + h * DV, WV + (h + 1) * DV)
        parts.append((n * gz_ref[0, rows, gcols].astype(F32)).astype(BF16))
    mixed = jnp.concatenate(parts, axis=-1)
    out = _dot(mixed, wout_ref[...])
    y = x_ref[0, rows, :] + gate * out
    yn = y * lax.rsqrt(jnp.mean(y * y, axis=-1, keepdims=True) + EPS)
    y_ref[0, rows, :] = yn * fw_ref[...]


def _inproj_kernel(use_rope, x_ref, mod_ref, wqk_ref, wz_ref, wvt_ref, wa_ref, ba_ref, *rest):
    if use_rope:
        cos_ref, sin_ref, qk_ref, z_ref, vt_ref, lg_ref = rest
    else:
        qk_ref, z_ref, vt_ref, lg_ref = rest
    shift = mod_ref[0, :, 0:D_MODEL]
    scale1 = 1.0 + mod_ref[0, :, D_MODEL:2 * D_MODEL]

    def gate_logs(lr_t, rows):
        logit = lax.dot_general(lr_t, wa_ref[...], _TN, preferred_element_type=F32) + ba_ref[...]
        log_sig = jnp.minimum(logit, 0.0) - jnp.log(1.0 + jnp.exp(-jnp.abs(logit)))
        lg_ref[0, rows, :] = log_sig * (1.0 / GLA_TAU)

    for sub in range(TOKEN_BLOCK // INPROJ_SUB):
        rows = slice(sub * INPROJ_SUB, (sub + 1) * INPROJ_SUB)
        x = x_ref[0, rows, :]
        inv = lax.rsqrt(jnp.mean(x * x, axis=-1, keepdims=True) + EPS)
        hb = ((x * inv) * scale1 + shift).astype(BF16)

        lr_t = _dot_nt(wvt_ref[2 * WV:, :], hb).astype(BF16)

        for half in range(2):
            res = _dot(hb, wqk_ref[:, half * 2 * QK:(half + 1) * 2 * QK])
            if use_rope and half == 0:
                for t in range(4):
                    tile = res[:, t * LANES:(t + 1) * LANES]
                    if t >= 2:
                        tile = tile * (DK ** -0.5)
                    tile = _rope_tile(tile, cos_ref[rows, :], sin_ref[rows, :])
                    qk_ref[0, rows, t * LANES:(t + 1) * LANES] = tile.astype(BF16)
            else:
                lane = lax.broadcasted_iota(jnp.int32, (1, 2 * QK), 1)
                scaled = (lane >= QK) if half == 0 else (lane < QK)
                col_scale = jnp.where(scaled, DK ** -0.5, 1.0)
                qk_ref[0, rows, half * 2 * QK:(half + 1) * 2 * QK] = (res * col_scale).astype(BF16)

        gate_logs(lr_t, rows)

        z_ref[0, rows, :] = _silu(_dot(hb, wz_ref[...])).astype(BF16)

        vt = _dot_nt(wvt_ref[0:2 * WV, :], hb)
        for j in range(INPROJ_SUB // SCAN_CHUNK):
            jj = sub * (INPROJ_SUB // SCAN_CHUNK) + j
            vt_ref[0, jj] = vt[:, j * SCAN_CHUNK:(j + 1) * SCAN_CHUNK].astype(BF16)


def _inproj(x, mod, mod_row_fn, weights, rope):
    B, T, _ = x.shape
    wqk, wz, wvt, wa, ba = weights
    nt = T // TOKEN_BLOCK
    cpb = TOKEN_BLOCK // SCAN_CHUNK
    const = lambda b, t: (0, 0)
    in_specs = [
        pl.BlockSpec((1, TOKEN_BLOCK, D_MODEL), lambda b, t: (b, t, 0)),
        pl.BlockSpec((1, 1, 3 * D_MODEL), lambda b, t: (mod_row_fn(b), 0, 0)),
        pl.BlockSpec(wqk.shape, const),
        pl.BlockSpec(wz.shape, const),
        pl.BlockSpec(wvt.shape, const),
        pl.BlockSpec(wa.shape, const),
        pl.BlockSpec(ba.shape, const),
    ]
    args = [x, mod, wqk, wz, wvt, wa, ba]
    if rope is not None:
        in_specs += [pl.BlockSpec((TOKEN_BLOCK, LANES), lambda b, t: (t, 0))] * 2
        args += list(rope)
    out_shape = [
        jax.ShapeDtypeStruct((B, T, 4 * QK), BF16),
        jax.ShapeDtypeStruct((B, T, 2 * WV), BF16),
        jax.ShapeDtypeStruct((B, T // SCAN_CHUNK, 2 * WV, SCAN_CHUNK), BF16),
        jax.ShapeDtypeStruct((B, T, 2 * QK), F32),
    ]
    out_specs = [
        pl.BlockSpec((1, TOKEN_BLOCK, 4 * QK), lambda b, t: (b, t, 0)),
        pl.BlockSpec((1, TOKEN_BLOCK, 2 * WV), lambda b, t: (b, t, 0)),
        pl.BlockSpec((1, cpb, 2 * WV, SCAN_CHUNK), lambda b, t: (b, t, 0, 0)),
        pl.BlockSpec((1, TOKEN_BLOCK, 2 * QK), lambda b, t: (b, t, 0)),
    ]
    return pl.pallas_call(
        functools.partial(_inproj_kernel, rope is not None),
        grid=(B, nt),
        in_specs=in_specs,
        out_specs=out_specs,
        out_shape=out_shape,
        compiler_params=pltpu.CompilerParams(
            dimension_semantics=("parallel", "parallel"), vmem_limit_bytes=VMEM_LIMIT),
        name="inproj_rope" if rope is not None else "inproj",
    )(*args)


def _scan_kernel(is_gla, has_init, want_state, n_chunks, seq_blk, unroll, *refs):
    refs = list(refs)
    q_ref, k_ref, vt_ref = refs[:3]
    pos = 3
    if is_gla:
        lgf_ref, lgb_ref = refs[pos:pos + 2]
        pos += 2
    else:
        ld_ref = refs[pos]
        pos += 1
    if has_init:
        s0_ref = refs[pos]
        pos += 1
    o_ref = refs[pos]
    pos += 1
    if want_state:
        sn_ref = refs[pos]
        pos += 1
    st_ref, oacc_ref = refs[pos:pos + 2]

    C = SCAN_CHUNK
    mid = C // 2
    n_pairs = N_HEADS // 2
    lane = lax.broadcasted_iota(jnp.int32, (1, LANES), 1)
    head_mask = (lane < DK, lane >= DK)
    ri = lax.broadcasted_iota(jnp.int32, (C, C), 0)
    ci = lax.broadcasted_iota(jnp.int32, (C, C), 1)
    tpos = lax.broadcasted_iota(jnp.int32, (C, 1), 0).astype(F32)
    tri = ((ri >= ci), (ci >= ri))

    if has_init:
        for s in range(seq_blk):
            for p in range(n_pairs):
                for d in range(2):
                    s0 = s0_ref[s, 0, d]
                    both = jnp.concatenate([s0[2 * p], s0[2 * p + 1]], axis=0).T
                    st_ref[s, p, d, 0:DV] = both
                    st_ref[s, p, d, DV:2 * DV] = both
    else:
        st_ref[...] = jnp.zeros(st_ref.shape, F32)

    if is_gla:
        tri_ones = [t.astype(BF16) for t in tri]
    else:
        q_mul, k_mul, ret_decay, score_mul = {}, {}, {}, {}
        for p in range(n_pairs):
            for d in range(2):
                ld0 = ld_ref[d, 2 * p]
                ld1 = ld_ref[d, 2 * p + 1]
                ldr = jnp.where(lane < DK, ld0, ld1)
                if d == 0:
                    q_mul[p, d] = jnp.exp(ldr * (tpos + 1.0))
                    k_mul[p, d] = jnp.exp(ldr * (C - 1.0 - tpos))
                    dist = (ri - ci).astype(F32)
                else:
                    q_mul[p, d] = jnp.exp(ldr * (C - tpos))
                    k_mul[p, d] = jnp.exp(ldr * tpos)
                    dist = (ci - ri).astype(F32)
                ret_decay[p, d] = jnp.exp(ldr * float(C))
                score_mul[p, d] = [jnp.where(tri[d], jnp.exp(ldh * dist), 0.0) for ldh in (ld0, ld1)]
            score_mul[p] = [score_mul[p, 0][hh] + score_mul[p, 1][hh] for hh in range(2)]

    def has_intra(d):
        return is_gla or d == 0

    zero = jnp.zeros((), BF16)
    groups = [(s, p, d) for s in range(seq_blk) for p in range(n_pairs) for d in range(2)]

    def run_block(n0, accumulate):
        keys = [g + (u,) for u in range(unroll) for g in groups]
        chains = {}

        def stage_load(key):
            s, p, d, u = key
            ptile = slice(p * LANES, (p + 1) * LANES)
            n = n0 + u
            c = n if d == 0 else n_chunks - 1 - n
            t = dict(c=c, rows=pl.ds(pl.multiple_of(c * C, C), C))
            t["q"] = q_ref[s, t["rows"], ptile]
            t["k"] = k_ref[s, t["rows"], ptile]
            if is_gla:
                lg = (lgf_ref if d == 0 else lgb_ref)[s, t["rows"], ptile]
                hi = lg.astype(BF16)
                lo = (lg - hi.astype(F32)).astype(BF16)
                bb = _dot(tri_ones[d], jnp.concatenate([hi, lo], axis=1))
                t["b"] = bb[:, :LANES] + bb[:, LANES:]
            chains[key] = t

        def stage_scores(key):
            s, p, d, u = key
            t = chains[key]
            q, k = t.pop("q"), t.pop("k")
            if is_gla:
                b = t.pop("b")
                r = b[mid - 1:mid] if d == 0 else b[mid:mid + 1]
                bl = b[C - 1:C] if d == 0 else b[0:1]
                qs = q.astype(F32) * jnp.exp(b - r)
                ks = k.astype(F32) * jnp.exp(r - b)
                q_sc = qs.astype(BF16)
                k_sc = ks.astype(BF16)
                q_dec = (qs * jnp.exp(r)).astype(BF16)
                t["k_dec"] = (ks * jnp.exp(bl - r)).astype(BF16)
                t["decay"] = jnp.exp(bl)
            else:
                q_sc = q
                k_sc = k
                q_dec = (q.astype(F32) * q_mul[p, d]).astype(BF16)
                t["k_dec"] = (k.astype(F32) * k_mul[p, d]).astype(BF16)
                t["decay"] = ret_decay[p, d]
            t["q_dec"] = [jnp.where(head_mask[hh], q_dec, zero) for hh in range(2)]
            if has_intra(d):
                k_heads = jnp.concatenate([jnp.where(head_mask[hh], k_sc, zero) for hh in range(2)], axis=0)
                t["sc"] = _dot_nt(q_sc, k_heads)

        def stage_kv(key):
            s, p, d, u = key
            t = chains[key]
            t["vt"] = vt_ref[s, t["c"], p * 2 * DV:(p + 1) * 2 * DV, :]
            t["kv"] = _dot(t["vt"], t.pop("k_dec"))
            t["scb"] = []
            if has_intra(d):
                sc_all = t.pop("sc")
                for hh in range(2):
                    sc = sc_all[:, hh * C:(hh + 1) * C]
                    if is_gla:
                        sc = jnp.where(tri[d], sc, 0.0)
                    else:
                        sc = sc * score_mul[p][hh]
                    t["scb"].append(sc.astype(BF16))

        def stage_out(key):
            s, p, d, u = key
            t = chains[key]
            st = st_ref[s, p, d]
            stb = st.astype(BF16)
            t["o"] = []
            for hh in range(2):
                hrows = slice(hh * DV, (hh + 1) * DV)
                if has_intra(d):
                    lhs = jnp.concatenate([t["scb"][hh], t["q_dec"][hh]], axis=1)
                    rhs = jnp.concatenate([t["vt"][hrows], stb[hrows]], axis=1)
                else:
                    lhs, rhs = t["q_dec"][hh], stb[hrows]
                t["o"].append(_dot_nt(lhs, rhs))
            st_ref[s, p, d] = st * t.pop("decay") + t.pop("kv")

        def stage_store(key):
            s, p, d, u = key
            t = chains.pop(key)
            for hh in range(2):
                cols = slice((2 * p + hh) * DV, (2 * p + hh + 1) * DV)
                if accumulate:
                    o_ref[s, t["rows"], cols] = (oacc_ref[s, t["rows"], cols] + t["o"][hh]).astype(BF16)
                else:
                    oacc_ref[s, t["rows"], cols] = t["o"][hh]

        stages = (stage_load, stage_scores, stage_kv, stage_out, stage_store)
        grp = GLA_PIPE_GROUP if is_gla else RET_PIPE_GROUP
        key_groups = [keys[i:i + grp] for i in range(0, len(keys), grp)]
        for step in range(len(key_groups) + len(stages) - 1):
            for k in reversed(range(len(stages))):
                if 0 <= step - k < len(key_groups):
                    for key in key_groups[step - k]:
                        stages[k](key)

    def loop_body(accumulate, base):
        def body(m, carry):
            run_block(base + m * unroll, accumulate)
            return carry
        return body

    half = n_chunks // 2
    assert half % unroll == 0
    lax.fori_loop(0, half // unroll, loop_body(False, 0), 0)
    lax.fori_loop(0, half // unroll, loop_body(True, half), 0)

    if want_state:
        for s, p, d in groups:
            for hh in range(2):
                sn_ref[s, 0, d, 2 * p + hh] = st_ref[s, p, d, hh * DV:(hh + 1) * DV].T[hh * DK:(hh + 1) * DK, :]


def _scan(is_gla, qk, vt, lg, ld, s0, want_state, seq_blk, unroll):
    B, T, _ = qk.shape
    n_chunks = T // SCAN_CHUNK
    assert n_chunks % 2 == 0 and B % seq_blk == 0
    sb = seq_blk
    grp = 1 if is_gla else 0
    in_specs = [
        pl.BlockSpec((sb, T, QK), lambda b: (b, 0, 2 * grp)),
        pl.BlockSpec((sb, T, QK), lambda b: (b, 0, 2 * grp + 1)),
        pl.BlockSpec((sb, n_chunks, WV, SCAN_CHUNK), lambda b: (b, 0, grp, 0)),
    ]
    args = [qk, qk, vt]
    if is_gla:
        in_specs += [
            pl.BlockSpec((sb, T, QK), lambda b: (b, 0, 0)),
            pl.BlockSpec((sb, T, QK), lambda b: (b, 0, 1)),
        ]
        args += [lg, lg]
    else:
        in_specs += [pl.BlockSpec(memory_space=pltpu.SMEM)]
        args += [ld]
    state_spec = pl.BlockSpec((sb, 1, 2, N_HEADS, DK, DV), lambda b: (b, 0, 0, 0, 0, 0))
    if s0 is not None:
        in_specs += [state_spec]
        args += [s0]
    out_shape = [jax.ShapeDtypeStruct((B, T, WV), BF16)]
    out_specs = [pl.BlockSpec((sb, T, WV), lambda b: (b, 0, 0))]
    if want_state:
        out_shape += [jax.ShapeDtypeStruct((B, 1, 2, N_HEADS, DK, DV), F32)]
        out_specs += [state_spec]
    name = ("gla" if is_gla else "ret") + ("_init" if s0 is not None else "") + "_scan"
    return pl.pallas_call(
        functools.partial(_scan_kernel, is_gla, s0 is not None, want_state, n_chunks, sb, unroll),
        grid=(B // sb,),
        in_specs=in_specs,
        out_specs=out_specs,
        out_shape=out_shape,
        scratch_shapes=[pltpu.VMEM((sb, N_HEADS // 2, 2, 2 * DV, LANES), F32),
                        pltpu.VMEM((sb, T, WV), F32)],
        compiler_params=pltpu.CompilerParams(
            dimension_semantics=("arbitrary",), vmem_limit_bytes=VMEM_LIMIT),
        name=name,
    )(*args)


def _outproj_kernel(*refs):
    for sub in range(OUT_TOKEN_BLOCK // OUTPROJ_SUB):
        _outproj_rows(slice(sub * OUTPROJ_SUB, (sub + 1) * OUTPROJ_SUB), *refs)


def _outproj(x, mod, mod_row_fn, o_r, o_g, gz, gw, fw, wout):
    n_tok = x.shape[0] * x.shape[1]
    B = n_tok // OUT_TOKEN_BLOCK
    x, o_r, o_g, gz = (a.reshape(B, OUT_TOKEN_BLOCK, a.shape[-1]) for a in (x, o_r, o_g, gz))
    const = lambda b: (0, 0)
    blk = lambda width: pl.BlockSpec((1, OUT_TOKEN_BLOCK, width), lambda b: (b, 0, 0))
    return pl.pallas_call(
        _outproj_kernel,
        grid=(B,),
        in_specs=[
            blk(D_MODEL),
            pl.BlockSpec((1, 1, 3 * D_MODEL), lambda b: (mod_row_fn(b), 0, 0)),
            blk(WV),
            blk(WV),
            blk(2 * WV),
            pl.BlockSpec((1, DV), const),
            pl.BlockSpec((1, D_MODEL), const),
            pl.BlockSpec(wout.shape, const),
        ],
        out_specs=blk(D_MODEL),
        out_shape=jax.ShapeDtypeStruct((B, OUT_TOKEN_BLOCK, D_MODEL), F32),
        compiler_params=pltpu.CompilerParams(
            dimension_semantics=("parallel",), vmem_limit_bytes=OUT_VMEM_LIMIT),
        name="outproj",
    )(x, mod, o_r, o_g, gz, gw, fw, wout)


def _rope_tables(n_tokens):
    rows = n_tokens // GRID_W
    rr, cc = np.meshgrid(np.arange(rows), np.arange(GRID_W), indexing="ij")
    rr = rr.reshape(-1).astype(np.float32)
    cc = cc.reshape(-1).astype(np.float32)
    n_freq = DK // 4
    inv = np.float32(ROPE_BASE) ** (-np.arange(n_freq, dtype=np.float32) / np.float32(n_freq))
    ang = np.concatenate([rr[:, None] * inv, cc[:, None] * inv], axis=-1)
    cos = np.cos(ang.astype(np.float64)).astype(np.float32)
    sin = np.sin(ang.astype(np.float64)).astype(np.float32)
    cos_t = np.tile(np.concatenate([cos, cos], axis=-1), (1, LANES // DK))
    sin_t = np.tile(np.concatenate([-sin, sin], axis=-1), (1, LANES // DK))
    return jnp.asarray(cos_t), jnp.asarray(sin_t)


def _prep_weights(w_in, gla_w_alpha, gla_b_alpha):
    q_r, k_r = w_in[:, 0:QK], w_in[:, QK:2 * QK]
    o = 2 * QK
    v_r, z_r = w_in[:, o:o + WV], w_in[:, o + WV:o + 2 * WV]
    o += 2 * WV
    q_g, k_g = w_in[:, o:o + QK], w_in[:, o + QK:o + 2 * QK]
    o += 2 * QK
    v_g, z_g = w_in[:, o:o + WV], w_in[:, o + WV:o + 2 * WV]
    o += 2 * WV
    lr = w_in[:, o:o + 2 * GLA_RANK]
    wqk = jnp.concatenate([q_r, k_r, q_g, k_g], axis=1).astype(BF16)
    wz = jnp.concatenate([z_r, z_g], axis=1).astype(BF16)
    wvt = jnp.concatenate([v_r, v_g, lr], axis=1).T.astype(BF16)
    wa = jnp.zeros((2 * GLA_RANK, 2 * QK), F32)
    wa = wa.at[0:GLA_RANK, 0:QK].set(gla_w_alpha[0])
    wa = wa.at[GLA_RANK:2 * GLA_RANK, QK:2 * QK].set(gla_w_alpha[1])
    ba = jnp.concatenate([gla_b_alpha[0], gla_b_alpha[1]]).reshape(1, 2 * QK)
    return wqk, wz, wvt, wa.astype(BF16), ba


def kernel(x_prompt, x_sample, c, state_ret, state_gla, c_ctx, w_mod, b_mod, w_in, ret_log_decay,
           gla_w_alpha, gla_b_alpha, gla_norm_w, w_out, final_norm_w):
    assert w_mod.shape[0] == 1, "single-layer model"
    b_ctx, t_ctx, _ = x_prompt.shape
    b_dec, t_dec, _ = x_sample.shape
    assert t_ctx % SCAN_CHUNK == 0 and t_dec % TOKEN_BLOCK == 0 and TOKEN_BLOCK % t_ctx == 0
    assert 1 + b_dec <= MOD_ROWS

    cond = jnp.concatenate(
        [c_ctx[None, :], c, jnp.zeros((MOD_ROWS - 1 - b_dec, D_MODEL), F32)], axis=0)
    mod = _modulation(cond, w_mod[0], b_mod[0]).reshape(MOD_ROWS, 1, 3 * D_MODEL)

    weights = _prep_weights(w_in[0], gla_w_alpha[0], gla_b_alpha[0])
    wout = w_out[0].astype(BF16)
    gw = gla_norm_w[0].reshape(1, DV)
    fw = final_norm_w.reshape(1, D_MODEL)
    ld = ret_log_decay[0]

    per_blk = TOKEN_BLOCK // t_ctx
    xp = x_prompt.reshape(b_ctx // per_blk, TOKEN_BLOCK, D_MODEL)
    ctx_row = lambda b: 0
    qk, gz, vt, lg = _inproj(xp, mod, ctx_row, weights, None)
    qk = qk.reshape(b_ctx, t_ctx, 4 * QK)
    vt = vt.reshape(b_ctx, t_ctx // SCAN_CHUNK, 2 * WV, SCAN_CHUNK)
    lg = lg.reshape(b_ctx, t_ctx, 2 * QK)
    o_r, new_ret = _scan(False, qk, vt, None, ld, None, True, CTX_SEQ_BLOCK, 1)
    o_g, new_gla = _scan(True, qk, vt, lg, None, None, True, CTX_SEQ_BLOCK, 1)
    y_prompt = _outproj(xp, mod, ctx_row, o_r, o_g, gz, gw, fw, wout)
    y_prompt = y_prompt.reshape(b_ctx, t_ctx, D_MODEL)

    dec_row = lambda b: b + 1
    rope = _rope_tables(t_dec)
    qk, gz, vt, lg = _inproj(x_sample, mod, dec_row, weights, rope)
    (o_r,) = _scan(False, qk, vt, None, ld, state_ret, False, 1, DEC_UNROLL)
    (o_g,) = _scan(True, qk, vt, lg, None, state_gla, False, 1, DEC_UNROLL)
    per_seq = t_dec // OUT_TOKEN_BLOCK
    y_sample = _outproj(x_sample, mod, lambda b: b // per_seq + 1, o_r, o_g, gz, gw, fw, wout)
    y_sample = y_sample.reshape(b_dec, t_dec, D_MODEL)

    return (y_prompt, y_sample, new_ret, new_gla)
```

```python
import functools

import jax
import jax.numpy as jnp
import numpy as np
from jax import lax
from jax.experimental import pallas as pl
from jax.experimental.pallas import tpu as pltpu

F32 = jnp.float32
BF16 = jnp.bfloat16

D_MODEL = 1024
N_HEADS = 4
DK = 64
DV = 128
QK = N_HEADS * DK
WV = N_HEADS * DV
GLA_RANK = 16
GLA_TAU = 16.0
EPS = 1e-6
GRID_W = 64
ROPE_BASE = 10000.0

LANES = 128
SCAN_CHUNK = 128
CTX_SEQ_BLOCK = 8
DEC_UNROLL = 8
RET_PIPE_GROUP = 2
GLA_PIPE_GROUP = 8
TOKEN_BLOCK = 1024
INPROJ_SUB = 256
OUTPROJ_SUB = 256
MOD_ROWS = 16
MOD_COL_BLOCK = 512
OUT_TOKEN_BLOCK = 1024
VMEM_LIMIT = 48 * 1024 * 1024
OUT_VMEM_LIMIT = VMEM_LIMIT

_NT = (((1,), (1,)), ((), ()))
_TN = (((0,), (0,)), ((), ()))
_TNT = (((0,), (1,)), ((), ()))


def _dot(a, b):
    return jnp.dot(a, b, preferred_element_type=F32)


def _dot_nt(a, b):
    return lax.dot_general(a, b, _NT, preferred_element_type=F32)


def _silu(x):
    return x * jax.nn.sigmoid(x)


def _mod_kernel(c_ref, w_ref, b_ref, o_ref):
    s = _silu(c_ref[...]).astype(BF16)
    o_ref[...] = _dot(s, w_ref[...].astype(BF16)) + b_ref[...]


def _modulation(cond, w_mod, b_mod):
    n_col = w_mod.shape[1]
    return pl.pallas_call(
        _mod_kernel,
        grid=(n_col // MOD_COL_BLOCK,),
        in_specs=[
            pl.BlockSpec((MOD_ROWS, D_MODEL), lambda j: (0, 0)),
            pl.BlockSpec((D_MODEL, MOD_COL_BLOCK), lambda j: (0, j)),
            pl.BlockSpec((1, MOD_COL_BLOCK), lambda j: (0, j)),
        ],
        out_specs=pl.BlockSpec((MOD_ROWS, MOD_COL_BLOCK), lambda j: (0, j)),
        out_shape=jax.ShapeDtypeStruct((MOD_ROWS, n_col), F32),
        compiler_params=pltpu.CompilerParams(vmem_limit_bytes=VMEM_LIMIT),
        name="modulation",
    )(cond, w_mod, b_mod.reshape(1, n_col))


def _rope_tile(t, cos, sin_signed):
    lane = lax.broadcasted_iota(jnp.int32, t.shape, 1)
    first_half = (lane & 32) == 0
    swapped = jnp.where(first_half, pltpu.roll(t, LANES - 32, 1), pltpu.roll(t, 32, 1))
    return t * cos + swapped * sin_signed


def _outproj_rows(rows, x_ref, mod_ref, or_ref, og_ref, gz_ref, gw_ref, fw_ref, wout_ref, y_ref):
    gate = mod_ref[0, :, 2 * D_MODEL:3 * D_MODEL]
    gw = gw_ref[...]
    parts = []
    for h in range(N_HEADS):
        cols = slice(h * DV, (h + 1) * DV)
        t = or_ref[0, rows, cols].astype(F32)
        mu = jnp.mean(t, axis=-1, keepdims=True)
        dlt = t - mu
        var = jnp.mean(dlt * dlt, axis=-1, keepdims=True)
        n = dlt * lax.rsqrt(var + EPS)
        parts.append((n * gz_ref[0, rows, cols].astype(F32)).astype(BF16))
    for h in range(N_HEADS):
        cols = slice(h * DV, (h + 1) * DV)
        t = og_ref[0, rows, cols].astype(F32)
        n = t * lax.rsqrt(jnp.mean(t * t, axis=-1, keepdims=True) + EPS) * gw
        gcols = slice(WV + h * DV, WV + (h + 1) * DV)
        parts.append((n * gz_ref[0, rows, gcols].astype(F32)).astype(BF16))
    mixed = jnp.concatenate(parts, axis=-1)
    out = _dot(mixed, wout_ref[...])
    y = x_ref[0, rows, :] + gate * out
    yn = y * lax.rsqrt(jnp.mean(y * y, axis=-1, keepdims=True) + EPS)
    y_ref[0, rows, :] = yn * fw_ref[...]


def _inproj_kernel(use_rope, x_ref, mod_ref, wqk_ref, wz_ref, wvt_ref, wa_ref, ba_ref, *rest):
    if use_rope:
        cos_ref, sin_ref, qk_ref, z_ref, vt_ref, lg_ref = rest
    else:
        qk_ref, z_ref, vt_ref, lg_ref = rest
    shift = mod_ref[0, :, 0:D_MODEL]
    scale1 = 1.0 + mod_ref[0, :, D_MODEL:2 * D_MODEL]

    def gate_logs(lr_t, rows):
        logit = lax.dot_general(lr_t, wa_ref[...], _TN, preferred_element_type=F32) + ba_ref[...]
        log_sig = jnp.minimum(logit, 0.0) - jnp.log(1.0 + jnp.exp(-jnp.abs(logit)))
        lg_ref[0, rows, :] = log_sig * (1.0 / GLA_TAU)

    for sub in range(TOKEN_BLOCK // INPROJ_SUB):
        rows = slice(sub * INPROJ_SUB, (sub + 1) * INPROJ_SUB)
        x = x_ref[0, rows, :]
        inv = lax.rsqrt(jnp.mean(x * x, axis=-1, keepdims=True) + EPS)
        hb = ((x * inv) * scale1 + shift).astype(BF16)

        lr_t = lax.dot_general(wvt_ref[:, 2 * WV:], hb, _TNT,
                               preferred_element_type=F32).astype(BF16)

        for half in range(2):
            res = _dot(hb, wqk_ref[:, half * 2 * QK:(half + 1) * 2 * QK])
            if use_rope and half == 0:
                for t in range(4):
                    tile = res[:, t * LANES:(t + 1) * LANES]
                    if t >= 2:
                        tile = tile * (DK ** -0.5)
                    tile = _rope_tile(tile, cos_ref[rows, :], sin_ref[rows, :])
                    qk_ref[0, rows, t * LANES:(t + 1) * LANES] = tile.astype(BF16)
            else:
                lane = lax.broadcasted_iota(jnp.int32, (1, 2 * QK), 1)
                scaled = (lane >= QK) if half == 0 else (lane < QK)
                col_scale = jnp.where(scaled, DK ** -0.5, 1.0)
                qk_ref[0, rows, half * 2 * QK:(half + 1) * 2 * QK] = (res * col_scale).astype(BF16)

        gate_logs(lr_t, rows)

        z_ref[0, rows, :] = _silu(_dot(hb, wz_ref[...])).astype(BF16)

        vt = lax.dot_general(wvt_ref[:, 0:2 * WV], hb, _TNT, preferred_element_type=F32)
        for j in range(INPROJ_SUB // SCAN_CHUNK):
            jj = sub * (INPROJ_SUB // SCAN_CHUNK) + j
            vt_ref[0, jj] = vt[:, j * SCAN_CHUNK:(j + 1) * SCAN_CHUNK].astype(BF16)


def _inproj(x, mod, mod_row_fn, weights, rope):
    B, T, _ = x.shape
    wqk, wz, wvt, wa, ba = weights
    nt = T // TOKEN_BLOCK
    cpb = TOKEN_BLOCK // SCAN_CHUNK
    const = lambda b, t: (0, 0)
    in_specs = [
        pl.BlockSpec((1, TOKEN_BLOCK, D_MODEL), lambda b, t: (b, t, 0)),
        pl.BlockSpec((1, 1, 3 * D_MODEL), lambda b, t: (mod_row_fn(b), 0, 0)),
        pl.BlockSpec(wqk.shape, const),
        pl.BlockSpec(wz.shape, const),
        pl.BlockSpec(wvt.shape, const),
        pl.BlockSpec(wa.shape, const),
        pl.BlockSpec(ba.shape, const),
    ]
    args = [x, mod, wqk, wz, wvt, wa, ba]
    if rope is not None:
        in_specs += [pl.BlockSpec((TOKEN_BLOCK, LANES), lambda b, t: (t, 0))] * 2
        args += list(rope)
    out_shape = [
        jax.ShapeDtypeStruct((B, T, 4 * QK), BF16),
        jax.ShapeDtypeStruct((B, T, 2 * WV), BF16),
        jax.ShapeDtypeStruct((B, T // SCAN_CHUNK, 2 * WV, SCAN_CHUNK), BF16),
        jax.ShapeDtypeStruct((B, T, 2 * QK), F32),
    ]
    out_specs = [
        pl.BlockSpec((1, TOKEN_BLOCK, 4 * QK), lambda b, t: (b, t, 0)),
        pl.BlockSpec((1, TOKEN_BLOCK, 2 * WV), lambda b, t: (b, t, 0)),
        pl.BlockSpec((1, cpb, 2 * WV, SCAN_CHUNK), lambda b, t: (b, t, 0, 0)),
        pl.BlockSpec((1, TOKEN_BLOCK, 2 * QK), lambda b, t: (b, t, 0)),
    ]
    return pl.pallas_call(
        functools.partial(_inproj_kernel, rope is not None),
        grid=(B, nt),
        in_specs=in_specs,
        out_specs=out_specs,
        out_shape=out_shape,
        compiler_params=pltpu.CompilerParams(
            dimension_semantics=("parallel", "parallel"), vmem_limit_bytes=VMEM_LIMIT),
        name="inproj_rope" if rope is not None else "inproj",
    )(*args)


def _scan_kernel(is_gla, has_init, want_state, n_chunks, seq_blk, unroll, *refs):
    refs = list(refs)
    q_ref, k_ref, vt_ref = refs[:3]
    pos = 3
    if is_gla:
        lgf_ref, lgb_ref = refs[pos:pos + 2]
        pos += 2
    else:
        ld_ref = refs[pos]
        pos += 1
    if has_init:
        s0_ref = refs[pos]
        pos += 1
    o_ref = refs[pos]
    pos += 1
    if want_state:
        sn_ref = refs[pos]
        pos += 1
    st_ref, oacc_ref = refs[pos:pos + 2]

    C = SCAN_CHUNK
    mid = C // 2
    n_pairs = N_HEADS // 2
    lane = lax.broadcasted_iota(jnp.int32, (1, LANES), 1)
    head_mask = (lane < DK, lane >= DK)
    ri = lax.broadcasted_iota(jnp.int32, (C, C), 0)
    ci = lax.broadcasted_iota(jnp.int32, (C, C), 1)
    tpos = lax.broadcasted_iota(jnp.int32, (C, 1), 0).astype(F32)
    tri = ((ri >= ci), (ci >= ri))

    if has_init:
        for s in range(seq_blk):
            for p in range(n_pairs):
                for d in range(2):
                    s0 = s0_ref[s, 0, d]
                    both = jnp.concatenate([s0[2 * p], s0[2 * p + 1]], axis=0).T
                    st_ref[s, p, d, 0:DV] = both
                    st_ref[s, p, d, DV:2 * DV] = both
    else:
        st_ref[...] = jnp.zeros(st_ref.shape, F32)

    if is_gla:
        tri_ones = [t.astype(BF16) for t in tri]
    else:
        q_mul, k_mul, ret_decay, score_mul = {}, {}, {}, {}
        for p in range(n_pairs):
            for d in range(2):
                ld0 = ld_ref[d, 2 * p]
                ld1 = ld_ref[d, 2 * p + 1]
                ldr = jnp.where(lane < DK, ld0, ld1)
                if d == 0:
                    q_mul[p, d] = jnp.exp(ldr * (tpos + 1.0))
                    k_mul[p, d] = jnp.exp(ldr * (C - 1.0 - tpos))
                    dist = (ri - ci).astype(F32)
                else:
                    q_mul[p, d] = jnp.exp(ldr * (C - tpos))
                    k_mul[p, d] = jnp.exp(ldr * tpos)
                    dist = (ci - ri).astype(F32)
                ret_decay[p, d] = jnp.exp(ldr * float(C))
                score_mul[p, d] = [jnp.where(tri[d], jnp.exp(ldh * dist), 0.0) for ldh in (ld0, ld1)]
            score_mul[p] = [score_mul[p, 0][hh] + score_mul[p, 1][hh] for hh in range(2)]

    def has_intra(d):
        return is_gla or d == 0

    zero = jnp.zeros((), BF16)
    groups = [(s, p, d) for s in range(seq_blk) for p in range(n_pairs) for d in range(2)]

    def run_block(n0, accumulate):
        keys = [g + (u,) for u in range(unroll) for g in groups]
        chains = {}

        def stage_load(key):
            s, p, d, u = key
            ptile = slice(p * LANES, (p + 1) * LANES)
            n = n0 + u
            c = n if d == 0 else n_chunks - 1 - n
            t = dict(c=c, rows=pl.ds(pl.multiple_of(c * C, C), C))
            t["q"] = q_ref[s, t["rows"], ptile]
            t["k"] = k_ref[s, t["rows"], ptile]
            if is_gla:
                lg = (lgf_ref if d == 0 else lgb_ref)[s, t["rows"], ptile]
                hi = lg.astype(BF16)
                lo = (lg - hi.astype(F32)).astype(BF16)
                bb = _dot(tri_ones[d], jnp.concatenate([hi, lo], axis=1))
                t["b"] = bb[:, :LANES] + bb[:, LANES:]
            chains[key] = t

        def stage_scores(key):
            s, p, d, u = key
            t = chains[key]
            q, k = t.pop("q"), t.pop("k")
            if is_gla:
                b = t.pop("b")
                r = b[mid - 1:mid] if d == 0 else b[mid:mid + 1]
                bl = b[C - 1:C] if d == 0 else b[0:1]
                qs = q.astype(F32) * jnp.exp(b - r)
                ks = k.astype(F32) * jnp.exp(r - b)
                q_sc = qs.astype(BF16)
                k_sc = ks.astype(BF16)
                q_dec = (qs * jnp.exp(r)).astype(BF16)
                t["k_dec"] = (ks * jnp.exp(bl - r)).astype(BF16)
                t["decay"] = jnp.exp(bl)
            else:
                q_sc = q
                k_sc = k
                q_dec = (q.astype(F32) * q_mul[p, d]).astype(BF16)
                t["k_dec"] = (k.astype(F32) * k_mul[p, d]).astype(BF16)
                t["decay"] = ret_decay[p, d]
            t["q_dec"] = [jnp.where(head_mask[hh], q_dec, zero) for hh in range(2)]
            if has_intra(d):
                k_heads = jnp.concatenate([jnp.where(head_mask[hh], k_sc, zero) for hh in range(2)], axis=0)
                t["sc"] = _dot_nt(q_sc, k_heads)

        def stage_kv(key):
            s, p, d, u = key
            t = chains[key]
            t["vt"] = vt_ref[s, t["c"], p * 2 * DV:(p + 1) * 2 * DV, :]
            t["kv"] = _dot(t["vt"], t.pop("k_dec"))
            t["scb"] = []
            if has_intra(d):
                sc_all = t.pop("sc")
                for hh in range(2):
                    sc = sc_all[:, hh * C:(hh + 1) * C]
                    if is_gla:
                        sc = jnp.where(tri[d], sc, 0.0)
                    else:
                        sc = sc * score_mul[p][hh]
                    t["scb"].append(sc.astype(BF16))

        def stage_out(key):
            s, p, d, u = key
            t = chains[key]
            st = st_ref[s, p, d]
            stb = st.astype(BF16)
            t["o"] = []
            for hh in range(2):
                hrows = slice(hh * DV, (hh + 1) * DV)
                if has_intra(d):
                    lhs = jnp.concatenate([t["scb"][hh], t["q_dec"][hh]], axis=1)
                    rhs = jnp.concatenate([t["vt"][hrows], stb[hrows]], axis=1)
                else:
                    lhs, rhs = t["q_dec"][hh], stb[hrows]
                t["o"].append(_dot_nt(lhs, rhs))
            st_ref[s, p, d] = st * t.pop("decay") + t.pop("kv")

        def stage_store(key):
            s, p, d, u = key
            t = chains.pop(key)
            for hh in range(2):
                cols = slice((2 * p + hh) * DV, (2 * p + hh + 1) * DV)
                if accumulate:
                    o_ref[s, t["rows"], cols] = (oacc_ref[s, t["rows"], cols] + t["o"][hh]).astype(BF16)
                else:
                    oacc_ref[s, t["rows"], cols] = t["o"][hh]

        stages = (stage_load, stage_scores, stage_kv, stage_out, stage_store)
        grp = GLA_PIPE_GROUP if is_gla else RET_PIPE_GROUP
        key_groups = [keys[i:i + grp] for i in range(0, len(keys), grp)]
        for step in range(len(key_groups) + len(stages) - 1):
            for k in reversed(range(len(stages))):
                if 0 <= step - k < len(key_groups):
                    for key in key_groups[step - k]:
                        stages[k](key)

    def loop_body(accumulate, base):
        def body(m, carry):
            run_block(base + m * unroll, accumulate)
            return carry
        return body

    half = n_chunks // 2
    assert half % unroll == 0
    lax.fori_loop(0, half // unroll, loop_body(False, 0), 0)
    lax.fori_loop(0, half // unroll, loop_body(True, half), 0)

    if want_state:
        for s, p, d in groups:
            for hh in range(2):
                sn_ref[s, 0, d, 2 * p + hh] = st_ref[s, p, d, hh * DV:(hh + 1) * DV].T[hh * DK:(hh + 1) * DK, :]


def _scan(is_gla, qk, vt, lg, ld, s0, want_state, seq_blk, unroll):
    B, T, _ = qk.shape
    n_chunks = T // SCAN_CHUNK
    assert n_chunks % 2 == 0 and B % seq_blk == 0
    sb = seq_blk
    grp = 1 if is_gla else 0
    in_specs = [
        pl.BlockSpec((sb, T, QK), lambda b: (b, 0, 2 * grp)),
        pl.BlockSpec((sb, T, QK), lambda b: (b, 0, 2 * grp + 1)),
        pl.BlockSpec((sb, n_chunks, WV, SCAN_CHUNK), lambda b: (b, 0, grp, 0)),
    ]
    args = [qk, qk, vt]
    if is_gla:
        in_specs += [
            pl.BlockSpec((sb, T, QK), lambda b: (b, 0, 0)),
            pl.BlockSpec((sb, T, QK), lambda b: (b, 0, 1)),
        ]
        args += [lg, lg]
    else:
        in_specs += [pl.BlockSpec(memory_space=pltpu.SMEM)]
        args += [ld]
    state_spec = pl.BlockSpec((sb, 1, 2, N_HEADS, DK, DV), lambda b: (b, 0, 0, 0, 0, 0))
    if s0 is not None:
        in_specs += [state_spec]
        args += [s0]
    out_shape = [jax.ShapeDtypeStruct((B, T, WV), BF16)]
    out_specs = [pl.BlockSpec((sb, T, WV), lambda b: (b, 0, 0))]
    if want_state:
        out_shape += [jax.ShapeDtypeStruct((B, 1, 2, N_HEADS, DK, DV), F32)]
        out_specs += [state_spec]
    name = ("gla" if is_gla else "ret") + ("_init" if s0 is not None else "") + "_scan"
    return pl.pallas_call(
        functools.partial(_scan_kernel, is_gla, s0 is not None, want_state, n_chunks, sb, unroll),
        grid=(B // sb,),
        in_specs=in_specs,
        out_specs=out_specs,
        out_shape=out_shape,
        scratch_shapes=[pltpu.VMEM((sb, N_HEADS // 2, 2, 2 * DV, LANES), F32),
                        pltpu.VMEM((sb, T, WV), F32)],
        compiler_params=pltpu.CompilerParams(
            dimension_semantics=("arbitrary",), vmem_limit_bytes=VMEM_LIMIT),
        name=name,
    )(*args)


def _outproj_kernel(*refs):
    for sub in range(OUT_TOKEN_BLOCK // OUTPROJ_SUB):
        _outproj_rows(slice(sub * OUTPROJ_SUB, (sub + 1) * OUTPROJ_SUB), *refs)


def _outproj(x, mod, mod_row_fn, o_r, o_g, gz, gw, fw, wout):
    n_tok = x.shape[0] * x.shape[1]
    B = n_tok // OUT_TOKEN_BLOCK
    x, o_r, o_g, gz = (a.reshape(B, OUT_TOKEN_BLOCK, a.shape[-1]) for a in (x, o_r, o_g, gz))
    const = lambda b: (0, 0)
    blk = lambda width: pl.BlockSpec((1, OUT_TOKEN_BLOCK, width), lambda b: (b, 0, 0))
    return pl.pallas_call(
        _outproj_kernel,
        grid=(B,),
        in_specs=[
            blk(D_MODEL),
            pl.BlockSpec((1, 1, 3 * D_MODEL), lambda b: (mod_row_fn(b), 0, 0)),
            blk(WV),
            blk(WV),
            blk(2 * WV),
            pl.BlockSpec((1, DV), const),
            pl.BlockSpec((1, D_MODEL), const),
            pl.BlockSpec(wout.shape, const),
        ],
        out_specs=blk(D_MODEL),
        out_shape=jax.ShapeDtypeStruct((B, OUT_TOKEN_BLOCK, D_MODEL), F32),
        compiler_params=pltpu.CompilerParams(
            dimension_semantics=("parallel",), vmem_limit_bytes=OUT_VMEM_LIMIT),
        name="outproj",
    )(x, mod, o_r, o_g, gz, gw, fw, wout)


def _rope_tables(n_tokens):
    rows = n_tokens // GRID_W
    rr, cc = np.meshgrid(np.arange(rows), np.arange(GRID_W), indexing="ij")
    rr = rr.reshape(-1).astype(np.float32)
    cc = cc.reshape(-1).astype(np.float32)
    n_freq = DK // 4
    inv = np.float32(ROPE_BASE) ** (-np.arange(n_freq, dtype=np.float32) / np.float32(n_freq))
    ang = np.concatenate([rr[:, None] * inv, cc[:, None] * inv], axis=-1)
    cos = np.cos(ang.astype(np.float64)).astype(np.float32)
    sin = np.sin(ang.astype(np.float64)).astype(np.float32)
    cos_t = np.tile(np.concatenate([cos, cos], axis=-1), (1, LANES // DK))
    sin_t = np.tile(np.concatenate([-sin, sin], axis=-1), (1, LANES // DK))
    return jnp.asarray(cos_t), jnp.asarray(sin_t)


def _prep_weights(w_in, gla_w_alpha, gla_b_alpha):
    q_r, k_r = w_in[:, 0:QK], w_in[:, QK:2 * QK]
    o = 2 * QK
    v_r, z_r = w_in[:, o:o + WV], w_in[:, o + WV:o + 2 * WV]
    o += 2 * WV
    q_g, k_g = w_in[:, o:o + QK], w_in[:, o + QK:o + 2 * QK]
    o += 2 * QK
    v_g, z_g = w_in[:, o:o + WV], w_in[:, o + WV:o + 2 * WV]
    o += 2 * WV
    lr = w_in[:, o:o + 2 * GLA_RANK]
    wqk = jnp.concatenate([q_r, k_r, q_g, k_g], axis=1).astype(BF16)
    wz = jnp.concatenate([z_r, z_g], axis=1).astype(BF16)
    wvt = jnp.concatenate([v_r, v_g, lr], axis=1).astype(BF16)
    wa = jnp.zeros((2 * GLA_RANK, 2 * QK), F32)
    wa = wa.at[0:GLA_RANK, 0:QK].set(gla_w_alpha[0])
    wa = wa.at[GLA_RANK:2 * GLA_RANK, QK:2 * QK].set(gla_w_alpha[1])
    ba = jnp.concatenate([gla_b_alpha[0], gla_b_alpha[1]]).reshape(1, 2 * QK)
    return wqk, wz, wvt, wa.astype(BF16), ba


def kernel(x_prompt, x_sample, c, state_ret, state_gla, c_ctx, w_mod, b_mod, w_in, ret_log_decay,
           gla_w_alpha, gla_b_alpha, gla_norm_w, w_out, final_norm_w):
    assert w_mod.shape[0] == 1, "single-layer model"
    b_ctx, t_ctx, _ = x_prompt.shape
    b_dec, t_dec, _ = x_sample.shape
    assert t_ctx % SCAN_CHUNK == 0 and t_dec % TOKEN_BLOCK == 0 and TOKEN_BLOCK % t_ctx == 0
    assert 1 + b_dec <= MOD_ROWS

    cond = jnp.concatenate(
        [c_ctx[None, :], c, jnp.zeros((MOD_ROWS - 1 - b_dec, D_MODEL), F32)], axis=0)
    mod = _modulation(cond, w_mod[0], b_mod[0]).reshape(MOD_ROWS, 1, 3 * D_MODEL)

    weights = _prep_weights(w_in[0], gla_w_alpha[0], gla_b_alpha[0])
    wout = w_out[0].astype(BF16)
    gw = gla_norm_w[0].reshape(1, DV)
    fw = final_norm_w.reshape(1, D_MODEL)
    ld = ret_log_decay[0]

    per_blk = TOKEN_BLOCK // t_ctx
    xp = x_prompt.reshape(b_ctx // per_blk, TOKEN_BLOCK, D_MODEL)
    ctx_row = lambda b: 0
    qk, gz, vt, lg = _inproj(xp, mod, ctx_row, weights, None)
    qk = qk.reshape(b_ctx, t_ctx, 4 * QK)
    vt = vt.reshape(b_ctx, t_ctx // SCAN_CHUNK, 2 * WV, SCAN_CHUNK)
    lg = lg.reshape(b_ctx, t_ctx, 2 * QK)
    o_r, new_ret = _scan(False, qk, vt, None, ld, None, True, CTX_SEQ_BLOCK, 1)
    o_g, new_gla = _scan(True, qk, vt, lg, None, None, True, CTX_SEQ_BLOCK, 1)
    y_prompt = _outproj(xp, mod, ctx_row, o_r, o_g, gz, gw, fw, wout)
    y_prompt = y_prompt.reshape(b_ctx, t_ctx, D_MODEL)

    dec_row = lambda b: b + 1
    rope = _rope_tables(t_dec)
    qk, gz, vt, lg = _inproj(x_sample, mod, dec_row, weights, rope)
    (o_r,) = _scan(False, qk, vt, None, ld, state_ret, False, 1, DEC_UNROLL)
    (o_g,) = _scan(True, qk, vt, lg, None, state_gla, False, 1, DEC_UNROLL)
    per_seq = t_dec // OUT_TOKEN_BLOCK
    y_sample = _outproj(x_sample, mod, lambda b: b // per_seq + 1, o_r, o_g, gz, gw, fw, wout)
    y_sample = y_sample.reshape(b_dec, t_dec, D_MODEL)

    return (y_prompt, y_sample, new_ret, new_gla)
```

```python
import functools

import jax
import jax.numpy as jnp
import numpy as np
from jax import lax
from jax.experimental import pallas as pl
from jax.experimental.pallas import tpu as pltpu

F32 = jnp.float32
BF16 = jnp.bfloat16

D_MODEL = 1024
N_HEADS = 4
DK = 64
DV = 128
QK = N_HEADS * DK
WV = N_HEADS * DV
GLA_RANK = 16
GLA_TAU = 16.0
EPS = 1e-6
GRID_W = 64
ROPE_BASE = 10000.0

COL_QK = (0, 2 * QK + 2 * WV)
COL_V = (2 * QK, 4 * QK + 2 * WV)
COL_Z = (2 * QK + WV, 4 * QK + 3 * WV)
COL_LR = 4 * QK + 4 * WV

LANES = 128
SCAN_CHUNK = 128
CTX_SEQ_BLOCK = 8
DEC_UNROLL = 8
RET_PIPE_GROUP = 2
GLA_PIPE_GROUP = 8
TOKEN_BLOCK = 1024
INPROJ_SUB = 256
OUTPROJ_SUB = 256
MOD_ROWS = 16
MOD_COL_BLOCK = 512
OUT_TOKEN_BLOCK = 1024
VMEM_LIMIT = 48 * 1024 * 1024
OUT_VMEM_LIMIT = VMEM_LIMIT

_NT = (((1,), (1,)), ((), ()))
_TN = (((0,), (0,)), ((), ()))
_TNT = (((0,), (1,)), ((), ()))


def _dot(a, b):
    return jnp.dot(a, b, preferred_element_type=F32)


def _dot_nt(a, b):
    return lax.dot_general(a, b, _NT, preferred_element_type=F32)


def _silu(x):
    return x * jax.nn.sigmoid(x)


def _mod_kernel(c_ref, w_ref, b_ref, o_ref):
    s = _silu(c_ref[...]).astype(BF16)
    o_ref[...] = _dot(s, w_ref[...].astype(BF16)) + b_ref[...]


def _modulation(cond, w_mod, b_mod):
    n_col = w_mod.shape[1]
    return pl.pallas_call(
        _mod_kernel,
        grid=(n_col // MOD_COL_BLOCK,),
        in_specs=[
            pl.BlockSpec((MOD_ROWS, D_MODEL), lambda j: (0, 0)),
            pl.BlockSpec((D_MODEL, MOD_COL_BLOCK), lambda j: (0, j)),
            pl.BlockSpec((1, MOD_COL_BLOCK), lambda j: (0, j)),
        ],
        out_specs=pl.BlockSpec((MOD_ROWS, MOD_COL_BLOCK), lambda j: (0, j)),
        out_shape=jax.ShapeDtypeStruct((MOD_ROWS, n_col), F32),
        compiler_params=pltpu.CompilerParams(vmem_limit_bytes=VMEM_LIMIT),
        name="modulation",
    )(cond, w_mod, b_mod.reshape(1, n_col))


def _rope_tile(t, cos, sin_signed):
    lane = lax.broadcasted_iota(jnp.int32, t.shape, 1)
    first_half = (lane & 32) == 0
    swapped = jnp.where(first_half, pltpu.roll(t, LANES - 32, 1), pltpu.roll(t, 32, 1))
    return t * cos + swapped * sin_signed


def _outproj_rows(rows, x_ref, mod_ref, or_ref, og_ref, gz_ref, gw_ref, fw_ref, wout_ref, y_ref):
    gate = mod_ref[0, :, 2 * D_MODEL:3 * D_MODEL]
    gw = gw_ref[...]
    parts = []
    for h in range(N_HEADS):
        cols = slice(h * DV, (h + 1) * DV)
        t = or_ref[0, rows, cols].astype(F32)
        mu = jnp.mean(t, axis=-1, keepdims=True)
        dlt = t - mu
        var = jnp.mean(dlt * dlt, axis=-1, keepdims=True)
        n = dlt * lax.rsqrt(var + EPS)
        parts.append((n * gz_ref[0, rows, cols].astype(F32)).astype(BF16))
    for h in range(N_HEADS):
        cols = slice(h * DV, (h + 1) * DV)
        t = og_ref[0, rows, cols].astype(F32)
        n = t * lax.rsqrt(jnp.mean(t * t, axis=-1, keepdims=True) + EPS) * gw
        gcols = slice(WV + h * DV, WV + (h + 1) * DV)
        parts.append((n * gz_ref[0, rows, gcols].astype(F32)).astype(BF16))
    mixed = jnp.concatenate(parts, axis=-1)
    out = _dot(mixed, wout_ref[...])
    y = x_ref[0, rows, :] + gate * out
    yn = y * lax.rsqrt(jnp.mean(y * y, axis=-1, keepdims=True) + EPS)
    y_ref[0, rows, :] = yn * fw_ref[...]


def _inproj_kernel(use_rope, x_ref, mod_ref, w_ref, wa_ref, ba_ref, *rest):
    if use_rope:
        cos_ref, sin_ref, qk_ref, z_ref, vt_ref, lg_ref = rest
    else:
        qk_ref, z_ref, vt_ref, lg_ref = rest
    shift = mod_ref[0, :, 0:D_MODEL]
    scale1 = 1.0 + mod_ref[0, :, D_MODEL:2 * D_MODEL]

    def gate_logs(lr_t, rows):
        logit = lax.dot_general(lr_t, wa_ref[...], _TN, preferred_element_type=F32) + ba_ref[...]
        log_sig = jnp.minimum(logit, 0.0) - jnp.log(1.0 + jnp.exp(-jnp.abs(logit)))
        lg_ref[0, rows, :] = log_sig * (1.0 / GLA_TAU)

    for sub in range(TOKEN_BLOCK // INPROJ_SUB):
        rows = slice(sub * INPROJ_SUB, (sub + 1) * INPROJ_SUB)
        x = x_ref[0, rows, :]
        inv = lax.rsqrt(jnp.mean(x * x, axis=-1, keepdims=True) + EPS)
        hb = ((x * inv) * scale1 + shift).astype(BF16)

        lr_t = lax.dot_general(w_ref[:, COL_LR:COL_LR + 2 * GLA_RANK], hb, _TNT,
                               preferred_element_type=F32).astype(BF16)

        for half in range(2):
            res = _dot(hb, w_ref[:, COL_QK[half]:COL_QK[half] + 2 * QK])
            if use_rope and half == 0:
                for t in range(4):
                    tile = res[:, t * LANES:(t + 1) * LANES]
                    if t >= 2:
                        tile = tile * (DK ** -0.5)
                    tile = _rope_tile(tile, cos_ref[rows, :], sin_ref[rows, :])
                    qk_ref[0, rows, t * LANES:(t + 1) * LANES] = tile.astype(BF16)
            else:
                lane = lax.broadcasted_iota(jnp.int32, (1, 2 * QK), 1)
                scaled = (lane >= QK) if half == 0 else (lane < QK)
                col_scale = jnp.where(scaled, DK ** -0.5, 1.0)
                qk_ref[0, rows, half * 2 * QK:(half + 1) * 2 * QK] = (res * col_scale).astype(BF16)

        gate_logs(lr_t, rows)

        for half in range(2):
            z_ref[0, rows, half * WV:(half + 1) * WV] = _silu(
                _dot(hb, w_ref[:, COL_Z[half]:COL_Z[half] + WV])).astype(BF16)

        for half in range(2):
            vt = lax.dot_general(w_ref[:, COL_V[half]:COL_V[half] + WV], hb, _TNT,
                                 preferred_element_type=F32)
            for j in range(INPROJ_SUB // SCAN_CHUNK):
                jj = sub * (INPROJ_SUB // SCAN_CHUNK) + j
                vt_ref[0, jj, half * WV:(half + 1) * WV, :] = vt[:, j * SCAN_CHUNK:(j + 1) * SCAN_CHUNK].astype(BF16)


def _inproj(x, mod, mod_row_fn, weights, rope):
    B, T, _ = x.shape
    w, wa, ba = weights
    nt = T // TOKEN_BLOCK
    cpb = TOKEN_BLOCK // SCAN_CHUNK
    const = lambda b, t: (0, 0)
    in_specs = [
        pl.BlockSpec((1, TOKEN_BLOCK, D_MODEL), lambda b, t: (b, t, 0)),
        pl.BlockSpec((1, 1, 3 * D_MODEL), lambda b, t: (mod_row_fn(b), 0, 0)),
        pl.BlockSpec(w.shape, const),
        pl.BlockSpec(wa.shape, const),
        pl.BlockSpec(ba.shape, const),
    ]
    args = [x, mod, w, wa, ba]
    if rope is not None:
        in_specs += [pl.BlockSpec((TOKEN_BLOCK, LANES), lambda b, t: (t, 0))] * 2
        args += list(rope)
    out_shape = [
        jax.ShapeDtypeStruct((B, T, 4 * QK), BF16),
        jax.ShapeDtypeStruct((B, T, 2 * WV), BF16),
        jax.ShapeDtypeStruct((B, T // SCAN_CHUNK, 2 * WV, SCAN_CHUNK), BF16),
        jax.ShapeDtypeStruct((B, T, 2 * QK), F32),
    ]
    out_specs = [
        pl.BlockSpec((1, TOKEN_BLOCK, 4 * QK), lambda b, t: (b, t, 0)),
        pl.BlockSpec((1, TOKEN_BLOCK, 2 * WV), lambda b, t: (b, t, 0)),
        pl.BlockSpec((1, cpb, 2 * WV, SCAN_CHUNK), lambda b, t: (b, t, 0, 0)),
        pl.BlockSpec((1, TOKEN_BLOCK, 2 * QK), lambda b, t: (b, t, 0)),
    ]
    return pl.pallas_call(
        functools.partial(_inproj_kernel, rope is not None),
        grid=(B, nt),
        in_specs=in_specs,
        out_specs=out_specs,
        out_shape=out_shape,
        compiler_params=pltpu.CompilerParams(
            dimension_semantics=("parallel", "parallel"), vmem_limit_bytes=VMEM_LIMIT),
        name="inproj_rope" if rope is not None else "inproj",
    )(*args)


def _scan_kernel(is_gla, has_init, want_state, n_chunks, seq_blk, unroll, *refs):
    refs = list(refs)
    q_ref, k_ref, vt_ref = refs[:3]
    pos = 3
    if is_gla:
        lgf_ref, lgb_ref = refs[pos:pos + 2]
        pos += 2
    else:
        ld_ref = refs[pos]
        pos += 1
    if has_init:
        s0_ref = refs[pos]
        pos += 1
    o_ref = refs[pos]
    pos += 1
    if want_state:
        sn_ref = refs[pos]
        pos += 1
    st_ref, oacc_ref = refs[pos:pos + 2]

    C = SCAN_CHUNK
    mid = C // 2
    n_pairs = N_HEADS // 2
    lane = lax.broadcasted_iota(jnp.int32, (1, LANES), 1)
    head_mask = (lane < DK, lane >= DK)
    ri = lax.broadcasted_iota(jnp.int32, (C, C), 0)
    ci = lax.broadcasted_iota(jnp.int32, (C, C), 1)
    tpos = lax.broadcasted_iota(jnp.int32, (C, 1), 0).astype(F32)
    tri = ((ri >= ci), (ci >= ri))

    if has_init:
        for s in range(seq_blk):
            for p in range(n_pairs):
                for d in range(2):
                    s0 = s0_ref[s, 0, d]
                    both = jnp.concatenate([s0[2 * p], s0[2 * p + 1]], axis=0).T
                    st_ref[s, p, d, 0:DV] = both
                    st_ref[s, p, d, DV:2 * DV] = both
    else:
        st_ref[...] = jnp.zeros(st_ref.shape, F32)

    if is_gla:
        tri_ones = [t.astype(BF16) for t in tri]
    else:
        q_mul, k_mul, ret_decay, score_mul = {}, {}, {}, {}
        for p in range(n_pairs):
            for d in range(2):
                ld0 = ld_ref[d, 2 * p]
                ld1 = ld_ref[d, 2 * p + 1]
                ldr = jnp.where(lane < DK, ld0, ld1)
                if d == 0:
                    q_mul[p, d] = jnp.exp(ldr * (tpos + 1.0))
                    k_mul[p, d] = jnp.exp(ldr * (C - 1.0 - tpos))
                    dist = (ri - ci).astype(F32)
                else:
                    q_mul[p, d] = jnp.exp(ldr * (C - tpos))
                    k_mul[p, d] = jnp.exp(ldr * tpos)
                    dist = (ci - ri).astype(F32)
                ret_decay[p, d] = jnp.exp(ldr * float(C))
                score_mul[p, d] = [jnp.where(tri[d], jnp.exp(ldh * dist), 0.0) for ldh in (ld0, ld1)]
            score_mul[p] = [score_mul[p, 0][hh] + score_mul[p, 1][hh] for hh in range(2)]

    def has_intra(d):
        return is_gla or d == 0

    zero = jnp.zeros((), BF16)
    groups = [(s, p, d) for s in range(seq_blk) for p in range(n_pairs) for d in range(2)]

    def run_block(n0, accumulate):
        keys = [g + (u,) for u in range(unroll) for g in groups]
        chains = {}

        def stage_load(key):
            s, p, d, u = key
            ptile = slice(p * LANES, (p + 1) * LANES)
            n = n0 + u
            c = n if d == 0 else n_chunks - 1 - n
            t = dict(c=c, rows=pl.ds(pl.multiple_of(c * C, C), C))
            t["q"] = q_ref[s, t["rows"], ptile]
            t["k"] = k_ref[s, t["rows"], ptile]
            if is_gla:
                lg = (lgf_ref if d == 0 else lgb_ref)[s, t["rows"], ptile]
                hi = lg.astype(BF16)
                lo = (lg - hi.astype(F32)).astype(BF16)
                bb = _dot(tri_ones[d], jnp.concatenate([hi, lo], axis=1))
                t["b"] = bb[:, :LANES] + bb[:, LANES:]
            chains[key] = t

        def stage_scores(key):
            s, p, d, u = key
            t = chains[key]
            q, k = t.pop("q"), t.pop("k")
            if is_gla:
                b = t.pop("b")
                r = b[mid - 1:mid] if d == 0 else b[mid:mid + 1]
                bl = b[C - 1:C] if d == 0 else b[0:1]
                qs = q.astype(F32) * jnp.exp(b - r)
                ks = k.astype(F32) * jnp.exp(r - b)
                q_sc = qs.astype(BF16)
                k_sc = ks.astype(BF16)
                q_dec = (qs * jnp.exp(r)).astype(BF16)
                t["k_dec"] = (ks * jnp.exp(bl - r)).astype(BF16)
                t["decay"] = jnp.exp(bl)
            else:
                q_sc = q
                k_sc = k
                q_dec = (q.astype(F32) * q_mul[p, d]).astype(BF16)
                t["k_dec"] = (k.astype(F32) * k_mul[p, d]).astype(BF16)
                t["decay"] = ret_decay[p, d]
            t["q_dec"] = [jnp.where(head_mask[hh], q_dec, zero) for hh in range(2)]
            if has_intra(d):
                k_heads = jnp.concatenate([jnp.where(head_mask[hh], k_sc, zero) for hh in range(2)], axis=0)
                t["sc"] = _dot_nt(q_sc, k_heads)

        def stage_kv(key):
            s, p, d, u = key
            t = chains[key]
            t["vt"] = vt_ref[s, t["c"], p * 2 * DV:(p + 1) * 2 * DV, :]
            t["kv"] = _dot(t["vt"], t.pop("k_dec"))
            t["scb"] = []
            if has_intra(d):
                sc_all = t.pop("sc")
                for hh in range(2):
                    sc = sc_all[:, hh * C:(hh + 1) * C]
                    if is_gla:
                        sc = jnp.where(tri[d], sc, 0.0)
                    else:
                        sc = sc * score_mul[p][hh]
                    t["scb"].append(sc.astype(BF16))

        def stage_out(key):
            s, p, d, u = key
            t = chains[key]
            st = st_ref[s, p, d]
            stb = st.astype(BF16)
            t["o"] = []
            for hh in range(2):
                hrows = slice(hh * DV, (hh + 1) * DV)
                if has_intra(d):
                    lhs = jnp.concatenate([t["scb"][hh], t["q_dec"][hh]], axis=1)
                    rhs = jnp.concatenate([t["vt"][hrows], stb[hrows]], axis=1)
                else:
                    lhs, rhs = t["q_dec"][hh], stb[hrows]
                t["o"].append(_dot_nt(lhs, rhs))
            st_ref[s, p, d] = st * t.pop("decay") + t.pop("kv")

        def stage_store(key):
            s, p, d, u = key
            t = chains.pop(key)
            for hh in range(2):
                cols = slice((2 * p + hh) * DV, (2 * p + hh + 1) * DV)
                if accumulate:
                    o_ref[s, t["rows"], cols] = (oacc_ref[s, t["rows"], cols] + t["o"][hh]).astype(BF16)
                else:
                    oacc_ref[s, t["rows"], cols] = t["o"][hh]

        stages = (stage_load, stage_scores, stage_kv, stage_out, stage_store)
        grp = GLA_PIPE_GROUP if is_gla else RET_PIPE_GROUP
        key_groups = [keys[i:i + grp] for i in range(0, len(keys), grp)]
        for step in range(len(key_groups) + len(stages) - 1):
            for k in reversed(range(len(stages))):
                if 0 <= step - k < len(key_groups):
                    for key in key_groups[step - k]:
                        stages[k](key)

    def loop_body(accumulate, base):
        def body(m, carry):
            run_block(base + m * unroll, accumulate)
            return carry
        return body

    half = n_chunks // 2
    assert half % unroll == 0
    lax.fori_loop(0, half // unroll, loop_body(False, 0), 0)
    lax.fori_loop(0, half // unroll, loop_body(True, half), 0)

    if want_state:
        for s, p, d in groups:
            for hh in range(2):
                sn_ref[s, 0, d, 2 * p + hh] = st_ref[s, p, d, hh * DV:(hh + 1) * DV].T[hh * DK:(hh + 1) * DK, :]


def _scan(is_gla, qk, vt, lg, ld, s0, want_state, seq_blk, unroll):
    B, T, _ = qk.shape
    n_chunks = T // SCAN_CHUNK
    assert n_chunks % 2 == 0 and B % seq_blk == 0
    sb = seq_blk
    grp = 1 if is_gla else 0
    in_specs = [
        pl.BlockSpec((sb, T, QK), lambda b: (b, 0, 2 * grp)),
        pl.BlockSpec((sb, T, QK), lambda b: (b, 0, 2 * grp + 1)),
        pl.BlockSpec((sb, n_chunks, WV, SCAN_CHUNK), lambda b: (b, 0, grp, 0)),
    ]
    args = [qk, qk, vt]
    if is_gla:
        in_specs += [
            pl.BlockSpec((sb, T, QK), lambda b: (b, 0, 0)),
            pl.BlockSpec((sb, T, QK), lambda b: (b, 0, 1)),
        ]
        args += [lg, lg]
    else:
        in_specs += [pl.BlockSpec(memory_space=pltpu.SMEM)]
        args += [ld]
    state_spec = pl.BlockSpec((sb, 1, 2, N_HEADS, DK, DV), lambda b: (b, 0, 0, 0, 0, 0))
    if s0 is not None:
        in_specs += [state_spec]
        args += [s0]
    out_shape = [jax.ShapeDtypeStruct((B, T, WV), BF16)]
    out_specs = [pl.BlockSpec((sb, T, WV), lambda b: (b, 0, 0))]
    if want_state:
        out_shape += [jax.ShapeDtypeStruct((B, 1, 2, N_HEADS, DK, DV), F32)]
        out_specs += [state_spec]
    name = ("gla" if is_gla else "ret") + ("_init" if s0 is not None else "") + "_scan"
    return pl.pallas_call(
        functools.partial(_scan_kernel, is_gla, s0 is not None, want_state, n_chunks, sb, unroll),
        grid=(B // sb,),
        in_specs=in_specs,
        out_specs=out_specs,
        out_shape=out_shape,
        scratch_shapes=[pltpu.VMEM((sb, N_HEADS // 2, 2, 2 * DV, LANES), F32),
                        pltpu.VMEM((sb, T, WV), F32)],
        compiler_params=pltpu.CompilerParams(
            dimension_semantics=("arbitrary",), vmem_limit_bytes=VMEM_LIMIT),
        name=name,
    )(*args)


def _outproj_kernel(*refs):
    for sub in range(OUT_TOKEN_BLOCK // OUTPROJ_SUB):
        _outproj_rows(slice(sub * OUTPROJ_SUB, (sub + 1) * OUTPROJ_SUB), *refs)


def _outproj(x, mod, mod_row_fn, o_r, o_g, gz, gw, fw, wout):
    n_tok = x.shape[0] * x.shape[1]
    B = n_tok // OUT_TOKEN_BLOCK
    x, o_r, o_g, gz = (a.reshape(B, OUT_TOKEN_BLOCK, a.shape[-1]) for a in (x, o_r, o_g, gz))
    const = lambda b: (0, 0)
    blk = lambda width: pl.BlockSpec((1, OUT_TOKEN_BLOCK, width), lambda b: (b, 0, 0))
    return pl.pallas_call(
        _outproj_kernel,
        grid=(B,),
        in_specs=[
            blk(D_MODEL),
            pl.BlockSpec((1, 1, 3 * D_MODEL), lambda b: (mod_row_fn(b), 0, 0)),
            blk(WV),
            blk(WV),
            blk(2 * WV),
            pl.BlockSpec((1, DV), const),
            pl.BlockSpec((1, D_MODEL), const),
            pl.BlockSpec(wout.shape, const),
        ],
        out_specs=blk(D_MODEL),
        out_shape=jax.ShapeDtypeStruct((B, OUT_TOKEN_BLOCK, D_MODEL), F32),
        compiler_params=pltpu.CompilerParams(
            dimension_semantics=("parallel",), vmem_limit_bytes=OUT_VMEM_LIMIT),
        name="outproj",
    )(x, mod, o_r, o_g, gz, gw, fw, wout)


def _rope_tables(n_tokens):
    rows = n_tokens // GRID_W
    rr, cc = np.meshgrid(np.arange(rows), np.arange(GRID_W), indexing="ij")
    rr = rr.reshape(-1).astype(np.float32)
    cc = cc.reshape(-1).astype(np.float32)
    n_freq = DK // 4
    inv = np.float32(ROPE_BASE) ** (-np.arange(n_freq, dtype=np.float32) / np.float32(n_freq))
    ang = np.concatenate([rr[:, None] * inv, cc[:, None] * inv], axis=-1)
    cos = np.cos(ang.astype(np.float64)).astype(np.float32)
    sin = np.sin(ang.astype(np.float64)).astype(np.float32)
    cos_t = np.tile(np.concatenate([cos, cos], axis=-1), (1, LANES // DK))
    sin_t = np.tile(np.concatenate([-sin, sin], axis=-1), (1, LANES // DK))
    return jnp.asarray(cos_t), jnp.asarray(sin_t)


def _prep_weights(w_in, gla_w_alpha, gla_b_alpha):
    assert w_in.shape == (D_MODEL, COL_LR + 2 * GLA_RANK)
    wa = jnp.zeros((2 * GLA_RANK, 2 * QK), F32)
    wa = wa.at[0:GLA_RANK, 0:QK].set(gla_w_alpha[0])
    wa = wa.at[GLA_RANK:2 * GLA_RANK, QK:2 * QK].set(gla_w_alpha[1])
    ba = jnp.concatenate([gla_b_alpha[0], gla_b_alpha[1]]).reshape(1, 2 * QK)
    return w_in.astype(BF16), wa.astype(BF16), ba


def kernel(x_prompt, x_sample, c, state_ret, state_gla, c_ctx, w_mod, b_mod, w_in, ret_log_decay,
           gla_w_alpha, gla_b_alpha, gla_norm_w, w_out, final_norm_w):
    assert w_mod.shape[0] == 1, "single-layer model"
    b_ctx, t_ctx, _ = x_prompt.shape
    b_dec, t_dec, _ = x_sample.shape
    assert t_ctx % SCAN_CHUNK == 0 and t_dec % TOKEN_BLOCK == 0 and TOKEN_BLOCK % t_ctx == 0
    assert 1 + b_dec <= MOD_ROWS

    cond = jnp.concatenate(
        [c_ctx[None, :], c, jnp.zeros((MOD_ROWS - 1 - b_dec, D_MODEL), F32)], axis=0)
    mod = _modulation(cond, w_mod[0], b_mod[0]).reshape(MOD_ROWS, 1, 3 * D_MODEL)

    weights = _prep_weights(w_in[0], gla_w_alpha[0], gla_b_alpha[0])
    wout = w_out[0].astype(BF16)
    gw = gla_norm_w[0].reshape(1, DV)
    fw = final_norm_w.reshape(1, D_MODEL)
    ld = ret_log_decay[0]

    per_blk = TOKEN_BLOCK // t_ctx
    xp = x_prompt.reshape(b_ctx // per_blk, TOKEN_BLOCK, D_MODEL)
    ctx_row = lambda b: 0
    qk, gz, vt, lg = _inproj(xp, mod, ctx_row, weights, None)
    qk = qk.reshape(b_ctx, t_ctx, 4 * QK)
    vt = vt.reshape(b_ctx, t_ctx // SCAN_CHUNK, 2 * WV, SCAN_CHUNK)
    lg = lg.reshape(b_ctx, t_ctx, 2 * QK)
    o_r, new_ret = _scan(False, qk, vt, None, ld, None, True, CTX_SEQ_BLOCK, 1)
    o_g, new_gla = _scan(True, qk, vt, lg, None, None, True, CTX_SEQ_BLOCK, 1)
    y_prompt = _outproj(xp, mod, ctx_row, o_r, o_g, gz, gw, fw, wout)
    y_prompt = y_prompt.reshape(b_ctx, t_ctx, D_MODEL)

    dec_row = lambda b: b + 1
    rope = _rope_tables(t_dec)
    qk, gz, vt, lg = _inproj(x_sample, mod, dec_row, weights, rope)
    (o_r,) = _scan(False, qk, vt, None, ld, state_ret, False, 1, DEC_UNROLL)
    (o_g,) = _scan(True, qk, vt, lg, None, state_gla, False, 1, DEC_UNROLL)
    per_seq = t_dec // OUT_TOKEN_BLOCK
    y_sample = _outproj(x_sample, mod, lambda b: b // per_seq + 1, o_r, o_g, gz, gw, fw, wout)
    y_sample = y_sample.reshape(b_dec, t_dec, D_MODEL)

    return (y_prompt, y_sample, new_ret, new_gla)
```

```python
import functools

import jax
import jax.numpy as jnp
import numpy as np
from jax import lax
from jax.experimental import pallas as pl
from jax.experimental.pallas import tpu as pltpu

F32 = jnp.float32
BF16 = jnp.bfloat16

D_MODEL = 1024
N_HEADS = 4
DK = 64
DV = 128
QK = N_HEADS * DK
WV = N_HEADS * DV
GLA_RANK = 16
GLA_TAU = 16.0
EPS = 1e-6
GRID_W = 64
ROPE_BASE = 10000.0

COL_QK = (0, 2 * QK + 2 * WV)
COL_V = (2 * QK, 4 * QK + 2 * WV)
COL_Z = (2 * QK + WV, 4 * QK + 3 * WV)
COL_LR = 4 * QK + 4 * WV

LANES = 128
SCAN_CHUNK = 128
CTX_SEQ_BLOCK = 8
DEC_UNROLL = 8
RET_PIPE_GROUP = 2
GLA_PIPE_GROUP = 8
TOKEN_BLOCK = 1024
INPROJ_SUB = 256
OUTPROJ_SUB = 256
MOD_ROWS = 16
MOD_COL_BLOCK = 512
OUT_TOKEN_BLOCK = 1024
CAST_ROWS = 128
VMEM_LIMIT = 48 * 1024 * 1024

_NT = (((1,), (1,)), ((), ()))
_TN = (((0,), (0,)), ((), ()))
_TNT = (((0,), (1,)), ((), ()))


def _dot(a, b):
    return jnp.dot(a, b, preferred_element_type=F32)


def _dot_nt(a, b):
    return lax.dot_general(a, b, _NT, preferred_element_type=F32)


def _silu(x):
    return x * jax.nn.sigmoid(x)


def _mod_kernel(c_ref, w_ref, b_ref, o_ref):
    s = _silu(c_ref[...]).astype(BF16)
    o_ref[...] = _dot(s, w_ref[...].astype(BF16)) + b_ref[...]


def _modulation(cond, w_mod, b_mod):
    n_col = w_mod.shape[1]
    return pl.pallas_call(
        _mod_kernel,
        grid=(n_col // MOD_COL_BLOCK,),
        in_specs=[
            pl.BlockSpec((MOD_ROWS, D_MODEL), lambda j: (0, 0)),
            pl.BlockSpec((D_MODEL, MOD_COL_BLOCK), lambda j: (0, j)),
            pl.BlockSpec((1, MOD_COL_BLOCK), lambda j: (0, j)),
        ],
        out_specs=pl.BlockSpec((MOD_ROWS, MOD_COL_BLOCK), lambda j: (0, j)),
        out_shape=jax.ShapeDtypeStruct((MOD_ROWS, n_col), F32),
        compiler_params=pltpu.CompilerParams(vmem_limit_bytes=VMEM_LIMIT),
        name="modulation",
    )(cond, w_mod, b_mod.reshape(1, n_col))


def _cast_kernel(w_ref, o_ref):
    o_ref[...] = w_ref[...].astype(BF16)


def _to_bf16(w):
    rows, cols = w.shape
    spec = pl.BlockSpec((CAST_ROWS, cols), lambda i: (i, 0))
    return pl.pallas_call(
        _cast_kernel,
        grid=(rows // CAST_ROWS,),
        in_specs=[spec],
        out_specs=spec,
        out_shape=jax.ShapeDtypeStruct((rows, cols), BF16),
        compiler_params=pltpu.CompilerParams(vmem_limit_bytes=VMEM_LIMIT),
        name="cast_bf16",
    )(w)


def _rope_tile(t, cos, sin_signed):
    lane = lax.broadcasted_iota(jnp.int32, t.shape, 1)
    first_half = (lane & 32) == 0
    swapped = jnp.where(first_half, pltpu.roll(t, LANES - 32, 1), pltpu.roll(t, 32, 1))
    return t * cos + swapped * sin_signed


def _inproj_kernel(use_rope, x_ref, mod_ref, w_ref, wa_ref, ba_ref, *rest):
    if use_rope:
        cos_ref, sin_ref, qk_ref, z_ref, vt_ref, lg_ref = rest
    else:
        qk_ref, z_ref, vt_ref, lg_ref = rest
    shift = mod_ref[0, :, 0:D_MODEL]
    scale1 = 1.0 + mod_ref[0, :, D_MODEL:2 * D_MODEL]

    def gate_logs(lr_t, rows):
        logit = lax.dot_general(lr_t, wa_ref[...], _TN, preferred_element_type=F32) + ba_ref[...]
        log_sig = jnp.minimum(logit, 0.0) - jnp.log(1.0 + jnp.exp(-jnp.abs(logit)))
        lg_ref[0, rows, :] = log_sig * (1.0 / GLA_TAU)

    for sub in range(TOKEN_BLOCK // INPROJ_SUB):
        rows = slice(sub * INPROJ_SUB, (sub + 1) * INPROJ_SUB)
        x = x_ref[0, rows, :]
        inv = lax.rsqrt(jnp.mean(x * x, axis=-1, keepdims=True) + EPS)
        hb = ((x * inv) * scale1 + shift).astype(BF16)

        lr_t = lax.dot_general(w_ref[:, COL_LR:COL_LR + 2 * GLA_RANK], hb, _TNT,
                               preferred_element_type=F32).astype(BF16)

        for half in range(2):
            res = _dot(hb, w_ref[:, COL_QK[half]:COL_QK[half] + 2 * QK])
            if use_rope and half == 0:
                for t in range(4):
                    tile = res[:, t * LANES:(t + 1) * LANES]
                    if t >= 2:
                        tile = tile * (DK ** -0.5)
                    tile = _rope_tile(tile, cos_ref[rows, :], sin_ref[rows, :])
                    qk_ref[0, rows, t * LANES:(t + 1) * LANES] = tile.astype(BF16)
            else:
                lane = lax.broadcasted_iota(jnp.int32, (1, 2 * QK), 1)
                scaled = (lane >= QK) if half == 0 else (lane < QK)
                col_scale = jnp.where(scaled, DK ** -0.5, 1.0)
                qk_ref[0, rows, half * 2 * QK:(half + 1) * 2 * QK] = (res * col_scale).astype(BF16)

        gate_logs(lr_t, rows)

        for half in range(2):
            z_ref[0, rows, half * WV:(half + 1) * WV] = _silu(
                _dot(hb, w_ref[:, COL_Z[half]:COL_Z[half] + WV])).astype(BF16)

        for half in range(2):
            vt = lax.dot_general(w_ref[:, COL_V[half]:COL_V[half] + WV], hb, _TNT,
                                 preferred_element_type=F32)
            for j in range(INPROJ_SUB // SCAN_CHUNK):
                jj = sub * (INPROJ_SUB // SCAN_CHUNK) + j
                vt_ref[0, jj, half * WV:(half + 1) * WV, :] = vt[:, j * SCAN_CHUNK:(j + 1) * SCAN_CHUNK].astype(BF16)


def _inproj(x, mod, mod_row_fn, weights, rope):
    B, T, _ = x.shape
    w, wa, ba = weights
    nt = T // TOKEN_BLOCK
    cpb = TOKEN_BLOCK // SCAN_CHUNK
    const = lambda b, t: (0, 0)
    in_specs = [
        pl.BlockSpec((1, TOKEN_BLOCK, D_MODEL), lambda b, t: (b, t, 0)),
        pl.BlockSpec((1, 1, 3 * D_MODEL), lambda b, t: (mod_row_fn(b), 0, 0)),
        pl.BlockSpec(w.shape, const),
        pl.BlockSpec(wa.shape, const),
        pl.BlockSpec(ba.shape, const),
    ]
    args = [x, mod, w, wa, ba]
    if rope is not None:
        in_specs += [pl.BlockSpec((TOKEN_BLOCK, LANES), lambda b, t: (t, 0))] * 2
        args += list(rope)
    out_shape = [
        jax.ShapeDtypeStruct((B, T, 4 * QK), BF16),
        jax.ShapeDtypeStruct((B, T, 2 * WV), BF16),
        jax.ShapeDtypeStruct((B, T // SCAN_CHUNK, 2 * WV, SCAN_CHUNK), BF16),
        jax.ShapeDtypeStruct((B, T, 2 * QK), F32),
    ]
    out_specs = [
        pl.BlockSpec((1, TOKEN_BLOCK, 4 * QK), lambda b, t: (b, t, 0)),
        pl.BlockSpec((1, TOKEN_BLOCK, 2 * WV), lambda b, t: (b, t, 0)),
        pl.BlockSpec((1, cpb, 2 * WV, SCAN_CHUNK), lambda b, t: (b, t, 0, 0)),
        pl.BlockSpec((1, TOKEN_BLOCK, 2 * QK), lambda b, t: (b, t, 0)),
    ]
    return pl.pallas_call(
        functools.partial(_inproj_kernel, rope is not None),
        grid=(B, nt),
        in_specs=in_specs,
        out_specs=out_specs,
        out_shape=out_shape,
        compiler_params=pltpu.CompilerParams(
            dimension_semantics=("parallel", "parallel"), vmem_limit_bytes=VMEM_LIMIT),
        name="inproj_rope" if rope is not None else "inproj",
    )(*args)


def _scan_kernel(is_gla, has_init, want_state, n_chunks, seq_blk, unroll, *refs):
    refs = list(refs)
    q_ref, k_ref, vt_ref = refs[:3]
    pos = 3
    if is_gla:
        lgf_ref, lgb_ref = refs[pos:pos + 2]
        pos += 2
    else:
        ld_ref = refs[pos]
        pos += 1
    if has_init:
        s0_ref = refs[pos]
        pos += 1
    o_ref = refs[pos]
    pos += 1
    if want_state:
        sn_ref = refs[pos]
        pos += 1
    st_ref, oacc_ref = refs[pos:pos + 2]

    C = SCAN_CHUNK
    mid = C // 2
    n_pairs = N_HEADS // 2
    lane = lax.broadcasted_iota(jnp.int32, (1, LANES), 1)
    head_mask = (lane < DK, lane >= DK)
    ri = lax.broadcasted_iota(jnp.int32, (C, C), 0)
    ci = lax.broadcasted_iota(jnp.int32, (C, C), 1)
    tpos = lax.broadcasted_iota(jnp.int32, (C, 1), 0).astype(F32)
    tri = ((ri >= ci), (ci >= ri))

    if has_init:
        for s in range(seq_blk):
            for p in range(n_pairs):
                for d in range(2):
                    s0 = s0_ref[s, 0, d]
                    both = jnp.concatenate([s0[2 * p], s0[2 * p + 1]], axis=0).T
                    st_ref[s, p, d, 0:DV] = both
                    st_ref[s, p, d, DV:2 * DV] = both
    else:
        st_ref[...] = jnp.zeros(st_ref.shape, F32)

    if is_gla:
        tri_ones = [t.astype(BF16) for t in tri]
    else:
        q_mul, k_mul, ret_decay, score_mul = {}, {}, {}, {}
        for p in range(n_pairs):
            for d in range(2):
                ld0 = ld_ref[d, 2 * p]
                ld1 = ld_ref[d, 2 * p + 1]
                ldr = jnp.where(lane < DK, ld0, ld1)
                if d == 0:
                    q_mul[p, d] = jnp.exp(ldr * (tpos + 1.0))
                    k_mul[p, d] = jnp.exp(ldr * (C - 1.0 - tpos))
                    dist = (ri - ci).astype(F32)
                else:
                    q_mul[p, d] = jnp.exp(ldr * (C - tpos))
                    k_mul[p, d] = jnp.exp(ldr * tpos)
                    dist = (ci - ri).astype(F32)
                ret_decay[p, d] = jnp.exp(ldr * float(C))
                score_mul[p, d] = [jnp.where(tri[d], jnp.exp(ldh * dist), 0.0) for ldh in (ld0, ld1)]
            score_mul[p] = [score_mul[p, 0][hh] + score_mul[p, 1][hh] for hh in range(2)]

    def has_intra(d):
        return is_gla or d == 0

    zero = jnp.zeros((), BF16)
    groups = [(s, p, d) for s in range(seq_blk) for p in range(n_pairs) for d in range(2)]

    def run_block(n0, accumulate):
        keys = [g + (u,) for u in range(unroll) for g in groups]
        chains = {}

        def stage_load(key):
            s, p, d, u = key
            ptile = slice(p * LANES, (p + 1) * LANES)
            n = n0 + u
            c = n if d == 0 else n_chunks - 1 - n
            t = dict(c=c, rows=pl.ds(pl.multiple_of(c * C, C), C))
            t["q"] = q_ref[s, t["rows"], ptile]
            t["k"] = k_ref[s, t["rows"], ptile]
            if is_gla:
                lg = (lgf_ref if d == 0 else lgb_ref)[s, t["rows"], ptile]
                hi = lg.astype(BF16)
                lo = (lg - hi.astype(F32)).astype(BF16)
                bb = _dot(tri_ones[d], jnp.concatenate([hi, lo], axis=1))
                t["b"] = bb[:, :LANES] + bb[:, LANES:]
            chains[key] = t

        def stage_scores(key):
            s, p, d, u = key
            t = chains[key]
            q, k = t.pop("q"), t.pop("k")
            if is_gla:
                b = t.pop("b")
                r = b[mid - 1:mid] if d == 0 else b[mid:mid + 1]
                bl = b[C - 1:C] if d == 0 else b[0:1]
                qs = q.astype(F32) * jnp.exp(b - r)
                ks = k.astype(F32) * jnp.exp(r - b)
                q_sc = qs.astype(BF16)
                k_sc = ks.astype(BF16)
                q_dec = (qs * jnp.exp(r)).astype(BF16)
                t["k_dec"] = (ks * jnp.exp(bl - r)).astype(BF16)
                t["decay"] = jnp.exp(bl)
            else:
                q_sc = q
                k_sc = k
                q_dec = (q.astype(F32) * q_mul[p, d]).astype(BF16)
                t["k_dec"] = (k.astype(F32) * k_mul[p, d]).astype(BF16)
                t["decay"] = ret_decay[p, d]
            t["q_dec"] = [jnp.where(head_mask[hh], q_dec, zero) for hh in range(2)]
            if has_intra(d):
                k_heads = jnp.concatenate([jnp.where(head_mask[hh], k_sc, zero) for hh in range(2)], axis=0)
                t["sc"] = _dot_nt(q_sc, k_heads)

        def stage_kv(key):
            s, p, d, u = key
            t = chains[key]
            t["vt"] = vt_ref[s, t["c"], p * 2 * DV:(p + 1) * 2 * DV, :]
            t["kv"] = _dot(t["vt"], t.pop("k_dec"))
            t["scb"] = []
            if has_intra(d):
                sc_all = t.pop("sc")
                for hh in range(2):
                    sc = sc_all[:, hh * C:(hh + 1) * C]
                    if is_gla:
                        sc = jnp.where(tri[d], sc, 0.0)
                    else:
                        sc = sc * score_mul[p][hh]
                    t["scb"].append(sc.astype(BF16))

        def stage_out(key):
            s, p, d, u = key
            t = chains[key]
            st = st_ref[s, p, d]
            stb = st.astype(BF16)
            t["o"] = []
            for hh in range(2):
                hrows = slice(hh * DV, (hh + 1) * DV)
                if has_intra(d):
                    lhs = jnp.concatenate([t["scb"][hh], t["q_dec"][hh]], axis=1)
                    rhs = jnp.concatenate([t["vt"][hrows], stb[hrows]], axis=1)
                else:
                    lhs, rhs = t["q_dec"][hh], stb[hrows]
                t["o"].append(_dot_nt(lhs, rhs))
            st_ref[s, p, d] = st * t.pop("decay") + t.pop("kv")

        def stage_store(key):
            s, p, d, u = key
            t = chains.pop(key)
            for hh in range(2):
                cols = slice((2 * p + hh) * DV, (2 * p + hh + 1) * DV)
                if accumulate:
                    o_ref[s, t["rows"], cols] = (oacc_ref[s, t["rows"], cols] + t["o"][hh]).astype(BF16)
                else:
                    oacc_ref[s, t["rows"], cols] = t["o"][hh]

        stages = (stage_load, stage_scores, stage_kv, stage_out, stage_store)
        grp = GLA_PIPE_GROUP if is_gla else RET_PIPE_GROUP
        key_groups = [keys[i:i + grp] for i in range(0, len(keys), grp)]
        for step in range(len(key_groups) + len(stages) - 1):
            for k in reversed(range(len(stages))):
                if 0 <= step - k < len(key_groups):
                    for key in key_groups[step - k]:
                        stages[k](key)

    def loop_body(accumulate, base):
        def body(m, carry):
            run_block(base + m * unroll, accumulate)
            return carry
        return body

    half = n_chunks // 2
    assert half % unroll == 0
    lax.fori_loop(0, half // unroll, loop_body(False, 0), 0)
    lax.fori_loop(0, half // unroll, loop_body(True, half), 0)

    if want_state:
        for s, p, d in groups:
            for hh in range(2):
                sn_ref[s, 0, d, 2 * p + hh] = st_ref[s, p, d, hh * DV:(hh + 1) * DV].T[hh * DK:(hh + 1) * DK, :]


def _scan(is_gla, qk, vt, lg, ld, s0, want_state, seq_blk, unroll):
    B, T, _ = qk.shape
    n_chunks = T // SCAN_CHUNK
    assert n_chunks % 2 == 0 and B % seq_blk == 0
    sb = seq_blk
    grp = 1 if is_gla else 0
    in_specs = [
        pl.BlockSpec((sb, T, QK), lambda b: (b, 0, 2 * grp)),
        pl.BlockSpec((sb, T, QK), lambda b: (b, 0, 2 * grp + 1)),
        pl.BlockSpec((sb, n_chunks, WV, SCAN_CHUNK), lambda b: (b, 0, grp, 0)),
    ]
    args = [qk, qk, vt]
    if is_gla:
        in_specs += [
            pl.BlockSpec((sb, T, QK), lambda b: (b, 0, 0)),
            pl.BlockSpec((sb, T, QK), lambda b: (b, 0, 1)),
        ]
        args += [lg, lg]
    else:
        in_specs += [pl.BlockSpec(memory_space=pltpu.SMEM)]
        args += [ld]
    state_spec = pl.BlockSpec((sb, 1, 2, N_HEADS, DK, DV), lambda b: (b, 0, 0, 0, 0, 0))
    if s0 is not None:
        in_specs += [state_spec]
        args += [s0]
    out_shape = [jax.ShapeDtypeStruct((B, T, WV), BF16)]
    out_specs = [pl.BlockSpec((sb, T, WV), lambda b: (b, 0, 0))]
    if want_state:
        out_shape += [jax.ShapeDtypeStruct((B, 1, 2, N_HEADS, DK, DV), F32)]
        out_specs += [state_spec]
    name = ("gla" if is_gla else "ret") + ("_init" if s0 is not None else "") + "_scan"
    return pl.pallas_call(
        functools.partial(_scan_kernel, is_gla, s0 is not None, want_state, n_chunks, sb, unroll),
        grid=(B // sb,),
        in_specs=in_specs,
        out_specs=out_specs,
        out_shape=out_shape,
        scratch_shapes=[pltpu.VMEM((sb, N_HEADS // 2, 2, 2 * DV, LANES), F32),
                        pltpu.VMEM((sb, T, WV), F32)],
        compiler_params=pltpu.CompilerParams(
            dimension_semantics=("arbitrary",), vmem_limit_bytes=VMEM_LIMIT),
        name=name,
    )(*args)


def _outproj_rows(rows, x_ref, mod_ref, or_ref, og_ref, gz_ref, gw_ref, fw_ref, wout_ref, y_ref):
    gate = mod_ref[0, :, 2 * D_MODEL:3 * D_MODEL]
    gw = gw_ref[...]
    parts = []
    for h in range(N_HEADS):
        cols = slice(h * DV, (h + 1) * DV)
        t = or_ref[0, rows, cols].astype(F32)
        mu = jnp.mean(t, axis=-1, keepdims=True)
        dlt = t - mu
        var = jnp.mean(dlt * dlt, axis=-1, keepdims=True)
        n = dlt * lax.rsqrt(var + EPS)
        parts.append((n * gz_ref[0, rows, cols].astype(F32)).astype(BF16))
    for h in range(N_HEADS):
        cols = slice(h * DV, (h + 1) * DV)
        t = og_ref[0, rows, cols].astype(F32)
        n = t * lax.rsqrt(jnp.mean(t * t, axis=-1, keepdims=True) + EPS) * gw
        gcols = slice(WV + h * DV, WV + (h + 1) * DV)
        parts.append((n * gz_ref[0, rows, gcols].astype(F32)).astype(BF16))
    mixed = jnp.concatenate(parts, axis=-1)
    out = _dot(mixed, wout_ref[...])
    y = x_ref[0, rows, :] + gate * out
    yn = y * lax.rsqrt(jnp.mean(y * y, axis=-1, keepdims=True) + EPS)
    y_ref[0, rows, :] = yn * fw_ref[...]


def _outproj_kernel(*refs):
    for sub in range(OUT_TOKEN_BLOCK // OUTPROJ_SUB):
        _outproj_rows(slice(sub * OUTPROJ_SUB, (sub + 1) * OUTPROJ_SUB), *refs)


def _outproj(x, mod, mod_row_fn, o_r, o_g, gz, gw, fw, wout):
    n_tok = x.shape[0] * x.shape[1]
    B = n_tok // OUT_TOKEN_BLOCK
    x, o_r, o_g, gz = (a.reshape(B, OUT_TOKEN_BLOCK, a.shape[-1]) for a in (x, o_r, o_g, gz))
    const = lambda b: (0, 0)
    blk = lambda width: pl.BlockSpec((1, OUT_TOKEN_BLOCK, width), lambda b: (b, 0, 0))
    return pl.pallas_call(
        _outproj_kernel,
        grid=(B,),
        in_specs=[
            blk(D_MODEL),
            pl.BlockSpec((1, 1, 3 * D_MODEL), lambda b: (mod_row_fn(b), 0, 0)),
            blk(WV),
            blk(WV),
            blk(2 * WV),
            pl.BlockSpec((1, DV), const),
            pl.BlockSpec((1, D_MODEL), const),
            pl.BlockSpec(wout.shape, const),
        ],
        out_specs=blk(D_MODEL),
        out_shape=jax.ShapeDtypeStruct((B, OUT_TOKEN_BLOCK, D_MODEL), F32),
        compiler_params=pltpu.CompilerParams(
            dimension_semantics=("parallel",), vmem_limit_bytes=VMEM_LIMIT),
        name="outproj",
    )(x, mod, o_r, o_g, gz, gw, fw, wout)


def _rope_tables(n_tokens):
    rows = n_tokens // GRID_W
    rr, cc = np.meshgrid(np.arange(rows), np.arange(GRID_W), indexing="ij")
    rr = rr.reshape(-1).astype(np.float32)
    cc = cc.reshape(-1).astype(np.float32)
    n_freq = DK // 4
    inv = np.float32(ROPE_BASE) ** (-np.arange(n_freq, dtype=np.float32) / np.float32(n_freq))
    ang = np.concatenate([rr[:, None] * inv, cc[:, None] * inv], axis=-1)
    cos = np.cos(ang.astype(np.float64)).astype(np.float32)
    sin = np.sin(ang.astype(np.float64)).astype(np.float32)
    cos_t = np.tile(np.concatenate([cos, cos], axis=-1), (1, LANES // DK))
    sin_t = np.tile(np.concatenate([-sin, sin], axis=-1), (1, LANES // DK))
    return jnp.asarray(cos_t), jnp.asarray(sin_t)


def _prep_weights(w_in, gla_w_alpha, gla_b_alpha):
    assert w_in.shape == (D_MODEL, COL_LR + 2 * GLA_RANK)
    wa = jnp.zeros((2 * GLA_RANK, 2 * QK), F32)
    wa = wa.at[0:GLA_RANK, 0:QK].set(gla_w_alpha[0])
    wa = wa.at[GLA_RANK:2 * GLA_RANK, QK:2 * QK].set(gla_w_alpha[1])
    ba = jnp.concatenate([gla_b_alpha[0], gla_b_alpha[1]]).reshape(1, 2 * QK)
    return _to_bf16(w_in), wa.astype(BF16), ba


def kernel(x_prompt, x_sample, c, state_ret, state_gla, c_ctx, w_mod, b_mod, w_in, ret_log_decay,
           gla_w_alpha, gla_b_alpha, gla_norm_w, w_out, final_norm_w):
    assert w_mod.shape[0] == 1, "single-layer model"
    b_ctx, t_ctx, _ = x_prompt.shape
    b_dec, t_dec, _ = x_sample.shape
    assert t_ctx % SCAN_CHUNK == 0 and t_dec % TOKEN_BLOCK == 0 and TOKEN_BLOCK % t_ctx == 0
    assert 1 + b_dec <= MOD_ROWS

    cond = jnp.concatenate(
        [c_ctx[None, :], c, jnp.zeros((MOD_ROWS - 1 - b_dec, D_MODEL), F32)], axis=0)
    mod = _modulation(cond, w_mod[0], b_mod[0]).reshape(MOD_ROWS, 1, 3 * D_MODEL)

    weights = _prep_weights(w_in[0], gla_w_alpha[0], gla_b_alpha[0])
    wout = _to_bf16(w_out[0])
    gw = gla_norm_w[0].reshape(1, DV)
    fw = final_norm_w.reshape(1, D_MODEL)
    ld = ret_log_decay[0]

    per_blk = TOKEN_BLOCK // t_ctx
    xp = x_prompt.reshape(b_ctx // per_blk, TOKEN_BLOCK, D_MODEL)
    ctx_row = lambda b: 0
    qk, gz, vt, lg = _inproj(xp, mod, ctx_row, weights, None)
    qk = qk.reshape(b_ctx, t_ctx, 4 * QK)
    vt = vt.reshape(b_ctx, t_ctx // SCAN_CHUNK, 2 * WV, SCAN_CHUNK)
    lg = lg.reshape(b_ctx, t_ctx, 2 * QK)
    o_r, new_ret = _scan(False, qk, vt, None, ld, None, True, CTX_SEQ_BLOCK, 1)
    o_g, new_gla = _scan(True, qk, vt, lg, None, None, True, CTX_SEQ_BLOCK, 1)
    y_prompt = _outproj(xp, mod, ctx_row, o_r, o_g, gz, gw, fw, wout)
    y_prompt = y_prompt.reshape(b_ctx, t_ctx, D_MODEL)

    dec_row = lambda b: b + 1
    rope = _rope_tables(t_dec)
    qk, gz, vt, lg = _inproj(x_sample, mod, dec_row, weights, rope)
    (o_r,) = _scan(False, qk, vt, None, ld, state_ret, False, 1, DEC_UNROLL)
    (o_g,) = _scan(True, qk, vt, lg, None, state_gla, False, 1, DEC_UNROLL)
    per_seq = t_dec // OUT_TOKEN_BLOCK
    y_sample = _outproj(x_sample, mod, lambda b: b // per_seq + 1, o_r, o_g, gz, gw, fw, wout)
    y_sample = y_sample.reshape(b_dec, t_dec, D_MODEL)

    return (y_prompt, y_sample, new_ret, new_gla)
```

```python
import functools

import jax
import jax.numpy as jnp
import numpy as np
from jax import lax
from jax.experimental import pallas as pl
from jax.experimental.pallas import tpu as pltpu

F32 = jnp.float32
BF16 = jnp.bfloat16

D_MODEL = 1024
N_HEADS = 4
DK = 64
DV = 128
QK = N_HEADS * DK
WV = N_HEADS * DV
GLA_RANK = 16
GLA_TAU = 16.0
EPS = 1e-6
GRID_W = 64
ROPE_BASE = 10000.0

COL_QK = (0, 2 * QK + 2 * WV)
COL_V = (2 * QK, 4 * QK + 2 * WV)
COL_Z = (2 * QK + WV, 4 * QK + 3 * WV)
COL_LR = 4 * QK + 4 * WV

LANES = 128
SCAN_CHUNK = 128
CTX_SEQ_BLOCK = 8
DEC_UNROLL = 8
RET_PIPE_GROUP = 2
GLA_PIPE_GROUP = 8
TOKEN_BLOCK = 1024
INPROJ_SUB = 256
OUTPROJ_SUB = 256
MOD_ROWS = 16
MOD_COL_BLOCK = 512
OUT_TOKEN_BLOCK = 1024
VMEM_LIMIT = 48 * 1024 * 1024

_NT = (((1,), (1,)), ((), ()))
_TN = (((0,), (0,)), ((), ()))
_TNT = (((0,), (1,)), ((), ()))


def _dot(a, b):
    return jnp.dot(a, b, preferred_element_type=F32)


def _dot_nt(a, b):
    return lax.dot_general(a, b, _NT, preferred_element_type=F32)


def _silu(x):
    return x * jax.nn.sigmoid(x)


def _mod_kernel(c_ref, w_ref, b_ref, o_ref):
    s = _silu(c_ref[...]).astype(BF16)
    o_ref[...] = _dot(s, w_ref[...].astype(BF16)) + b_ref[...]


def _modulation(cond, w_mod, b_mod):
    n_col = w_mod.shape[1]
    return pl.pallas_call(
        _mod_kernel,
        grid=(n_col // MOD_COL_BLOCK,),
        in_specs=[
            pl.BlockSpec((MOD_ROWS, D_MODEL), lambda j: (0, 0)),
            pl.BlockSpec((D_MODEL, MOD_COL_BLOCK), lambda j: (0, j)),
            pl.BlockSpec((1, MOD_COL_BLOCK), lambda j: (0, j)),
        ],
        out_specs=pl.BlockSpec((MOD_ROWS, MOD_COL_BLOCK), lambda j: (0, j)),
        out_shape=jax.ShapeDtypeStruct((MOD_ROWS, n_col), F32),
        compiler_params=pltpu.CompilerParams(vmem_limit_bytes=VMEM_LIMIT),
        name="modulation",
    )(cond, w_mod, b_mod.reshape(1, n_col))


def _rope_tile(t, cos, sin_signed):
    lane = lax.broadcasted_iota(jnp.int32, t.shape, 1)
    first_half = (lane & 32) == 0
    swapped = jnp.where(first_half, pltpu.roll(t, LANES - 32, 1), pltpu.roll(t, 32, 1))
    return t * cos + swapped * sin_signed


def _inproj_kernel(use_rope, x_ref, mod_ref, w_ref, wa_ref, ba_ref, *rest):
    if use_rope:
        cos_ref, sin_ref, qk_ref, z_ref, vt_ref, lg_ref = rest
    else:
        qk_ref, z_ref, vt_ref, lg_ref = rest
    shift = mod_ref[0, :, 0:D_MODEL]
    scale1 = 1.0 + mod_ref[0, :, D_MODEL:2 * D_MODEL]

    def gate_logs(lr_t, rows):
        logit = lax.dot_general(lr_t, wa_ref[...], _TN, preferred_element_type=F32) + ba_ref[...]
        log_sig = jnp.minimum(logit, 0.0) - jnp.log(1.0 + jnp.exp(-jnp.abs(logit)))
        lg_ref[0, rows, :] = log_sig * (1.0 / GLA_TAU)

    for sub in range(TOKEN_BLOCK // INPROJ_SUB):
        rows = slice(sub * INPROJ_SUB, (sub + 1) * INPROJ_SUB)
        x = x_ref[0, rows, :]
        inv = lax.rsqrt(jnp.mean(x * x, axis=-1, keepdims=True) + EPS)
        hb = ((x * inv) * scale1 + shift).astype(BF16)

        lr_t = lax.dot_general(w_ref[:, COL_LR:COL_LR + 2 * GLA_RANK], hb, _TNT,
                               preferred_element_type=F32).astype(BF16)

        for half in range(2):
            res = _dot(hb, w_ref[:, COL_QK[half]:COL_QK[half] + 2 * QK])
            if use_rope and half == 0:
                for t in range(4):
                    tile = res[:, t * LANES:(t + 1) * LANES]
                    if t >= 2:
                        tile = tile * (DK ** -0.5)
                    tile = _rope_tile(tile, cos_ref[rows, :], sin_ref[rows, :])
                    qk_ref[0, rows, t * LANES:(t + 1) * LANES] = tile.astype(BF16)
            else:
                lane = lax.broadcasted_iota(jnp.int32, (1, 2 * QK), 1)
                scaled = (lane >= QK) if half == 0 else (lane < QK)
                col_scale = jnp.where(scaled, DK ** -0.5, 1.0)
                qk_ref[0, rows, half * 2 * QK:(half + 1) * 2 * QK] = (res * col_scale).astype(BF16)

        gate_logs(lr_t, rows)

        for half in range(2):
            z_ref[0, rows, half * WV:(half + 1) * WV] = _silu(
                _dot(hb, w_ref[:, COL_Z[half]:COL_Z[half] + WV])).astype(BF16)

        for half in range(2):
            vt = lax.dot_general(w_ref[:, COL_V[half]:COL_V[half] + WV], hb, _TNT,
                                 preferred_element_type=F32)
            for j in range(INPROJ_SUB // SCAN_CHUNK):
                jj = sub * (INPROJ_SUB // SCAN_CHUNK) + j
                vt_ref[0, jj, half * WV:(half + 1) * WV, :] = vt[:, j * SCAN_CHUNK:(j + 1) * SCAN_CHUNK].astype(BF16)


def _inproj(x, mod, mod_row_fn, weights, rope):
    B, T, _ = x.shape
    w, wa, ba = weights
    nt = T // TOKEN_BLOCK
    cpb = TOKEN_BLOCK // SCAN_CHUNK
    const = lambda b, t: (0, 0)
    in_specs = [
        pl.BlockSpec((1, TOKEN_BLOCK, D_MODEL), lambda b, t: (b, t, 0)),
        pl.BlockSpec((1, 1, 3 * D_MODEL), lambda b, t: (mod_row_fn(b), 0, 0)),
        pl.BlockSpec(w.shape, const),
        pl.BlockSpec(wa.shape, const),
        pl.BlockSpec(ba.shape, const),
    ]
    args = [x, mod, w, wa, ba]
    if rope is not None:
        in_specs += [pl.BlockSpec((TOKEN_BLOCK, LANES), lambda b, t: (t, 0))] * 2
        args += list(rope)
    out_shape = [
        jax.ShapeDtypeStruct((B, T, 4 * QK), BF16),
        jax.ShapeDtypeStruct((B, T, 2 * WV), BF16),
        jax.ShapeDtypeStruct((B, T // SCAN_CHUNK, 2 * WV, SCAN_CHUNK), BF16),
        jax.ShapeDtypeStruct((B, T, 2 * QK), F32),
    ]
    out_specs = [
        pl.BlockSpec((1, TOKEN_BLOCK, 4 * QK), lambda b, t: (b, t, 0)),
        pl.BlockSpec((1, TOKEN_BLOCK, 2 * WV), lambda b, t: (b, t, 0)),
        pl.BlockSpec((1, cpb, 2 * WV, SCAN_CHUNK), lambda b, t: (b, t, 0, 0)),
        pl.BlockSpec((1, TOKEN_BLOCK, 2 * QK), lambda b, t: (b, t, 0)),
    ]
    return pl.pallas_call(
        functools.partial(_inproj_kernel, rope is not None),
        grid=(B, nt),
        in_specs=in_specs,
        out_specs=out_specs,
        out_shape=out_shape,
        compiler_params=pltpu.CompilerParams(
            dimension_semantics=("parallel", "parallel"), vmem_limit_bytes=VMEM_LIMIT),
        name="inproj_rope" if rope is not None else "inproj",
    )(*args)


def _scan_kernel(is_gla, has_init, want_state, n_chunks, seq_blk, unroll, *refs):
    refs = list(refs)
    q_ref, k_ref, vt_ref = refs[:3]
    pos = 3
    if is_gla:
        lgf_ref, lgb_ref = refs[pos:pos + 2]
        pos += 2
    else:
        ld_ref = refs[pos]
        pos += 1
    if has_init:
        s0_ref = refs[pos]
        pos += 1
    o_ref = refs[pos]
    pos += 1
    if want_state:
        sn_ref = refs[pos]
        pos += 1
    st_ref, oacc_ref = refs[pos:pos + 2]

    C = SCAN_CHUNK
    mid = C // 2
    n_pairs = N_HEADS // 2
    lane = lax.broadcasted_iota(jnp.int32, (1, LANES), 1)
    head_mask = (lane < DK, lane >= DK)
    ri = lax.broadcasted_iota(jnp.int32, (C, C), 0)
    ci = lax.broadcasted_iota(jnp.int32, (C, C), 1)
    tpos = lax.broadcasted_iota(jnp.int32, (C, 1), 0).astype(F32)
    tri = ((ri >= ci), (ci >= ri))

    if has_init:
        for s in range(seq_blk):
            for p in range(n_pairs):
                for d in range(2):
                    s0 = s0_ref[s, 0, d]
                    both = jnp.concatenate([s0[2 * p], s0[2 * p + 1]], axis=0).T
                    st_ref[s, p, d, 0:DV] = both
                    st_ref[s, p, d, DV:2 * DV] = both
    else:
        st_ref[...] = jnp.zeros(st_ref.shape, F32)

    if is_gla:
        tri_ones = [t.astype(BF16) for t in tri]
    else:
        q_mul, k_mul, ret_decay, score_mul = {}, {}, {}, {}
        for p in range(n_pairs):
            for d in range(2):
                ld0 = ld_ref[d, 2 * p]
                ld1 = ld_ref[d, 2 * p + 1]
                ldr = jnp.where(lane < DK, ld0, ld1)
                if d == 0:
                    q_mul[p, d] = jnp.exp(ldr * (tpos + 1.0))
                    k_mul[p, d] = jnp.exp(ldr * (C - 1.0 - tpos))
                    dist = (ri - ci).astype(F32)
                else:
                    q_mul[p, d] = jnp.exp(ldr * (C - tpos))
                    k_mul[p, d] = jnp.exp(ldr * tpos)
                    dist = (ci - ri).astype(F32)
                ret_decay[p, d] = jnp.exp(ldr * float(C))
                score_mul[p, d] = [jnp.where(tri[d], jnp.exp(ldh * dist), 0.0) for ldh in (ld0, ld1)]
            score_mul[p] = [score_mul[p, 0][hh] + score_mul[p, 1][hh] for hh in range(2)]

    def has_intra(d):
        return is_gla or d == 0

    zero = jnp.zeros((), BF16)
    groups = [(s, p, d) for s in range(seq_blk) for p in range(n_pairs) for d in range(2)]

    def run_block(n0, accumulate):
        keys = [g + (u,) for u in range(unroll) for g in groups]
        chains = {}

        def stage_load(key):
            s, p, d, u = key
            ptile = slice(p * LANES, (p + 1) * LANES)
            n = n0 + u
            c = n if d == 0 else n_chunks - 1 - n
            t = dict(c=c, rows=pl.ds(pl.multiple_of(c * C, C), C))
            t["q"] = q_ref[s, t["rows"], ptile]
            t["k"] = k_ref[s, t["rows"], ptile]
            if is_gla:
                lg = (lgf_ref if d == 0 else lgb_ref)[s, t["rows"], ptile]
                hi = lg.astype(BF16)
                lo = (lg - hi.astype(F32)).astype(BF16)
                bb = _dot(tri_ones[d], jnp.concatenate([hi, lo], axis=1))
                t["b"] = bb[:, :LANES] + bb[:, LANES:]
            chains[key] = t

        def stage_scores(key):
            s, p, d, u = key
            t = chains[key]
            q, k = t.pop("q"), t.pop("k")
            if is_gla:
                b = t.pop("b")
                r = b[mid - 1:mid] if d == 0 else b[mid:mid + 1]
                bl = b[C - 1:C] if d == 0 else b[0:1]
                qs = q.astype(F32) * jnp.exp(b - r)
                ks = k.astype(F32) * jnp.exp(r - b)
                q_sc = qs.astype(BF16)
                k_sc = ks.astype(BF16)
                q_dec = (qs * jnp.exp(r)).astype(BF16)
                t["k_dec"] = (ks * jnp.exp(bl - r)).astype(BF16)
                t["decay"] = jnp.exp(bl)
            else:
                q_sc = q
                k_sc = k
                q_dec = (q.astype(F32) * q_mul[p, d]).astype(BF16)
                t["k_dec"] = (k.astype(F32) * k_mul[p, d]).astype(BF16)
                t["decay"] = ret_decay[p, d]
            t["q_dec"] = [jnp.where(head_mask[hh], q_dec, zero) for hh in range(2)]
            if has_intra(d):
                k_heads = jnp.concatenate([jnp.where(head_mask[hh], k_sc, zero) for hh in range(2)], axis=0)
                t["sc"] = _dot_nt(q_sc, k_heads)

        def stage_kv(key):
            s, p, d, u = key
            t = chains[key]
            t["vt"] = vt_ref[s, t["c"], p * 2 * DV:(p + 1) * 2 * DV, :]
            t["kv"] = _dot(t["vt"], t.pop("k_dec"))
            t["scb"] = []
            if has_intra(d):
                sc_all = t.pop("sc")
                for hh in range(2):
                    sc = sc_all[:, hh * C:(hh + 1) * C]
                    if is_gla:
                        sc = jnp.where(tri[d], sc, 0.0)
                    else:
                        sc = sc * score_mul[p][hh]
                    t["scb"].append(sc.astype(BF16))

        def stage_out(key):
            s, p, d, u = key
            t = chains[key]
            st = st_ref[s, p, d]
            stb = st.astype(BF16)
            t["o"] = []
            for hh in range(2):
                hrows = slice(hh * DV, (hh + 1) * DV)
                if has_intra(d):
                    lhs = jnp.concatenate([t["scb"][hh], t["q_dec"][hh]], axis=1)
                    rhs = jnp.concatenate([t["vt"][hrows], stb[hrows]], axis=1)
                else:
                    lhs, rhs = t["q_dec"][hh], stb[hrows]
                t["o"].append(_dot_nt(lhs, rhs))
            st_ref[s, p, d] = st * t.pop("decay") + t.pop("kv")

        def stage_store(key):
            s, p, d, u = key
            t = chains.pop(key)
            for hh in range(2):
                cols = slice((2 * p + hh) * DV, (2 * p + hh + 1) * DV)
                if accumulate:
                    o_ref[s, t["rows"], cols] = (oacc_ref[s, t["rows"], cols] + t["o"][hh]).astype(BF16)
                else:
                    oacc_ref[s, t["rows"], cols] = t["o"][hh]

        stages = (stage_load, stage_scores, stage_kv, stage_out, stage_store)
        grp = GLA_PIPE_GROUP if is_gla else RET_PIPE_GROUP
        key_groups = [keys[i:i + grp] for i in range(0, len(keys), grp)]
        for step in range(len(key_groups) + len(stages) - 1):
            for k in reversed(range(len(stages))):
                if 0 <= step - k < len(key_groups):
                    for key in key_groups[step - k]:
                        stages[k](key)

    def loop_body(accumulate, base):
        def body(m, carry):
            run_block(base + m * unroll, accumulate)
            return carry
        return body

    half = n_chunks // 2
    assert half % unroll == 0
    lax.fori_loop(0, half // unroll, loop_body(False, 0), 0)
    lax.fori_loop(0, half // unroll, loop_body(True, half), 0)

    if want_state:
        for s, p, d in groups:
            for hh in range(2):
                sn_ref[s, 0, d, 2 * p + hh] = st_ref[s, p, d, hh * DV:(hh + 1) * DV].T[hh * DK:(hh + 1) * DK, :]


def _scan(is_gla, qk, vt, lg, ld, s0, want_state, seq_blk, unroll):
    B, T, _ = qk.shape
    n_chunks = T // SCAN_CHUNK
    assert n_chunks % 2 == 0 and B % seq_blk == 0
    sb = seq_blk
    grp = 1 if is_gla else 0
    in_specs = [
        pl.BlockSpec((sb, T, QK), lambda b: (b, 0, 2 * grp)),
        pl.BlockSpec((sb, T, QK), lambda b: (b, 0, 2 * grp + 1)),
        pl.BlockSpec((sb, n_chunks, WV, SCAN_CHUNK), lambda b: (b, 0, grp, 0)),
    ]
    args = [qk, qk, vt]
    if is_gla:
        in_specs += [
            pl.BlockSpec((sb, T, QK), lambda b: (b, 0, 0)),
            pl.BlockSpec((sb, T, QK), lambda b: (b, 0, 1)),
        ]
        args += [lg, lg]
    else:
        in_specs += [pl.BlockSpec(memory_space=pltpu.SMEM)]
        args += [ld]
    state_spec = pl.BlockSpec((sb, 1, 2, N_HEADS, DK, DV), lambda b: (b, 0, 0, 0, 0, 0))
    if s0 is not None:
        in_specs += [state_spec]
        args += [s0]
    out_shape = [jax.ShapeDtypeStruct((B, T, WV), BF16)]
    out_specs = [pl.BlockSpec((sb, T, WV), lambda b: (b, 0, 0))]
    if want_state:
        out_shape += [jax.ShapeDtypeStruct((B, 1, 2, N_HEADS, DK, DV), F32)]
        out_specs += [state_spec]
    name = ("gla" if is_gla else "ret") + ("_init" if s0 is not None else "") + "_scan"
    return pl.pallas_call(
        functools.partial(_scan_kernel, is_gla, s0 is not None, want_state, n_chunks, sb, unroll),
        grid=(B // sb,),
        in_specs=in_specs,
        out_specs=out_specs,
        out_shape=out_shape,
        scratch_shapes=[pltpu.VMEM((sb, N_HEADS // 2, 2, 2 * DV, LANES), F32),
                        pltpu.VMEM((sb, T, WV), F32)],
        compiler_params=pltpu.CompilerParams(
            dimension_semantics=("arbitrary",), vmem_limit_bytes=VMEM_LIMIT),
        name=name,
    )(*args)


def _outproj_rows(rows, x_ref, mod_ref, or_ref, og_ref, gz_ref, gw_ref, fw_ref, wout_ref, y_ref):
    gate = mod_ref[0, :, 2 * D_MODEL:3 * D_MODEL]
    gw = gw_ref[...]
    parts = []
    for h in range(N_HEADS):
        cols = slice(h * DV, (h + 1) * DV)
        t = or_ref[0, rows, cols].astype(F32)
        mu = jnp.mean(t, axis=-1, keepdims=True)
        dlt = t - mu
        var = jnp.mean(dlt * dlt, axis=-1, keepdims=True)
        n = dlt * lax.rsqrt(var + EPS)
        parts.append((n * gz_ref[0, rows, cols].astype(F32)).astype(BF16))
    for h in range(N_HEADS):
        cols = slice(h * DV, (h + 1) * DV)
        t = og_ref[0, rows, cols].astype(F32)
        n = t * lax.rsqrt(jnp.mean(t * t, axis=-1, keepdims=True) + EPS) * gw
        gcols = slice(WV + h * DV, WV + (h + 1) * DV)
        parts.append((n * gz_ref[0, rows, gcols].astype(F32)).astype(BF16))
    mixed = jnp.concatenate(parts, axis=-1)
    out = _dot(mixed, wout_ref[...])
    y = x_ref[0, rows, :] + gate * out
    yn = y * lax.rsqrt(jnp.mean(y * y, axis=-1, keepdims=True) + EPS)
    y_ref[0, rows, :] = yn * fw_ref[...]


def _outproj_kernel(*refs):
    for sub in range(OUT_TOKEN_BLOCK // OUTPROJ_SUB):
        _outproj_rows(slice(sub * OUTPROJ_SUB, (sub + 1) * OUTPROJ_SUB), *refs)


def _outproj(x, mod, mod_row_fn, o_r, o_g, gz, gw, fw, wout):
    n_tok = x.shape[0] * x.shape[1]
    B = n_tok // OUT_TOKEN_BLOCK
    x, o_r, o_g, gz = (a.reshape(B, OUT_TOKEN_BLOCK, a.shape[-1]) for a in (x, o_r, o_g, gz))
    const = lambda b: (0, 0)
    blk = lambda width: pl.BlockSpec((1, OUT_TOKEN_BLOCK, width), lambda b: (b, 0, 0))
    return pl.pallas_call(
        _outproj_kernel,
        grid=(B,),
        in_specs=[
            blk(D_MODEL),
            pl.BlockSpec((1, 1, 3 * D_MODEL), lambda b: (mod_row_fn(b), 0, 0)),
            blk(WV),
            blk(WV),
            blk(2 * WV),
            pl.BlockSpec((1, DV), const),
            pl.BlockSpec((1, D_MODEL), const),
            pl.BlockSpec(wout.shape, const),
        ],
        out_specs=blk(D_MODEL),
        out_shape=jax.ShapeDtypeStruct((B, OUT_TOKEN_BLOCK, D_MODEL), F32),
        compiler_params=pltpu.CompilerParams(
            dimension_semantics=("parallel",), vmem_limit_bytes=VMEM_LIMIT),
        name="outproj",
    )(x, mod, o_r, o_g, gz, gw, fw, wout)


def _rope_tables(n_tokens):
    rows = n_tokens // GRID_W
    rr, cc = np.meshgrid(np.arange(rows), np.arange(GRID_W), indexing="ij")
    rr = rr.reshape(-1).astype(np.float32)
    cc = cc.reshape(-1).astype(np.float32)
    n_freq = DK // 4
    inv = np.float32(ROPE_BASE) ** (-np.arange(n_freq, dtype=np.float32) / np.float32(n_freq))
    ang = np.concatenate([rr[:, None] * inv, cc[:, None] * inv], axis=-1)
    cos = np.cos(ang.astype(np.float64)).astype(np.float32)
    sin = np.sin(ang.astype(np.float64)).astype(np.float32)
    cos_t = np.tile(np.concatenate([cos, cos], axis=-1), (1, LANES // DK))
    sin_t = np.tile(np.concatenate([-sin, sin], axis=-1), (1, LANES // DK))
    return jnp.asarray(cos_t), jnp.asarray(sin_t)


def _prep_weights(w_in, gla_w_alpha, gla_b_alpha):
    assert w_in.shape == (D_MODEL, COL_LR + 2 * GLA_RANK)
    wa = jnp.zeros((2 * GLA_RANK, 2 * QK), F32)
    wa = wa.at[0:GLA_RANK, 0:QK].set(gla_w_alpha[0])
    wa = wa.at[GLA_RANK:2 * GLA_RANK, QK:2 * QK].set(gla_w_alpha[1])
    ba = jnp.concatenate([gla_b_alpha[0], gla_b_alpha[1]]).reshape(1, 2 * QK)
    return w_in.astype(BF16), wa.astype(BF16), ba


def kernel(x_prompt, x_sample, c, state_ret, state_gla, c_ctx, w_mod, b_mod, w_in, ret_log_decay,
           gla_w_alpha, gla_b_alpha, gla_norm_w, w_out, final_norm_w):
    assert w_mod.shape[0] == 1, "single-layer model"
    b_ctx, t_ctx, _ = x_prompt.shape
    b_dec, t_dec, _ = x_sample.shape
    assert t_ctx % SCAN_CHUNK == 0 and t_dec % TOKEN_BLOCK == 0 and TOKEN_BLOCK % t_ctx == 0
    assert 1 + b_dec <= MOD_ROWS

    cond = jnp.concatenate(
        [c_ctx[None, :], c, jnp.zeros((MOD_ROWS - 1 - b_dec, D_MODEL), F32)], axis=0)
    mod = _modulation(cond, w_mod[0], b_mod[0]).reshape(MOD_ROWS, 1, 3 * D_MODEL)

    weights = _prep_weights(w_in[0], gla_w_alpha[0], gla_b_alpha[0])
    wout = w_out[0].astype(BF16)
    gw = gla_norm_w[0].reshape(1, DV)
    fw = final_norm_w.reshape(1, D_MODEL)
    ld = ret_log_decay[0]

    per_blk = TOKEN_BLOCK // t_ctx
    xp = x_prompt.reshape(b_ctx // per_blk, TOKEN_BLOCK, D_MODEL)
    ctx_row = lambda b: 0
    qk, gz, vt, lg = _inproj(xp, mod, ctx_row, weights, None)
    qk = qk.reshape(b_ctx, t_ctx, 4 * QK)
    vt = vt.reshape(b_ctx, t_ctx // SCAN_CHUNK, 2 * WV, SCAN_CHUNK)
    lg = lg.reshape(b_ctx, t_ctx, 2 * QK)
    o_r, new_ret = _scan(False, qk, vt, None, ld, None, True, CTX_SEQ_BLOCK, 1)
    o_g, new_gla = _scan(True, qk, vt, lg, None, None, True, CTX_SEQ_BLOCK, 1)
    y_prompt = _outproj(xp, mod, ctx_row, o_r, o_g, gz, gw, fw, wout)
    y_prompt = y_prompt.reshape(b_ctx, t_ctx, D_MODEL)

    dec_row = lambda b: b + 1
    rope = _rope_tables(t_dec)
    qk, gz, vt, lg = _inproj(x_sample, mod, dec_row, weights, rope)
    (o_r,) = _scan(False, qk, vt, None, ld, state_ret, False, 1, DEC_UNROLL)
    (o_g,) = _scan(True, qk, vt, lg, None, state_gla, False, 1, DEC_UNROLL)
    per_seq = t_dec // OUT_TOKEN_BLOCK
    y_sample = _outproj(x_sample, mod, lambda b: b // per_seq + 1, o_r, o_g, gz, gw, fw, wout)
    y_sample = y_sample.reshape(b_dec, t_dec, D_MODEL)

    return (y_prompt, y_sample, new_ret, new_gla)
```

```python
import functools

import jax
import jax.numpy as jnp
import numpy as np
from jax import lax
from jax.experimental import pallas as pl
from jax.experimental.pallas import tpu as pltpu

F32 = jnp.float32
BF16 = jnp.bfloat16

D_MODEL = 1024
N_HEADS = 4
DK = 64
DV = 128
QK = N_HEADS * DK
WV = N_HEADS * DV
GLA_RANK = 16
GLA_TAU = 16.0
EPS = 1e-6
GRID_W = 64
ROPE_BASE = 10000.0

COL_QK = (0, 2 * QK + 2 * WV)
COL_V = (2 * QK, 4 * QK + 2 * WV)
COL_Z = (2 * QK + WV, 4 * QK + 3 * WV)
COL_LR = 4 * QK + 4 * WV

LANES = 128
SCAN_CHUNK = 128
CTX_SEQ_BLOCK = 8
DEC_UNROLL = 8
RET_PIPE_GROUP = 2
GLA_PIPE_GROUP = 8
TOKEN_BLOCK = 1024
INPROJ_SUB = 256
OUTPROJ_SUB = 256
MOD_ROWS = 16
MOD_COL_BLOCK = 1024
OUT_TOKEN_BLOCK = 1024
V7X_VMEM_BYTES = 64 * 1024 * 1024
VMEM_LIMIT = 3 * V7X_VMEM_BYTES // 4

_NT = (((1,), (1,)), ((), ()))
_TN = (((0,), (0,)), ((), ()))
_TNT = (((0,), (1,)), ((), ()))


def _dot(a, b):
    return jnp.dot(a, b, preferred_element_type=F32)


def _dot_nt(a, b):
    return lax.dot_general(a, b, _NT, preferred_element_type=F32)


def _silu(x):
    return x * jax.nn.sigmoid(x)


def _mod_kernel(c_ref, w_ref, b_ref, o_ref):
    s = _silu(c_ref[...]).astype(BF16)
    o_ref[...] = _dot(s, w_ref[...].astype(BF16)) + b_ref[...]


def _modulation(cond, w_mod, b_mod):
    n_col = w_mod.shape[1]
    return pl.pallas_call(
        _mod_kernel,
        grid=(n_col // MOD_COL_BLOCK,),
        in_specs=[
            pl.BlockSpec((MOD_ROWS, D_MODEL), lambda j: (0, 0)),
            pl.BlockSpec((D_MODEL, MOD_COL_BLOCK), lambda j: (0, j)),
            pl.BlockSpec((1, MOD_COL_BLOCK), lambda j: (0, j)),
        ],
        out_specs=pl.BlockSpec((MOD_ROWS, MOD_COL_BLOCK), lambda j: (0, j)),
        out_shape=jax.ShapeDtypeStruct((MOD_ROWS, n_col), F32),
        compiler_params=pltpu.CompilerParams(vmem_limit_bytes=VMEM_LIMIT),
        name="modulation",
    )(cond, w_mod, b_mod.reshape(1, n_col))


def _rope_tile(t, cos, sin_signed):
    lane = lax.broadcasted_iota(jnp.int32, t.shape, 1)
    first_half = (lane & 32) == 0
    swapped = jnp.where(first_half, pltpu.roll(t, LANES - 32, 1), pltpu.roll(t, 32, 1))
    return t * cos + swapped * sin_signed


def _inproj_kernel(use_rope, x_ref, mod_ref, w_ref, wa_ref, ba_ref, *rest):
    if use_rope:
        cos_ref, sin_ref, qk_ref, z_ref, vt_ref, lg_ref = rest
    else:
        qk_ref, z_ref, vt_ref, lg_ref = rest
    shift = mod_ref[0, :, 0:D_MODEL]
    scale1 = 1.0 + mod_ref[0, :, D_MODEL:2 * D_MODEL]

    def gate_logs(lr_t, rows):
        logit = lax.dot_general(lr_t, wa_ref[...], _TN, preferred_element_type=F32) + ba_ref[...]
        log_sig = jnp.minimum(logit, 0.0) - jnp.log(1.0 + jnp.exp(-jnp.abs(logit)))
        lg_ref[0, rows, :] = log_sig * (1.0 / GLA_TAU)

    for sub in range(TOKEN_BLOCK // INPROJ_SUB):
        rows = slice(sub * INPROJ_SUB, (sub + 1) * INPROJ_SUB)
        x = x_ref[0, rows, :]
        inv = lax.rsqrt(jnp.mean(x * x, axis=-1, keepdims=True) + EPS)
        hb = ((x * inv) * scale1 + shift).astype(BF16)

        lr_t = lax.dot_general(w_ref[:, COL_LR:COL_LR + 2 * GLA_RANK], hb, _TNT,
                               preferred_element_type=F32).astype(BF16)

        for half in range(2):
            res = _dot(hb, w_ref[:, COL_QK[half]:COL_QK[half] + 2 * QK])
            if use_rope and half == 0:
                for t in range(4):
                    tile = res[:, t * LANES:(t + 1) * LANES]
                    if t >= 2:
                        tile = tile * (DK ** -0.5)
                    tile = _rope_tile(tile, cos_ref[rows, :], sin_ref[rows, :])
                    qk_ref[0, rows, t * LANES:(t + 1) * LANES] = tile.astype(BF16)
            else:
                lane = lax.broadcasted_iota(jnp.int32, (1, 2 * QK), 1)
                scaled = (lane >= QK) if half == 0 else (lane < QK)
                col_scale = jnp.where(scaled, DK ** -0.5, 1.0)
                qk_ref[0, rows, half * 2 * QK:(half + 1) * 2 * QK] = (res * col_scale).astype(BF16)

        gate_logs(lr_t, rows)

        for half in range(2):
            z_ref[0, rows, half * WV:(half + 1) * WV] = _silu(
                _dot(hb, w_ref[:, COL_Z[half]:COL_Z[half] + WV])).astype(BF16)

        for half in range(2):
            vt = lax.dot_general(w_ref[:, COL_V[half]:COL_V[half] + WV], hb, _TNT,
                                 preferred_element_type=F32)
            for j in range(INPROJ_SUB // SCAN_CHUNK):
                jj = sub * (INPROJ_SUB // SCAN_CHUNK) + j
                vt_ref[0, jj, half * WV:(half + 1) * WV, :] = vt[:, j * SCAN_CHUNK:(j + 1) * SCAN_CHUNK].astype(BF16)


def _inproj(x, mod, mod_row_fn, weights, rope):
    B, T, _ = x.shape
    w, wa, ba = weights
    nt = T // TOKEN_BLOCK
    cpb = TOKEN_BLOCK // SCAN_CHUNK
    const = lambda b, t: (0, 0)
    in_specs = [
        pl.BlockSpec((1, TOKEN_BLOCK, D_MODEL), lambda b, t: (b, t, 0)),
        pl.BlockSpec((1, 1, 3 * D_MODEL), lambda b, t: (mod_row_fn(b), 0, 0)),
        pl.BlockSpec(w.shape, const),
        pl.BlockSpec(wa.shape, const),
        pl.BlockSpec(ba.shape, const),
    ]
    args = [x, mod, w, wa, ba]
    if rope is not None:
        in_specs += [pl.BlockSpec((TOKEN_BLOCK, LANES), lambda b, t: (t, 0))] * 2
        args += list(rope)
    out_shape = [
        jax.ShapeDtypeStruct((B, T, 4 * QK), BF16),
        jax.ShapeDtypeStruct((B, T, 2 * WV), BF16),
        jax.ShapeDtypeStruct((B, T // SCAN_CHUNK, 2 * WV, SCAN_CHUNK), BF16),
        jax.ShapeDtypeStruct((B, T, 2 * QK), F32),
    ]
    out_specs = [
        pl.BlockSpec((1, TOKEN_BLOCK, 4 * QK), lambda b, t: (b, t, 0)),
        pl.BlockSpec((1, TOKEN_BLOCK, 2 * WV), lambda b, t: (b, t, 0)),
        pl.BlockSpec((1, cpb, 2 * WV, SCAN_CHUNK), lambda b, t: (b, t, 0, 0)),
        pl.BlockSpec((1, TOKEN_BLOCK, 2 * QK), lambda b, t: (b, t, 0)),
    ]
    return pl.pallas_call(
        functools.partial(_inproj_kernel, rope is not None),
        grid=(B, nt),
        in_specs=in_specs,
        out_specs=out_specs,
        out_shape=out_shape,
        compiler_params=pltpu.CompilerParams(
            dimension_semantics=("parallel", "parallel"), vmem_limit_bytes=VMEM_LIMIT),
        name="inproj_rope" if rope is not None else "inproj",
    )(*args)


def _scan_kernel(is_gla, has_init, want_state, n_chunks, seq_blk, unroll, *refs):
    refs = list(refs)
    q_ref, k_ref, vt_ref = refs[:3]
    pos = 3
    if is_gla:
        lgf_ref, lgb_ref = refs[pos:pos + 2]
        pos += 2
    else:
        ld_ref = refs[pos]
        pos += 1
    if has_init:
        s0_ref = refs[pos]
        pos += 1
    o_ref = refs[pos]
    pos += 1
    if want_state:
        sn_ref = refs[pos]
        pos += 1
    st_ref, oacc_ref = refs[pos:pos + 2]

    C = SCAN_CHUNK
    mid = C // 2
    n_pairs = N_HEADS // 2
    lane = lax.broadcasted_iota(jnp.int32, (1, LANES), 1)
    head_mask = (lane < DK, lane >= DK)
    ri = lax.broadcasted_iota(jnp.int32, (C, C), 0)
    ci = lax.broadcasted_iota(jnp.int32, (C, C), 1)
    tpos = lax.broadcasted_iota(jnp.int32, (C, 1), 0).astype(F32)
    tri = ((ri >= ci), (ci >= ri))

    if has_init:
        for s in range(seq_blk):
            for p in range(n_pairs):
                for d in range(2):
                    s0 = s0_ref[s, 0, d]
                    both = jnp.concatenate([s0[2 * p], s0[2 * p + 1]], axis=0).T
                    st_ref[s, p, d, 0:DV] = both
                    st_ref[s, p, d, DV:2 * DV] = both
    else:
        st_ref[...] = jnp.zeros(st_ref.shape, F32)

    if is_gla:
        tri_ones = [t.astype(BF16) for t in tri]
    else:
        q_mul, k_mul, ret_decay, score_mul = {}, {}, {}, {}
        for p in range(n_pairs):
            for d in range(2):
                ld0 = ld_ref[d, 2 * p]
                ld1 = ld_ref[d, 2 * p + 1]
                ldr = jnp.where(lane < DK, ld0, ld1)
                if d == 0:
                    q_mul[p, d] = jnp.exp(ldr * (tpos + 1.0))
                    k_mul[p, d] = jnp.exp(ldr * (C - 1.0 - tpos))
                    dist = (ri - ci).astype(F32)
                else:
                    q_mul[p, d] = jnp.exp(ldr * (C - tpos))
                    k_mul[p, d] = jnp.exp(ldr * tpos)
                    dist = (ci - ri).astype(F32)
                ret_decay[p, d] = jnp.exp(ldr * float(C))
                score_mul[p, d] = [jnp.where(tri[d], jnp.exp(ldh * dist), 0.0) for ldh in (ld0, ld1)]
            score_mul[p] = [score_mul[p, 0][hh] + score_mul[p, 1][hh] for hh in range(2)]

    def has_intra(d):
        return is_gla or d == 0

    zero = jnp.zeros((), BF16)
    groups = [(s, p, d) for s in range(seq_blk) for p in range(n_pairs) for d in range(2)]

    def run_block(n0, accumulate):
        keys = [g + (u,) for u in range(unroll) for g in groups]
        chains = {}

        def stage_load(key):
            s, p, d, u = key
            ptile = slice(p * LANES, (p + 1) * LANES)
            n = n0 + u
            c = n if d == 0 else n_chunks - 1 - n
            t = dict(c=c, rows=pl.ds(pl.multiple_of(c * C, C), C))
            t["q"] = q_ref[s, t["rows"], ptile]
            t["k"] = k_ref[s, t["rows"], ptile]
            if is_gla:
                lg = (lgf_ref if d == 0 else lgb_ref)[s, t["rows"], ptile]
                hi = lg.astype(BF16)
                lo = (lg - hi.astype(F32)).astype(BF16)
                bb = _dot(tri_ones[d], jnp.concatenate([hi, lo], axis=1))
                t["b"] = bb[:, :LANES] + bb[:, LANES:]
            chains[key] = t

        def stage_scores(key):
            s, p, d, u = key
            t = chains[key]
            q, k = t.pop("q"), t.pop("k")
            if is_gla:
                b = t.pop("b")
                r = b[mid - 1:mid] if d == 0 else b[mid:mid + 1]
                bl = b[C - 1:C] if d == 0 else b[0:1]
                qs = q.astype(F32) * jnp.exp(b - r)
                ks = k.astype(F32) * jnp.exp(r - b)
                q_sc = qs.astype(BF16)
                k_sc = ks.astype(BF16)
                q_dec = (qs * jnp.exp(r)).astype(BF16)
                t["k_dec"] = (ks * jnp.exp(bl - r)).astype(BF16)
                t["decay"] = jnp.exp(bl)
            else:
                q_sc = q
                k_sc = k
                q_dec = (q.astype(F32) * q_mul[p, d]).astype(BF16)
                t["k_dec"] = (k.astype(F32) * k_mul[p, d]).astype(BF16)
                t["decay"] = ret_decay[p, d]
            t["q_dec"] = [jnp.where(head_mask[hh], q_dec, zero) for hh in range(2)]
            if has_intra(d):
                k_heads = jnp.concatenate([jnp.where(head_mask[hh], k_sc, zero) for hh in range(2)], axis=0)
                t["sc"] = _dot_nt(q_sc, k_heads)

        def stage_kv(key):
            s, p, d, u = key
            t = chains[key]
            t["vt"] = vt_ref[s, t["c"], p * 2 * DV:(p + 1) * 2 * DV, :]
            t["kv"] = _dot(t["vt"], t.pop("k_dec"))
            t["scb"] = []
            if has_intra(d):
                sc_all = t.pop("sc")
                for hh in range(2):
                    sc = sc_all[:, hh * C:(hh + 1) * C]
                    if is_gla:
                        sc = jnp.where(tri[d], sc, 0.0)
                    else:
                        sc = sc * score_mul[p][hh]
                    t["scb"].append(sc.astype(BF16))

        def stage_out(key):
            s, p, d, u = key
            t = chains[key]
            st = st_ref[s, p, d]
            stb = st.astype(BF16)
            t["o"] = []
            for hh in range(2):
                hrows = slice(hh * DV, (hh + 1) * DV)
                if has_intra(d):
                    lhs = jnp.concatenate([t["scb"][hh], t["q_dec"][hh]], axis=1)
                    rhs = jnp.concatenate([t["vt"][hrows], stb[hrows]], axis=1)
                else:
                    lhs, rhs = t["q_dec"][hh], stb[hrows]
                t["o"].append(_dot_nt(lhs, rhs))
            st_ref[s, p, d] = st * t.pop("decay") + t.pop("kv")

        def stage_store(key):
            s, p, d, u = key
            t = chains.pop(key)
            for hh in range(2):
                cols = slice((2 * p + hh) * DV, (2 * p + hh + 1) * DV)
                if accumulate:
                    o_ref[s, t["rows"], cols] = (oacc_ref[s, t["rows"], cols] + t["o"][hh]).astype(BF16)
                else:
                    oacc_ref[s, t["rows"], cols] = t["o"][hh]

        stages = (stage_load, stage_scores, stage_kv, stage_out, stage_store)
        grp = GLA_PIPE_GROUP if is_gla else RET_PIPE_GROUP
        key_groups = [keys[i:i + grp] for i in range(0, len(keys), grp)]
        for step in range(len(key_groups) + len(stages) - 1):
            for k in reversed(range(len(stages))):
                if 0 <= step - k < len(key_groups):
                    for key in key_groups[step - k]:
                        stages[k](key)

    def loop_body(accumulate, base):
        def body(m, carry):
            run_block(base + m * unroll, accumulate)
            return carry
        return body

    half = n_chunks // 2
    assert half % unroll == 0
    lax.fori_loop(0, half // unroll, loop_body(False, 0), 0)
    lax.fori_loop(0, half // unroll, loop_body(True, half), 0)

    if want_state:
        for s, p, d in groups:
            for hh in range(2):
                sn_ref[s, 0, d, 2 * p + hh] = st_ref[s, p, d, hh * DV:(hh + 1) * DV].T[hh * DK:(hh + 1) * DK, :]


def _scan(is_gla, qk, vt, lg, ld, s0, want_state, seq_blk, unroll):
    B, T, _ = qk.shape
    n_chunks = T // SCAN_CHUNK
    assert n_chunks % 2 == 0 and B % seq_blk == 0
    sb = seq_blk
    grp = 1 if is_gla else 0
    in_specs = [
        pl.BlockSpec((sb, T, QK), lambda b: (b, 0, 2 * grp)),
        pl.BlockSpec((sb, T, QK), lambda b: (b, 0, 2 * grp + 1)),
        pl.BlockSpec((sb, n_chunks, WV, SCAN_CHUNK), lambda b: (b, 0, grp, 0)),
    ]
    args = [qk, qk, vt]
    if is_gla:
        in_specs += [
            pl.BlockSpec((sb, T, QK), lambda b: (b, 0, 0)),
            pl.BlockSpec((sb, T, QK), lambda b: (b, 0, 1)),
        ]
        args += [lg, lg]
    else:
        in_specs += [pl.BlockSpec(memory_space=pltpu.SMEM)]
        args += [ld]
    state_spec = pl.BlockSpec((sb, 1, 2, N_HEADS, DK, DV), lambda b: (b, 0, 0, 0, 0, 0))
    if s0 is not None:
        in_specs += [state_spec]
        args += [s0]
    out_shape = [jax.ShapeDtypeStruct((B, T, WV), BF16)]
    out_specs = [pl.BlockSpec((sb, T, WV), lambda b: (b, 0, 0))]
    if want_state:
        out_shape += [jax.ShapeDtypeStruct((B, 1, 2, N_HEADS, DK, DV), F32)]
        out_specs += [state_spec]
    name = ("gla" if is_gla else "ret") + ("_init" if s0 is not None else "") + "_scan"
    return pl.pallas_call(
        functools.partial(_scan_kernel, is_gla, s0 is not None, want_state, n_chunks, sb, unroll),
        grid=(B // sb,),
        in_specs=in_specs,
        out_specs=out_specs,
        out_shape=out_shape,
        scratch_shapes=[pltpu.VMEM((sb, N_HEADS // 2, 2, 2 * DV, LANES), F32),
                        pltpu.VMEM((sb, T, WV), F32)],
        compiler_params=pltpu.CompilerParams(
            dimension_semantics=("arbitrary",), vmem_limit_bytes=VMEM_LIMIT),
        name=name,
    )(*args)


def _outproj_rows(rows, x_ref, mod_ref, or_ref, og_ref, gz_ref, gw_ref, fw_ref, wout_ref, y_ref):
    gate = mod_ref[0, :, 2 * D_MODEL:3 * D_MODEL]
    gw = gw_ref[...]
    parts = []
    for h in range(N_HEADS):
        cols = slice(h * DV, (h + 1) * DV)
        t = or_ref[0, rows, cols].astype(F32)
        mu = jnp.mean(t, axis=-1, keepdims=True)
        dlt = t - mu
        var = jnp.mean(dlt * dlt, axis=-1, keepdims=True)
        n = dlt * lax.rsqrt(var + EPS)
        parts.append((n * gz_ref[0, rows, cols].astype(F32)).astype(BF16))
    for h in range(N_HEADS):
        cols = slice(h * DV, (h + 1) * DV)
        t = og_ref[0, rows, cols].astype(F32)
        n = t * lax.rsqrt(jnp.mean(t * t, axis=-1, keepdims=True) + EPS) * gw
        gcols = slice(WV + h * DV, WV + (h + 1) * DV)
        parts.append((n * gz_ref[0, rows, gcols].astype(F32)).astype(BF16))
    mixed = jnp.concatenate(parts, axis=-1)
    out = _dot(mixed, wout_ref[...])
    y = x_ref[0, rows, :] + gate * out
    yn = y * lax.rsqrt(jnp.mean(y * y, axis=-1, keepdims=True) + EPS)
    y_ref[0, rows, :] = yn * fw_ref[...]


def _outproj_kernel(*refs):
    for sub in range(OUT_TOKEN_BLOCK // OUTPROJ_SUB):
        _outproj_rows(slice(sub * OUTPROJ_SUB, (sub + 1) * OUTPROJ_SUB), *refs)


def _outproj(x, mod, mod_row_fn, o_r, o_g, gz, gw, fw, wout):
    n_tok = x.shape[0] * x.shape[1]
    B = n_tok // OUT_TOKEN_BLOCK
    x, o_r, o_g, gz = (a.reshape(B, OUT_TOKEN_BLOCK, a.shape[-1]) for a in (x, o_r, o_g, gz))
    const = lambda b: (0, 0)
    blk = lambda width: pl.BlockSpec((1, OUT_TOKEN_BLOCK, width), lambda b: (b, 0, 0))
    return pl.pallas_call(
        _outproj_kernel,
        grid=(B,),
        in_specs=[
            blk(D_MODEL),
            pl.BlockSpec((1, 1, 3 * D_MODEL), lambda b: (mod_row_fn(b), 0, 0)),
            blk(WV),
            blk(WV),
            blk(2 * WV),
            pl.BlockSpec((1, DV), const),
            pl.BlockSpec((1, D_MODEL), const),
            pl.BlockSpec(wout.shape, const),
        ],
        out_specs=blk(D_MODEL),
        out_shape=jax.ShapeDtypeStruct((B, OUT_TOKEN_BLOCK, D_MODEL), F32),
        compiler_params=pltpu.CompilerParams(
            dimension_semantics=("parallel",), vmem_limit_bytes=VMEM_LIMIT),
        name="outproj",
    )(x, mod, o_r, o_g, gz, gw, fw, wout)


def _rope_tables(n_tokens):
    rows = n_tokens // GRID_W
    rr, cc = np.meshgrid(np.arange(rows), np.arange(GRID_W), indexing="ij")
    rr = rr.reshape(-1).astype(np.float32)
    cc = cc.reshape(-1).astype(np.float32)
    n_freq = DK // 4
    inv = np.float32(ROPE_BASE) ** (-np.arange(n_freq, dtype=np.float32) / np.float32(n_freq))
    ang = np.concatenate([rr[:, None] * inv, cc[:, None] * inv], axis=-1)
    cos = np.cos(ang.astype(np.float64)).astype(np.float32)
    sin = np.sin(ang.astype(np.float64)).astype(np.float32)
    cos_t = np.tile(np.concatenate([cos, cos], axis=-1), (1, LANES // DK))
    sin_t = np.tile(np.concatenate([-sin, sin], axis=-1), (1, LANES // DK))
    return jnp.asarray(cos_t), jnp.asarray(sin_t)


def _prep_weights(w_in, gla_w_alpha, gla_b_alpha):
    assert w_in.shape == (D_MODEL, COL_LR + 2 * GLA_RANK)
    wa = jnp.zeros((2 * GLA_RANK, 2 * QK), F32)
    wa = wa.at[0:GLA_RANK, 0:QK].set(gla_w_alpha[0])
    wa = wa.at[GLA_RANK:2 * GLA_RANK, QK:2 * QK].set(gla_w_alpha[1])
    ba = jnp.concatenate([gla_b_alpha[0], gla_b_alpha[1]]).reshape(1, 2 * QK)
    return w_in.astype(BF16), wa.astype(BF16), ba


def kernel(x_prompt, x_sample, c, state_ret, state_gla, c_ctx, w_mod, b_mod, w_in, ret_log_decay,
           gla_w_alpha, gla_b_alpha, gla_norm_w, w_out, final_norm_w):
    assert w_mod.shape[0] == 1, "single-layer model"
    b_ctx, t_ctx, _ = x_prompt.shape
    b_dec, t_dec, _ = x_sample.shape
    assert t_ctx % SCAN_CHUNK == 0 and t_dec % TOKEN_BLOCK == 0 and TOKEN_BLOCK % t_ctx == 0
    assert 1 + b_dec <= MOD_ROWS

    cond = jnp.concatenate(
        [c_ctx[None, :], c, jnp.zeros((MOD_ROWS - 1 - b_dec, D_MODEL), F32)], axis=0)
    mod = _modulation(cond, w_mod[0], b_mod[0]).reshape(MOD_ROWS, 1, 3 * D_MODEL)

    weights = _prep_weights(w_in[0], gla_w_alpha[0], gla_b_alpha[0])
    wout = w_out[0].astype(BF16)
    gw = gla_norm_w[0].reshape(1, DV)
    fw = final_norm_w.reshape(1, D_MODEL)
    ld = ret_log_decay[0]

    per_blk = TOKEN_BLOCK // t_ctx
    xp = x_prompt.reshape(b_ctx // per_blk, TOKEN_BLOCK, D_MODEL)
    ctx_row = lambda b: 0
    qk, gz, vt, lg = _inproj(xp, mod, ctx_row, weights, None)
    qk = qk.reshape(b_ctx, t_ctx, 4 * QK)
    vt = vt.reshape(b_ctx, t_ctx // SCAN_CHUNK, 2 * WV, SCAN_CHUNK)
    lg = lg.reshape(b_ctx, t_ctx, 2 * QK)
    o_r, new_ret = _scan(False, qk, vt, None, ld, None, True, CTX_SEQ_BLOCK, 1)
    o_g, new_gla = _scan(True, qk, vt, lg, None, None, True, CTX_SEQ_BLOCK, 1)
    y_prompt = _outproj(xp, mod, ctx_row, o_r, o_g, gz, gw, fw, wout)
    y_prompt = y_prompt.reshape(b_ctx, t_ctx, D_MODEL)

    dec_row = lambda b: b + 1
    rope = _rope_tables(t_dec)
    qk, gz, vt, lg = _inproj(x_sample, mod, dec_row, weights, rope)
    (o_r,) = _scan(False, qk, vt, None, ld, state_ret, False, 1, DEC_UNROLL)
    (o_g,) = _scan(True, qk, vt, lg, None, state_gla, False, 1, DEC_UNROLL)
    per_seq = t_dec // OUT_TOKEN_BLOCK
    y_sample = _outproj(x_sample, mod, lambda b: b // per_seq + 1, o_r, o_g, gz, gw, fw, wout)
    y_sample = y_sample.reshape(b_dec, t_dec, D_MODEL)

    return (y_prompt, y_sample, new_ret, new_gla)
```

```python
import functools

import jax
import jax.numpy as jnp
import numpy as np
from jax import lax
from jax.experimental import pallas as pl
from jax.experimental.pallas import tpu as pltpu

F32 = jnp.float32
BF16 = jnp.bfloat16

D_MODEL = 1024
N_HEADS = 4
DK = 64
DV = 128
QK = N_HEADS * DK
WV = N_HEADS * DV
GLA_RANK = 16
GLA_TAU = 16.0
EPS = 1e-6
GRID_W = 64
ROPE_BASE = 10000.0

COL_QK = (0, 2 * QK + 2 * WV)
COL_V = (2 * QK, 4 * QK + 2 * WV)
COL_Z = (2 * QK + WV, 4 * QK + 3 * WV)
COL_LR = 4 * QK + 4 * WV

LANES = 128
SCAN_CHUNK = 128
CTX_SEQ_BLOCK = 8
CTX_GLA_SEQ_BLOCK = 4
DEC_UNROLL = 8
RET_PIPE_GROUP = 2
GLA_PIPE_GROUP = 8
TOKEN_BLOCK = 1024
INPROJ_SUB = 256
OUTPROJ_SUB = 256
MOD_ROWS = 16
MOD_COL_BLOCK = 1536
OUT_TOKEN_BLOCK = 1024
V7X_VMEM_BYTES = 64 * 1024 * 1024
VMEM_LIMIT = 3 * V7X_VMEM_BYTES // 4

_NT = (((1,), (1,)), ((), ()))
_TN = (((0,), (0,)), ((), ()))
_TNT = (((0,), (1,)), ((), ()))


def _dot(a, b):
    return jnp.dot(a, b, preferred_element_type=F32)


def _dot_nt(a, b):
    return lax.dot_general(a, b, _NT, preferred_element_type=F32)


def _silu(x):
    return x * jax.nn.sigmoid(x)


def _mod_kernel(c_ref, w_ref, b_ref, o_ref):
    s = _silu(c_ref[...]).astype(BF16)
    o_ref[...] = _dot(s, w_ref[...].astype(BF16)) + b_ref[...]


def _modulation(cond, w_mod, b_mod):
    n_col = w_mod.shape[1]
    return pl.pallas_call(
        _mod_kernel,
        grid=(n_col // MOD_COL_BLOCK,),
        in_specs=[
            pl.BlockSpec((MOD_ROWS, D_MODEL), lambda j: (0, 0)),
            pl.BlockSpec((D_MODEL, MOD_COL_BLOCK), lambda j: (0, j)),
            pl.BlockSpec((1, MOD_COL_BLOCK), lambda j: (0, j)),
        ],
        out_specs=pl.BlockSpec((MOD_ROWS, MOD_COL_BLOCK), lambda j: (0, j)),
        out_shape=jax.ShapeDtypeStruct((MOD_ROWS, n_col), F32),
        compiler_params=pltpu.CompilerParams(vmem_limit_bytes=VMEM_LIMIT),
        name="modulation",
    )(cond, w_mod, b_mod.reshape(1, n_col))


def _rope_tile(t, cos, sin_signed):
    lane = lax.broadcasted_iota(jnp.int32, t.shape, 1)
    first_half = (lane & 32) == 0
    swapped = jnp.where(first_half, pltpu.roll(t, LANES - 32, 1), pltpu.roll(t, 32, 1))
    return t * cos + swapped * sin_signed


def _inproj_kernel(use_rope, x_ref, mod_ref, w_ref, wa_ref, ba_ref, *rest):
    if use_rope:
        cos_ref, sin_ref, qk_ref, z_ref, vt_ref, lg_ref = rest
    else:
        qk_ref, z_ref, vt_ref, lg_ref = rest
    shift = mod_ref[0, :, 0:D_MODEL]
    scale1 = 1.0 + mod_ref[0, :, D_MODEL:2 * D_MODEL]

    def gate_logs(lr_t, rows):
        logit = lax.dot_general(lr_t, wa_ref[...], _TN, preferred_element_type=F32) + ba_ref[...]
        log_sig = jnp.minimum(logit, 0.0) - jnp.log(1.0 + jnp.exp(-jnp.abs(logit)))
        lg_ref[0, rows, :] = log_sig * (1.0 / GLA_TAU)

    for sub in range(TOKEN_BLOCK // INPROJ_SUB):
        rows = slice(sub * INPROJ_SUB, (sub + 1) * INPROJ_SUB)
        x = x_ref[0, rows, :]
        inv = lax.rsqrt(jnp.mean(x * x, axis=-1, keepdims=True) + EPS)
        hb = ((x * inv) * scale1 + shift).astype(BF16)

        lr_t = lax.dot_general(w_ref[:, COL_LR:COL_LR + 2 * GLA_RANK], hb, _TNT,
                               preferred_element_type=F32).astype(BF16)

        for half in range(2):
            res = _dot(hb, w_ref[:, COL_QK[half]:COL_QK[half] + 2 * QK])
            if use_rope and half == 0:
                for t in range(4):
                    tile = res[:, t * LANES:(t + 1) * LANES]
                    if t >= 2:
                        tile = tile * (DK ** -0.5)
                    tile = _rope_tile(tile, cos_ref[rows, :], sin_ref[rows, :])
                    qk_ref[0, rows, t * LANES:(t + 1) * LANES] = tile.astype(BF16)
            else:
                lane = lax.broadcasted_iota(jnp.int32, (1, 2 * QK), 1)
                scaled = (lane >= QK) if half == 0 else (lane < QK)
                col_scale = jnp.where(scaled, DK ** -0.5, 1.0)
                qk_ref[0, rows, half * 2 * QK:(half + 1) * 2 * QK] = (res * col_scale).astype(BF16)

        gate_logs(lr_t, rows)

        for half in range(2):
            z_ref[0, rows, half * WV:(half + 1) * WV] = _silu(
                _dot(hb, w_ref[:, COL_Z[half]:COL_Z[half] + WV])).astype(BF16)

        for half in range(2):
            vt = lax.dot_general(w_ref[:, COL_V[half]:COL_V[half] + WV], hb, _TNT,
                                 preferred_element_type=F32)
            for j in range(INPROJ_SUB // SCAN_CHUNK):
                jj = sub * (INPROJ_SUB // SCAN_CHUNK) + j
                vt_ref[0, jj, half * WV:(half + 1) * WV, :] = vt[:, j * SCAN_CHUNK:(j + 1) * SCAN_CHUNK].astype(BF16)


def _inproj(x, mod, mod_row_fn, weights, rope):
    B, T, _ = x.shape
    w, wa, ba = weights
    nt = T // TOKEN_BLOCK
    cpb = TOKEN_BLOCK // SCAN_CHUNK
    const = lambda b, t: (0, 0)
    in_specs = [
        pl.BlockSpec((1, TOKEN_BLOCK, D_MODEL), lambda b, t: (b, t, 0)),
        pl.BlockSpec((1, 1, 3 * D_MODEL), lambda b, t: (mod_row_fn(b), 0, 0)),
        pl.BlockSpec(w.shape, const),
        pl.BlockSpec(wa.shape, const),
        pl.BlockSpec(ba.shape, const),
    ]
    args = [x, mod, w, wa, ba]
    if rope is not None:
        in_specs += [pl.BlockSpec((TOKEN_BLOCK, LANES), lambda b, t: (t, 0))] * 2
        args += list(rope)
    out_shape = [
        jax.ShapeDtypeStruct((B, T, 4 * QK), BF16),
        jax.ShapeDtypeStruct((B, T, 2 * WV), BF16),
        jax.ShapeDtypeStruct((B, T // SCAN_CHUNK, 2 * WV, SCAN_CHUNK), BF16),
        jax.ShapeDtypeStruct((B, T, 2 * QK), F32),
    ]
    out_specs = [
        pl.BlockSpec((1, TOKEN_BLOCK, 4 * QK), lambda b, t: (b, t, 0)),
        pl.BlockSpec((1, TOKEN_BLOCK, 2 * WV), lambda b, t: (b, t, 0)),
        pl.BlockSpec((1, cpb, 2 * WV, SCAN_CHUNK), lambda b, t: (b, t, 0, 0)),
        pl.BlockSpec((1, TOKEN_BLOCK, 2 * QK), lambda b, t: (b, t, 0)),
    ]
    return pl.pallas_call(
        functools.partial(_inproj_kernel, rope is not None),
        grid=(B, nt),
        in_specs=in_specs,
        out_specs=out_specs,
        out_shape=out_shape,
        compiler_params=pltpu.CompilerParams(
            dimension_semantics=("parallel", "parallel"), vmem_limit_bytes=VMEM_LIMIT),
        name="inproj_rope" if rope is not None else "inproj",
    )(*args)


def _scan_kernel(is_gla, has_init, want_state, n_chunks, seq_blk, unroll, *refs):
    refs = list(refs)
    q_ref, k_ref, vt_ref = refs[:3]
    pos = 3
    if is_gla:
        lgf_ref, lgb_ref = refs[pos:pos + 2]
        pos += 2
    else:
        ld_ref = refs[pos]
        pos += 1
    if has_init:
        s0_ref = refs[pos]
        pos += 1
    o_ref = refs[pos]
    pos += 1
    if want_state:
        sn_ref = refs[pos]
        pos += 1
    st_ref, oacc_ref = refs[pos:pos + 2]

    C = SCAN_CHUNK
    mid = C // 2
    n_pairs = N_HEADS // 2
    lane = lax.broadcasted_iota(jnp.int32, (1, LANES), 1)
    head_mask = (lane < DK, lane >= DK)
    ri = lax.broadcasted_iota(jnp.int32, (C, C), 0)
    ci = lax.broadcasted_iota(jnp.int32, (C, C), 1)
    tpos = lax.broadcasted_iota(jnp.int32, (C, 1), 0).astype(F32)
    tri = ((ri >= ci), (ci >= ri))

    if has_init:
        for s in range(seq_blk):
            for p in range(n_pairs):
                for d in range(2):
                    s0 = s0_ref[s, 0, d]
                    both = jnp.concatenate([s0[2 * p], s0[2 * p + 1]], axis=0).T
                    st_ref[s, p, d, 0:DV] = both
                    st_ref[s, p, d, DV:2 * DV] = both
    else:
        st_ref[...] = jnp.zeros(st_ref.shape, F32)

    if is_gla:
        tri_ones = [t.astype(BF16) for t in tri]
    else:
        q_mul, k_mul, ret_decay, score_mul = {}, {}, {}, {}
        for p in range(n_pairs):
            for d in range(2):
                ld0 = ld_ref[d, 2 * p]
                ld1 = ld_ref[d, 2 * p + 1]
                ldr = jnp.where(lane < DK, ld0, ld1)
                if d == 0:
                    q_mul[p, d] = jnp.exp(ldr * (tpos + 1.0))
                    k_mul[p, d] = jnp.exp(ldr * (C - 1.0 - tpos))
                    dist = (ri - ci).astype(F32)
                else:
                    q_mul[p, d] = jnp.exp(ldr * (C - tpos))
                    k_mul[p, d] = jnp.exp(ldr * tpos)
                    dist = (ci - ri).astype(F32)
                ret_decay[p, d] = jnp.exp(ldr * float(C))
                score_mul[p, d] = [jnp.where(tri[d], jnp.exp(ldh * dist), 0.0) for ldh in (ld0, ld1)]
            score_mul[p] = [score_mul[p, 0][hh] + score_mul[p, 1][hh] for hh in range(2)]

    def has_intra(d):
        return is_gla or d == 0

    zero = jnp.zeros((), BF16)
    groups = [(s, p, d) for s in range(seq_blk) for p in range(n_pairs) for d in range(2)]

    def run_block(n0, accumulate):
        keys = [g + (u,) for u in range(unroll) for g in groups]
        chains = {}

        def stage_load(key):
            s, p, d, u = key
            ptile = slice(p * LANES, (p + 1) * LANES)
            n = n0 + u
            c = n if d == 0 else n_chunks - 1 - n
            t = dict(c=c, rows=pl.ds(pl.multiple_of(c * C, C), C))
            t["q"] = q_ref[s, t["rows"], ptile]
            t["k"] = k_ref[s, t["rows"], ptile]
            if is_gla:
                lg = (lgf_ref if d == 0 else lgb_ref)[s, t["rows"], ptile]
                hi = lg.astype(BF16)
                lo = (lg - hi.astype(F32)).astype(BF16)
                bb = _dot(tri_ones[d], jnp.concatenate([hi, lo], axis=1))
                t["b"] = bb[:, :LANES] + bb[:, LANES:]
            chains[key] = t

        def stage_scores(key):
            s, p, d, u = key
            t = chains[key]
            q, k = t.pop("q"), t.pop("k")
            if is_gla:
                b = t.pop("b")
                r = b[mid - 1:mid] if d == 0 else b[mid:mid + 1]
                bl = b[C - 1:C] if d == 0 else b[0:1]
                qs = q.astype(F32) * jnp.exp(b - r)
                ks = k.astype(F32) * jnp.exp(r - b)
                q_sc = qs.astype(BF16)
                k_sc = ks.astype(BF16)
                q_dec = (qs * jnp.exp(r)).astype(BF16)
                t["k_dec"] = (ks * jnp.exp(bl - r)).astype(BF16)
                t["decay"] = jnp.exp(bl)
            else:
                q_sc = q
                k_sc = k
                q_dec = (q.astype(F32) * q_mul[p, d]).astype(BF16)
                t["k_dec"] = (k.astype(F32) * k_mul[p, d]).astype(BF16)
                t["decay"] = ret_decay[p, d]
            t["q_dec"] = [jnp.where(head_mask[hh], q_dec, zero) for hh in range(2)]
            if has_intra(d):
                k_heads = jnp.concatenate([jnp.where(head_mask[hh], k_sc, zero) for hh in range(2)], axis=0)
                t["sc"] = _dot_nt(q_sc, k_heads)

        def stage_kv(key):
            s, p, d, u = key
            t = chains[key]
            t["vt"] = vt_ref[s, t["c"], p * 2 * DV:(p + 1) * 2 * DV, :]
            t["kv"] = _dot(t["vt"], t.pop("k_dec"))
            t["scb"] = []
            if has_intra(d):
                sc_all = t.pop("sc")
                for hh in range(2):
                    sc = sc_all[:, hh * C:(hh + 1) * C]
                    if is_gla:
                        sc = jnp.where(tri[d], sc, 0.0)
                    else:
                        sc = sc * score_mul[p][hh]
                    t["scb"].append(sc.astype(BF16))

        def stage_out(key):
            s, p, d, u = key
            t = chains[key]
            st = st_ref[s, p, d]
            stb = st.astype(BF16)
            t["o"] = []
            for hh in range(2):
                hrows = slice(hh * DV, (hh + 1) * DV)
                if has_intra(d):
                    lhs = jnp.concatenate([t["scb"][hh], t["q_dec"][hh]], axis=1)
                    rhs = jnp.concatenate([t["vt"][hrows], stb[hrows]], axis=1)
                else:
                    lhs, rhs = t["q_dec"][hh], stb[hrows]
                t["o"].append(_dot_nt(lhs, rhs))
            st_ref[s, p, d] = st * t.pop("decay") + t.pop("kv")

        def stage_store(key):
            s, p, d, u = key
            t = chains.pop(key)
            for hh in range(2):
                cols = slice((2 * p + hh) * DV, (2 * p + hh + 1) * DV)
                if accumulate:
                    o_ref[s, t["rows"], cols] = (oacc_ref[s, t["rows"], cols] + t["o"][hh]).astype(BF16)
                else:
                    oacc_ref[s, t["rows"], cols] = t["o"][hh]

        stages = (stage_load, stage_scores, stage_kv, stage_out, stage_store)
        grp = GLA_PIPE_GROUP if is_gla else RET_PIPE_GROUP
        key_groups = [keys[i:i + grp] for i in range(0, len(keys), grp)]
        for step in range(len(key_groups) + len(stages) - 1):
            for k in reversed(range(len(stages))):
                if 0 <= step - k < len(key_groups):
                    for key in key_groups[step - k]:
                        stages[k](key)

    def loop_body(accumulate, base):
        def body(m, carry):
            run_block(base + m * unroll, accumulate)
            return carry
        return body

    half = n_chunks // 2
    assert half % unroll == 0
    lax.fori_loop(0, half // unroll, loop_body(False, 0), 0)
    lax.fori_loop(0, half // unroll, loop_body(True, half), 0)

    if want_state:
        for s, p, d in groups:
            for hh in range(2):
                sn_ref[s, 0, d, 2 * p + hh] = st_ref[s, p, d, hh * DV:(hh + 1) * DV].T[hh * DK:(hh + 1) * DK, :]


def _scan(is_gla, qk, vt, lg, ld, s0, want_state, seq_blk, unroll):
    B, T, _ = qk.shape
    n_chunks = T // SCAN_CHUNK
    assert n_chunks % 2 == 0 and B % seq_blk == 0
    sb = seq_blk
    grp = 1 if is_gla else 0
    in_specs = [
        pl.BlockSpec((sb, T, QK), lambda b: (b, 0, 2 * grp)),
        pl.BlockSpec((sb, T, QK), lambda b: (b, 0, 2 * grp + 1)),
        pl.BlockSpec((sb, n_chunks, WV, SCAN_CHUNK), lambda b: (b, 0, grp, 0)),
    ]
    args = [qk, qk, vt]
    if is_gla:
        in_specs += [
            pl.BlockSpec((sb, T, QK), lambda b: (b, 0, 0)),
            pl.BlockSpec((sb, T, QK), lambda b: (b, 0, 1)),
        ]
        args += [lg, lg]
    else:
        in_specs += [pl.BlockSpec(memory_space=pltpu.SMEM)]
        args += [ld]
    state_spec = pl.BlockSpec((sb, 1, 2, N_HEADS, DK, DV), lambda b: (b, 0, 0, 0, 0, 0))
    if s0 is not None:
        in_specs += [state_spec]
        args += [s0]
    out_shape = [jax.ShapeDtypeStruct((B, T, WV), BF16)]
    out_specs = [pl.BlockSpec((sb, T, WV), lambda b: (b, 0, 0))]
    if want_state:
        out_shape += [jax.ShapeDtypeStruct((B, 1, 2, N_HEADS, DK, DV), F32)]
        out_specs += [state_spec]
    name = ("gla" if is_gla else "ret") + ("_init" if s0 is not None else "") + "_scan"
    return pl.pallas_call(
        functools.partial(_scan_kernel, is_gla, s0 is not None, want_state, n_chunks, sb, unroll),
        grid=(B // sb,),
        in_specs=in_specs,
        out_specs=out_specs,
        out_shape=out_shape,
        scratch_shapes=[pltpu.VMEM((sb, N_HEADS // 2, 2, 2 * DV, LANES), F32),
                        pltpu.VMEM((sb, T, WV), F32)],
        compiler_params=pltpu.CompilerParams(
            dimension_semantics=("arbitrary",), vmem_limit_bytes=VMEM_LIMIT),
        name=name,
    )(*args)


def _outproj_rows(rows, x_ref, mod_ref, or_ref, og_ref, gz_ref, gw_ref, fw_ref, wout_ref, y_ref):
    gate = mod_ref[0, :, 2 * D_MODEL:3 * D_MODEL]
    gw = gw_ref[...]
    parts = []
    for h in range(N_HEADS):
        cols = slice(h * DV, (h + 1) * DV)
        t = or_ref[0, rows, cols].astype(F32)
        mu = jnp.mean(t, axis=-1, keepdims=True)
        dlt = t - mu
        var = jnp.mean(dlt * dlt, axis=-1, keepdims=True)
        n = dlt * lax.rsqrt(var + EPS)
        parts.append((n * gz_ref[0, rows, cols].astype(F32)).astype(BF16))
    for h in range(N_HEADS):
        cols = slice(h * DV, (h + 1) * DV)
        t = og_ref[0, rows, cols].astype(F32)
        n = t * lax.rsqrt(jnp.mean(t * t, axis=-1, keepdims=True) + EPS) * gw
        gcols = slice(WV + h * DV, WV + (h + 1) * DV)
        parts.append((n * gz_ref[0, rows, gcols].astype(F32)).astype(BF16))
    mixed = jnp.concatenate(parts, axis=-1)
    out = _dot(mixed, wout_ref[...])
    y = x_ref[0, rows, :] + gate * out
    yn = y * lax.rsqrt(jnp.mean(y * y, axis=-1, keepdims=True) + EPS)
    y_ref[0, rows, :] = yn * fw_ref[...]


def _outproj_kernel(*refs):
    for sub in range(OUT_TOKEN_BLOCK // OUTPROJ_SUB):
        _outproj_rows(slice(sub * OUTPROJ_SUB, (sub + 1) * OUTPROJ_SUB), *refs)


def _outproj(x, mod, mod_row_fn, o_r, o_g, gz, gw, fw, wout):
    n_tok = x.shape[0] * x.shape[1]
    B = n_tok // OUT_TOKEN_BLOCK
    x, o_r, o_g, gz = (a.reshape(B, OUT_TOKEN_BLOCK, a.shape[-1]) for a in (x, o_r, o_g, gz))
    const = lambda b: (0, 0)
    blk = lambda width: pl.BlockSpec((1, OUT_TOKEN_BLOCK, width), lambda b: (b, 0, 0))
    return pl.pallas_call(
        _outproj_kernel,
        grid=(B,),
        in_specs=[
            blk(D_MODEL),
            pl.BlockSpec((1, 1, 3 * D_MODEL), lambda b: (mod_row_fn(b), 0, 0)),
            blk(WV),
            blk(WV),
            blk(2 * WV),
            pl.BlockSpec((1, DV), const),
            pl.BlockSpec((1, D_MODEL), const),
            pl.BlockSpec(wout.shape, const),
        ],
        out_specs=blk(D_MODEL),
        out_shape=jax.ShapeDtypeStruct((B, OUT_TOKEN_BLOCK, D_MODEL), F32),
        compiler_params=pltpu.CompilerParams(
            dimension_semantics=("parallel",), vmem_limit_bytes=VMEM_LIMIT),
        name="outproj",
    )(x, mod, o_r, o_g, gz, gw, fw, wout)


def _rope_tables(n_tokens):
    rows = n_tokens // GRID_W
    rr, cc = np.meshgrid(np.arange(rows), np.arange(GRID_W), indexing="ij")
    rr = rr.reshape(-1).astype(np.float32)
    cc = cc.reshape(-1).astype(np.float32)
    n_freq = DK // 4
    inv = np.float32(ROPE_BASE) ** (-np.arange(n_freq, dtype=np.float32) / np.float32(n_freq))
    ang = np.concatenate([rr[:, None] * inv, cc[:, None] * inv], axis=-1)
    cos = np.cos(ang.astype(np.float64)).astype(np.float32)
    sin = np.sin(ang.astype(np.float64)).astype(np.float32)
    cos_t = np.tile(np.concatenate([cos, cos], axis=-1), (1, LANES // DK))
    sin_t = np.tile(np.concatenate([-sin, sin], axis=-1), (1, LANES // DK))
    return jnp.asarray(cos_t), jnp.asarray(sin_t)


def _prep_weights(w_in, gla_w_alpha, gla_b_alpha):
    assert w_in.shape == (D_MODEL, COL_LR + 2 * GLA_RANK)
    wa = jnp.zeros((2 * GLA_RANK, 2 * QK), F32)
    wa = wa.at[0:GLA_RANK, 0:QK].set(gla_w_alpha[0])
    wa = wa.at[GLA_RANK:2 * GLA_RANK, QK:2 * QK].set(gla_w_alpha[1])
    ba = jnp.concatenate([gla_b_alpha[0], gla_b_alpha[1]]).reshape(1, 2 * QK)
    return w_in.astype(BF16), wa.astype(BF16), ba


def kernel(x_prompt, x_sample, c, state_ret, state_gla, c_ctx, w_mod, b_mod, w_in, ret_log_decay,
           gla_w_alpha, gla_b_alpha, gla_norm_w, w_out, final_norm_w):
    assert w_mod.shape[0] == 1, "single-layer model"
    b_ctx, t_ctx, _ = x_prompt.shape
    b_dec, t_dec, _ = x_sample.shape
    assert t_ctx % SCAN_CHUNK == 0 and t_dec % TOKEN_BLOCK == 0 and TOKEN_BLOCK % t_ctx == 0
    assert 1 + b_dec <= MOD_ROWS

    cond = jnp.concatenate(
        [c_ctx[None, :], c, jnp.zeros((MOD_ROWS - 1 - b_dec, D_MODEL), F32)], axis=0)
    mod = _modulation(cond, w_mod[0], b_mod[0]).reshape(MOD_ROWS, 1, 3 * D_MODEL)

    weights = _prep_weights(w_in[0], gla_w_alpha[0], gla_b_alpha[0])
    wout = w_out[0].astype(BF16)
    gw = gla_norm_w[0].reshape(1, DV)
    fw = final_norm_w.reshape(1, D_MODEL)
    ld = ret_log_decay[0]

    per_blk = TOKEN_BLOCK // t_ctx
    xp = x_prompt.reshape(b_ctx // per_blk, TOKEN_BLOCK, D_MODEL)
    ctx_row = lambda b: 0
    qk, gz, vt, lg = _inproj(xp, mod, ctx_row, weights, None)
    qk = qk.reshape(b_ctx, t_ctx, 4 * QK)
    vt = vt.reshape(b_ctx, t_ctx // SCAN_CHUNK, 2 * WV, SCAN_CHUNK)
    lg = lg.reshape(b_ctx, t_ctx, 2 * QK)
    o_r, new_ret = _scan(False, qk, vt, None, ld, None, True, CTX_SEQ_BLOCK, 1)
    o_g, new_gla = _scan(True, qk, vt, lg, None, None, True, CTX_GLA_SEQ_BLOCK, 1)
    y_prompt = _outproj(xp, mod, ctx_row, o_r, o_g, gz, gw, fw, wout)
    y_prompt = y_prompt.reshape(b_ctx, t_ctx, D_MODEL)

    dec_row = lambda b: b + 1
    rope = _rope_tables(t_dec)
    qk, gz, vt, lg = _inproj(x_sample, mod, dec_row, weights, rope)
    (o_r,) = _scan(False, qk, vt, None, ld, state_ret, False, 1, DEC_UNROLL)
    (o_g,) = _scan(True, qk, vt, lg, None, state_gla, False, 1, DEC_UNROLL)
    per_seq = t_dec // OUT_TOKEN_BLOCK
    y_sample = _outproj(x_sample, mod, lambda b: b // per_seq + 1, o_r, o_g, gz, gw, fw, wout)
    y_sample = y_sample.reshape(b_dec, t_dec, D_MODEL)

    return (y_prompt, y_sample, new_ret, new_gla)
```

```python
import functools

import jax
import jax.numpy as jnp
import numpy as np
from jax import lax
from jax.experimental import pallas as pl
from jax.experimental.pallas import tpu as pltpu

F32 = jnp.float32
BF16 = jnp.bfloat16

D_MODEL = 1024
N_HEADS = 4
DK = 64
DV = 128
QK = N_HEADS * DK
WV = N_HEADS * DV
GLA_RANK = 16
GLA_TAU = 16.0
EPS = 1e-6
GRID_W = 64
ROPE_BASE = 10000.0

COL_QK = (0, 2 * QK + 2 * WV)
COL_V = (2 * QK, 4 * QK + 2 * WV)
COL_Z = (2 * QK + WV, 4 * QK + 3 * WV)
COL_LR = 4 * QK + 4 * WV

LANES = 128
SCAN_CHUNK = 128
CTX_SEQ_BLOCK = 8
CTX_GLA_SEQ_BLOCK = 4
DEC_UNROLL = 8
RET_PIPE_GROUP = 2
GLA_PIPE_GROUP = 8
TOKEN_BLOCK = 1024
INPROJ_SUB = 256
OUTPROJ_SUB = 256
MOD_ROWS = 16
MOD_COL_BLOCK = 1536
OUT_TOKEN_BLOCK = 1024
V7X_VMEM_BYTES = 64 * 1024 * 1024
VMEM_LIMIT = 3 * V7X_VMEM_BYTES // 4

_NT = (((1,), (1,)), ((), ()))
_TN = (((0,), (0,)), ((), ()))
_TNT = (((0,), (1,)), ((), ()))


def _dot(a, b):
    return jnp.dot(a, b, preferred_element_type=F32)


def _dot_nt(a, b):
    return lax.dot_general(a, b, _NT, preferred_element_type=F32)


def _silu(x):
    return x * jax.nn.sigmoid(x)


def _mod_kernel(c_ref, w_ref, b_ref, o_ref):
    s = _silu(c_ref[...]).astype(BF16)
    o_ref[...] = _dot(s, w_ref[...].astype(BF16)) + b_ref[...]


def _modulation(cond, w_mod, b_mod):
    n_col = w_mod.shape[1]
    return pl.pallas_call(
        _mod_kernel,
        grid=(n_col // MOD_COL_BLOCK,),
        in_specs=[
            pl.BlockSpec((MOD_ROWS, D_MODEL), lambda j: (0, 0)),
            pl.BlockSpec((D_MODEL, MOD_COL_BLOCK), lambda j: (0, j)),
            pl.BlockSpec((1, MOD_COL_BLOCK), lambda j: (0, j)),
        ],
        out_specs=pl.BlockSpec((MOD_ROWS, MOD_COL_BLOCK), lambda j: (0, j)),
        out_shape=jax.ShapeDtypeStruct((MOD_ROWS, n_col), F32),
        compiler_params=pltpu.CompilerParams(vmem_limit_bytes=VMEM_LIMIT),
        name="modulation",
    )(cond, w_mod, b_mod.reshape(1, n_col))


def _rope_tile(t, cos, sin_signed):
    lane = lax.broadcasted_iota(jnp.int32, t.shape, 1)
    first_half = (lane & 32) == 0
    swapped = jnp.where(first_half, pltpu.roll(t, LANES - 32, 1), pltpu.roll(t, 32, 1))
    return t * cos + swapped * sin_signed


def _inproj_kernel(use_rope, x_ref, mod_ref, w_ref, wa_ref, ba_ref, *rest):
    if use_rope:
        cos_ref, sin_ref, qk_ref, z_ref, vt_ref, lg_ref = rest
    else:
        qk_ref, z_ref, vt_ref, lg_ref = rest
    shift = mod_ref[0, :, 0:D_MODEL]
    scale1 = 1.0 + mod_ref[0, :, D_MODEL:2 * D_MODEL]

    def gate_logs(lr_t, rows):
        logit = lax.dot_general(lr_t, wa_ref[...], _TN, preferred_element_type=F32) + ba_ref[...]
        log_sig = jnp.minimum(logit, 0.0) - jnp.log(1.0 + jnp.exp(-jnp.abs(logit)))
        lg_ref[0, rows, :] = log_sig * (1.0 / GLA_TAU)

    for sub in range(TOKEN_BLOCK // INPROJ_SUB):
        rows = slice(sub * INPROJ_SUB, (sub + 1) * INPROJ_SUB)
        x = x_ref[0, rows, :]
        inv = lax.rsqrt(jnp.mean(x * x, axis=-1, keepdims=True) + EPS)
        hb = ((x * inv) * scale1 + shift).astype(BF16)

        lr_t = lax.dot_general(w_ref[:, COL_LR:COL_LR + 2 * GLA_RANK], hb, _TNT,
                               preferred_element_type=F32).astype(BF16)

        for half in range(2):
            res = _dot(hb, w_ref[:, COL_QK[half]:COL_QK[half] + 2 * QK])
            if use_rope and half == 0:
                for t in range(4):
                    tile = res[:, t * LANES:(t + 1) * LANES]
                    if t >= 2:
                        tile = tile * (DK ** -0.5)
                    tile = _rope_tile(tile, cos_ref[rows, :], sin_ref[rows, :])
                    qk_ref[0, rows, t * LANES:(t + 1) * LANES] = tile.astype(BF16)
            else:
                lane = lax.broadcasted_iota(jnp.int32, (1, 2 * QK), 1)
                scaled = (lane >= QK) if half == 0 else (lane < QK)
                col_scale = jnp.where(scaled, DK ** -0.5, 1.0)
                qk_ref[0, rows, half * 2 * QK:(half + 1) * 2 * QK] = (res * col_scale).astype(BF16)

        gate_logs(lr_t, rows)

        for half in range(2):
            z_ref[0, rows, half * WV:(half + 1) * WV] = _silu(
                _dot(hb, w_ref[:, COL_Z[half]:COL_Z[half] + WV])).astype(BF16)

        for half in range(2):
            vt = lax.dot_general(w_ref[:, COL_V[half]:COL_V[half] + WV], hb, _TNT,
                                 preferred_element_type=F32)
            for j in range(INPROJ_SUB // SCAN_CHUNK):
                jj = sub * (INPROJ_SUB // SCAN_CHUNK) + j
                vt_ref[0, jj, half * WV:(half + 1) * WV, :] = vt[:, j * SCAN_CHUNK:(j + 1) * SCAN_CHUNK].astype(BF16)


def _inproj(x, mod, mod_row_fn, weights, rope):
    B, T, _ = x.shape
    w, wa, ba = weights
    nt = T // TOKEN_BLOCK
    cpb = TOKEN_BLOCK // SCAN_CHUNK
    const = lambda b, t: (0, 0)
    in_specs = [
        pl.BlockSpec((1, TOKEN_BLOCK, D_MODEL), lambda b, t: (b, t, 0)),
        pl.BlockSpec((1, 1, 3 * D_MODEL), lambda b, t: (mod_row_fn(b), 0, 0)),
        pl.BlockSpec(w.shape, const),
        pl.BlockSpec(wa.shape, const),
        pl.BlockSpec(ba.shape, const),
    ]
    args = [x, mod, w, wa, ba]
    if rope is not None:
        in_specs += [pl.BlockSpec((TOKEN_BLOCK, LANES), lambda b, t: (t, 0))] * 2
        args += list(rope)
    out_shape = [
        jax.ShapeDtypeStruct((B, T, 4 * QK), BF16),
        jax.ShapeDtypeStruct((B, T, 2 * WV), BF16),
        jax.ShapeDtypeStruct((B, T // SCAN_CHUNK, 2 * WV, SCAN_CHUNK), BF16),
        jax.ShapeDtypeStruct((B, T, 2 * QK), F32),
    ]
    out_specs = [
        pl.BlockSpec((1, TOKEN_BLOCK, 4 * QK), lambda b, t: (b, t, 0)),
        pl.BlockSpec((1, TOKEN_BLOCK, 2 * WV), lambda b, t: (b, t, 0)),
        pl.BlockSpec((1, cpb, 2 * WV, SCAN_CHUNK), lambda b, t: (b, t, 0, 0)),
        pl.BlockSpec((1, TOKEN_BLOCK, 2 * QK), lambda b, t: (b, t, 0)),
    ]
    return pl.pallas_call(
        functools.partial(_inproj_kernel, rope is not None),
        grid=(B, nt),
        in_specs=in_specs,
        out_specs=out_specs,
        out_shape=out_shape,
        compiler_params=pltpu.CompilerParams(
            dimension_semantics=("parallel", "parallel"), vmem_limit_bytes=VMEM_LIMIT),
        name="inproj_rope" if rope is not None else "inproj",
    )(*args)


def _scan_kernel(is_gla, has_init, want_state, n_chunks, seq_blk, unroll, *refs):
    refs = list(refs)
    q_ref, k_ref, vt_ref = refs[:3]
    pos = 3
    if is_gla:
        lgf_ref, lgb_ref = refs[pos:pos + 2]
        pos += 2
    else:
        ld_ref = refs[pos]
        pos += 1
    if has_init:
        s0_ref = refs[pos]
        pos += 1
    o_ref = refs[pos]
    pos += 1
    if want_state:
        sn_ref = refs[pos]
        pos += 1
    st_ref, oacc_ref = refs[pos:pos + 2]

    C = SCAN_CHUNK
    mid = C // 2
    n_pairs = N_HEADS // 2
    lane = lax.broadcasted_iota(jnp.int32, (1, LANES), 1)
    head_mask = (lane < DK, lane >= DK)
    ri = lax.broadcasted_iota(jnp.int32, (C, C), 0)
    ci = lax.broadcasted_iota(jnp.int32, (C, C), 1)
    tpos = lax.broadcasted_iota(jnp.int32, (C, 1), 0).astype(F32)
    tri = ((ri >= ci), (ci >= ri))

    if has_init:
        for s in range(seq_blk):
            for p in range(n_pairs):
                for d in range(2):
                    s0 = s0_ref[s, 0, d]
                    both = jnp.concatenate([s0[2 * p], s0[2 * p + 1]], axis=0).T
                    st_ref[s, p, d, 0:DV] = both
                    st_ref[s, p, d, DV:2 * DV] = both
    else:
        st_ref[...] = jnp.zeros(st_ref.shape, F32)

    if is_gla:
        tri_ones = [t.astype(BF16) for t in tri]
    else:
        q_mul, k_mul, ret_decay, score_mul = {}, {}, {}, {}
        for p in range(n_pairs):
            for d in range(2):
                ld0 = ld_ref[d, 2 * p]
                ld1 = ld_ref[d, 2 * p + 1]
                ldr = jnp.where(lane < DK, ld0, ld1)
                if d == 0:
                    q_mul[p, d] = jnp.exp(ldr * (tpos + 1.0))
                    k_mul[p, d] = jnp.exp(ldr * (C - 1.0 - tpos))
                    dist = (ri - ci).astype(F32)
                else:
                    q_mul[p, d] = jnp.exp(ldr * (C - tpos))
                    k_mul[p, d] = jnp.exp(ldr * tpos)
                    dist = (ci - ri).astype(F32)
                ret_decay[p, d] = jnp.exp(ldr * float(C))
                score_mul[p, d] = [jnp.where(tri[d], jnp.exp(ldh * dist), 0.0) for ldh in (ld0, ld1)]
            score_mul[p] = [score_mul[p, 0][hh] + score_mul[p, 1][hh] for hh in range(2)]

    def has_intra(d):
        return is_gla or d == 0

    zero = jnp.zeros((), BF16)
    groups = [(s, p, d) for s in range(seq_blk) for p in range(n_pairs) for d in range(2)]

    def run_block(n0, accumulate):
        keys = [g + (u,) for u in range(unroll) for g in groups]
        chains = {}

        def stage_load(key):
            s, p, d, u = key
            ptile = slice(p * LANES, (p + 1) * LANES)
            n = n0 + u
            c = n if d == 0 else n_chunks - 1 - n
            t = dict(c=c, rows=pl.ds(pl.multiple_of(c * C, C), C))
            t["q"] = q_ref[s, t["rows"], ptile]
            t["k"] = k_ref[s, t["rows"], ptile]
            if is_gla:
                lg = (lgf_ref if d == 0 else lgb_ref)[s, t["rows"], ptile]
                hi = lg.astype(BF16)
                lo = (lg - hi.astype(F32)).astype(BF16)
                bb = _dot(tri_ones[d], jnp.concatenate([hi, lo], axis=1))
                t["b"] = bb[:, :LANES] + bb[:, LANES:]
            chains[key] = t

        def stage_scores(key):
            s, p, d, u = key
            t = chains[key]
            q, k = t.pop("q"), t.pop("k")
            if is_gla:
                b = t.pop("b")
                r = b[mid - 1:mid] if d == 0 else b[mid:mid + 1]
                bl = b[C - 1:C] if d == 0 else b[0:1]
                qs = q.astype(F32) * jnp.exp(b - r)
                ks = k.astype(F32) * jnp.exp(r - b)
                q_sc = qs.astype(BF16)
                k_sc = ks.astype(BF16)
                q_dec = (qs * jnp.exp(r)).astype(BF16)
                t["k_dec"] = (ks * jnp.exp(bl - r)).astype(BF16)
                t["decay"] = jnp.exp(bl)
            else:
                q_sc = q
                k_sc = k
                q_dec = (q.astype(F32) * q_mul[p, d]).astype(BF16)
                t["k_dec"] = (k.astype(F32) * k_mul[p, d]).astype(BF16)
                t["decay"] = ret_decay[p, d]
            t["q_dec"] = [jnp.where(head_mask[hh], q_dec, zero) for hh in range(2)]
            if has_intra(d):
                k_heads = jnp.concatenate([jnp.where(head_mask[hh], k_sc, zero) for hh in range(2)], axis=0)
                t["sc"] = _dot_nt(q_sc, k_heads)
            if is_gla:
                stage_kv_dot(key)

        def stage_kv_dot(key):
            s, p, d, u = key
            t = chains[key]
            t["vt"] = vt_ref[s, t["c"], p * 2 * DV:(p + 1) * 2 * DV, :]
            t["kv"] = _dot(t["vt"], t.pop("k_dec"))

        def stage_kv(key):
            s, p, d, u = key
            t = chains[key]
            if not is_gla:
                stage_kv_dot(key)
            t["scb"] = []
            if has_intra(d):
                sc_all = t.pop("sc")
                for hh in range(2):
                    sc = sc_all[:, hh * C:(hh + 1) * C]
                    if is_gla:
                        sc = jnp.where(tri[d], sc, 0.0)
                    else:
                        sc = sc * score_mul[p][hh]
                    t["scb"].append(sc.astype(BF16))

        def stage_out(key):
            s, p, d, u = key
            t = chains[key]
            st = st_ref[s, p, d]
            stb = st.astype(BF16)
            t["o"] = []
            for hh in range(2):
                hrows = slice(hh * DV, (hh + 1) * DV)
                if has_intra(d):
                    lhs = jnp.concatenate([t["scb"][hh], t["q_dec"][hh]], axis=1)
                    rhs = jnp.concatenate([t["vt"][hrows], stb[hrows]], axis=1)
                else:
                    lhs, rhs = t["q_dec"][hh], stb[hrows]
                t["o"].append(_dot_nt(lhs, rhs))
            st_ref[s, p, d] = st * t.pop("decay") + t.pop("kv")

        def stage_store(key):
            s, p, d, u = key
            t = chains.pop(key)
            for hh in range(2):
                cols = slice((2 * p + hh) * DV, (2 * p + hh + 1) * DV)
                if accumulate:
                    o_ref[s, t["rows"], cols] = (oacc_ref[s, t["rows"], cols] + t["o"][hh]).astype(BF16)
                else:
                    oacc_ref[s, t["rows"], cols] = t["o"][hh]

        stages = (stage_load, stage_scores, stage_kv, stage_out, stage_store)
        grp = GLA_PIPE_GROUP if is_gla else RET_PIPE_GROUP
        key_groups = [keys[i:i + grp] for i in range(0, len(keys), grp)]
        for step in range(len(key_groups) + len(stages) - 1):
            for k in reversed(range(len(stages))):
                if 0 <= step - k < len(key_groups):
                    for key in key_groups[step - k]:
                        stages[k](key)

    def loop_body(accumulate, base):
        def body(m, carry):
            run_block(base + m * unroll, accumulate)
            return carry
        return body

    half = n_chunks // 2
    assert half % unroll == 0
    lax.fori_loop(0, half // unroll, loop_body(False, 0), 0)
    lax.fori_loop(0, half // unroll, loop_body(True, half), 0)

    if want_state:
        for s, p, d in groups:
            for hh in range(2):
                sn_ref[s, 0, d, 2 * p + hh] = st_ref[s, p, d, hh * DV:(hh + 1) * DV].T[hh * DK:(hh + 1) * DK, :]


def _scan(is_gla, qk, vt, lg, ld, s0, want_state, seq_blk, unroll):
    B, T, _ = qk.shape
    n_chunks = T // SCAN_CHUNK
    assert n_chunks % 2 == 0 and B % seq_blk == 0
    sb = seq_blk
    grp = 1 if is_gla else 0
    in_specs = [
        pl.BlockSpec((sb, T, QK), lambda b: (b, 0, 2 * grp)),
        pl.BlockSpec((sb, T, QK), lambda b: (b, 0, 2 * grp + 1)),
        pl.BlockSpec((sb, n_chunks, WV, SCAN_CHUNK), lambda b: (b, 0, grp, 0)),
    ]
    args = [qk, qk, vt]
    if is_gla:
        in_specs += [
            pl.BlockSpec((sb, T, QK), lambda b: (b, 0, 0)),
            pl.BlockSpec((sb, T, QK), lambda b: (b, 0, 1)),
        ]
        args += [lg, lg]
    else:
        in_specs += [pl.BlockSpec(memory_space=pltpu.SMEM)]
        args += [ld]
    state_spec = pl.BlockSpec((sb, 1, 2, N_HEADS, DK, DV), lambda b: (b, 0, 0, 0, 0, 0))
    if s0 is not None:
        in_specs += [state_spec]
        args += [s0]
    out_shape = [jax.ShapeDtypeStruct((B, T, WV), BF16)]
    out_specs = [pl.BlockSpec((sb, T, WV), lambda b: (b, 0, 0))]
    if want_state:
        out_shape += [jax.ShapeDtypeStruct((B, 1, 2, N_HEADS, DK, DV), F32)]
        out_specs += [state_spec]
    name = ("gla" if is_gla else "ret") + ("_init" if s0 is not None else "") + "_scan"
    return pl.pallas_call(
        functools.partial(_scan_kernel, is_gla, s0 is not None, want_state, n_chunks, sb, unroll),
        grid=(B // sb,),
        in_specs=in_specs,
        out_specs=out_specs,
        out_shape=out_shape,
        scratch_shapes=[pltpu.VMEM((sb, N_HEADS // 2, 2, 2 * DV, LANES), F32),
                        pltpu.VMEM((sb, T, WV), F32)],
        compiler_params=pltpu.CompilerParams(
            dimension_semantics=("arbitrary",), vmem_limit_bytes=VMEM_LIMIT),
        name=name,
    )(*args)


def _outproj_rows(rows, x_ref, mod_ref, or_ref, og_ref, gz_ref, gw_ref, fw_ref, wout_ref, y_ref):
    gate = mod_ref[0, :, 2 * D_MODEL:3 * D_MODEL]
    gw = gw_ref[...]
    parts = []
    for h in range(N_HEADS):
        cols = slice(h * DV, (h + 1) * DV)
        t = or_ref[0, rows, cols].astype(F32)
        mu = jnp.mean(t, axis=-1, keepdims=True)
        dlt = t - mu
        var = jnp.mean(dlt * dlt, axis=-1, keepdims=True)
        n = dlt * lax.rsqrt(var + EPS)
        parts.append((n * gz_ref[0, rows, cols].astype(F32)).astype(BF16))
    for h in range(N_HEADS):
        cols = slice(h * DV, (h + 1) * DV)
        t = og_ref[0, rows, cols].astype(F32)
        n = t * lax.rsqrt(jnp.mean(t * t, axis=-1, keepdims=True) + EPS) * gw
        gcols = slice(WV + h * DV, WV + (h + 1) * DV)
        parts.append((n * gz_ref[0, rows, gcols].astype(F32)).astype(BF16))
    mixed = jnp.concatenate(parts, axis=-1)
    out = _dot(mixed, wout_ref[...])
    y = x_ref[0, rows, :] + gate * out
    yn = y * lax.rsqrt(jnp.mean(y * y, axis=-1, keepdims=True) + EPS)
    y_ref[0, rows, :] = yn * fw_ref[...]


def _outproj_kernel(*refs):
    for sub in range(OUT_TOKEN_BLOCK // OUTPROJ_SUB):
        _outproj_rows(slice(sub * OUTPROJ_SUB, (sub + 1) * OUTPROJ_SUB), *refs)


def _outproj(x, mod, mod_row_fn, o_r, o_g, gz, gw, fw, wout):
    n_tok = x.shape[0] * x.shape[1]
    B = n_tok // OUT_TOKEN_BLOCK
    x, o_r, o_g, gz = (a.reshape(B, OUT_TOKEN_BLOCK, a.shape[-1]) for a in (x, o_r, o_g, gz))
    const = lambda b: (0, 0)
    blk = lambda width: pl.BlockSpec((1, OUT_TOKEN_BLOCK, width), lambda b: (b, 0, 0))
    return pl.pallas_call(
        _outproj_kernel,
        grid=(B,),
        in_specs=[
            blk(D_MODEL),
            pl.BlockSpec((1, 1, 3 * D_MODEL), lambda b: (mod_row_fn(b), 0, 0)),
            blk(WV),
            blk(WV),
            blk(2 * WV),
            pl.BlockSpec((1, DV), const),
            pl.BlockSpec((1, D_MODEL), const),
            pl.BlockSpec(wout.shape, const),
        ],
        out_specs=blk(D_MODEL),
        out_shape=jax.ShapeDtypeStruct((B, OUT_TOKEN_BLOCK, D_MODEL), F32),
        compiler_params=pltpu.CompilerParams(
            dimension_semantics=("parallel",), vmem_limit_bytes=VMEM_LIMIT),
        name="outproj",
    )(x, mod, o_r, o_g, gz, gw, fw, wout)


def _rope_tables(n_tokens):
    rows = n_tokens // GRID_W
    rr, cc = np.meshgrid(np.arange(rows), np.arange(GRID_W), indexing="ij")
    rr = rr.reshape(-1).astype(np.float32)
    cc = cc.reshape(-1).astype(np.float32)
    n_freq = DK // 4
    inv = np.float32(ROPE_BASE) ** (-np.arange(n_freq, dtype=np.float32) / np.float32(n_freq))
    ang = np.concatenate([rr[:, None] * inv, cc[:, None] * inv], axis=-1)
    cos = np.cos(ang.astype(np.float64)).astype(np.float32)
    sin = np.sin(ang.astype(np.float64)).astype(np.float32)
    cos_t = np.tile(np.concatenate([cos, cos], axis=-1), (1, LANES // DK))
    sin_t = np.tile(np.concatenate([-sin, sin], axis=-1), (1, LANES // DK))
    return jnp.asarray(cos_t), jnp.asarray(sin_t)


def _prep_weights(w_in, gla_w_alpha, gla_b_alpha):
    assert w_in.shape == (D_MODEL, COL_LR + 2 * GLA_RANK)
    wa = jnp.zeros((2 * GLA_RANK, 2 * QK), F32)
    wa = wa.at[0:GLA_RANK, 0:QK].set(gla_w_alpha[0])
    wa = wa.at[GLA_RANK:2 * GLA_RANK, QK:2 * QK].set(gla_w_alpha[1])
    ba = jnp.concatenate([gla_b_alpha[0], gla_b_alpha[1]]).reshape(1, 2 * QK)
    return w_in.astype(BF16), wa.astype(BF16), ba


def kernel(x_prompt, x_sample, c, state_ret, state_gla, c_ctx, w_mod, b_mod, w_in, ret_log_decay,
           gla_w_alpha, gla_b_alpha, gla_norm_w, w_out, final_norm_w):
    assert w_mod.shape[0] == 1, "single-layer model"
    b_ctx, t_ctx, _ = x_prompt.shape
    b_dec, t_dec, _ = x_sample.shape
    assert t_ctx % SCAN_CHUNK == 0 and t_dec % TOKEN_BLOCK == 0 and TOKEN_BLOCK % t_ctx == 0
    assert 1 + b_dec <= MOD_ROWS

    cond = jnp.concatenate(
        [c_ctx[None, :], c, jnp.zeros((MOD_ROWS - 1 - b_dec, D_MODEL), F32)], axis=0)
    mod = _modulation(cond, w_mod[0], b_mod[0]).reshape(MOD_ROWS, 1, 3 * D_MODEL)

    weights = _prep_weights(w_in[0], gla_w_alpha[0], gla_b_alpha[0])
    wout = w_out[0].astype(BF16)
    gw = gla_norm_w[0].reshape(1, DV)
    fw = final_norm_w.reshape(1, D_MODEL)
    ld = ret_log_decay[0]

    per_blk = TOKEN_BLOCK // t_ctx
    xp = x_prompt.reshape(b_ctx // per_blk, TOKEN_BLOCK, D_MODEL)
    ctx_row = lambda b: 0
    qk, gz, vt, lg = _inproj(xp, mod, ctx_row, weights, None)
    qk = qk.reshape(b_ctx, t_ctx, 4 * QK)
    vt = vt.reshape(b_ctx, t_ctx // SCAN_CHUNK, 2 * WV, SCAN_CHUNK)
    lg = lg.reshape(b_ctx, t_ctx, 2 * QK)
    o_r, new_ret = _scan(False, qk, vt, None, ld, None, True, CTX_SEQ_BLOCK, 1)
    o_g, new_gla = _scan(True, qk, vt, lg, None, None, True, CTX_GLA_SEQ_BLOCK, 1)
    y_prompt = _outproj(xp, mod, ctx_row, o_r, o_g, gz, gw, fw, wout)
    y_prompt = y_prompt.reshape(b_ctx, t_ctx, D_MODEL)

    dec_row = lambda b: b + 1
    rope = _rope_tables(t_dec)
    qk, gz, vt, lg = _inproj(x_sample, mod, dec_row, weights, rope)
    (o_r,) = _scan(False, qk, vt, None, ld, state_ret, False, 1, DEC_UNROLL)
    (o_g,) = _scan(True, qk, vt, lg, None, state_gla, False, 1, DEC_UNROLL)
    per_seq = t_dec // OUT_TOKEN_BLOCK
    y_sample = _outproj(x_sample, mod, lambda b: b // per_seq + 1, o_r, o_g, gz, gw, fw, wout)
    y_sample = y_sample.reshape(b_dec, t_dec, D_MODEL)

    return (y_prompt, y_sample, new_ret, new_gla)
```

```python
import functools

import jax
import jax.numpy as jnp
import numpy as np
from jax import lax
from jax.experimental import pallas as pl
from jax.experimental.pallas import tpu as pltpu

F32 = jnp.float32
BF16 = jnp.bfloat16

D_MODEL = 1024
N_HEADS = 4
DK = 64
DV = 128
QK = N_HEADS * DK
WV = N_HEADS * DV
GLA_RANK = 16
GLA_TAU = 16.0
EPS = 1e-6
GRID_W = 64
ROPE_BASE = 10000.0

COL_QK = (0, 2 * QK + 2 * WV)
COL_V = (2 * QK, 4 * QK + 2 * WV)
COL_Z = (2 * QK + WV, 4 * QK + 3 * WV)
COL_LR = 4 * QK + 4 * WV

LANES = 128
SCAN_CHUNK = 128
CTX_SEQ_BLOCK = 8
CTX_GLA_SEQ_BLOCK = 4
DEC_UNROLL = 8
RET_PIPE_GROUP = 2
GLA_PIPE_GROUP = 8
TOKEN_BLOCK = 1024
INPROJ_SUB = 256
OUTPROJ_SUB = 256
MOD_ROWS = 16
MOD_COL_BLOCK = 1536
OUT_TOKEN_BLOCK = 1024
V7X_VMEM_BYTES = 64 * 1024 * 1024
VMEM_LIMIT = 3 * V7X_VMEM_BYTES // 4

_NT = (((1,), (1,)), ((), ()))
_TN = (((0,), (0,)), ((), ()))
_TNT = (((0,), (1,)), ((), ()))


def _dot(a, b):
    return jnp.dot(a, b, preferred_element_type=F32)


def _dot_nt(a, b):
    return lax.dot_general(a, b, _NT, preferred_element_type=F32)


def _silu(x):
    return x * jax.nn.sigmoid(x)


def _mod_kernel(c_ref, w_ref, b_ref, o_ref):
    s = _silu(c_ref[...]).astype(BF16)
    o_ref[...] = _dot(s, w_ref[...].astype(BF16)) + b_ref[...]


def _modulation(cond, w_mod, b_mod):
    n_col = w_mod.shape[1]
    return pl.pallas_call(
        _mod_kernel,
        grid=(n_col // MOD_COL_BLOCK,),
        in_specs=[
            pl.BlockSpec((MOD_ROWS, D_MODEL), lambda j: (0, 0)),
            pl.BlockSpec((D_MODEL, MOD_COL_BLOCK), lambda j: (0, j)),
            pl.BlockSpec((1, MOD_COL_BLOCK), lambda j: (0, j)),
        ],
        out_specs=pl.BlockSpec((MOD_ROWS, MOD_COL_BLOCK), lambda j: (0, j)),
        out_shape=jax.ShapeDtypeStruct((MOD_ROWS, n_col), F32),
        compiler_params=pltpu.CompilerParams(vmem_limit_bytes=VMEM_LIMIT),
        name="modulation",
    )(cond, w_mod, b_mod.reshape(1, n_col))


def _rope_tile(t, cos, sin_signed):
    lane = lax.broadcasted_iota(jnp.int32, t.shape, 1)
    first_half = (lane & 32) == 0
    swapped = jnp.where(first_half, pltpu.roll(t, LANES - 32, 1), pltpu.roll(t, 32, 1))
    return t * cos + swapped * sin_signed


def _inproj_kernel(use_rope, x_ref, mod_ref, w_ref, wa_ref, ba_ref, *rest):
    if use_rope:
        cos_ref, sin_ref, qk_ref, z_ref, vt_ref, lg_ref = rest
    else:
        qk_ref, z_ref, vt_ref, lg_ref = rest
    shift = mod_ref[0, :, 0:D_MODEL]
    scale1 = 1.0 + mod_ref[0, :, D_MODEL:2 * D_MODEL]

    def gate_logs(lr_t, rows):
        logit = lax.dot_general(lr_t, wa_ref[...], _TN, preferred_element_type=F32) + ba_ref[...]
        log_sig = jnp.minimum(logit, 0.0) - jnp.log(1.0 + jnp.exp(-jnp.abs(logit)))
        lg_ref[0, rows, :] = log_sig * (1.0 / GLA_TAU)

    for sub in range(TOKEN_BLOCK // INPROJ_SUB):
        rows = slice(sub * INPROJ_SUB, (sub + 1) * INPROJ_SUB)
        x = x_ref[0, rows, :]
        inv = lax.rsqrt(jnp.mean(x * x, axis=-1, keepdims=True) + EPS)
        hb = ((x * inv) * scale1 + shift).astype(BF16)

        lr_t = lax.dot_general(w_ref[:, COL_LR:COL_LR + 2 * GLA_RANK], hb, _TNT,
                               preferred_element_type=F32).astype(BF16)

        for half in range(2):
            res = _dot(hb, w_ref[:, COL_QK[half]:COL_QK[half] + 2 * QK])
            if use_rope and half == 0:
                for t in range(4):
                    tile = res[:, t * LANES:(t + 1) * LANES]
                    if t >= 2:
                        tile = tile * (DK ** -0.5)
                    tile = _rope_tile(tile, cos_ref[rows, :], sin_ref[rows, :])
                    qk_ref[0, rows, t * LANES:(t + 1) * LANES] = tile.astype(BF16)
            else:
                lane = lax.broadcasted_iota(jnp.int32, (1, 2 * QK), 1)
                scaled = (lane >= QK) if half == 0 else (lane < QK)
                col_scale = jnp.where(scaled, DK ** -0.5, 1.0)
                qk_ref[0, rows, half * 2 * QK:(half + 1) * 2 * QK] = (res * col_scale).astype(BF16)

        gate_logs(lr_t, rows)

        for half in range(2):
            z_ref[0, rows, half * WV:(half + 1) * WV] = _silu(
                _dot(hb, w_ref[:, COL_Z[half]:COL_Z[half] + WV])).astype(BF16)

        for half in range(2):
            vt = lax.dot_general(w_ref[:, COL_V[half]:COL_V[half] + WV], hb, _TNT,
                                 preferred_element_type=F32)
            for j in range(INPROJ_SUB // SCAN_CHUNK):
                jj = sub * (INPROJ_SUB // SCAN_CHUNK) + j
                vt_ref[0, jj, half * WV:(half + 1) * WV, :] = vt[:, j * SCAN_CHUNK:(j + 1) * SCAN_CHUNK].astype(BF16)


def _inproj(x, mod, mod_row_fn, weights, rope):
    B, T, _ = x.shape
    w, wa, ba = weights
    nt = T // TOKEN_BLOCK
    cpb = TOKEN_BLOCK // SCAN_CHUNK
    const = lambda b, t: (0, 0)
    in_specs = [
        pl.BlockSpec((1, TOKEN_BLOCK, D_MODEL), lambda b, t: (b, t, 0)),
        pl.BlockSpec((1, 1, 3 * D_MODEL), lambda b, t: (mod_row_fn(b), 0, 0)),
        pl.BlockSpec(w.shape, const),
        pl.BlockSpec(wa.shape, const),
        pl.BlockSpec(ba.shape, const),
    ]
    args = [x, mod, w, wa, ba]
    if rope is not None:
        in_specs += [pl.BlockSpec((TOKEN_BLOCK, LANES), lambda b, t: (t, 0))] * 2
        args += list(rope)
    out_shape = [
        jax.ShapeDtypeStruct((B, T, 4 * QK), BF16),
        jax.ShapeDtypeStruct((B, T, 2 * WV), BF16),
        jax.ShapeDtypeStruct((B, T // SCAN_CHUNK, 2 * WV, SCAN_CHUNK), BF16),
        jax.ShapeDtypeStruct((B, T, 2 * QK), F32),
    ]
    out_specs = [
        pl.BlockSpec((1, TOKEN_BLOCK, 4 * QK), lambda b, t: (b, t, 0)),
        pl.BlockSpec((1, TOKEN_BLOCK, 2 * WV), lambda b, t: (b, t, 0)),
        pl.BlockSpec((1, cpb, 2 * WV, SCAN_CHUNK), lambda b, t: (b, t, 0, 0)),
        pl.BlockSpec((1, TOKEN_BLOCK, 2 * QK), lambda b, t: (b, t, 0)),
    ]
    return pl.pallas_call(
        functools.partial(_inproj_kernel, rope is not None),
        grid=(B, nt),
        in_specs=in_specs,
        out_specs=out_specs,
        out_shape=out_shape,
        compiler_params=pltpu.CompilerParams(
            dimension_semantics=("parallel", "parallel"), vmem_limit_bytes=VMEM_LIMIT,
            allow_input_fusion=[i == 2 for i in range(len(args))]),
        name="inproj_rope" if rope is not None else "inproj",
    )(*args)


def _scan_kernel(is_gla, has_init, want_state, n_chunks, seq_blk, unroll, *refs):
    refs = list(refs)
    q_ref, k_ref, vt_ref = refs[:3]
    pos = 3
    if is_gla:
        lgf_ref, lgb_ref = refs[pos:pos + 2]
        pos += 2
    else:
        ld_ref = refs[pos]
        pos += 1
    if has_init:
        s0_ref = refs[pos]
        pos += 1
    o_ref = refs[pos]
    pos += 1
    if want_state:
        sn_ref = refs[pos]
        pos += 1
    st_ref, oacc_ref = refs[pos:pos + 2]

    C = SCAN_CHUNK
    mid = C // 2
    n_pairs = N_HEADS // 2
    lane = lax.broadcasted_iota(jnp.int32, (1, LANES), 1)
    head_mask = (lane < DK, lane >= DK)
    ri = lax.broadcasted_iota(jnp.int32, (C, C), 0)
    ci = lax.broadcasted_iota(jnp.int32, (C, C), 1)
    tpos = lax.broadcasted_iota(jnp.int32, (C, 1), 0).astype(F32)
    tri = ((ri >= ci), (ci >= ri))

    if has_init:
        for s in range(seq_blk):
            for p in range(n_pairs):
                for d in range(2):
                    s0 = s0_ref[s, 0, d]
                    both = jnp.concatenate([s0[2 * p], s0[2 * p + 1]], axis=0).T
                    st_ref[s, p, d, 0:DV] = both
                    st_ref[s, p, d, DV:2 * DV] = both
    else:
        st_ref[...] = jnp.zeros(st_ref.shape, F32)

    if is_gla:
        tri_ones = [t.astype(BF16) for t in tri]
    else:
        q_mul, k_mul, ret_decay, score_mul = {}, {}, {}, {}
        for p in range(n_pairs):
            for d in range(2):
                ld0 = ld_ref[d, 2 * p]
                ld1 = ld_ref[d, 2 * p + 1]
                ldr = jnp.where(lane < DK, ld0, ld1)
                if d == 0:
                    q_mul[p, d] = jnp.exp(ldr * (tpos + 1.0))
                    k_mul[p, d] = jnp.exp(ldr * (C - 1.0 - tpos))
                    dist = (ri - ci).astype(F32)
                else:
                    q_mul[p, d] = jnp.exp(ldr * (C - tpos))
                    k_mul[p, d] = jnp.exp(ldr * tpos)
                    dist = (ci - ri).astype(F32)
                ret_decay[p, d] = jnp.exp(ldr * float(C))
                score_mul[p, d] = [jnp.where(tri[d], jnp.exp(ldh * dist), 0.0) for ldh in (ld0, ld1)]
            score_mul[p] = [score_mul[p, 0][hh] + score_mul[p, 1][hh] for hh in range(2)]

    def has_intra(d):
        return is_gla or d == 0

    zero = jnp.zeros((), BF16)
    groups = [(s, p, d) for s in range(seq_blk) for p in range(n_pairs) for d in range(2)]

    def run_block(n0, accumulate):
        keys = [g + (u,) for u in range(unroll) for g in groups]
        chains = {}

        def stage_load(key):
            s, p, d, u = key
            ptile = slice(p * LANES, (p + 1) * LANES)
            n = n0 + u
            c = n if d == 0 else n_chunks - 1 - n
            t = dict(c=c, rows=pl.ds(pl.multiple_of(c * C, C), C))
            t["q"] = q_ref[s, t["rows"], ptile]
            t["k"] = k_ref[s, t["rows"], ptile]
            if is_gla:
                lg = (lgf_ref if d == 0 else lgb_ref)[s, t["rows"], ptile]
                hi = lg.astype(BF16)
                lo = (lg - hi.astype(F32)).astype(BF16)
                bb = _dot(tri_ones[d], jnp.concatenate([hi, lo], axis=1))
                t["b"] = bb[:, :LANES] + bb[:, LANES:]
            chains[key] = t

        def stage_scores(key):
            s, p, d, u = key
            t = chains[key]
            q, k = t.pop("q"), t.pop("k")
            if is_gla:
                b = t.pop("b")
                r = b[mid - 1:mid] if d == 0 else b[mid:mid + 1]
                bl = b[C - 1:C] if d == 0 else b[0:1]
                qs = q.astype(F32) * jnp.exp(b - r)
                ks = k.astype(F32) * jnp.exp(r - b)
                q_sc = qs.astype(BF16)
                k_sc = ks.astype(BF16)
                q_dec = (qs * jnp.exp(r)).astype(BF16)
                t["k_dec"] = (ks * jnp.exp(bl - r)).astype(BF16)
                t["decay"] = jnp.exp(bl)
            else:
                q_sc = q
                k_sc = k
                q_dec = (q.astype(F32) * q_mul[p, d]).astype(BF16)
                t["k_dec"] = (k.astype(F32) * k_mul[p, d]).astype(BF16)
                t["decay"] = ret_decay[p, d]
            t["q_dec"] = [jnp.where(head_mask[hh], q_dec, zero) for hh in range(2)]
            if has_intra(d):
                k_heads = jnp.concatenate([jnp.where(head_mask[hh], k_sc, zero) for hh in range(2)], axis=0)
                t["sc"] = _dot_nt(q_sc, k_heads)

        def stage_kv(key):
            s, p, d, u = key
            t = chains[key]
            t["vt"] = vt_ref[s, t["c"], p * 2 * DV:(p + 1) * 2 * DV, :]
            t["kv"] = _dot(t["vt"], t.pop("k_dec"))
            t["scb"] = []
            if has_intra(d):
                sc_all = t.pop("sc")
                for hh in range(2):
                    sc = sc_all[:, hh * C:(hh + 1) * C]
                    if is_gla:
                        sc = jnp.where(tri[d], sc, 0.0)
                    else:
                        sc = sc * score_mul[p][hh]
                    t["scb"].append(sc.astype(BF16))

        def stage_out(key):
            s, p, d, u = key
            t = chains[key]
            st = st_ref[s, p, d]
            stb = st.astype(BF16)
            t["o"] = []
            for hh in range(2):
                hrows = slice(hh * DV, (hh + 1) * DV)
                if has_intra(d):
                    lhs = jnp.concatenate([t["scb"][hh], t["q_dec"][hh]], axis=1)
                    rhs = jnp.concatenate([t["vt"][hrows], stb[hrows]], axis=1)
                else:
                    lhs, rhs = t["q_dec"][hh], stb[hrows]
                t["o"].append(_dot_nt(lhs, rhs))
            st_ref[s, p, d] = st * t.pop("decay") + t.pop("kv")

        def stage_store(key):
            s, p, d, u = key
            t = chains.pop(key)
            for hh in range(2):
                cols = slice((2 * p + hh) * DV, (2 * p + hh + 1) * DV)
                if accumulate:
                    o_ref[s, t["rows"], cols] = (oacc_ref[s, t["rows"], cols] + t["o"][hh]).astype(BF16)
                else:
                    oacc_ref[s, t["rows"], cols] = t["o"][hh]

        stages = (stage_load, stage_scores, stage_kv, stage_out, stage_store)
        grp = GLA_PIPE_GROUP if is_gla else RET_PIPE_GROUP
        key_groups = [keys[i:i + grp] for i in range(0, len(keys), grp)]
        for step in range(len(key_groups) + len(stages) - 1):
            for k in reversed(range(len(stages))):
                if 0 <= step - k < len(key_groups):
                    for key in key_groups[step - k]:
                        stages[k](key)

    def loop_body(accumulate, base):
        def body(m, carry):
            run_block(base + m * unroll, accumulate)
            return carry
        return body

    half = n_chunks // 2
    assert half % unroll == 0
    lax.fori_loop(0, half // unroll, loop_body(False, 0), 0)
    lax.fori_loop(0, half // unroll, loop_body(True, half), 0)

    if want_state:
        for s, p, d in groups:
            for hh in range(2):
                sn_ref[s, 0, d, 2 * p + hh] = st_ref[s, p, d, hh * DV:(hh + 1) * DV].T[hh * DK:(hh + 1) * DK, :]


def _scan(is_gla, qk, vt, lg, ld, s0, want_state, seq_blk, unroll):
    B, T, _ = qk.shape
    n_chunks = T // SCAN_CHUNK
    assert n_chunks % 2 == 0 and B % seq_blk == 0
    sb = seq_blk
    grp = 1 if is_gla else 0
    in_specs = [
        pl.BlockSpec((sb, T, QK), lambda b: (b, 0, 2 * grp)),
        pl.BlockSpec((sb, T, QK), lambda b: (b, 0, 2 * grp + 1)),
        pl.BlockSpec((sb, n_chunks, WV, SCAN_CHUNK), lambda b: (b, 0, grp, 0)),
    ]
    args = [qk, qk, vt]
    if is_gla:
        in_specs += [
            pl.BlockSpec((sb, T, QK), lambda b: (b, 0, 0)),
            pl.BlockSpec((sb, T, QK), lambda b: (b, 0, 1)),
        ]
        args += [lg, lg]
    else:
        in_specs += [pl.BlockSpec(memory_space=pltpu.SMEM)]
        args += [ld]
    state_spec = pl.BlockSpec((sb, 1, 2, N_HEADS, DK, DV), lambda b: (b, 0, 0, 0, 0, 0))
    if s0 is not None:
        in_specs += [state_spec]
        args += [s0]
    out_shape = [jax.ShapeDtypeStruct((B, T, WV), BF16)]
    out_specs = [pl.BlockSpec((sb, T, WV), lambda b: (b, 0, 0))]
    if want_state:
        out_shape += [jax.ShapeDtypeStruct((B, 1, 2, N_HEADS, DK, DV), F32)]
        out_specs += [state_spec]
    name = ("gla" if is_gla else "ret") + ("_init" if s0 is not None else "") + "_scan"
    return pl.pallas_call(
        functools.partial(_scan_kernel, is_gla, s0 is not None, want_state, n_chunks, sb, unroll),
        grid=(B // sb,),
        in_specs=in_specs,
        out_specs=out_specs,
        out_shape=out_shape,
        scratch_shapes=[pltpu.VMEM((sb, N_HEADS // 2, 2, 2 * DV, LANES), F32),
                        pltpu.VMEM((sb, T, WV), F32)],
        compiler_params=pltpu.CompilerParams(
            dimension_semantics=("arbitrary",), vmem_limit_bytes=VMEM_LIMIT),
        name=name,
    )(*args)


def _outproj_rows(rows, x_ref, mod_ref, or_ref, og_ref, gz_ref, gw_ref, fw_ref, wout_ref, y_ref):
    gate = mod_ref[0, :, 2 * D_MODEL:3 * D_MODEL]
    gw = gw_ref[...]
    parts = []
    for h in range(N_HEADS):
        cols = slice(h * DV, (h + 1) * DV)
        t = or_ref[0, rows, cols].astype(F32)
        mu = jnp.mean(t, axis=-1, keepdims=True)
        dlt = t - mu
        var = jnp.mean(dlt * dlt, axis=-1, keepdims=True)
        n = dlt * lax.rsqrt(var + EPS)
        parts.append((n * gz_ref[0, rows, cols].astype(F32)).astype(BF16))
    for h in range(N_HEADS):
        cols = slice(h * DV, (h + 1) * DV)
        t = og_ref[0, rows, cols].astype(F32)
        n = t * lax.rsqrt(jnp.mean(t * t, axis=-1, keepdims=True) + EPS) * gw
        gcols = slice(WV + h * DV, WV + (h + 1) * DV)
        parts.append((n * gz_ref[0, rows, gcols].astype(F32)).astype(BF16))
    mixed = jnp.concatenate(parts, axis=-1)
    out = _dot(mixed, wout_ref[...])
    y = x_ref[0, rows, :] + gate * out
    yn = y * lax.rsqrt(jnp.mean(y * y, axis=-1, keepdims=True) + EPS)
    y_ref[0, rows, :] = yn * fw_ref[...]


def _outproj_kernel(*refs):
    for sub in range(OUT_TOKEN_BLOCK // OUTPROJ_SUB):
        _outproj_rows(slice(sub * OUTPROJ_SUB, (sub + 1) * OUTPROJ_SUB), *refs)


def _outproj(x, mod, mod_row_fn, o_r, o_g, gz, gw, fw, wout):
    n_tok = x.shape[0] * x.shape[1]
    B = n_tok // OUT_TOKEN_BLOCK
    x, o_r, o_g, gz = (a.reshape(B, OUT_TOKEN_BLOCK, a.shape[-1]) for a in (x, o_r, o_g, gz))
    const = lambda b: (0, 0)
    blk = lambda width: pl.BlockSpec((1, OUT_TOKEN_BLOCK, width), lambda b: (b, 0, 0))
    return pl.pallas_call(
        _outproj_kernel,
        grid=(B,),
        in_specs=[
            blk(D_MODEL),
            pl.BlockSpec((1, 1, 3 * D_MODEL), lambda b: (mod_row_fn(b), 0, 0)),
            blk(WV),
            blk(WV),
            blk(2 * WV),
            pl.BlockSpec((1, DV), const),
            pl.BlockSpec((1, D_MODEL), const),
            pl.BlockSpec(wout.shape, const),
        ],
        out_specs=blk(D_MODEL),
        out_shape=jax.ShapeDtypeStruct((B, OUT_TOKEN_BLOCK, D_MODEL), F32),
        compiler_params=pltpu.CompilerParams(
            dimension_semantics=("parallel",), vmem_limit_bytes=VMEM_LIMIT),
        name="outproj",
    )(x, mod, o_r, o_g, gz, gw, fw, wout)


def _rope_tables(n_tokens):
    rows = n_tokens // GRID_W
    rr, cc = np.meshgrid(np.arange(rows), np.arange(GRID_W), indexing="ij")
    rr = rr.reshape(-1).astype(np.float32)
    cc = cc.reshape(-1).astype(np.float32)
    n_freq = DK // 4
    inv = np.float32(ROPE_BASE) ** (-np.arange(n_freq, dtype=np.float32) / np.float32(n_freq))
    ang = np.concatenate([rr[:, None] * inv, cc[:, None] * inv], axis=-1)
    cos = np.cos(ang.astype(np.float64)).astype(np.float32)
    sin = np.sin(ang.astype(np.float64)).astype(np.float32)
    cos_t = np.tile(np.concatenate([cos, cos], axis=-1), (1, LANES // DK))
    sin_t = np.tile(np.concatenate([-sin, sin], axis=-1), (1, LANES // DK))
    return jnp.asarray(cos_t), jnp.asarray(sin_t)


def _prep_weights(w_in, gla_w_alpha, gla_b_alpha):
    assert w_in.shape == (D_MODEL, COL_LR + 2 * GLA_RANK)
    wa = jnp.zeros((2 * GLA_RANK, 2 * QK), F32)
    wa = wa.at[0:GLA_RANK, 0:QK].set(gla_w_alpha[0])
    wa = wa.at[GLA_RANK:2 * GLA_RANK, QK:2 * QK].set(gla_w_alpha[1])
    ba = jnp.concatenate([gla_b_alpha[0], gla_b_alpha[1]]).reshape(1, 2 * QK)
    return w_in.astype(BF16), wa.astype(BF16), ba


def kernel(x_prompt, x_sample, c, state_ret, state_gla, c_ctx, w_mod, b_mod, w_in, ret_log_decay,
           gla_w_alpha, gla_b_alpha, gla_norm_w, w_out, final_norm_w):
    assert w_mod.shape[0] == 1, "single-layer model"
    b_ctx, t_ctx, _ = x_prompt.shape
    b_dec, t_dec, _ = x_sample.shape
    assert t_ctx % SCAN_CHUNK == 0 and t_dec % TOKEN_BLOCK == 0 and TOKEN_BLOCK % t_ctx == 0
    assert 1 + b_dec <= MOD_ROWS

    cond = jnp.concatenate(
        [c_ctx[None, :], c, jnp.zeros((MOD_ROWS - 1 - b_dec, D_MODEL), F32)], axis=0)
    mod = _modulation(cond, w_mod[0], b_mod[0]).reshape(MOD_ROWS, 1, 3 * D_MODEL)

    weights = _prep_weights(w_in[0], gla_w_alpha[0], gla_b_alpha[0])
    wout = w_out[0].astype(BF16)
    gw = gla_norm_w[0].reshape(1, DV)
    fw = final_norm_w.reshape(1, D_MODEL)
    ld = ret_log_decay[0]

    per_blk = TOKEN_BLOCK // t_ctx
    xp = x_prompt.reshape(b_ctx // per_blk, TOKEN_BLOCK, D_MODEL)
    ctx_row = lambda b: 0
    qk, gz, vt, lg = _inproj(xp, mod, ctx_row, weights, None)
    qk = qk.reshape(b_ctx, t_ctx, 4 * QK)
    vt = vt.reshape(b_ctx, t_ctx // SCAN_CHUNK, 2 * WV, SCAN_CHUNK)
    lg = lg.reshape(b_ctx, t_ctx, 2 * QK)
    o_r, new_ret = _scan(False, qk, vt, None, ld, None, True, CTX_SEQ_BLOCK, 1)
    o_g, new_gla = _scan(True, qk, vt, lg, None, None, True, CTX_GLA_SEQ_BLOCK, 1)
    y_prompt = _outproj(xp, mod, ctx_row, o_r, o_g, gz, gw, fw, wout)
    y_prompt = y_prompt.reshape(b_ctx, t_ctx, D_MODEL)

    dec_row = lambda b: b + 1
    rope = _rope_tables(t_dec)
    qk, gz, vt, lg = _inproj(x_sample, mod, dec_row, weights, rope)
    (o_r,) = _scan(False, qk, vt, None, ld, state_ret, False, 1, DEC_UNROLL)
    (o_g,) = _scan(True, qk, vt, lg, None, state_gla, False, 1, DEC_UNROLL)
    per_seq = t_dec // OUT_TOKEN_BLOCK
    y_sample = _outproj(x_sample, mod, lambda b: b // per_seq + 1, o_r, o_g, gz, gw, fw, wout)
    y_sample = y_sample.reshape(b_dec, t_dec, D_MODEL)

    return (y_prompt, y_sample, new_ret, new_gla)
```

```python
import functools

import jax
import jax.numpy as jnp
import numpy as np
from jax import lax
from jax.experimental import pallas as pl
from jax.experimental.pallas import tpu as pltpu

F32 = jnp.float32
BF16 = jnp.bfloat16

D_MODEL = 1024
N_HEADS = 4
DK = 64
DV = 128
QK = N_HEADS * DK
WV = N_HEADS * DV
GLA_RANK = 16
GLA_TAU = 16.0
EPS = 1e-6
GRID_W = 64
ROPE_BASE = 10000.0

COL_QK = (0, 2 * QK + 2 * WV)
COL_V = (2 * QK, 4 * QK + 2 * WV)
COL_Z = (2 * QK + WV, 4 * QK + 3 * WV)
COL_LR = 4 * QK + 4 * WV

LANES = 128
SCAN_CHUNK = 128
CTX_SEQ_BLOCK = 8
CTX_GLA_SEQ_BLOCK = 4
DEC_UNROLL = 8
RET_PIPE_GROUP = 2
GLA_PIPE_GROUP = 8
TOKEN_BLOCK = 1024
INPROJ_SUB = 256
OUTPROJ_SUB = 256
MOD_ROWS = 16
MOD_COL_BLOCK = 1536
OUT_TOKEN_BLOCK = 1024
V7X_VMEM_BYTES = 64 * 1024 * 1024
VMEM_LIMIT = 3 * V7X_VMEM_BYTES // 4

_NT = (((1,), (1,)), ((), ()))
_TN = (((0,), (0,)), ((), ()))
_TNT = (((0,), (1,)), ((), ()))


def _dot(a, b):
    return jnp.dot(a, b, preferred_element_type=F32)


def _dot_nt(a, b):
    return lax.dot_general(a, b, _NT, preferred_element_type=F32)


def _silu(x):
    return x * jax.nn.sigmoid(x)


def _mod_kernel(c_ref, w_ref, b_ref, o_ref):
    s = _silu(c_ref[...]).astype(BF16)
    o_ref[...] = _dot(s, w_ref[...].astype(BF16)) + b_ref[...]


def _modulation(cond, w_mod, b_mod):
    n_col = w_mod.shape[1]
    return pl.pallas_call(
        _mod_kernel,
        grid=(n_col // MOD_COL_BLOCK,),
        in_specs=[
            pl.BlockSpec((MOD_ROWS, D_MODEL), lambda j: (0, 0)),
            pl.BlockSpec((D_MODEL, MOD_COL_BLOCK), lambda j: (0, j)),
            pl.BlockSpec((1, MOD_COL_BLOCK), lambda j: (0, j)),
        ],
        out_specs=pl.BlockSpec((MOD_ROWS, MOD_COL_BLOCK), lambda j: (0, j)),
        out_shape=jax.ShapeDtypeStruct((MOD_ROWS, n_col), F32),
        compiler_params=pltpu.CompilerParams(vmem_limit_bytes=VMEM_LIMIT),
        name="modulation",
    )(cond, w_mod, b_mod.reshape(1, n_col))


def _rope_tile(t, cos, sin_signed):
    lane = lax.broadcasted_iota(jnp.int32, t.shape, 1)
    first_half = (lane & 32) == 0
    swapped = jnp.where(first_half, pltpu.roll(t, LANES - 32, 1), pltpu.roll(t, 32, 1))
    return t * cos + swapped * sin_signed


def _inproj_kernel(use_rope, x_ref, mod_ref, w_ref, wa_ref, ba_ref, *rest):
    if use_rope:
        cos_ref, sin_ref, qk_ref, z_ref, vt_ref, lg_ref = rest
    else:
        qk_ref, z_ref, vt_ref, lg_ref = rest
    shift = mod_ref[0, :, 0:D_MODEL]
    scale1 = 1.0 + mod_ref[0, :, D_MODEL:2 * D_MODEL]

    def gate_logs(lr_t, rows):
        logit = lax.dot_general(lr_t, wa_ref[...], _TN, preferred_element_type=F32) + ba_ref[...]
        log_sig = jnp.minimum(logit, 0.0) - jnp.log(1.0 + jnp.exp(-jnp.abs(logit)))
        lg_ref[0, rows, :] = log_sig * (1.0 / GLA_TAU)

    for sub in range(TOKEN_BLOCK // INPROJ_SUB):
        rows = slice(sub * INPROJ_SUB, (sub + 1) * INPROJ_SUB)
        x = x_ref[0, rows, :]
        inv = lax.rsqrt(jnp.mean(x * x, axis=-1, keepdims=True) + EPS)
        hb = ((x * inv) * scale1 + shift).astype(BF16)

        lr_t = lax.dot_general(w_ref[:, COL_LR:COL_LR + 2 * GLA_RANK], hb, _TNT,
                               preferred_element_type=F32).astype(BF16)

        for half in range(2):
            res = _dot(hb, w_ref[:, COL_QK[half]:COL_QK[half] + 2 * QK])
            if use_rope and half == 0:
                for t in range(4):
                    tile = res[:, t * LANES:(t + 1) * LANES]
                    if t >= 2:
                        tile = tile * (DK ** -0.5)
                    tile = _rope_tile(tile, cos_ref[rows, :], sin_ref[rows, :])
                    qk_ref[0, rows, t * LANES:(t + 1) * LANES] = tile.astype(BF16)
            else:
                lane = lax.broadcasted_iota(jnp.int32, (1, 2 * QK), 1)
                scaled = (lane >= QK) if half == 0 else (lane < QK)
                col_scale = jnp.where(scaled, DK ** -0.5, 1.0)
                qk_ref[0, rows, half * 2 * QK:(half + 1) * 2 * QK] = (res * col_scale).astype(BF16)

        gate_logs(lr_t, rows)

        for half in range(2):
            z_ref[0, rows, half * WV:(half + 1) * WV] = _silu(
                _dot(hb, w_ref[:, COL_Z[half]:COL_Z[half] + WV])).astype(BF16)

        for half in range(2):
            vt = lax.dot_general(w_ref[:, COL_V[half]:COL_V[half] + WV], hb, _TNT,
                                 preferred_element_type=F32)
            for j in range(INPROJ_SUB // SCAN_CHUNK):
                jj = sub * (INPROJ_SUB // SCAN_CHUNK) + j
                vt_ref[0, jj, half * WV:(half + 1) * WV, :] = vt[:, j * SCAN_CHUNK:(j + 1) * SCAN_CHUNK].astype(BF16)


def _inproj(x, mod, mod_row_fn, weights, rope):
    B, T, _ = x.shape
    w, wa, ba = weights
    nt = T // TOKEN_BLOCK
    cpb = TOKEN_BLOCK // SCAN_CHUNK
    const = lambda b, t: (0, 0)
    in_specs = [
        pl.BlockSpec((1, TOKEN_BLOCK, D_MODEL), lambda b, t: (b, t, 0)),
        pl.BlockSpec((1, 1, 3 * D_MODEL), lambda b, t: (mod_row_fn(b), 0, 0)),
        pl.BlockSpec(w.shape, const),
        pl.BlockSpec(wa.shape, const),
        pl.BlockSpec(ba.shape, const),
    ]
    args = [x, mod, w, wa, ba]
    if rope is not None:
        in_specs += [pl.BlockSpec((TOKEN_BLOCK, LANES), lambda b, t: (t, 0))] * 2
        args += list(rope)
    out_shape = [
        jax.ShapeDtypeStruct((B, T, 4 * QK), BF16),
        jax.ShapeDtypeStruct((B, T, 2 * WV), BF16),
        jax.ShapeDtypeStruct((B, T // SCAN_CHUNK, 2 * WV, SCAN_CHUNK), BF16),
        jax.ShapeDtypeStruct((B, T, 2 * QK), F32),
    ]
    out_specs = [
        pl.BlockSpec((1, TOKEN_BLOCK, 4 * QK), lambda b, t: (b, t, 0)),
        pl.BlockSpec((1, TOKEN_BLOCK, 2 * WV), lambda b, t: (b, t, 0)),
        pl.BlockSpec((1, cpb, 2 * WV, SCAN_CHUNK), lambda b, t: (b, t, 0, 0)),
        pl.BlockSpec((1, TOKEN_BLOCK, 2 * QK), lambda b, t: (b, t, 0)),
    ]
    return pl.pallas_call(
        functools.partial(_inproj_kernel, rope is not None),
        grid=(B, nt),
        in_specs=in_specs,
        out_specs=out_specs,
        out_shape=out_shape,
        compiler_params=pltpu.CompilerParams(
            dimension_semantics=("parallel", "parallel"), vmem_limit_bytes=VMEM_LIMIT),
        name="inproj_rope" if rope is not None else "inproj",
    )(*args)


def _scan_kernel(is_gla, has_init, want_state, n_chunks, seq_blk, unroll, *refs):
    refs = list(refs)
    q_ref, k_ref, vt_ref = refs[:3]
    pos = 3
    if is_gla:
        lgf_ref, lgb_ref = refs[pos:pos + 2]
        pos += 2
    else:
        ld_ref = refs[pos]
        pos += 1
    if has_init:
        s0_ref = refs[pos]
        pos += 1
    o_ref = refs[pos]
    pos += 1
    if want_state:
        sn_ref = refs[pos]
        pos += 1
    st_ref, oacc_ref = refs[pos:pos + 2]

    C = SCAN_CHUNK
    mid = C // 2
    n_pairs = N_HEADS // 2
    lane = lax.broadcasted_iota(jnp.int32, (1, LANES), 1)
    head_mask = (lane < DK, lane >= DK)
    ri = lax.broadcasted_iota(jnp.int32, (C, C), 0)
    ci = lax.broadcasted_iota(jnp.int32, (C, C), 1)
    tpos = lax.broadcasted_iota(jnp.int32, (C, 1), 0).astype(F32)
    tri = ((ri >= ci), (ci >= ri))

    if has_init:
        for s in range(seq_blk):
            for p in range(n_pairs):
                for d in range(2):
                    s0 = s0_ref[s, 0, d]
                    both = jnp.concatenate([s0[2 * p], s0[2 * p + 1]], axis=0).T
                    st_ref[s, p, d, 0:DV] = both
                    st_ref[s, p, d, DV:2 * DV] = both
    else:
        st_ref[...] = jnp.zeros(st_ref.shape, F32)

    if is_gla:
        tri_ones = [t.astype(BF16) for t in tri]
    else:
        q_mul, k_mul, ret_decay, score_mul = {}, {}, {}, {}
        for p in range(n_pairs):
            for d in range(2):
                ld0 = ld_ref[d, 2 * p]
                ld1 = ld_ref[d, 2 * p + 1]
                ldr = jnp.where(lane < DK, ld0, ld1)
                if d == 0:
                    q_mul[p, d] = jnp.exp(ldr * (tpos + 1.0))
                    k_mul[p, d] = jnp.exp(ldr * (C - 1.0 - tpos))
                    dist = (ri - ci).astype(F32)
                else:
                    q_mul[p, d] = jnp.exp(ldr * (C - tpos))
                    k_mul[p, d] = jnp.exp(ldr * tpos)
                    dist = (ci - ri).astype(F32)
                ret_decay[p, d] = jnp.exp(ldr * float(C))
                score_mul[p, d] = [jnp.where(tri[d], jnp.exp(ldh * dist), 0.0) for ldh in (ld0, ld1)]
            score_mul[p] = [score_mul[p, 0][hh] + score_mul[p, 1][hh] for hh in range(2)]

    def has_intra(d):
        return is_gla or d == 0

    zero = jnp.zeros((), BF16)
    groups = [(s, p, d) for s in range(seq_blk) for p in range(n_pairs) for d in range(2)]

    def run_block(n0, accumulate):
        keys = [g + (u,) for u in range(unroll) for g in groups]
        chains = {}

        def stage_load(key):
            s, p, d, u = key
            ptile = slice(p * LANES, (p + 1) * LANES)
            n = n0 + u
            c = n if d == 0 else n_chunks - 1 - n
            t = dict(c=c, rows=pl.ds(pl.multiple_of(c * C, C), C))
            t["q"] = q_ref[s, t["rows"], ptile]
            t["k"] = k_ref[s, t["rows"], ptile]
            if is_gla:
                lg = (lgf_ref if d == 0 else lgb_ref)[s, t["rows"], ptile]
                hi = lg.astype(BF16)
                lo = (lg - hi.astype(F32)).astype(BF16)
                bb = _dot(tri_ones[d], jnp.concatenate([hi, lo], axis=1))
                t["b"] = bb[:, :LANES] + bb[:, LANES:]
            chains[key] = t

        def stage_scores(key):
            s, p, d, u = key
            t = chains[key]
            q, k = t.pop("q"), t.pop("k")
            if is_gla:
                b = t.pop("b")
                r = b[mid - 1:mid] if d == 0 else b[mid:mid + 1]
                bl = b[C - 1:C] if d == 0 else b[0:1]
                qs = q.astype(F32) * jnp.exp(b - r)
                ks = k.astype(F32) * jnp.exp(r - b)
                q_sc = qs.astype(BF16)
                k_sc = ks.astype(BF16)
                q_dec = (qs * jnp.exp(r)).astype(BF16)
                t["k_dec"] = (ks * jnp.exp(bl - r)).astype(BF16)
                t["decay"] = jnp.exp(bl)
            else:
                q_sc = q
                k_sc = k
                q_dec = (q.astype(F32) * q_mul[p, d]).astype(BF16)
                t["k_dec"] = (k.astype(F32) * k_mul[p, d]).astype(BF16)
                t["decay"] = ret_decay[p, d]
            t["q_dec"] = [jnp.where(head_mask[hh], q_dec, zero) for hh in range(2)]
            if has_intra(d):
                k_heads = jnp.concatenate([jnp.where(head_mask[hh], k_sc, zero) for hh in range(2)], axis=0)
                t["sc"] = _dot_nt(q_sc, k_heads)

        def stage_kv(key):
            s, p, d, u = key
            t = chains[key]
            t["vt"] = vt_ref[s, t["c"], p * 2 * DV:(p + 1) * 2 * DV, :]
            t["kv"] = _dot(t["vt"], t.pop("k_dec"))
            t["scb"] = []
            if has_intra(d):
                sc_all = t.pop("sc")
                for hh in range(2):
                    sc = sc_all[:, hh * C:(hh + 1) * C]
                    if is_gla:
                        sc = jnp.where(tri[d], sc, 0.0)
                    else:
                        sc = sc * score_mul[p][hh]
                    t["scb"].append(sc.astype(BF16))

        def stage_out(key):
            s, p, d, u = key
            t = chains[key]
            st = st_ref[s, p, d]
            stb = st.astype(BF16)
            t["o"] = []
            for hh in range(2):
                hrows = slice(hh * DV, (hh + 1) * DV)
                if has_intra(d):
                    lhs = jnp.concatenate([t["scb"][hh], t["q_dec"][hh]], axis=1)
                    rhs = jnp.concatenate([t["vt"][hrows], stb[hrows]], axis=1)
                else:
                    lhs, rhs = t["q_dec"][hh], stb[hrows]
                t["o"].append(_dot_nt(lhs, rhs))
            st_ref[s, p, d] = st * t.pop("decay") + t.pop("kv")

        def stage_store(key):
            s, p, d, u = key
            t = chains.pop(key)
            for hh in range(2):
                cols = slice((2 * p + hh) * DV, (2 * p + hh + 1) * DV)
                if accumulate:
                    o_ref[s, t["rows"], cols] = (oacc_ref[s, t["rows"], cols] + t["o"][hh]).astype(BF16)
                else:
                    oacc_ref[s, t["rows"], cols] = t["o"][hh]

        stages = (stage_load, stage_scores, stage_kv, stage_out, stage_store)
        grp = GLA_PIPE_GROUP if is_gla else RET_PIPE_GROUP
        key_groups = [keys[i:i + grp] for i in range(0, len(keys), grp)]
        for step in range(len(key_groups) + len(stages) - 1):
            for k in reversed(range(len(stages))):
                if 0 <= step - k < len(key_groups):
                    for key in key_groups[step - k]:
                        stages[k](key)

    def loop_body(accumulate, base):
        def body(m, carry):
            run_block(base + m * unroll, accumulate)
            return carry
        return body

    half = n_chunks // 2
    assert half % unroll == 0
    lax.fori_loop(0, half // unroll, loop_body(False, 0), 0)
    lax.fori_loop(0, half // unroll, loop_body(True, half), 0)

    if want_state:
        for s, p, d in groups:
            for hh in range(2):
                sn_ref[s, 0, d, 2 * p + hh] = st_ref[s, p, d, hh * DV:(hh + 1) * DV].T[hh * DK:(hh + 1) * DK, :]


def _scan(is_gla, qk, vt, lg, ld, s0, want_state, seq_blk, unroll):
    B, T, _ = qk.shape
    n_chunks = T // SCAN_CHUNK
    assert n_chunks % 2 == 0 and B % seq_blk == 0
    sb = seq_blk
    grp = 1 if is_gla else 0
    in_specs = [
        pl.BlockSpec((sb, T, QK), lambda b: (b, 0, 2 * grp)),
        pl.BlockSpec((sb, T, QK), lambda b: (b, 0, 2 * grp + 1)),
        pl.BlockSpec((sb, n_chunks, WV, SCAN_CHUNK), lambda b: (b, 0, grp, 0)),
    ]
    args = [qk, qk, vt]
    if is_gla:
        in_specs += [
            pl.BlockSpec((sb, T, QK), lambda b: (b, 0, 0)),
            pl.BlockSpec((sb, T, QK), lambda b: (b, 0, 1)),
        ]
        args += [lg, lg]
    else:
        in_specs += [pl.BlockSpec(memory_space=pltpu.SMEM)]
        args += [ld]
    state_spec = pl.BlockSpec((sb, 1, 2, N_HEADS, DK, DV), lambda b: (b, 0, 0, 0, 0, 0))
    if s0 is not None:
        in_specs += [state_spec]
        args += [s0]
    out_shape = [jax.ShapeDtypeStruct((B, T, WV), BF16)]
    out_specs = [pl.BlockSpec((sb, T, WV), lambda b: (b, 0, 0))]
    if want_state:
        out_shape += [jax.ShapeDtypeStruct((B, 1, 2, N_HEADS, DK, DV), F32)]
        out_specs += [state_spec]
    name = ("gla" if is_gla else "ret") + ("_init" if s0 is not None else "") + "_scan"
    return pl.pallas_call(
        functools.partial(_scan_kernel, is_gla, s0 is not None, want_state, n_chunks, sb, unroll),
        grid=(B // sb,),
        in_specs=in_specs,
        out_specs=out_specs,
        out_shape=out_shape,
        scratch_shapes=[pltpu.VMEM((sb, N_HEADS // 2, 2, 2 * DV, LANES), F32),
                        pltpu.VMEM((sb, T, WV), F32)],
        compiler_params=pltpu.CompilerParams(
            dimension_semantics=("arbitrary",), vmem_limit_bytes=VMEM_LIMIT),
        name=name,
    )(*args)


def _outproj_rows(rows, x_ref, mod_ref, or_ref, og_ref, gz_ref, gw_ref, fw_ref, wout_ref, y_ref):
    gate = mod_ref[0, :, 2 * D_MODEL:3 * D_MODEL]
    gw = gw_ref[...]
    parts = []
    for h in range(N_HEADS):
        cols = slice(h * DV, (h + 1) * DV)
        t = or_ref[0, rows, cols].astype(F32)
        mu = jnp.mean(t, axis=-1, keepdims=True)
        dlt = t - mu
        var = jnp.mean(dlt * dlt, axis=-1, keepdims=True)
        n = dlt * lax.rsqrt(var + EPS)
        parts.append((n * gz_ref[0, rows, cols].astype(F32)).astype(BF16))
    for h in range(N_HEADS):
        cols = slice(h * DV, (h + 1) * DV)
        t = og_ref[0, rows, cols].astype(F32)
        n = t * lax.rsqrt(jnp.mean(t * t, axis=-1, keepdims=True) + EPS) * gw
        gcols = slice(WV + h * DV, WV + (h + 1) * DV)
        parts.append((n * gz_ref[0, rows, gcols].astype(F32)).astype(BF16))
    mixed = jnp.concatenate(parts, axis=-1)
    out = _dot(mixed, wout_ref[...])
    y = x_ref[0, rows, :] + gate * out
    yn = y * lax.rsqrt(jnp.mean(y * y, axis=-1, keepdims=True) + EPS)
    y_ref[0, rows, :] = yn * fw_ref[...]


def _outproj_kernel(*refs):
    for sub in range(OUT_TOKEN_BLOCK // OUTPROJ_SUB):
        _outproj_rows(slice(sub * OUTPROJ_SUB, (sub + 1) * OUTPROJ_SUB), *refs)


def _outproj(x, mod, mod_row_fn, o_r, o_g, gz, gw, fw, wout):
    n_tok = x.shape[0] * x.shape[1]
    B = n_tok // OUT_TOKEN_BLOCK
    x, o_r, o_g, gz = (a.reshape(B, OUT_TOKEN_BLOCK, a.shape[-1]) for a in (x, o_r, o_g, gz))
    const = lambda b: (0, 0)
    blk = lambda width: pl.BlockSpec((1, OUT_TOKEN_BLOCK, width), lambda b: (b, 0, 0))
    return pl.pallas_call(
        _outproj_kernel,
        grid=(B,),
        in_specs=[
            blk(D_MODEL),
            pl.BlockSpec((1, 1, 3 * D_MODEL), lambda b: (mod_row_fn(b), 0, 0)),
            blk(WV),
            blk(WV),
            blk(2 * WV),
            pl.BlockSpec((1, DV), const),
            pl.BlockSpec((1, D_MODEL), const),
            pl.BlockSpec(wout.shape, const),
        ],
        out_specs=blk(D_MODEL),
        out_shape=jax.ShapeDtypeStruct((B, OUT_TOKEN_BLOCK, D_MODEL), F32),
        compiler_params=pltpu.CompilerParams(
            dimension_semantics=("parallel",), vmem_limit_bytes=VMEM_LIMIT,
            allow_input_fusion=[i == 7 for i in range(8)]),
        name="outproj",
    )(x, mod, o_r, o_g, gz, gw, fw, wout)


def _rope_tables(n_tokens):
    rows = n_tokens // GRID_W
    rr, cc = np.meshgrid(np.arange(rows), np.arange(GRID_W), indexing="ij")
    rr = rr.reshape(-1).astype(np.float32)
    cc = cc.reshape(-1).astype(np.float32)
    n_freq = DK // 4
    inv = np.float32(ROPE_BASE) ** (-np.arange(n_freq, dtype=np.float32) / np.float32(n_freq))
    ang = np.concatenate([rr[:, None] * inv, cc[:, None] * inv], axis=-1)
    cos = np.cos(ang.astype(np.float64)).astype(np.float32)
    sin = np.sin(ang.astype(np.float64)).astype(np.float32)
    cos_t = np.tile(np.concatenate([cos, cos], axis=-1), (1, LANES // DK))
    sin_t = np.tile(np.concatenate([-sin, sin], axis=-1), (1, LANES // DK))
    return jnp.asarray(cos_t), jnp.asarray(sin_t)


def _prep_weights(w_in, gla_w_alpha, gla_b_alpha):
    assert w_in.shape == (D_MODEL, COL_LR + 2 * GLA_RANK)
    wa = jnp.zeros((2 * GLA_RANK, 2 * QK), F32)
    wa = wa.at[0:GLA_RANK, 0:QK].set(gla_w_alpha[0])
    wa = wa.at[GLA_RANK:2 * GLA_RANK, QK:2 * QK].set(gla_w_alpha[1])
    ba = jnp.concatenate([gla_b_alpha[0], gla_b_alpha[1]]).reshape(1, 2 * QK)
    return w_in.astype(BF16), wa.astype(BF16), ba


def kernel(x_prompt, x_sample, c, state_ret, state_gla, c_ctx, w_mod, b_mod, w_in, ret_log_decay,
           gla_w_alpha, gla_b_alpha, gla_norm_w, w_out, final_norm_w):
    assert w_mod.shape[0] == 1, "single-layer model"
    b_ctx, t_ctx, _ = x_prompt.shape
    b_dec, t_dec, _ = x_sample.shape
    assert t_ctx % SCAN_CHUNK == 0 and t_dec % TOKEN_BLOCK == 0 and TOKEN_BLOCK % t_ctx == 0
    assert 1 + b_dec <= MOD_ROWS

    cond = jnp.concatenate(
        [c_ctx[None, :], c, jnp.zeros((MOD_ROWS - 1 - b_dec, D_MODEL), F32)], axis=0)
    mod = _modulation(cond, w_mod[0], b_mod[0]).reshape(MOD_ROWS, 1, 3 * D_MODEL)

    weights = _prep_weights(w_in[0], gla_w_alpha[0], gla_b_alpha[0])
    wout = w_out[0].astype(BF16)
    gw = gla_norm_w[0].reshape(1, DV)
    fw = final_norm_w.reshape(1, D_MODEL)
    ld = ret_log_decay[0]

    per_blk = TOKEN_BLOCK // t_ctx
    xp = x_prompt.reshape(b_ctx // per_blk, TOKEN_BLOCK, D_MODEL)
    ctx_row = lambda b: 0
    qk, gz, vt, lg = _inproj(xp, mod, ctx_row, weights, None)
    qk = qk.reshape(b_ctx, t_ctx, 4 * QK)
    vt = vt.reshape(b_ctx, t_ctx // SCAN_CHUNK, 2 * WV, SCAN_CHUNK)
    lg = lg.reshape(b_ctx, t_ctx, 2 * QK)
    o_r, new_ret = _scan(False, qk, vt, None, ld, None, True, CTX_SEQ_BLOCK, 1)
    o_g, new_gla = _scan(True, qk, vt, lg, None, None, True, CTX_GLA_SEQ_BLOCK, 1)
    y_prompt = _outproj(xp, mod, ctx_row, o_r, o_g, gz, gw, fw, wout)
    y_prompt = y_prompt.reshape(b_ctx, t_ctx, D_MODEL)

    dec_row = lambda b: b + 1
    rope = _rope_tables(t_dec)
    qk, gz, vt, lg = _inproj(x_sample, mod, dec_row, weights, rope)
    (o_r,) = _scan(False, qk, vt, None, ld, state_ret, False, 1, DEC_UNROLL)
    (o_g,) = _scan(True, qk, vt, lg, None, state_gla, False, 1, DEC_UNROLL)
    per_seq = t_dec // OUT_TOKEN_BLOCK
    y_sample = _outproj(x_sample, mod, lambda b: b // per_seq + 1, o_r, o_g, gz, gw, fw, wout)
    y_sample = y_sample.reshape(b_dec, t_dec, D_MODEL)

    return (y_prompt, y_sample, new_ret, new_gla)
```
